```python
import jax, jax.numpy as jnp
from jax import lax
import numpy as np

D_MODEL = 2048
BATCH = 8
SEQ = 4096
DEPTH = 4

ATTN_WIDTH = D_MODEL // 2
HEAD_DIM_ATTN = 128
N_HEADS_ATTN = ATTN_WIDTH // HEAD_DIM_ATTN
DILATION_PATTERNS = ((128, 1), (512, 4), (2048, 16))
RET_WIDTH = D_MODEL // 2
N_HEADS_RET = 4
RET_V_DIM = RET_WIDTH // N_HEADS_RET
RET_QK_DIM = RET_V_DIM // 2
RET_CHUNK = 128
MIX_WIDTH = ATTN_WIDTH + RET_WIDTH
IN_SPLIT_SIZES = (ATTN_WIDTH, ATTN_WIDTH, ATTN_WIDTH, ATTN_WIDTH,
                  N_HEADS_RET * RET_QK_DIM, N_HEADS_RET * RET_QK_DIM, RET_WIDTH, RET_WIDTH)
IN_PROJ_WIDTH = sum(IN_SPLIT_SIZES)
IN_SPLIT_IDX = tuple(int(i) for i in np.cumsum(IN_SPLIT_SIZES)[:-1])
NORM_EPS = 1e-6
MASK_VALUE = -1e30

kernel_name = "hybrid_dilated_attn_retention_encoder"


def rms_norm(x, g):
    xf = x.astype(jnp.float32)
    y = xf * lax.rsqrt(jnp.mean(xf * xf, axis=-1, keepdims=True) + NORM_EPS)
    return (y * g.astype(jnp.float32)).astype(x.dtype)


def alibi_slopes(n_heads):
    return jnp.exp2(-8.0 * (jnp.arange(n_heads, dtype=jnp.float32) + 1.0) / n_heads)


def dilated_window_attention(q, k, v, window, dilation, slopes):
    b, s, h, e = q.shape
    radius = window // (2 * dilation)
    blk = radius
    sub_len = s // dilation
    n_blk = -(-sub_len // blk)
    pad_len = n_blk * blk
    def to_sub(t):
        return t.reshape(b, sub_len, dilation, h, e)
    qs = jnp.pad(to_sub(q), ((0, 0), (0, pad_len - sub_len), (0, 0), (0, 0), (0, 0)))
    qb = qs.reshape(b, n_blk, blk, dilation, h, e)
    def key_windows(t):
        tp = jnp.pad(to_sub(t), ((0, 0), (blk, blk + pad_len - sub_len), (0, 0), (0, 0), (0, 0)))
        tb = tp.reshape(b, n_blk + 2, blk, dilation, h, e)
        return jnp.concatenate([tb[:, :-2], tb[:, 1:-1], tb[:, 2:]], axis=2)
    kw = key_windows(k)
    vw = key_windows(v)
    scores = jnp.einsum('bnirhe,bnjrhe->bnrhij', qb, kw).astype(jnp.float32)
    i_idx = jnp.arange(blk)[:, None]
    j_idx = jnp.arange(3 * blk)[None, :]
    rel = j_idx - blk - i_idx
    key_pos = jnp.arange(n_blk)[:, None] * blk + jnp.arange(3 * blk)[None, :] - blk
    valid = (key_pos >= 0) & (key_pos < sub_len)
    mask = (jnp.abs(rel) <= radius)[None] & valid[:, None, :]
    dist = (jnp.abs(rel) * dilation).astype(jnp.float32)
    bias = -slopes[:, None, None] * dist[None]
    scores = jnp.where(mask[None, :, None, None], scores + bias[None, None, None], MASK_VALUE)
    m = jnp.max(scores, axis=-1, keepdims=True)
    p = jnp.exp(scores - m)
    den = jnp.sum(p, axis=-1)
    o = jnp.einsum('bnrhij,bnjrhe->bnirhe', p, vw.astype(jnp.float32))
    o = o / den.transpose(0, 1, 4, 2, 3)[..., None]
    lse = (m[..., 0] + jnp.log(den)).transpose(0, 1, 4, 2, 3)
    o = o.reshape(b, pad_len, dilation, h, e)[:, :sub_len].reshape(b, s, h, e)
    lse = lse.reshape(b, pad_len, dilation, h)[:, :sub_len].reshape(b, s, h)
    return o, lse


def dilated_attention_mixture(q, k, v):
    slopes = alibi_slopes(q.shape[2])
    outs, lses = [], []
    for window, dilation in DILATION_PATTERNS:
        o, lse = dilated_window_attention(q, k, v, window, dilation, slopes)
        outs.append(o)
        lses.append(lse)
    w = jax.nn.softmax(jnp.stack(lses, axis=0), axis=0)
    o = jnp.sum(w[..., None] * jnp.stack(outs, axis=0), axis=0)
    return o.astype(v.dtype)


def retention_one_direction(q, k, v, log_gamma):
    b, s, h, dk = q.shape
    dv = v.shape[-1]
    c = RET_CHUNK
    n = s // c
    def chunk(t):
        return t.reshape(b, n, c, h, t.shape[-1]).transpose(1, 0, 3, 2, 4)
    qc, kc, vc = chunk(q), chunk(k), chunk(v)
    idx = jnp.arange(c, dtype=jnp.float32)
    rel = idx[:, None] - idx[None, :]
    lg = log_gamma[:, None, None]
    decay = jnp.where(rel[None] >= 0, jnp.exp(jnp.maximum(rel, 0.0)[None] * lg), 0.0).astype(q.dtype)
    xi = jnp.exp((idx[None] + 1.0) * log_gamma[:, None]).astype(q.dtype)[..., None]
    zeta = jnp.exp((c - 1.0 - idx[None]) * log_gamma[:, None]).astype(q.dtype)[..., None]
    g_chunk = jnp.exp(c * log_gamma).astype(q.dtype)[:, None, None]

    def step(state, inp):
        qi, ki, vi = inp
        inner = jnp.einsum('bhid,bhjd->bhij', qi, ki) * decay
        o = jnp.einsum('bhij,bhje->bhie', inner, vi) + jnp.einsum('bhid,bhde->bhie', qi, state) * xi
        state = state * g_chunk + jnp.einsum('bhjd,bhje->bhde', ki * zeta, vi)
        return state, o

    state0 = jnp.zeros((b, h, dk, dv), dtype=q.dtype)
    _, oc = lax.scan(step, state0, (qc, kc, vc))
    return oc.transpose(1, 0, 3, 2, 4).reshape(b, s, h, dv)


def bidirectional_retention(q, k, v, decay_logit_f, decay_logit_b):
    lg_f = jax.nn.log_sigmoid(decay_logit_f.astype(jnp.float32))
    lg_b = jax.nn.log_sigmoid(decay_logit_b.astype(jnp.float32))
    o_f = retention_one_direction(q, k, v, lg_f)
    o_b = jnp.flip(retention_one_direction(jnp.flip(q, 1), jnp.flip(k, 1), jnp.flip(v, 1), lg_b), 1)
    o = (o_f + o_b).astype(jnp.float32)
    o = o * lax.rsqrt(jnp.mean(o * o, axis=-1, keepdims=True) + NORM_EPS)
    return o.astype(v.dtype)


def hybrid_layer(x, c_act, g, w_ada, b_ada, w_in, w_out, dec_f, dec_b):
    b, s, _ = x.shape
    mod = c_act @ w_ada + b_ada
    shift, scale, gate = jnp.split(mod, 3, axis=-1)
    h = rms_norm(x, g) * (1.0 + scale[:, None]) + shift[:, None]
    proj = jnp.einsum('bsd,df->bsf', h, w_in)
    q_a, k_a, v_a, z_a, q_r, k_r, v_r, z_r = jnp.split(proj, IN_SPLIT_IDX, axis=-1)
    q_a = q_a.reshape(b, s, N_HEADS_ATTN, HEAD_DIM_ATTN) * (HEAD_DIM_ATTN ** -0.5)
    k_a = k_a.reshape(b, s, N_HEADS_ATTN, HEAD_DIM_ATTN)
    v_a = v_a.reshape(b, s, N_HEADS_ATTN, HEAD_DIM_ATTN)
    y_a = dilated_attention_mixture(q_a, k_a, v_a).reshape(b, s, ATTN_WIDTH)
    q_r = q_r.reshape(b, s, N_HEADS_RET, RET_QK_DIM)
    k_r = k_r.reshape(b, s, N_HEADS_RET, RET_QK_DIM) * (RET_QK_DIM ** -0.5)
    v_r = v_r.reshape(b, s, N_HEADS_RET, RET_V_DIM)
    y_r = bidirectional_retention(q_r, k_r, v_r, dec_f, dec_b).reshape(b, s, RET_WIDTH)
    y = jnp.concatenate([y_a * jax.nn.silu(z_a), y_r * jax.nn.silu(z_r)], axis=-1)
    out = jnp.einsum('bsf,fd->bsd', y, w_out)
    return x + gate[:, None] * out


def _fwd_setup_inputs(seed: int = 0) -> dict:
    key = jax.random.key(seed)
    ks = jax.random.split(key, 10)
    f32 = jnp.float32
    x = jax.random.normal(ks[0], (BATCH, SEQ, D_MODEL), f32)
    c = jax.random.normal(ks[1], (BATCH, D_MODEL), f32)
    norm_gain = 1.0 + 0.02 * jax.random.normal(ks[2], (DEPTH, D_MODEL), f32)
    w_ada = 0.5 * D_MODEL ** -0.5 * jax.random.normal(ks[3], (DEPTH, D_MODEL, 3 * D_MODEL), f32)
    b_ada = 0.02 * jax.random.normal(ks[4], (DEPTH, 3 * D_MODEL), f32)
    w_in = D_MODEL ** -0.5 * jax.random.normal(ks[5], (DEPTH, D_MODEL, IN_PROJ_WIDTH), f32)
    w_out = MIX_WIDTH ** -0.5 * jax.random.normal(ks[6], (DEPTH, MIX_WIDTH, D_MODEL), f32)
    gamma = 1.0 - jnp.exp2(-5.0 - jnp.arange(N_HEADS_RET, dtype=f32))
    base_logit = jnp.log(gamma) - jnp.log1p(-gamma)
    ret_decay_logit_f = base_logit[None] + 0.1 * jax.random.normal(ks[7], (DEPTH, N_HEADS_RET), f32)
    ret_decay_logit_b = base_logit[None] + 0.1 * jax.random.normal(ks[8], (DEPTH, N_HEADS_RET), f32)
    final_gain = 1.0 + 0.02 * jax.random.normal(ks[9], (D_MODEL,), f32)
    return {"x": x, "c": c, "norm_gain": norm_gain, "w_ada": w_ada, "b_ada": b_ada,
            "w_in": w_in, "w_out": w_out, "ret_decay_logit_f": ret_decay_logit_f,
            "ret_decay_logit_b": ret_decay_logit_b, "final_gain": final_gain}


def _fwd_reference(x, c, norm_gain, w_ada, b_ada, w_in, w_out, ret_decay_logit_f, ret_decay_logit_b, final_gain):
    c_act = jax.nn.silu(c)
    h = x
    for layer in range(DEPTH):
        h = hybrid_layer(h, c_act, norm_gain[layer], w_ada[layer], b_ada[layer], w_in[layer],
                         w_out[layer], ret_decay_logit_f[layer], ret_decay_logit_b[layer])
    return rms_norm(h, final_gain)


import jax as _jax
import jax.numpy as _jnp

TWIN_FORMAT = 'train_step'
FWD_PARAMS = ['x', 'c', 'norm_gain', 'w_ada', 'b_ada', 'w_in', 'w_out', 'ret_decay_logit_f', 'ret_decay_logit_b', 'final_gain']
TWIN_WEIGHTS = ['norm_gain', 'w_ada', 'b_ada', 'w_in', 'w_out', 'ret_decay_logit_f', 'ret_decay_logit_b', 'final_gain']
TWIN_DIFF_INPUT = 'x'
TWIN_INPUTS = ['x', 'c', 'norm_gain', 'w_ada', 'b_ada', 'w_in', 'w_out', 'ret_decay_logit_f', 'ret_decay_logit_b', 'final_gain', 'loss_target', 'm_norm_gain', 'm_w_ada', 'm_b_ada', 'm_w_in', 'm_w_out', 'm_ret_decay_logit_f', 'm_ret_decay_logit_b', 'm_final_gain', 'v_norm_gain', 'v_w_ada', 'v_b_ada', 'v_w_in', 'v_w_out', 'v_ret_decay_logit_f', 'v_ret_decay_logit_b', 'v_final_gain']
TWIN_OUTPUTS = ['loss', 'grad_x', 'grad_norm_gain', 'grad_w_ada', 'grad_b_ada', 'grad_w_in', 'grad_w_out', 'grad_ret_decay_logit_f', 'grad_ret_decay_logit_b', 'grad_final_gain', 'delta_norm_gain', 'delta_w_ada', 'delta_b_ada', 'delta_w_in', 'delta_w_out', 'delta_ret_decay_logit_f', 'delta_ret_decay_logit_b', 'delta_final_gain', 'new_m_norm_gain', 'new_m_w_ada', 'new_m_b_ada', 'new_m_w_in', 'new_m_w_out', 'new_m_ret_decay_logit_f', 'new_m_ret_decay_logit_b', 'new_m_final_gain', 'new_v_norm_gain', 'new_v_w_ada', 'new_v_b_ada', 'new_v_w_in', 'new_v_w_out', 'new_v_ret_decay_logit_f', 'new_v_ret_decay_logit_b', 'new_v_final_gain']
TWIN_LEAF_KINDS = {'loss': 'loss', 'grad_x': 'grad_x', 'grad_norm_gain': 'grad_w', 'grad_w_ada': 'grad_w', 'grad_b_ada': 'grad_w', 'grad_w_in': 'grad_w', 'grad_w_out': 'grad_w', 'grad_ret_decay_logit_f': 'grad_w', 'grad_ret_decay_logit_b': 'grad_w', 'grad_final_gain': 'grad_w', 'delta_norm_gain': 'delta_w', 'delta_w_ada': 'delta_w', 'delta_b_ada': 'delta_w', 'delta_w_in': 'delta_w', 'delta_w_out': 'delta_w', 'delta_ret_decay_logit_f': 'delta_w', 'delta_ret_decay_logit_b': 'delta_w', 'delta_final_gain': 'delta_w', 'new_m_norm_gain': 'new_m', 'new_m_w_ada': 'new_m', 'new_m_b_ada': 'new_m', 'new_m_w_in': 'new_m', 'new_m_w_out': 'new_m', 'new_m_ret_decay_logit_f': 'new_m', 'new_m_ret_decay_logit_b': 'new_m', 'new_m_final_gain': 'new_m', 'new_v_norm_gain': 'new_v', 'new_v_w_ada': 'new_v', 'new_v_b_ada': 'new_v', 'new_v_w_in': 'new_v', 'new_v_w_out': 'new_v', 'new_v_ret_decay_logit_f': 'new_v', 'new_v_ret_decay_logit_b': 'new_v', 'new_v_final_gain': 'new_v'}


def _forward(args):
    return _fwd_reference(*[args[k] for k in FWD_PARAMS])


def _output_shape():
    def fwd():
        inp = _fwd_setup_inputs(0)
        return _fwd_reference(*[inp[k] for k in FWD_PARAMS])
    out = _jax.eval_shape(fwd)
    return out.shape, out.dtype

N_MICROBATCH = 1
ADAM_LR = 0.001
ADAM_B1 = 0.9
ADAM_B2 = 0.999
ADAM_EPS = 1e-08
ADAM_WD = 0.01
ADAM_STEP = 10
PER_EXAMPLE_BATCH_AXIS = {'x': 0, 'c': 0, 'loss_target': 0}
SHARED_INPUTS = []
_WEIGHT_DTYPES = {'norm_gain': _jnp.float32, 'w_ada': _jnp.float32, 'b_ada': _jnp.float32, 'w_in': _jnp.float32, 'w_out': _jnp.float32, 'ret_decay_logit_f': _jnp.float32, 'ret_decay_logit_b': _jnp.float32, 'final_gain': _jnp.float32}
MOMENT_SCALE = {'norm_gain': 2.472391e-02, 'w_ada': 2.561428e-02, 'b_ada': 4.294208e-02, 'w_in': 1.541706e-02, 'w_out': 1.359968e-02, 'ret_decay_logit_f': 1.250783e-01, 'ret_decay_logit_b': 9.625845e-02, 'final_gain': 1.598942e+01}


def _to_microbatches(a, axis):
    t = _jnp.moveaxis(a, axis, 0)
    t = t.reshape((N_MICROBATCH, t.shape[0] // N_MICROBATCH) + t.shape[1:])
    return _jnp.moveaxis(t, 1, axis + 1)


def setup_inputs(seed: int = 0) -> dict:
    inp = _fwd_setup_inputs(seed)
    key = _jax.random.fold_in(_jax.random.key(seed), 7919)
    shape, _ = _output_shape()
    out = dict(inp)
    out["loss_target"] = _jax.random.normal(_jax.random.fold_in(key, 0), shape, _jnp.float32)
    for i, name in enumerate(TWIN_WEIGHTS):
        w = inp[name].astype(_jnp.float32)
        if MOMENT_SCALE is None:
            s = _jnp.sqrt(_jnp.mean(_jnp.square(w)) + 1e-30)
        else:
            s = MOMENT_SCALE[name]
        km, kv = _jax.random.split(_jax.random.fold_in(key, i + 1))
        out[name] = w
        out["m_" + name] = s * _jax.random.normal(km, w.shape, _jnp.float32)
        out["v_" + name] = (s * s) * _jax.random.uniform(kv, w.shape, _jnp.float32, 0.5, 1.5)
    if N_MICROBATCH > 1:
        for name, axis in PER_EXAMPLE_BATCH_AXIS.items():
            out[name] = _to_microbatches(out[name], axis)
    return {'x': out['x'], 'c': out['c'], 'norm_gain': out['norm_gain'], 'w_ada': out['w_ada'], 'b_ada': out['b_ada'], 'w_in': out['w_in'], 'w_out': out['w_out'], 'ret_decay_logit_f': out['ret_decay_logit_f'], 'ret_decay_logit_b': out['ret_decay_logit_b'], 'final_gain': out['final_gain'], 'loss_target': out['loss_target'], 'm_norm_gain': out['m_norm_gain'], 'm_w_ada': out['m_w_ada'], 'm_b_ada': out['m_b_ada'], 'm_w_in': out['m_w_in'], 'm_w_out': out['m_w_out'], 'm_ret_decay_logit_f': out['m_ret_decay_logit_f'], 'm_ret_decay_logit_b': out['m_ret_decay_logit_b'], 'm_final_gain': out['m_final_gain'], 'v_norm_gain': out['v_norm_gain'], 'v_w_ada': out['v_w_ada'], 'v_b_ada': out['v_b_ada'], 'v_w_in': out['v_w_in'], 'v_w_out': out['v_w_out'], 'v_ret_decay_logit_f': out['v_ret_decay_logit_f'], 'v_ret_decay_logit_b': out['v_ret_decay_logit_b'], 'v_final_gain': out['v_final_gain']}


def _loss(weights, diff, rest, loss_target):
    with _jax.named_scope("forward"):
        args = {**rest, TWIN_DIFF_INPUT: diff, **{k: w.astype(_WEIGHT_DTYPES[k]) for k, w in weights.items()}}
        y = _forward(args)
    with _jax.named_scope("loss_head"):
        err = _jnp.square(y.astype(_jnp.float32) - loss_target)
        return 0.5 * _jnp.sum(_jnp.mean(err, axis=-1)) if err.ndim else 0.5 * err


def _adamw(w, g, m, v):
    m = ADAM_B1 * m + (1.0 - ADAM_B1) * g
    v = ADAM_B2 * v + (1.0 - ADAM_B2) * _jnp.square(g)
    m_hat = m / (1.0 - ADAM_B1 ** ADAM_STEP)
    v_hat = v / (1.0 - ADAM_B2 ** ADAM_STEP)
    delta = -ADAM_LR * (m_hat / (_jnp.sqrt(v_hat) + ADAM_EPS) + ADAM_WD * w)
    return delta, m, v


def reference(x, c, norm_gain, w_ada, b_ada, w_in, w_out, ret_decay_logit_f, ret_decay_logit_b, final_gain, loss_target, m_norm_gain, m_w_ada, m_b_ada, m_w_in, m_w_out, m_ret_decay_logit_f, m_ret_decay_logit_b, m_final_gain, v_norm_gain, v_w_ada, v_b_ada, v_w_in, v_w_out, v_ret_decay_logit_f, v_ret_decay_logit_b, v_final_gain):
    given = dict(x=x, c=c, norm_gain=norm_gain, w_ada=w_ada, b_ada=b_ada, w_in=w_in, w_out=w_out, ret_decay_logit_f=ret_decay_logit_f, ret_decay_logit_b=ret_decay_logit_b, final_gain=final_gain, loss_target=loss_target, m_norm_gain=m_norm_gain, m_w_ada=m_w_ada, m_b_ada=m_b_ada, m_w_in=m_w_in, m_w_out=m_w_out, m_ret_decay_logit_f=m_ret_decay_logit_f, m_ret_decay_logit_b=m_ret_decay_logit_b, m_final_gain=m_final_gain, v_norm_gain=v_norm_gain, v_w_ada=v_w_ada, v_b_ada=v_b_ada, v_w_in=v_w_in, v_w_out=v_w_out, v_ret_decay_logit_f=v_ret_decay_logit_f, v_ret_decay_logit_b=v_ret_decay_logit_b, v_final_gain=v_final_gain)
    weights = {n: given[n] for n in TWIN_WEIGHTS}
    shared = {n: given[n] for n in SHARED_INPUTS}
    per_example = {n: given[n] for n in ['x', 'c']}
    grad_fn = _jax.value_and_grad(_loss, argnums=(0, 1))

    def one_microbatch(ex, loss_target):
        ex = dict(ex)
        diff = ex.pop(TWIN_DIFF_INPUT)
        return grad_fn(weights, diff, {**shared, **ex}, loss_target)

    if N_MICROBATCH == 1:
        loss, (grad_w, grad_x) = one_microbatch(per_example, given["loss_target"])
    else:
        def body(carry, xs):
            loss_sum, grad_sum = carry
            l_k, (gw_k, gx_k) = one_microbatch(xs[0], xs[1])
            with _jax.named_scope("update"):
                return (loss_sum + l_k, _jax.tree.map(_jnp.add, grad_sum, gw_k)), gx_k

        init = (_jnp.zeros((), _jnp.float32), _jax.tree.map(_jnp.zeros_like, weights))
        (loss, grad_w), grad_x = _jax.lax.scan(body, init, (per_example, given["loss_target"]))
    with _jax.named_scope("update"):
        delta_w, new_m, new_v = {}, {}, {}
        for n in TWIN_WEIGHTS:
            delta_w[n], new_m[n], new_v[n] = _adamw(weights[n], grad_w[n], given["m_" + n], given["v_" + n])
    return (loss, grad_x, *[grad_w[n] for n in TWIN_WEIGHTS], *[delta_w[n] for n in TWIN_WEIGHTS],
            *[new_m[n] for n in TWIN_WEIGHTS], *[new_v[n] for n in TWIN_WEIGHTS])
```

```python
import functools
import math

import jax
import jax.numpy as jnp
from jax import lax
from jax.experimental import pallas as pl
from jax.experimental.pallas import tpu as pltpu

F32, BF16 = jnp.float32, jnp.bfloat16

D_MODEL = 2048
DEPTH = 4
N_DEV = 8
ATTN_WIDTH = 1024
HEAD_DIM = 128
N_HEADS_ATTN = 8
DILATIONS = (1, 4, 16)
RADIUS = 64
RET_HEADS = 4
RET_QK = 128
RET_V = 256
RET_CHUNK = 128
IN_W = 7168
QKV_A = 3 * ATTN_WIDTH
COL_ZA, COL_QR, COL_KR, COL_VR, COL_ZR = 3072, 4096, 4608, 5120, 6144
W_IN_SHARD = IN_W // N_DEV
W_OUT_SHARD = D_MODEL // N_DEV
W_ADA_SHARD = 3 * D_MODEL // N_DEV
NORM_EPS = 1e-6
MASK_VALUE = -1e30
ATTN_SCALE = HEAD_DIM ** -0.5
RET_SCALE = RET_QK ** -0.5
LN2 = math.log(2.0)

ADAM_LR, ADAM_B1, ADAM_B2, ADAM_EPS, ADAM_WD, ADAM_STEP = 0.001, 0.9, 0.999, 1e-08, 0.01, 10
ADAM_C1 = 1.0 / (1.0 - ADAM_B1 ** ADAM_STEP)
ADAM_C2 = 1.0 / (1.0 - ADAM_B2 ** ADAM_STEP)

VMEM_LIMIT_BYTES = 56 * 1024 * 1024
MESH = pl.DeviceIdType.MESH


def _params(*sem):
    return pltpu.CompilerParams(dimension_semantics=sem if sem else None, vmem_limit_bytes=VMEM_LIMIT_BYTES)


def _dot(a, b):
    return jnp.dot(a, b, preferred_element_type=F32)


def _dot_nt(a, b):
    return lax.dot_general(a, b, (((1,), (1,)), ((), ())), preferred_element_type=F32)


def _dot_tn(a, b):
    return lax.dot_general(a, b, (((0,), (0,)), ((), ())), preferred_element_type=F32)


def _iota(shape, dim):
    return lax.broadcasted_iota(jnp.int32, shape, dim)


def _sigmoid(z):
    return 1.0 / (1.0 + jnp.exp(-z))


def inproj_fwd(x, g, scale1p, shift, w):
    s_len = x.shape[0]
    tm, tn = min(512, s_len), 1024

    def body(x_ref, g_ref, sc_ref, sh_ref, w_ref, proj_ref, h_ref):
        @pl.when(pl.program_id(1) == 0)
        def _():
            xv = x_ref[...]
            r = lax.rsqrt(jnp.mean(xv * xv, axis=-1, keepdims=True) + NORM_EPS)
            h_ref[...] = ((xv * r * g_ref[...]) * sc_ref[...] + sh_ref[...]).astype(BF16)
        proj_ref[...] = _dot(h_ref[...], w_ref[...]).astype(BF16)

    vec = pl.BlockSpec((1, D_MODEL), lambda i, j: (0, 0))
    return pl.pallas_call(
        body, name="inproj_fwd", grid=(s_len // tm, IN_W // tn),
        in_specs=[pl.BlockSpec((tm, D_MODEL), lambda i, j: (i, 0)), vec, vec, vec,
                  pl.BlockSpec((D_MODEL, tn), lambda i, j: (0, j))],
        out_specs=[pl.BlockSpec((tm, tn), lambda i, j: (i, j)), pl.BlockSpec((tm, D_MODEL), lambda i, j: (i, 0))],
        out_shape=[jax.ShapeDtypeStruct((s_len, IN_W), BF16), jax.ShapeDtypeStruct((s_len, D_MODEL), BF16)],
        compiler_params=_params("arbitrary", "arbitrary"),
    )(x, g, scale1p, shift, w)


def _attn_geometry(sub_len):
    tq = min(sub_len, 128)
    win = min(sub_len, tq + 2 * RADIUS)
    return tq, win, sub_len // tq


def _attn_scores(q, k, q0, start, tq, win, slope, dil):
    s = _dot_nt(q, k) * ATTN_SCALE
    dist = jnp.abs((start + _iota((tq, win), 1)) - (q0 + _iota((tq, win), 0)))
    return jnp.where(dist <= RADIUS, s - slope * (dist * dil).astype(F32), MASK_VALUE)


def _head_slope():
    h = pl.program_id(1)
    return jnp.exp(-(h + 1).astype(F32) * LN2 * jnp.ones((1, 1), F32))


def attn_fwd(qkv, dil, q_col, k_col, v_col):
    n_cls, sub_len, _ = qkv.shape
    tq, win, n_tiles = _attn_geometry(sub_len)

    def body(q_ref, k_ref, v_ref, o_ref, lse_ref):
        slope = _head_slope()

        def tile(i, carry):
            q0 = pl.multiple_of(i * tq, tq)
            start = pl.multiple_of(jnp.clip(i * tq - RADIUS, 0, sub_len - win), 16)
            s = _attn_scores(q_ref[pl.ds(q0, tq), :], k_ref[pl.ds(start, win), :], q0, start, tq, win, slope, dil)
            m = jnp.max(s, axis=1, keepdims=True)
            p = jnp.exp(s - m)
            den = jnp.sum(p, axis=1, keepdims=True)
            o_ref[pl.ds(q0, tq), :] = _dot(p.astype(BF16), v_ref[pl.ds(start, win), :]) / den
            lse_ref[pl.ds(q0, tq), :] = jnp.broadcast_to(m + jnp.log(den), (tq, HEAD_DIM))
            return carry

        lax.fori_loop(0, n_tiles, tile, 0)

    def col(c0):
        return pl.BlockSpec((None, sub_len, HEAD_DIM), lambda r, h: (r, 0, c0 + h))

    out = jax.ShapeDtypeStruct((n_cls, sub_len, ATTN_WIDTH), F32)
    return pl.pallas_call(
        body, name=f"attn_fwd_d{dil}", grid=(n_cls, N_HEADS_ATTN),
        in_specs=[col(q_col), col(k_col), col(v_col)], out_specs=[col(0), col(0)], out_shape=[out, out],
        compiler_params=_params("arbitrary", "arbitrary"),
    )(qkv, qkv, qkv)


def attn_bwd(qkv, do, lse, delta, dil, q_col, k_col, v_col):
    n_cls, sub_len, _ = qkv.shape
    tq, win, n_tiles = _attn_geometry(sub_len)

    def body(q_ref, k_ref, v_ref, do_ref, lse_ref, dl_ref, dq_ref, dk_ref, dv_ref, dk_acc, dv_acc):
        slope = _head_slope()
        dk_acc[...] = jnp.zeros_like(dk_acc)
        dv_acc[...] = jnp.zeros_like(dv_acc)

        def tile(i, carry):
            q0 = pl.multiple_of(i * tq, tq)
            start = pl.multiple_of(jnp.clip(i * tq - RADIUS, 0, sub_len - win), 16)
            q = q_ref[pl.ds(q0, tq), :]
            k = k_ref[pl.ds(start, win), :]
            v = v_ref[pl.ds(start, win), :]
            dov = do_ref[pl.ds(q0, tq), :]
            s = _attn_scores(q, k, q0, start, tq, win, slope, dil)
            p = jnp.exp(s - lse_ref[pl.ds(q0, tq), 0:1])
            ds = (p * (_dot_nt(dov, v) - dl_ref[pl.ds(q0, tq), 0:1])).astype(BF16)
            dq_ref[pl.ds(q0, tq), :] = (_dot(ds, k) * ATTN_SCALE).astype(BF16)
            dk_acc[pl.ds(start, win), :] += _dot_tn(ds, q) * ATTN_SCALE
            dv_acc[pl.ds(start, win), :] += _dot_tn(p.astype(BF16), dov)
            return carry

        lax.fori_loop(0, n_tiles, tile, 0)
        dk_ref[...] = dk_acc[...].astype(BF16)
        dv_ref[...] = dv_acc[...].astype(BF16)

    def col(c0):
        return pl.BlockSpec((None, sub_len, HEAD_DIM), lambda r, h: (r, 0, c0 + h))

    out = jax.ShapeDtypeStruct((n_cls, sub_len, ATTN_WIDTH), BF16)
    return pl.pallas_call(
        body, name=f"attn_bwd_d{dil}", grid=(n_cls, N_HEADS_ATTN),
        in_specs=[col(q_col), col(k_col), col(v_col), col(0), col(0), col(0)],
        out_specs=[col(0), col(0), col(0)], out_shape=[out, out, out],
        scratch_shapes=[pltpu.VMEM((sub_len, HEAD_DIM), F32), pltpu.VMEM((sub_len, HEAD_DIM), F32)],
        compiler_params=_params("arbitrary", "arbitrary"),
    )(qkv, qkv, qkv, do, lse, delta)


def _ret_tables(lg_ref):
    h = pl.program_id(0)
    one = jnp.ones((1, 1), F32)
    lgf, lgb = lg_ref[h] * one, lg_ref[RET_HEADS + h] * one
    c = RET_CHUNK
    rel = (_iota((c, c), 0) - _iota((c, c), 1)).astype(F32)
    dec_f = jnp.where(rel >= 0, jnp.exp(jnp.maximum(rel, 0.0) * lgf), 0.0)
    dec_b = jnp.where(rel <= 0, jnp.exp(jnp.maximum(-rel, 0.0) * lgb), 0.0)
    ci = _iota((c, 1), 0).astype(F32)
    tab = dict(rel=rel, dec_f=dec_f, dec_b=dec_b, ci=ci,
               xi_f=jnp.exp((ci + 1.0) * lgf), ze_f=jnp.exp((c - 1.0 - ci) * lgf), g_f=jnp.exp(c * lgf),
               xi_b=jnp.exp((c - ci) * lgb), ze_b=jnp.exp(ci * lgb), g_b=jnp.exp(c * lgb))
    return tab


def _ret_specs(s_len):
    q = pl.BlockSpec((s_len, RET_QK), lambda h: (0, COL_QR // RET_QK + h))
    k = pl.BlockSpec((s_len, RET_QK), lambda h: (0, COL_KR // RET_QK + h))
    v = pl.BlockSpec((s_len, RET_V), lambda h: (0, COL_VR // RET_V + h))
    wide = pl.BlockSpec((s_len, RET_V), lambda h: (0, h))
    narrow = pl.BlockSpec((s_len, RET_QK), lambda h: (0, h))
    smem = pl.BlockSpec(memory_space=pltpu.SMEM)
    return smem, q, k, v, wide, narrow


def ret_fwd(proj, lg):
    s_len = proj.shape[0]
    c, n_chunks = RET_CHUNK, proj.shape[0] // RET_CHUNK

    def body(lg_ref, q_ref, k_ref, v_ref, opre_ref, y_ref, st_f, st_b):
        t = _ret_tables(lg_ref)
        dec = t["dec_f"] + t["dec_b"]
        st_f[...] = jnp.zeros_like(st_f)
        st_b[...] = jnp.zeros_like(st_b)

        def load(n):
            r0 = pl.multiple_of(n * c, c)
            q, k, v = q_ref[pl.ds(r0, c), :], k_ref[pl.ds(r0, c), :], v_ref[pl.ds(r0, c), :]
            return r0, q, (k.astype(F32) * RET_SCALE), v

        def fwd(n, carry):
            r0, q, kf, v = load(n)
            inner = (_dot_nt(q, kf.astype(BF16)) * dec).astype(BF16)
            opre_ref[pl.ds(r0, c), :] = _dot(inner, v) + _dot(q, st_f[...].astype(BF16)) * t["xi_f"]
            st_f[...] = st_f[...] * t["g_f"] + _dot_tn((kf * t["ze_f"]).astype(BF16), v)
            return carry

        def bwd(i, carry):
            r0, q, kf, v = load(n_chunks - 1 - i)
            o = opre_ref[pl.ds(r0, c), :] + _dot(q, st_b[...].astype(BF16)) * t["xi_b"]
            st_b[...] = st_b[...] * t["g_b"] + _dot_tn((kf * t["ze_b"]).astype(BF16), v)
            opre_ref[pl.ds(r0, c), :] = o
            y_ref[pl.ds(r0, c), :] = o * lax.rsqrt(jnp.mean(o * o, axis=-1, keepdims=True) + NORM_EPS)
            return carry

        lax.fori_loop(0, n_chunks, fwd, 0)
        lax.fori_loop(0, n_chunks, bwd, 0)

    smem, q, k, v, wide, _ = _ret_specs(s_len)
    out = jax.ShapeDtypeStruct((s_len, RET_HEADS * RET_V), F32)
    return pl.pallas_call(
        body, name="ret_fwd", grid=(RET_HEADS,), in_specs=[smem, q, k, v], out_specs=[wide, wide],
        out_shape=[out, out], scratch_shapes=[pltpu.VMEM((RET_QK, RET_V), F32), pltpu.VMEM((RET_QK, RET_V), F32)],
        compiler_params=_params("arbitrary"),
    )(lg, proj, proj, proj)


def ret_bwd(proj, lg, o_pre, dy):
    s_len = proj.shape[0]
    c, n_chunks = RET_CHUNK, proj.shape[0] // RET_CHUNK
    cf = float(c)

    def body(lg_ref, q_ref, k_ref, v_ref, o_ref, dy_ref, dq_ref, dk_ref, dv_ref, glf_ref, glb_ref,
             st_f, dst_b, st_b, dst_f, keep_sf, keep_dtb):
        t = _ret_tables(lg_ref)
        dec = t["dec_f"] + t["dec_b"]
        e_f, e_b, ci = t["rel"] * t["dec_f"], -t["rel"] * t["dec_b"], t["ci"]
        for ref in (st_f, dst_b, st_b, dst_f):
            ref[...] = jnp.zeros_like(ref)

        def load(n):
            r0 = pl.multiple_of(n * c, c)
            q, k, v = q_ref[pl.ds(r0, c), :], k_ref[pl.ds(r0, c), :], v_ref[pl.ds(r0, c), :]
            o, dyv = o_ref[pl.ds(r0, c), :], dy_ref[pl.ds(r0, c), :]
            rr = lax.rsqrt(jnp.mean(o * o, axis=-1, keepdims=True) + NORM_EPS)
            y = o * rr
            do = (rr * (dyv - y * jnp.mean(dyv * y, axis=-1, keepdims=True))).astype(BF16)
            return r0, q, k.astype(F32) * RET_SCALE, v, do

        def total(x):
            return jnp.sum(jnp.sum(x, axis=1, keepdims=True), axis=0, keepdims=True)

        def fwd(n, carry):
            glf, glb = carry
            r0, q, kf, v, do = load(n)
            qf, kb = q.astype(F32), kf.astype(BF16)
            a = _dot_nt(q, kb)
            b = _dot_nt(do, v)
            da = (b * dec).astype(BF16)
            ab = a * b
            glf = glf + total(e_f * ab)
            glb = glb + total(e_b * ab)
            sf_b, dtb_b = st_f[...].astype(BF16), dst_b[...].astype(BF16)
            dq_inter = _dot_nt(do, sf_b) * t["xi_f"]
            dk_inter = _dot_nt(v, dtb_b) * t["ze_b"]
            glf = glf + total((ci + 1.0) * (qf * dq_inter))
            glb = glb + total(ci * (kf * dk_inter))
            dq_ref[pl.ds(r0, c), :] = _dot(da, kb) + dq_inter
            dk_ref[pl.ds(r0, c), :] = _dot_tn(da, q) + dk_inter
            dv_ref[pl.ds(r0, c), :] = _dot_tn((a * dec).astype(BF16), do) + _dot((kf * t["ze_b"]).astype(BF16), dtb_b)
            keep_sf[n] = sf_b
            keep_dtb[n] = dtb_b
            st_f[...] = st_f[...] * t["g_f"] + _dot_tn((kf * t["ze_f"]).astype(BF16), v)
            dst_b[...] = dst_b[...] * t["g_b"] + _dot_tn((qf * t["xi_b"]).astype(BF16), do)
            return glf, glb

        def bwd(i, carry):
            glf, glb = carry
            n = n_chunks - 1 - i
            r0, q, kf, v, do = load(n)
            qf = q.astype(F32)
            tb_b, dsf_b = st_b[...].astype(BF16), dst_f[...].astype(BF16)
            dq_inter = _dot_nt(do, tb_b) * t["xi_b"]
            dk_inter = _dot_nt(v, dsf_b) * t["ze_f"]
            glb = glb + total((cf - ci) * (qf * dq_inter)) + cf * t["g_b"] * total(keep_dtb[n].astype(F32) * st_b[...])
            glf = glf + total((cf - 1.0 - ci) * (kf * dk_inter)) + cf * t["g_f"] * total(dst_f[...] * keep_sf[n].astype(F32))
            dq_ref[pl.ds(r0, c), :] += dq_inter
            dk_ref[pl.ds(r0, c), :] = (dk_ref[pl.ds(r0, c), :] + dk_inter) * RET_SCALE
            dv_ref[pl.ds(r0, c), :] += _dot((kf * t["ze_f"]).astype(BF16), dsf_b)
            st_b[...] = st_b[...] * t["g_b"] + _dot_tn((kf * t["ze_b"]).astype(BF16), v)
            dst_f[...] = dst_f[...] * t["g_f"] + _dot_tn((qf * t["xi_f"]).astype(BF16), do)
            return glf, glb

        zero = jnp.zeros((1, 1), F32)
        carry = lax.fori_loop(0, n_chunks, fwd, (zero, zero))
        glf, glb = lax.fori_loop(0, n_chunks, bwd, carry)
        glf_ref[...] = jnp.broadcast_to(glf, (8, 128))
        glb_ref[...] = jnp.broadcast_to(glb, (8, 128))

    smem, q, k, v, wide, narrow = _ret_specs(s_len)
    scal = pl.BlockSpec((None, 8, 128), lambda h: (h, 0, 0))
    state = pltpu.VMEM((RET_QK, RET_V), F32)
    keep = pltpu.VMEM((n_chunks, RET_QK, RET_V), BF16)
    return pl.pallas_call(
        body, name="ret_bwd", grid=(RET_HEADS,), in_specs=[smem, q, k, v, wide, wide],
        out_specs=[narrow, narrow, wide, scal, scal],
        out_shape=[jax.ShapeDtypeStruct((s_len, RET_HEADS * RET_QK), F32), jax.ShapeDtypeStruct((s_len, RET_HEADS * RET_QK), F32),
                   jax.ShapeDtypeStruct((s_len, RET_HEADS * RET_V), F32),
                   jax.ShapeDtypeStruct((RET_HEADS, 8, 128), F32), jax.ShapeDtypeStruct((RET_HEADS, 8, 128), F32)],
        scratch_shapes=[state, state, state, state, keep, keep],
        compiler_params=_params("arbitrary"),
    )(lg, proj, proj, proj, o_pre, dy)


def _silu_parts(z):
    sig = _sigmoid(z)
    return z * sig, sig * (1.0 + z * (1.0 - sig))


def outproj_fwd(x, gate, w_out, attn_outs, y_r, proj):
    s_len = x.shape[0]
    tm = min(256, s_len)

    def body(x_ref, gate_ref, w_ref, o1, l1, o2, l2, o3, l3, yr_ref, za_ref, zr_ref, xn_ref, oa_ref, lse_ref):
        la, lb, lc = l1[...], l2[...], l3[...]
        m = jnp.maximum(jnp.maximum(la, lb), lc)
        lse = m + jnp.log(jnp.exp(la - m) + jnp.exp(lb - m) + jnp.exp(lc - m))
        o_a = jnp.exp(la - lse) * o1[...] + jnp.exp(lb - lse) * o2[...] + jnp.exp(lc - lse) * o3[...]
        oa_ref[...] = o_a
        lse_ref[...] = lse
        silu_a, _ = _silu_parts(za_ref[...].astype(F32))
        silu_r, _ = _silu_parts(zr_ref[...].astype(F32))
        y = jnp.concatenate([(o_a * silu_a).astype(BF16), (yr_ref[...] * silu_r).astype(BF16)], axis=1)
        xn_ref[...] = x_ref[...] + gate_ref[...] * _dot(y, w_ref[...])

    row = lambda w: pl.BlockSpec((tm, w), lambda i: (i, 0))
    half = row(ATTN_WIDTH)
    flat = [a for pair in attn_outs for a in pair]
    return pl.pallas_call(
        body, name="outproj_fwd", grid=(s_len // tm,),
        in_specs=[row(D_MODEL), pl.BlockSpec((1, D_MODEL), lambda i: (0, 0)),
                  pl.BlockSpec((D_MODEL, D_MODEL), lambda i: (0, 0))] + [half] * 7
                 + [pl.BlockSpec((tm, ATTN_WIDTH), lambda i: (i, COL_ZA // ATTN_WIDTH)),
                    pl.BlockSpec((tm, ATTN_WIDTH), lambda i: (i, COL_ZR // ATTN_WIDTH))],
        out_specs=[row(D_MODEL), half, half],
        out_shape=[jax.ShapeDtypeStruct((s_len, D_MODEL), F32), jax.ShapeDtypeStruct((s_len, ATTN_WIDTH), F32),
                   jax.ShapeDtypeStruct((s_len, ATTN_WIDTH), F32)],
        compiler_params=_params("arbitrary"),
    )(x, gate, w_out, *flat, y_r, proj, proj)


def loss_head(x, gain, target):
    s_len = x.shape[0]
    tm = min(256, s_len)

    def body(x_ref, g_ref, t_ref, dx_ref, loss_ref, dg_ref):
        @pl.when(pl.program_id(0) == 0)
        def _():
            loss_ref[...] = jnp.zeros_like(loss_ref)
            dg_ref[...] = jnp.zeros_like(dg_ref)
        xv, g = x_ref[...], g_ref[...]
        r = lax.rsqrt(jnp.mean(xv * xv, axis=-1, keepdims=True) + NORM_EPS)
        xn = xv * r
        err = xn * g - t_ref[...]
        part = 0.5 * jnp.sum(jnp.mean(err * err, axis=-1, keepdims=True), axis=0, keepdims=True)
        loss_ref[...] += jnp.broadcast_to(part, loss_ref.shape)
        dy = err * (1.0 / D_MODEL)
        dg_ref[...] += jnp.sum(dy * xn, axis=0, keepdims=True)
        dxn = dy * g
        dx_ref[...] = r * (dxn - xn * jnp.mean(dxn * xn, axis=-1, keepdims=True))

    row = pl.BlockSpec((tm, D_MODEL), lambda i: (i, 0))
    vec = pl.BlockSpec((1, D_MODEL), lambda i: (0, 0))
    return pl.pallas_call(
        body, name="loss_head", grid=(s_len // tm,), in_specs=[row, vec, row],
        out_specs=[row, pl.BlockSpec((8, 128), lambda i: (0, 0)), vec],
        out_shape=[jax.ShapeDtypeStruct((s_len, D_MODEL), F32), jax.ShapeDtypeStruct((8, 128), F32),
                   jax.ShapeDtypeStruct((1, D_MODEL), F32)],
        compiler_params=_params("arbitrary"),
    )(x, gain, target)


def outproj_bwd(dxn, gate, w_out, o_a, y_r, proj):
    s_len = dxn.shape[0]
    tm = min(256, s_len)

    def body(dx_ref, gate_ref, w_ref, oa_ref, yr_ref, za_ref, zr_ref,
             doa_ref, dl_ref, dyr_ref, dza_ref, dzr_ref, y_ref, dxb_ref):
        dxv = dx_ref[...]
        dxb_ref[...] = dxv.astype(BF16)
        dy = _dot_nt((dxv * gate_ref[...]).astype(BF16), w_ref[...])
        dy_a, dy_r = dy[:, :ATTN_WIDTH], dy[:, ATTN_WIDTH:]
        o_a, y_rv = oa_ref[...], yr_ref[...]
        silu_a, dsilu_a = _silu_parts(za_ref[...].astype(F32))
        silu_r, dsilu_r = _silu_parts(zr_ref[...].astype(F32))
        do_a = dy_a * silu_a
        doa_ref[...] = do_a.astype(BF16)
        prod = do_a * o_a
        dl_ref[...] = jnp.concatenate(
            [jnp.broadcast_to(jnp.sum(prod[:, h * HEAD_DIM:(h + 1) * HEAD_DIM], axis=1, keepdims=True), (tm, HEAD_DIM))
             for h in range(N_HEADS_ATTN)], axis=1)
        dyr_ref[...] = dy_r * silu_r
        dza_ref[...] = (dy_a * o_a * dsilu_a).astype(BF16)
        dzr_ref[...] = (dy_r * y_rv * dsilu_r).astype(BF16)
        y_ref[...] = jnp.concatenate([(o_a * silu_a).astype(BF16), (y_rv * silu_r).astype(BF16)], axis=1)

    row = lambda w: pl.BlockSpec((tm, w), lambda i: (i, 0))
    half = row(ATTN_WIDTH)
    sds = lambda w, dt: jax.ShapeDtypeStruct((s_len, w), dt)
    return pl.pallas_call(
        body, name="outproj_bwd", grid=(s_len // tm,),
        in_specs=[row(D_MODEL), pl.BlockSpec((1, D_MODEL), lambda i: (0, 0)),
                  pl.BlockSpec((D_MODEL, D_MODEL), lambda i: (0, 0)), half, half,
                  pl.BlockSpec((tm, ATTN_WIDTH), lambda i: (i, COL_ZA // ATTN_WIDTH)),
                  pl.BlockSpec((tm, ATTN_WIDTH), lambda i: (i, COL_ZR // ATTN_WIDTH))],
        out_specs=[half, half, half, half, half, row(D_MODEL), row(D_MODEL)],
        out_shape=[sds(ATTN_WIDTH, BF16), sds(ATTN_WIDTH, F32), sds(ATTN_WIDTH, F32), sds(ATTN_WIDTH, BF16),
                   sds(ATTN_WIDTH, BF16), sds(D_MODEL, BF16), sds(D_MODEL, BF16)],
        compiler_params=_params("arbitrary"),
    )(dxn, gate, w_out, o_a, y_r, proj, proj)


def wout_grad(y, dxb, gate, w_out):
    s_len = y.shape[0]
    tf, ts = 512, min(512, s_len)

    def body(y_ref, dx_ref, gate_ref, w_ref, dw_ref, dgate_ref, acc):
        f, s = pl.program_id(0), pl.program_id(1)

        @pl.when((f == 0) & (s == 0))
        def _():
            dgate_ref[...] = jnp.zeros_like(dgate_ref)

        @pl.when(s == 0)
        def _():
            acc[...] = jnp.zeros_like(acc)
        acc[...] += _dot_tn(y_ref[...], dx_ref[...])

        @pl.when(s == pl.num_programs(1) - 1)
        def _():
            m = acc[...]
            dw_ref[...] = (m * gate_ref[...]).astype(BF16).reshape(dw_ref.shape)
            dgate_ref[...] += jnp.sum(m * w_ref[...].astype(F32), axis=0, keepdims=True)

    per = tf // W_OUT_SHARD
    return pl.pallas_call(
        body, name="wout_grad", grid=(D_MODEL // tf, s_len // ts),
        in_specs=[pl.BlockSpec((ts, tf), lambda f, s: (s, f)), pl.BlockSpec((ts, D_MODEL), lambda f, s: (s, 0)),
                  pl.BlockSpec((1, D_MODEL), lambda f, s: (0, 0)), pl.BlockSpec((tf, D_MODEL), lambda f, s: (f, 0))],
        out_specs=[pl.BlockSpec((per, W_OUT_SHARD, D_MODEL), lambda f, s: (f, 0, 0)),
                   pl.BlockSpec((1, D_MODEL), lambda f, s: (0, 0))],
        out_shape=[jax.ShapeDtypeStruct((N_DEV, W_OUT_SHARD, D_MODEL), BF16), jax.ShapeDtypeStruct((1, D_MODEL), F32)],
        scratch_shapes=[pltpu.VMEM((tf, D_MODEL), F32)],
        compiler_params=_params("arbitrary", "arbitrary"),
    )(y, dxb, gate, w_out)


def assemble_dproj(dqkv_a, dz_a, dq_r, dk_r, dv_r, dz_r):
    s_len = dz_a.shape[0]
    tm = min(256, s_len)

    def body(*refs):
        pat, (dza, dqr, dkr, dvr, dzr, out) = refs[:9], refs[9:]
        for t in range(3):
            tot = pat[t][...].astype(F32) + pat[3 + t][...].astype(F32) + pat[6 + t][...].astype(F32)
            out[:, t * ATTN_WIDTH:(t + 1) * ATTN_WIDTH] = tot.astype(BF16)
        out[:, COL_ZA:COL_QR] = dza[...]
        out[:, COL_QR:COL_KR] = dqr[...].astype(BF16)
        out[:, COL_KR:COL_VR] = dkr[...].astype(BF16)
        out[:, COL_VR:COL_ZR] = dvr[...].astype(BF16)
        out[:, COL_ZR:IN_W] = dzr[...]

    row = lambda w: pl.BlockSpec((tm, w), lambda i: (i, 0))
    flat = [dqkv_a[p][t] for p in range(3) for t in range(3)]
    return pl.pallas_call(
        body, name="assemble_dproj", grid=(s_len // tm,),
        in_specs=[row(ATTN_WIDTH)] * 10 + [row(512), row(512), row(ATTN_WIDTH), row(ATTN_WIDTH)],
        out_specs=row(IN_W), out_shape=jax.ShapeDtypeStruct((s_len, IN_W), BF16),
        compiler_params=_params("arbitrary"),
    )(*flat, dz_a, dq_r, dk_r, dv_r, dz_r)


def inproj_bwd(dproj, w, x, g, scale1p, dxn):
    s_len = x.shape[0]
    tm, tk = min(512, s_len), 1024

    def body(dp_ref, w_ref, x_ref, g_ref, sc_ref, dxn_ref, dx_ref, st_ref, acc):
        i, k = pl.program_id(0), pl.program_id(1)

        @pl.when((i == 0) & (k == 0))
        def _():
            st_ref[...] = jnp.zeros_like(st_ref)

        @pl.when(k == 0)
        def _():
            acc[...] = jnp.zeros_like(acc)
        acc[...] += _dot_nt(dp_ref[...], w_ref[...])

        @pl.when(k == pl.num_programs(1) - 1)
        def _():
            dh, xv, g, sc = acc[...], x_ref[...], g_ref[...], sc_ref[...]
            r = lax.rsqrt(jnp.mean(xv * xv, axis=-1, keepdims=True) + NORM_EPS)
            xn = xv * r
            da = dh * sc
            st_ref[0:1, :] += jnp.sum(dh, axis=0, keepdims=True)
            st_ref[1:2, :] += jnp.sum(dh * (xn * g), axis=0, keepdims=True)
            st_ref[2:3, :] += jnp.sum(da * xn, axis=0, keepdims=True)
            dn = da * g
            dx_ref[...] = r * (dn - xn * jnp.mean(dn * xn, axis=-1, keepdims=True)) + dxn_ref[...]

    row = pl.BlockSpec((tm, D_MODEL), lambda i, k: (i, 0))
    vec = pl.BlockSpec((1, D_MODEL), lambda i, k: (0, 0))
    return pl.pallas_call(
        body, name="inproj_bwd", grid=(s_len // tm, IN_W // tk),
        in_specs=[pl.BlockSpec((tm, tk), lambda i, k: (i, k)), pl.BlockSpec((D_MODEL, tk), lambda i, k: (0, k)),
                  row, vec, vec, row],
        out_specs=[row, pl.BlockSpec((8, D_MODEL), lambda i, k: (0, 0))],
        out_shape=[jax.ShapeDtypeStruct((s_len, D_MODEL), F32), jax.ShapeDtypeStruct((8, D_MODEL), F32)],
        scratch_shapes=[pltpu.VMEM((tm, D_MODEL), F32)],
        compiler_params=_params("arbitrary", "arbitrary"),
    )(dproj, w, x, g, scale1p, dxn)


def win_grad(h, dproj):
    s_len = h.shape[0]
    ts = min(512, s_len)

    def body(h_ref, dp_ref, dw_ref, acc):
        s = pl.program_id(1)

        @pl.when(s == 0)
        def _():
            acc[...] = jnp.zeros_like(acc)
        acc[...] += _dot_tn(h_ref[...], dp_ref[...])

        @pl.when(s == pl.num_programs(1) - 1)
        def _():
            dw_ref[...] = acc[...].astype(BF16)

    return pl.pallas_call(
        body, name="win_grad", grid=(N_DEV, s_len // ts),
        in_specs=[pl.BlockSpec((ts, D_MODEL), lambda j, s: (s, 0)), pl.BlockSpec((ts, W_IN_SHARD), lambda j, s: (s, j))],
        out_specs=pl.BlockSpec((None, D_MODEL, W_IN_SHARD), lambda j, s: (j, 0, 0)),
        out_shape=jax.ShapeDtypeStruct((N_DEV, D_MODEL, W_IN_SHARD), BF16),
        scratch_shapes=[pltpu.VMEM((D_MODEL, W_IN_SHARD), F32)],
        compiler_params=_params("arbitrary", "arbitrary"),
    )(h, dproj)


def ada_fwd(c_all, w_ada):
    def body(c_ref, w_ref, act_ref, part_ref):
        cv = c_ref[...]
        act = cv * _sigmoid(cv)
        act_ref[...] = act
        part_ref[...] = _dot(act.astype(BF16), w_ref[...].astype(BF16))

    return pl.pallas_call(
        body, name="ada_fwd", grid=(DEPTH,),
        in_specs=[pl.BlockSpec((N_DEV, D_MODEL), lambda l: (0, 0)),
                  pl.BlockSpec((None, D_MODEL, W_ADA_SHARD), lambda l: (l, 0, 0))],
        out_specs=[pl.BlockSpec((N_DEV, D_MODEL), lambda l: (0, 0)),
                   pl.BlockSpec((None, N_DEV, W_ADA_SHARD), lambda l: (l, 0, 0))],
        out_shape=[jax.ShapeDtypeStruct((N_DEV, D_MODEL), F32), jax.ShapeDtypeStruct((DEPTH, N_DEV, W_ADA_SHARD), F32)],
        compiler_params=_params("arbitrary"),
    )(c_all, w_ada)


def _adamw(w, g, m, v):
    m = ADAM_B1 * m + (1.0 - ADAM_B1) * g
    v = ADAM_B2 * v + (1.0 - ADAM_B2) * (g * g)
    delta = -ADAM_LR * ((m * ADAM_C1) / (jnp.sqrt(v * ADAM_C2) + ADAM_EPS) + ADAM_WD * w)
    return delta, m, v


def ada_update(act_t, dmod, w, m, v):
    tr = 512

    def body(a_ref, d_ref, w_ref, m_ref, v_ref, g_out, dl_out, m_out, v_out):
        a = a_ref[...].astype(BF16).astype(F32)
        d = d_ref[...].astype(BF16).astype(F32)
        g = a[:, 0:1] * d[0:1, :]
        for b in range(1, N_DEV):
            g = g + a[:, b:b + 1] * d[b:b + 1, :]
        g_out[...] = g
        dl_out[...], m_out[...], v_out[...] = _adamw(w_ref[...], g, m_ref[...], v_ref[...])

    blk = pl.BlockSpec((None, tr, W_ADA_SHARD), lambda l, r: (l, r, 0))
    out = jax.ShapeDtypeStruct(w.shape, F32)
    return pl.pallas_call(
        body, name="ada_update", grid=(DEPTH, D_MODEL // tr),
        in_specs=[pl.BlockSpec((tr, N_DEV), lambda l, r: (r, 0)),
                  pl.BlockSpec((None, N_DEV, W_ADA_SHARD), lambda l, r: (l, 0, 0)), blk, blk, blk],
        out_specs=[blk] * 4, out_shape=[out] * 4, compiler_params=_params("arbitrary", "arbitrary"),
    )(act_t, dmod, w, m, v)


def shard_update(parts, w, m, v, name):
    n_rows, n_cols = w.shape
    tr = min(256, n_rows)

    def body(p_ref, w_ref, m_ref, v_ref, g_out, dl_out, m_out, v_out):
        g = p_ref[0].astype(F32)
        for k in range(1, N_DEV):
            g = g + p_ref[k].astype(F32)
        g_out[...] = g
        dl_out[...], m_out[...], v_out[...] = _adamw(w_ref[...], g, m_ref[...], v_ref[...])

    blk = pl.BlockSpec((tr, n_cols), lambda r: (r, 0))
    out = jax.ShapeDtypeStruct(w.shape, F32)
    return pl.pallas_call(
        body, name=name, grid=(n_rows // tr,),
        in_specs=[pl.BlockSpec((N_DEV, tr, n_cols), lambda r: (0, r, 0)), blk, blk, blk],
        out_specs=[blk] * 4, out_shape=[out] * 4, compiler_params=_params("arbitrary"),
    )(parts, w, m, v)


def small_update(parts, w, m, v):
    def body(p_ref, w_ref, m_ref, v_ref, g_out, dl_out, m_out, v_out):
        g = p_ref[0]
        for k in range(1, N_DEV):
            g = g + p_ref[k]
        g_out[...] = g
        dl_out[...], m_out[...], v_out[...] = _adamw(w_ref[...], g, m_ref[...], v_ref[...])

    out = jax.ShapeDtypeStruct(w.shape, F32)
    return pl.pallas_call(body, name="small_update", out_shape=[out] * 4, compiler_params=_params())(parts, w, m, v)


def _position():
    return lax.axis_index("x"), lax.axis_index("y"), lax.axis_index("c")


def _flat(px, py, pc):
    return 4 * px + 2 * py + pc


def _two_level_allgather(srcs, dst_block, send_sems, recv_sems, local_sems):
    x, y, c = _position()
    me, sibling = (x, y, c), (x, y, 1 - c)
    chips = [(1 - x, y), (x, 1 - y), (1 - x, 1 - y)]
    n = len(srcs)

    def copy(a, k, block, to, src=None):
        dst = dst_block(a, _flat(*block))
        return pltpu.make_async_remote_copy(
            src_ref=dst if src is None else src, dst_ref=dst, send_sem=send_sems.at[a * 7 + k],
            recv_sem=recv_sems.at[a * 7 + k], device_id=to, device_id_type=MESH)

    mine = [pltpu.make_async_copy(srcs[a], dst_block(a, _flat(*me)), local_sems.at[a]) for a in range(n)]
    for cp in mine:
        cp.start()
    first = []
    for a in range(n):
        first.append(copy(a, 0, me, sibling, src=srcs[a]))
        first += [copy(a, 1 + j, me, (*chip, c), src=srcs[a]) for j, chip in enumerate(chips)]
    for cp in first:
        cp.start()
    passed = []
    for j, chip in enumerate(chips):
        for a in range(n):
            copy(a, 1 + j, (*chip, c), me).wait_recv()
            fwd = copy(a, 4 + j, (*chip, c), sibling)
            fwd.start()
            passed.append(fwd)
    for a in range(n):
        copy(a, 0, sibling, me).wait_recv()
        for j, chip in enumerate(chips):
            copy(a, 4 + j, (*chip, 1 - c), me).wait_recv()
    for cp in first + passed:
        cp.wait_send()
    for cp in mine:
        cp.wait()


def allgather_rows(x, name):
    def body(x_ref, out_ref, send_sems, recv_sems, local_sems):
        _two_level_allgather([x_ref], lambda a, idx: out_ref.at[idx], send_sems, recv_sems, local_sems)

    vmem = pl.BlockSpec(memory_space=pltpu.VMEM)
    return pl.pallas_call(
        body, name=name, in_specs=[vmem], out_specs=vmem,
        out_shape=jax.ShapeDtypeStruct((N_DEV,) + x.shape, x.dtype),
        scratch_shapes=[pltpu.SemaphoreType.DMA((7,)), pltpu.SemaphoreType.DMA((7,)), pltpu.SemaphoreType.DMA((1,))],
        compiler_params=_params(),
    )(x)


def allgather_weights(w_in_b, w_out_b):
    def body(win_ref, wout_ref, fin_ref, fout_ref, send_sems, recv_sems, local_sems):
        def dst_block(a, idx):
            if a == 0:
                return fin_ref.at[:, :, pl.ds(pl.multiple_of(idx * W_IN_SHARD, 128), W_IN_SHARD)]
            return fout_ref.at[:, pl.ds(pl.multiple_of(idx * W_OUT_SHARD, W_OUT_SHARD), W_OUT_SHARD), :]
        _two_level_allgather([win_ref, wout_ref], dst_block, send_sems, recv_sems, local_sems)

    hbm = pl.BlockSpec(memory_space=pl.ANY)
    return pl.pallas_call(
        body, name="allgather_weights", in_specs=[hbm, hbm], out_specs=[hbm, hbm],
        out_shape=[jax.ShapeDtypeStruct((DEPTH, D_MODEL, IN_W), BF16), jax.ShapeDtypeStruct((DEPTH, D_MODEL, D_MODEL), BF16)],
        scratch_shapes=[pltpu.SemaphoreType.DMA((14,)), pltpu.SemaphoreType.DMA((14,)), pltpu.SemaphoreType.DMA((2,))],
        compiler_params=_params(),
    )(w_in_b, w_out_b)


def exchange_grads(dw_in, dw_out, name):
    def body(in_a, in_b, out_a, out_b, send_sems, recv_sems, local_sems):
        x, y, c = _position()
        me = _flat(x, y, c)
        pairs = ((in_a, out_a), (in_b, out_b))
        local = [pltpu.make_async_copy(src.at[me], dst.at[0], local_sems.at[a]) for a, (src, dst) in enumerate(pairs)]
        for cp in local:
            cp.start()
        copies = []
        for k in range(1, N_DEV):
            px = 1 - x if k & 4 else x
            py = 1 - y if k & 2 else y
            pc = 1 - c if k & 1 else c
            for a, (src, dst) in enumerate(pairs):
                copies.append(pltpu.make_async_remote_copy(
                    src_ref=src.at[_flat(px, py, pc)], dst_ref=dst.at[k], send_sem=send_sems.at[a * 7 + k - 1],
                    recv_sem=recv_sems.at[a * 7 + k - 1], device_id=(px, py, pc), device_id_type=MESH))
        for cp in copies:
            cp.start()
        for cp in copies:
            cp.wait_recv()
        for cp in copies:
            cp.wait_send()
        for cp in local:
            cp.wait()

    hbm = pl.BlockSpec(memory_space=pl.ANY)
    return pl.pallas_call(
        body, name=name, in_specs=[hbm, hbm], out_specs=[hbm, hbm],
        out_shape=[jax.ShapeDtypeStruct(dw_in.shape, BF16), jax.ShapeDtypeStruct(dw_out.shape, BF16)],
        scratch_shapes=[pltpu.SemaphoreType.DMA((14,)), pltpu.SemaphoreType.DMA((14,)), pltpu.SemaphoreType.DMA((2,))],
        compiler_params=_params(),
    )(dw_in, dw_out)


def _to_classes(a, dil):
    s_len, n_cols = a.shape
    if dil == 1:
        return a.reshape(1, s_len, n_cols)
    return a.reshape(s_len // dil, dil, n_cols).transpose(1, 0, 2)


def _from_classes(a):
    n_cls, sub_len, n_cols = a.shape
    if n_cls == 1:
        return a.reshape(sub_len, n_cols)
    return a.transpose(1, 0, 2).reshape(n_cls * sub_len, n_cols)


def _qkv_views(proj):
    views = [(_to_classes(proj, 1), (0, N_HEADS_ATTN, 2 * N_HEADS_ATTN))]
    qkv = proj[:, :QKV_A]
    for dil in DILATIONS[1:]:
        views.append((_to_classes(qkv, dil), (0, N_HEADS_ATTN, 2 * N_HEADS_ATTN)))
    return views


def layer_fwd(x, g, scale, shift, gate, w_in, w_out, lg):
    proj, h = inproj_fwd(x, g, 1.0 + scale, shift, w_in)
    views = _qkv_views(proj)
    attn_outs = []
    for dil, (arr, cols) in zip(DILATIONS, views):
        o_p, lse_p = attn_fwd(arr, dil, *cols)
        attn_outs.append((_from_classes(o_p), _from_classes(lse_p)))
    o_pre, y_r = ret_fwd(proj, lg)
    x_new, o_a, lse = outproj_fwd(x, gate, w_out, attn_outs, y_r, proj)
    saved = dict(x=x, proj=proj, h=h, o_a=o_a, lse=lse, o_pre=o_pre, y_r=y_r)
    return x_new, saved


def layer_bwd(dxn, saved, g, scale, gate, w_in, w_out, lg):
    proj = saved["proj"]
    do_a, delta, dyr, dz_a, dz_r, y, dxb = outproj_bwd(dxn, gate, w_out, saved["o_a"], saved["y_r"], proj)
    dw_out, dgate = wout_grad(y, dxb, gate, w_out)
    dq_r, dk_r, dv_r, glf, glb = ret_bwd(proj, lg, saved["o_pre"], dyr)
    dqkv_a = []
    for dil, (arr, cols) in zip(DILATIONS, _qkv_views(proj)):
        grads = attn_bwd(arr, _to_classes(do_a, dil), _to_classes(saved["lse"], dil), _to_classes(delta, dil), dil, *cols)
        dqkv_a.append([_from_classes(t) for t in grads])
    dproj = assemble_dproj(dqkv_a, dz_a, dq_r, dk_r, dv_r, dz_r)
    dx, stats = inproj_bwd(dproj, w_in, saved["x"], g, 1.0 + scale, dxn)
    dw_in = win_grad(saved["h"], dproj)
    dlg = jnp.concatenate([glf[:, 0, 0], glb[:, 0, 0]])
    return dx, dw_in, dw_out, stats[0:1], stats[1:2], dgate, stats[2:3], dlg


ROWS_B_ADA = DEPTH * 3 * D_MODEL // 128
ROWS_GAIN = DEPTH * D_MODEL // 128
ROWS_FINAL = D_MODEL // 128
ROWS_MISC = 8
ROWS_SMALL = ROWS_B_ADA + ROWS_GAIN + ROWS_FINAL + ROWS_MISC


def _pack_small(b_ada_like, gain_like, final_like, dec_f, dec_b, loss=None):
    misc = jnp.zeros((ROWS_MISC, 128), F32)
    misc = misc.at[0, :2 * DEPTH * RET_HEADS].set(jnp.concatenate([dec_f.reshape(-1), dec_b.reshape(-1)]))
    if loss is not None:
        misc = misc.at[1, 0].set(loss)
    return jnp.concatenate([b_ada_like.reshape(ROWS_B_ADA, 128), gain_like.reshape(ROWS_GAIN, 128),
                            final_like.reshape(ROWS_FINAL, 128), misc], axis=0)


def _unpack_small(p):
    r0, r1, r2 = ROWS_B_ADA, ROWS_B_ADA + ROWS_GAIN, ROWS_B_ADA + ROWS_GAIN + ROWS_FINAL
    n = DEPTH * RET_HEADS
    return (p[:r0].reshape(DEPTH, 3 * D_MODEL), p[r0:r1].reshape(DEPTH, D_MODEL), p[r1:r2].reshape(D_MODEL),
            p[r2, :n].reshape(DEPTH, RET_HEADS), p[r2, n:2 * n].reshape(DEPTH, RET_HEADS))


def kernel(x, c, norm_gain, w_ada, b_ada, w_in, w_out, ret_decay_logit_f, ret_decay_logit_b, final_gain, loss_target, m_norm_gain, m_w_ada, m_b_ada, m_w_in, m_w_out, m_ret_decay_logit_f, m_ret_decay_logit_b, m_final_gain, v_norm_gain, v_w_ada, v_b_ada, v_w_in, v_w_out, v_ret_decay_logit_f, v_ret_decay_logit_b, v_final_gain):
    px, py, pc = _position()
    me = _flat(px, py, pc)
    x2, target = x[0], loss_target[0]

    w_in_full, w_out_full = allgather_weights(w_in.astype(BF16), w_out.astype(BF16))

    c_all = allgather_rows(c.reshape(D_MODEL // 128, 128), "allgather_c").reshape(N_DEV, D_MODEL)
    act, mod_part = ada_fwd(c_all, w_ada)
    mod_all = allgather_rows(mod_part.reshape(-1, 128), "allgather_mod").reshape(N_DEV, DEPTH, N_DEV, W_ADA_SHARD)
    mod = lax.dynamic_index_in_dim(mod_all, me, axis=2, keepdims=False)
    mod = mod.transpose(1, 0, 2).reshape(DEPTH, 3 * D_MODEL) + b_ada
    shift, scale, gate = mod[:, :D_MODEL], mod[:, D_MODEL:2 * D_MODEL], mod[:, 2 * D_MODEL:]

    lg = jnp.concatenate([jax.nn.log_sigmoid(ret_decay_logit_f), jax.nn.log_sigmoid(ret_decay_logit_b)], axis=1)

    h = x2
    saved = []
    for l in range(DEPTH):
        h, sv = layer_fwd(h, norm_gain[l:l + 1], scale[l:l + 1], shift[l:l + 1], gate[l:l + 1],
                          w_in_full[l], w_out_full[l], lg[l])
        saved.append(sv)
    dh, loss_part, dfinal = loss_head(h, final_gain.reshape(1, D_MODEL), target)

    dmod, dgain, dlg, grads_in, grads_out = [None] * DEPTH, [None] * DEPTH, [None] * DEPTH, [None] * DEPTH, [None] * DEPTH
    for l in reversed(range(DEPTH)):
        dh, dw_in, dw_out, dshift, dscale, dgate, dg, dlg[l] = layer_bwd(
            dh, saved[l], norm_gain[l:l + 1], scale[l:l + 1], gate[l:l + 1], w_in_full[l], w_out_full[l], lg[l])
        dmod[l] = jnp.concatenate([dshift, dscale, dgate], axis=1)
        dgain[l] = dg
        grads_in[l], grads_out[l] = exchange_grads(dw_in, dw_out, f"exchange_grads_{l}")

    dlg = jnp.stack(dlg)
    dlogit_f = dlg[:, :RET_HEADS] * jax.nn.sigmoid(-ret_decay_logit_f)
    dlogit_b = dlg[:, RET_HEADS:] * jax.nn.sigmoid(-ret_decay_logit_b)
    packed = _pack_small(jnp.concatenate(dmod, axis=0), jnp.concatenate(dgain, axis=0), dfinal, dlogit_f, dlogit_b,
                         loss=loss_part[0, 0])
    gathered = allgather_rows(packed, "allgather_small")
    small = small_update(gathered,
                         _pack_small(b_ada, norm_gain, final_gain, ret_decay_logit_f, ret_decay_logit_b),
                         _pack_small(m_b_ada, m_norm_gain, m_final_gain, m_ret_decay_logit_f, m_ret_decay_logit_b),
                         _pack_small(v_b_ada, v_norm_gain, v_final_gain, v_ret_decay_logit_f, v_ret_decay_logit_b))
    loss = small[0][ROWS_B_ADA + ROWS_GAIN + ROWS_FINAL + 1, 0]
    (g_b_ada, g_gain, g_final, g_dec_f, g_dec_b), (d_b_ada, d_gain, d_final, d_dec_f, d_dec_b), \
        (m_b_ada2, m_gain2, m_final2, m_dec_f2, m_dec_b2), (v_b_ada2, v_gain2, v_final2, v_dec_f2, v_dec_b2) = \
        [_unpack_small(p) for p in small]

    dmod_all = gathered[:, :ROWS_B_ADA].reshape(N_DEV, DEPTH, 3 * D_MODEL)
    dmod_mine = lax.dynamic_slice_in_dim(dmod_all, me * W_ADA_SHARD, W_ADA_SHARD, axis=2).transpose(1, 0, 2)
    g_w_ada, d_w_ada, m_w_ada2, v_w_ada2 = ada_update(act.T, dmod_mine, w_ada, m_w_ada, v_w_ada)

    upd_in = [shard_update(grads_in[l], w_in[l], m_w_in[l], v_w_in[l], f"w_in_update_{l}") for l in range(DEPTH)]
    upd_out = [shard_update(grads_out[l], w_out[l], m_w_out[l], v_w_out[l], f"w_out_update_{l}") for l in range(DEPTH)]
    g_w_in, d_w_in, m_w_in2, v_w_in2 = [jnp.stack([u[t] for u in upd_in]) for t in range(4)]
    g_w_out, d_w_out, m_w_out2, v_w_out2 = [jnp.stack([u[t] for u in upd_out]) for t in range(4)]

    return (loss, dh[None],
            g_gain, g_w_ada, g_b_ada, g_w_in, g_w_out, g_dec_f, g_dec_b, g_final,
            d_gain, d_w_ada, d_b_ada, d_w_in, d_w_out, d_dec_f, d_dec_b, d_final,
            m_gain2, m_w_ada2, m_b_ada2, m_w_in2, m_w_out2, m_dec_f2, m_dec_b2, m_final2,
            v_gain2, v_w_ada2, v_b_ada2, v_w_in2, v_w_out2, v_dec_f2, v_dec_b2, v_final2)
```

```python
import functools
import math

import jax
import jax.numpy as jnp
from jax import lax
from jax.experimental import pallas as pl
from jax.experimental.pallas import tpu as pltpu

F32, BF16 = jnp.float32, jnp.bfloat16

D_MODEL = 2048
DEPTH = 4
N_DEV = 8
ATTN_WIDTH = 1024
HEAD_DIM = 128
N_HEADS_ATTN = 8
DILATIONS = (1, 4, 16)
RADIUS = 64
RET_HEADS = 4
RET_QK = 128
RET_V = 256
RET_CHUNK = 128
IN_W = 7168
QKV_A = 3 * ATTN_WIDTH
COL_ZA, COL_QR, COL_KR, COL_VR, COL_ZR = 3072, 4096, 4608, 5120, 6144
W_IN_SHARD = IN_W // N_DEV
W_OUT_SHARD = D_MODEL // N_DEV
W_ADA_SHARD = 3 * D_MODEL // N_DEV
NORM_EPS = 1e-6
MASK_VALUE = -1e30
ATTN_SCALE = HEAD_DIM ** -0.5
RET_SCALE = RET_QK ** -0.5
LN2 = math.log(2.0)

ADAM_LR, ADAM_B1, ADAM_B2, ADAM_EPS, ADAM_WD, ADAM_STEP = 0.001, 0.9, 0.999, 1e-08, 0.01, 10
ADAM_C1 = 1.0 / (1.0 - ADAM_B1 ** ADAM_STEP)
ADAM_C2 = 1.0 / (1.0 - ADAM_B2 ** ADAM_STEP)

VMEM_LIMIT_BYTES = 56 * 1024 * 1024
MESH = pl.DeviceIdType.MESH


def _params(*sem):
    return pltpu.CompilerParams(dimension_semantics=sem if sem else None, vmem_limit_bytes=VMEM_LIMIT_BYTES)


def _dot(a, b):
    return jnp.dot(a, b, preferred_element_type=F32)


def _dot_nt(a, b):
    return lax.dot_general(a, b, (((1,), (1,)), ((), ())), preferred_element_type=F32)


def _dot_tn(a, b):
    return lax.dot_general(a, b, (((0,), (0,)), ((), ())), preferred_element_type=F32)


def _iota(shape, dim):
    return lax.broadcasted_iota(jnp.int32, shape, dim)


def _sigmoid(z):
    return 1.0 / (1.0 + jnp.exp(-z))


HBM = pl.BlockSpec(memory_space=pl.ANY)


def _position():
    return lax.axis_index("x"), lax.axis_index("y"), lax.axis_index("c")


def _flat(px, py, pc):
    return 4 * px + 2 * py + pc


def _remote(src, dst, send_sem, recv_sem, to):
    return pltpu.make_async_remote_copy(src_ref=src, dst_ref=dst, send_sem=send_sem, recv_sem=recv_sem,
                                        device_id=to, device_id_type=MESH)


def _weight_blocks(fin_ref, fout_ref):
    def block(a, idx):
        if a == 0:
            return fin_ref.at[:, pl.ds(pl.multiple_of(idx * W_IN_SHARD, 128), W_IN_SHARD)]
        return fout_ref.at[pl.ds(pl.multiple_of(idx * W_OUT_SHARD, W_OUT_SHARD), W_OUT_SHARD), :]
    return block


GATHER1_SEMS = [pltpu.SemaphoreType.DMA((8,)), pltpu.SemaphoreType.DMA((8,)), pltpu.SemaphoreType.DMA((2,))]
GATHER2_SEMS = [pltpu.SemaphoreType.DMA((6,)), pltpu.SemaphoreType.DMA((6,))]
REDUCE1_SEMS = [pltpu.SemaphoreType.DMA((2,)), pltpu.SemaphoreType.DMA((2,))]
REDUCE2_SEMS = [pltpu.SemaphoreType.DMA((6,)), pltpu.SemaphoreType.DMA((6,))]


def _gather_phase1(srcs, block, send_sems, recv_sems, local_sems):
    x, y, c = _position()
    mine = [block(a, _flat(x, y, c)) for a in range(2)]
    copies = [pltpu.make_async_copy(srcs[a], mine[a], local_sems.at[a]) for a in range(2)]
    for a in range(2):
        copies.append(_remote(srcs[a], mine[a], send_sems.at[4 * a], recv_sems.at[4 * a], (x, y, 1 - c)))
        for j, (px, py) in enumerate([(1 - x, y), (x, 1 - y), (1 - x, 1 - y)]):
            copies.append(_remote(srcs[a], mine[a], send_sems.at[4 * a + 1 + j], recv_sems.at[4 * a + 1 + j], (px, py, c)))
    return copies


def _gather_phase2(block, send_sems, recv_sems):
    x, y, c = _position()
    copies = []
    for a in range(2):
        for j, (px, py) in enumerate([(1 - x, y), (x, 1 - y), (1 - x, 1 - y)]):
            blk = block(a, _flat(px, py, c))
            copies.append(_remote(blk, blk, send_sems.at[3 * a + j], recv_sems.at[3 * a + j], (x, y, 1 - c)))
    return copies


def _reduce_phase1(grads, landings, send_sems, recv_sems):
    x, y, c = _position()
    return [_remote(g.at[:, 1 - c], r, send_sems.at[a], recv_sems.at[a], (x, y, 1 - c))
            for a, (g, r) in enumerate(zip(grads, landings))]


def _reduce_phase2(sums, landings, send_sems, recv_sems):
    x, y, c = _position()
    copies = []
    for a, (p, r) in enumerate(zip(sums, landings)):
        for k in (1, 2, 3):
            to = (1 - x if k & 2 else x, 1 - y if k & 1 else y, c)
            copies.append(_remote(p.at[k], r.at[k - 1], send_sems.at[3 * a + k - 1], recv_sems.at[3 * a + k - 1], to))
    return copies


def _landing_shapes(n):
    return [jax.ShapeDtypeStruct((n, D_MODEL, W_IN_SHARD), BF16), jax.ShapeDtypeStruct((n, W_OUT_SHARD, D_MODEL), BF16)]


def _split_cores(slabs):
    return tuple(s.reshape((N_DEV // 2, 2) + s.shape[1:]) for s in slabs)


def _start_all(copies):
    for cp in copies:
        cp.start()


def _wait_all(copies):
    for cp in copies:
        cp.wait()


def _grid_edge(n_axes):
    first = last = None
    for ax in range(n_axes):
        f = pl.program_id(ax) == 0
        e = pl.program_id(ax) == pl.num_programs(ax) - 1
        first = f if first is None else first & f
        last = e if last is None else last & e
    return first, last


def inproj_fwd(x, g, scale1p, shift, w, gather=None):
    s_len = x.shape[0]
    tm, tn = min(512, s_len), 1024

    def body(x_ref, g_ref, sc_ref, sh_ref, w_ref, *rest):
        if gather is not None:
            win_ref, wout_ref, proj_ref, h_ref, fin_ref, fout_ref, send_sems, recv_sems, local_sems = rest
            first, last = _grid_edge(2)
            copies = _gather_phase1([win_ref, wout_ref], _weight_blocks(fin_ref, fout_ref), send_sems, recv_sems, local_sems)
            pl.when(first)(lambda: _start_all(copies))
        else:
            proj_ref, h_ref = rest

        @pl.when(pl.program_id(1) == 0)
        def _():
            xv = x_ref[...]
            r = lax.rsqrt(jnp.mean(xv * xv, axis=-1, keepdims=True) + NORM_EPS)
            h_ref[...] = ((xv * r * g_ref[...]) * sc_ref[...] + sh_ref[...]).astype(BF16)
        proj_ref[...] = _dot(h_ref[...], w_ref[...]).astype(BF16)
        if gather is not None:
            pl.when(last)(lambda: _wait_all(copies))

    vec = pl.BlockSpec((1, D_MODEL), lambda i, j: (0, 0))
    in_specs = [pl.BlockSpec((tm, D_MODEL), lambda i, j: (i, 0)), vec, vec, vec,
                pl.BlockSpec((D_MODEL, tn), lambda i, j: (0, j))]
    out_specs = [pl.BlockSpec((tm, tn), lambda i, j: (i, j)), pl.BlockSpec((tm, D_MODEL), lambda i, j: (i, 0))]
    out_shape = [jax.ShapeDtypeStruct((s_len, IN_W), BF16), jax.ShapeDtypeStruct((s_len, D_MODEL), BF16)]
    extra = ()
    if gather is not None:
        in_specs += [HBM, HBM]
        out_specs += [HBM, HBM]
        out_shape += [jax.ShapeDtypeStruct((D_MODEL, IN_W), BF16), jax.ShapeDtypeStruct((D_MODEL, D_MODEL), BF16)]
        extra = tuple(gather)
    return pl.pallas_call(
        body, name="inproj_fwd_gather" if gather is not None else "inproj_fwd", grid=(s_len // tm, IN_W // tn),
        in_specs=in_specs, out_specs=out_specs, out_shape=out_shape,
        scratch_shapes=GATHER1_SEMS if gather is not None else [],
        compiler_params=_params("arbitrary", "arbitrary"),
    )(x, g, scale1p, shift, w, *extra)


def _attn_geometry(sub_len):
    tq = min(sub_len, 128)
    win = min(sub_len, tq + 2 * RADIUS)
    return tq, win, sub_len // tq


def _attn_scores(q, k, q0, start, tq, win, slope, dil):
    s = _dot_nt(q, k) * ATTN_SCALE
    dist = jnp.abs((start + _iota((tq, win), 1)) - (q0 + _iota((tq, win), 0)))
    return jnp.where(dist <= RADIUS, s - slope * (dist * dil).astype(F32), MASK_VALUE)


def _head_slope():
    h = pl.program_id(1)
    return jnp.exp(-(h + 1).astype(F32) * LN2 * jnp.ones((1, 1), F32))


def attn_fwd(qkv, dil, q_col, k_col, v_col):
    n_cls, sub_len, _ = qkv.shape
    tq, win, n_tiles = _attn_geometry(sub_len)

    def body(q_ref, k_ref, v_ref, o_ref, lse_ref):
        slope = _head_slope()

        def tile(i, carry):
            q0 = pl.multiple_of(i * tq, tq)
            start = pl.multiple_of(jnp.clip(i * tq - RADIUS, 0, sub_len - win), 16)
            s = _attn_scores(q_ref[pl.ds(q0, tq), :], k_ref[pl.ds(start, win), :], q0, start, tq, win, slope, dil)
            m = jnp.max(s, axis=1, keepdims=True)
            p = jnp.exp(s - m)
            den = jnp.sum(p, axis=1, keepdims=True)
            o_ref[pl.ds(q0, tq), :] = _dot(p.astype(BF16), v_ref[pl.ds(start, win), :]) / den
            lse_ref[pl.ds(q0, tq), :] = jnp.broadcast_to(m + jnp.log(den), (tq, HEAD_DIM))
            return carry

        lax.fori_loop(0, n_tiles, tile, 0)

    def col(c0):
        return pl.BlockSpec((None, sub_len, HEAD_DIM), lambda r, h: (r, 0, c0 + h))

    out = jax.ShapeDtypeStruct((n_cls, sub_len, ATTN_WIDTH), F32)
    return pl.pallas_call(
        body, name=f"attn_fwd_d{dil}", grid=(n_cls, N_HEADS_ATTN),
        in_specs=[col(q_col), col(k_col), col(v_col)], out_specs=[col(0), col(0)], out_shape=[out, out],
        compiler_params=_params("arbitrary", "arbitrary"),
    )(qkv, qkv, qkv)


def attn_bwd(qkv, do, lse, delta, dil, q_col, k_col, v_col):
    n_cls, sub_len, _ = qkv.shape
    tq, win, n_tiles = _attn_geometry(sub_len)

    def body(q_ref, k_ref, v_ref, do_ref, lse_ref, dl_ref, dq_ref, dk_ref, dv_ref, dk_acc, dv_acc):
        slope = _head_slope()
        dk_acc[...] = jnp.zeros_like(dk_acc)
        dv_acc[...] = jnp.zeros_like(dv_acc)

        def tile(i, carry):
            q0 = pl.multiple_of(i * tq, tq)
            start = pl.multiple_of(jnp.clip(i * tq - RADIUS, 0, sub_len - win), 16)
            q = q_ref[pl.ds(q0, tq), :]
            k = k_ref[pl.ds(start, win), :]
            v = v_ref[pl.ds(start, win), :]
            dov = do_ref[pl.ds(q0, tq), :]
            s = _attn_scores(q, k, q0, start, tq, win, slope, dil)
            p = jnp.exp(s - lse_ref[pl.ds(q0, tq), 0:1])
            ds = (p * (_dot_nt(dov, v) - dl_ref[pl.ds(q0, tq), 0:1])).astype(BF16)
            dq_ref[pl.ds(q0, tq), :] = (_dot(ds, k) * ATTN_SCALE).astype(BF16)
            dk_acc[pl.ds(start, win), :] += _dot_tn(ds, q) * ATTN_SCALE
            dv_acc[pl.ds(start, win), :] += _dot_tn(p.astype(BF16), dov)
            return carry

        lax.fori_loop(0, n_tiles, tile, 0)
        dk_ref[...] = dk_acc[...].astype(BF16)
        dv_ref[...] = dv_acc[...].astype(BF16)

    def col(c0):
        return pl.BlockSpec((None, sub_len, HEAD_DIM), lambda r, h: (r, 0, c0 + h))

    out = jax.ShapeDtypeStruct((n_cls, sub_len, ATTN_WIDTH), BF16)
    return pl.pallas_call(
        body, name=f"attn_bwd_d{dil}", grid=(n_cls, N_HEADS_ATTN),
        in_specs=[col(q_col), col(k_col), col(v_col), col(0), col(0), col(0)],
        out_specs=[col(0), col(0), col(0)], out_shape=[out, out, out],
        scratch_shapes=[pltpu.VMEM((sub_len, HEAD_DIM), F32), pltpu.VMEM((sub_len, HEAD_DIM), F32)],
        compiler_params=_params("arbitrary", "arbitrary"),
    )(qkv, qkv, qkv, do, lse, delta)


def _ret_tables(lg_ref):
    h = pl.program_id(0)
    one = jnp.ones((1, 1), F32)
    lgf, lgb = lg_ref[h] * one, lg_ref[RET_HEADS + h] * one
    c = RET_CHUNK
    rel = (_iota((c, c), 0) - _iota((c, c), 1)).astype(F32)
    dec_f = jnp.where(rel >= 0, jnp.exp(jnp.maximum(rel, 0.0) * lgf), 0.0)
    dec_b = jnp.where(rel <= 0, jnp.exp(jnp.maximum(-rel, 0.0) * lgb), 0.0)
    ci = _iota((c, 1), 0).astype(F32)
    tab = dict(rel=rel, dec_f=dec_f, dec_b=dec_b, ci=ci,
               xi_f=jnp.exp((ci + 1.0) * lgf), ze_f=jnp.exp((c - 1.0 - ci) * lgf), g_f=jnp.exp(c * lgf),
               xi_b=jnp.exp((c - ci) * lgb), ze_b=jnp.exp(ci * lgb), g_b=jnp.exp(c * lgb))
    return tab


def _ret_specs(s_len):
    q = pl.BlockSpec((s_len, RET_QK), lambda h: (0, COL_QR // RET_QK + h))
    k = pl.BlockSpec((s_len, RET_QK), lambda h: (0, COL_KR // RET_QK + h))
    v = pl.BlockSpec((s_len, RET_V), lambda h: (0, COL_VR // RET_V + h))
    wide = pl.BlockSpec((s_len, RET_V), lambda h: (0, h))
    narrow = pl.BlockSpec((s_len, RET_QK), lambda h: (0, h))
    smem = pl.BlockSpec(memory_space=pltpu.SMEM)
    return smem, q, k, v, wide, narrow


def ret_fwd(proj, lg):
    s_len = proj.shape[0]
    c, n_chunks = RET_CHUNK, proj.shape[0] // RET_CHUNK

    def body(lg_ref, q_ref, k_ref, v_ref, opre_ref, y_ref, st_f, st_b):
        t = _ret_tables(lg_ref)
        dec = t["dec_f"] + t["dec_b"]
        st_f[...] = jnp.zeros_like(st_f)
        st_b[...] = jnp.zeros_like(st_b)

        def load(n):
            r0 = pl.multiple_of(n * c, c)
            q, k, v = q_ref[pl.ds(r0, c), :], k_ref[pl.ds(r0, c), :], v_ref[pl.ds(r0, c), :]
            return r0, q, (k.astype(F32) * RET_SCALE), v

        def fwd(n, carry):
            r0, q, kf, v = load(n)
            inner = (_dot_nt(q, kf.astype(BF16)) * dec).astype(BF16)
            opre_ref[pl.ds(r0, c), :] = _dot(inner, v) + _dot(q, st_f[...].astype(BF16)) * t["xi_f"]
            st_f[...] = st_f[...] * t["g_f"] + _dot_tn((kf * t["ze_f"]).astype(BF16), v)
            return carry

        def bwd(i, carry):
            r0, q, kf, v = load(n_chunks - 1 - i)
            o = opre_ref[pl.ds(r0, c), :] + _dot(q, st_b[...].astype(BF16)) * t["xi_b"]
            st_b[...] = st_b[...] * t["g_b"] + _dot_tn((kf * t["ze_b"]).astype(BF16), v)
            opre_ref[pl.ds(r0, c), :] = o
            y_ref[pl.ds(r0, c), :] = o * lax.rsqrt(jnp.mean(o * o, axis=-1, keepdims=True) + NORM_EPS)
            return carry

        lax.fori_loop(0, n_chunks, fwd, 0)
        lax.fori_loop(0, n_chunks, bwd, 0)

    smem, q, k, v, wide, _ = _ret_specs(s_len)
    out = jax.ShapeDtypeStruct((s_len, RET_HEADS * RET_V), F32)
    return pl.pallas_call(
        body, name="ret_fwd", grid=(RET_HEADS,), in_specs=[smem, q, k, v], out_specs=[wide, wide],
        out_shape=[out, out], scratch_shapes=[pltpu.VMEM((RET_QK, RET_V), F32), pltpu.VMEM((RET_QK, RET_V), F32)],
        compiler_params=_params("arbitrary"),
    )(lg, proj, proj, proj)


def ret_bwd(proj, lg, o_pre, dy):
    s_len = proj.shape[0]
    c, n_chunks = RET_CHUNK, proj.shape[0] // RET_CHUNK
    cf = float(c)

    def body(lg_ref, q_ref, k_ref, v_ref, o_ref, dy_ref, dq_ref, dk_ref, dv_ref, glf_ref, glb_ref,
             st_f, dst_b, st_b, dst_f, keep_sf, keep_dtb):
        t = _ret_tables(lg_ref)
        dec = t["dec_f"] + t["dec_b"]
        e_f, e_b, ci = t["rel"] * t["dec_f"], -t["rel"] * t["dec_b"], t["ci"]
        for ref in (st_f, dst_b, st_b, dst_f):
            ref[...] = jnp.zeros_like(ref)

        def load(n):
            r0 = pl.multiple_of(n * c, c)
            q, k, v = q_ref[pl.ds(r0, c), :], k_ref[pl.ds(r0, c), :], v_ref[pl.ds(r0, c), :]
            o, dyv = o_ref[pl.ds(r0, c), :], dy_ref[pl.ds(r0, c), :]
            rr = lax.rsqrt(jnp.mean(o * o, axis=-1, keepdims=True) + NORM_EPS)
            y = o * rr
            do = (rr * (dyv - y * jnp.mean(dyv * y, axis=-1, keepdims=True))).astype(BF16)
            return r0, q, k.astype(F32) * RET_SCALE, v, do

        def total(x):
            return jnp.sum(jnp.sum(x, axis=1, keepdims=True), axis=0, keepdims=True)

        def fwd(n, carry):
            glf, glb = carry
            r0, q, kf, v, do = load(n)
            qf, kb = q.astype(F32), kf.astype(BF16)
            a = _dot_nt(q, kb)
            b = _dot_nt(do, v)
            da = (b * dec).astype(BF16)
            ab = a * b
            glf = glf + total(e_f * ab)
            glb = glb + total(e_b * ab)
            sf_b, dtb_b = st_f[...].astype(BF16), dst_b[...].astype(BF16)
            dq_inter = _dot_nt(do, sf_b) * t["xi_f"]
            dk_inter = _dot_nt(v, dtb_b) * t["ze_b"]
            glf = glf + total((ci + 1.0) * (qf * dq_inter))
            glb = glb + total(ci * (kf * dk_inter))
            dq_ref[pl.ds(r0, c), :] = _dot(da, kb) + dq_inter
            dk_ref[pl.ds(r0, c), :] = _dot_tn(da, q) + dk_inter
            dv_ref[pl.ds(r0, c), :] = _dot_tn((a * dec).astype(BF16), do) + _dot((kf * t["ze_b"]).astype(BF16), dtb_b)
            keep_sf[n] = sf_b
            keep_dtb[n] = dtb_b
            st_f[...] = st_f[...] * t["g_f"] + _dot_tn((kf * t["ze_f"]).astype(BF16), v)
            dst_b[...] = dst_b[...] * t["g_b"] + _dot_tn((qf * t["xi_b"]).astype(BF16), do)
            return glf, glb

        def bwd(i, carry):
            glf, glb = carry
            n = n_chunks - 1 - i
            r0, q, kf, v, do = load(n)
            qf = q.astype(F32)
            tb_b, dsf_b = st_b[...].astype(BF16), dst_f[...].astype(BF16)
            dq_inter = _dot_nt(do, tb_b) * t["xi_b"]
            dk_inter = _dot_nt(v, dsf_b) * t["ze_f"]
            glb = glb + total((cf - ci) * (qf * dq_inter)) + cf * t["g_b"] * total(keep_dtb[n].astype(F32) * st_b[...])
            glf = glf + total((cf - 1.0 - ci) * (kf * dk_inter)) + cf * t["g_f"] * total(dst_f[...] * keep_sf[n].astype(F32))
            dq_ref[pl.ds(r0, c), :] += dq_inter
            dk_ref[pl.ds(r0, c), :] = (dk_ref[pl.ds(r0, c), :] + dk_inter) * RET_SCALE
            dv_ref[pl.ds(r0, c), :] += _dot((kf * t["ze_f"]).astype(BF16), dsf_b)
            st_b[...] = st_b[...] * t["g_b"] + _dot_tn((kf * t["ze_b"]).astype(BF16), v)
            dst_f[...] = dst_f[...] * t["g_f"] + _dot_tn((qf * t["xi_f"]).astype(BF16), do)
            return glf, glb

        zero = jnp.zeros((1, 1), F32)
        carry = lax.fori_loop(0, n_chunks, fwd, (zero, zero))
        glf, glb = lax.fori_loop(0, n_chunks, bwd, carry)
        glf_ref[...] = jnp.broadcast_to(glf, (8, 128))
        glb_ref[...] = jnp.broadcast_to(glb, (8, 128))

    smem, q, k, v, wide, narrow = _ret_specs(s_len)
    scal = pl.BlockSpec((None, 8, 128), lambda h: (h, 0, 0))
    state = pltpu.VMEM((RET_QK, RET_V), F32)
    keep = pltpu.VMEM((n_chunks, RET_QK, RET_V), BF16)
    return pl.pallas_call(
        body, name="ret_bwd", grid=(RET_HEADS,), in_specs=[smem, q, k, v, wide, wide],
        out_specs=[narrow, narrow, wide, scal, scal],
        out_shape=[jax.ShapeDtypeStruct((s_len, RET_HEADS * RET_QK), F32), jax.ShapeDtypeStruct((s_len, RET_HEADS * RET_QK), F32),
                   jax.ShapeDtypeStruct((s_len, RET_HEADS * RET_V), F32),
                   jax.ShapeDtypeStruct((RET_HEADS, 8, 128), F32), jax.ShapeDtypeStruct((RET_HEADS, 8, 128), F32)],
        scratch_shapes=[state, state, state, state, keep, keep],
        compiler_params=_params("arbitrary"),
    )(lg, proj, proj, proj, o_pre, dy)


def _silu_parts(z):
    sig = _sigmoid(z)
    return z * sig, sig * (1.0 + z * (1.0 - sig))


def outproj_fwd(x, gate, w_out, attn_outs, y_r, proj):
    s_len = x.shape[0]
    tm = min(256, s_len)

    def body(x_ref, gate_ref, w_ref, o1, l1, o2, l2, o3, l3, yr_ref, za_ref, zr_ref, xn_ref, oa_ref, lse_ref):
        la, lb, lc = l1[...], l2[...], l3[...]
        m = jnp.maximum(jnp.maximum(la, lb), lc)
        lse = m + jnp.log(jnp.exp(la - m) + jnp.exp(lb - m) + jnp.exp(lc - m))
        o_a = jnp.exp(la - lse) * o1[...] + jnp.exp(lb - lse) * o2[...] + jnp.exp(lc - lse) * o3[...]
        oa_ref[...] = o_a
        lse_ref[...] = lse
        silu_a, _ = _silu_parts(za_ref[...].astype(F32))
        silu_r, _ = _silu_parts(zr_ref[...].astype(F32))
        y = jnp.concatenate([(o_a * silu_a).astype(BF16), (yr_ref[...] * silu_r).astype(BF16)], axis=1)
        xn_ref[...] = x_ref[...] + gate_ref[...] * _dot(y, w_ref[...])

    row = lambda w: pl.BlockSpec((tm, w), lambda i: (i, 0))
    half = row(ATTN_WIDTH)
    flat = [a for pair in attn_outs for a in pair]
    return pl.pallas_call(
        body, name="outproj_fwd", grid=(s_len // tm,),
        in_specs=[row(D_MODEL), pl.BlockSpec((1, D_MODEL), lambda i: (0, 0)),
                  pl.BlockSpec((D_MODEL, D_MODEL), lambda i: (0, 0))] + [half] * 7
                 + [pl.BlockSpec((tm, ATTN_WIDTH), lambda i: (i, COL_ZA // ATTN_WIDTH)),
                    pl.BlockSpec((tm, ATTN_WIDTH), lambda i: (i, COL_ZR // ATTN_WIDTH))],
        out_specs=[row(D_MODEL), half, half],
        out_shape=[jax.ShapeDtypeStruct((s_len, D_MODEL), F32), jax.ShapeDtypeStruct((s_len, ATTN_WIDTH), F32),
                   jax.ShapeDtypeStruct((s_len, ATTN_WIDTH), F32)],
        compiler_params=_params("arbitrary"),
    )(x, gate, w_out, *flat, y_r, proj, proj)


def loss_head(x, gain, target):
    s_len = x.shape[0]
    tm = min(256, s_len)

    def body(x_ref, g_ref, t_ref, dx_ref, loss_ref, dg_ref):
        @pl.when(pl.program_id(0) == 0)
        def _():
            loss_ref[...] = jnp.zeros_like(loss_ref)
            dg_ref[...] = jnp.zeros_like(dg_ref)
        xv, g = x_ref[...], g_ref[...]
        r = lax.rsqrt(jnp.mean(xv * xv, axis=-1, keepdims=True) + NORM_EPS)
        xn = xv * r
        err = xn * g - t_ref[...]
        part = 0.5 * jnp.sum(jnp.mean(err * err, axis=-1, keepdims=True), axis=0, keepdims=True)
        loss_ref[...] += jnp.broadcast_to(part, loss_ref.shape)
        dy = err * (1.0 / D_MODEL)
        dg_ref[...] += jnp.sum(dy * xn, axis=0, keepdims=True)
        dxn = dy * g
        dx_ref[...] = r * (dxn - xn * jnp.mean(dxn * xn, axis=-1, keepdims=True))

    row = pl.BlockSpec((tm, D_MODEL), lambda i: (i, 0))
    vec = pl.BlockSpec((1, D_MODEL), lambda i: (0, 0))
    return pl.pallas_call(
        body, name="loss_head", grid=(s_len // tm,), in_specs=[row, vec, row],
        out_specs=[row, pl.BlockSpec((8, 128), lambda i: (0, 0)), vec],
        out_shape=[jax.ShapeDtypeStruct((s_len, D_MODEL), F32), jax.ShapeDtypeStruct((8, 128), F32),
                   jax.ShapeDtypeStruct((1, D_MODEL), F32)],
        compiler_params=_params("arbitrary"),
    )(x, gain, target)


def outproj_bwd(dxn, gate, w_out, o_a, y_r, proj, exchange=None):
    s_len = dxn.shape[0]
    tm = min(256, s_len)

    def body(dx_ref, gate_ref, w_ref, oa_ref, yr_ref, za_ref, zr_ref, *rest):
        if exchange is not None:
            (ga_ref, gb_ref, doa_ref, dl_ref, dyr_ref, dza_ref, dzr_ref, y_ref, dxb_ref, ra_ref, rb_ref,
             send_sems, recv_sems) = rest
            first, last = _grid_edge(1)
            copies = _reduce_phase1([ga_ref, gb_ref], [ra_ref, rb_ref], send_sems, recv_sems)
            pl.when(first)(lambda: _start_all(copies))
        else:
            doa_ref, dl_ref, dyr_ref, dza_ref, dzr_ref, y_ref, dxb_ref = rest
        dxv = dx_ref[...]
        dxb_ref[...] = dxv.astype(BF16)
        dy = _dot_nt((dxv * gate_ref[...]).astype(BF16), w_ref[...])
        dy_a, dy_r = dy[:, :ATTN_WIDTH], dy[:, ATTN_WIDTH:]
        o_a, y_rv = oa_ref[...], yr_ref[...]
        silu_a, dsilu_a = _silu_parts(za_ref[...].astype(F32))
        silu_r, dsilu_r = _silu_parts(zr_ref[...].astype(F32))
        do_a = dy_a * silu_a
        doa_ref[...] = do_a.astype(BF16)
        prod = do_a * o_a
        dl_ref[...] = jnp.concatenate(
            [jnp.broadcast_to(jnp.sum(prod[:, h * HEAD_DIM:(h + 1) * HEAD_DIM], axis=1, keepdims=True), (tm, HEAD_DIM))
             for h in range(N_HEADS_ATTN)], axis=1)
        dyr_ref[...] = dy_r * silu_r
        dza_ref[...] = (dy_a * o_a * dsilu_a).astype(BF16)
        dzr_ref[...] = (dy_r * y_rv * dsilu_r).astype(BF16)
        y_ref[...] = jnp.concatenate([(o_a * silu_a).astype(BF16), (y_rv * silu_r).astype(BF16)], axis=1)
        if exchange is not None:
            pl.when(last)(lambda: _wait_all(copies))

    row = lambda w: pl.BlockSpec((tm, w), lambda i: (i, 0))
    half = row(ATTN_WIDTH)
    sds = lambda w, dt: jax.ShapeDtypeStruct((s_len, w), dt)
    in_specs = [row(D_MODEL), pl.BlockSpec((1, D_MODEL), lambda i: (0, 0)),
                pl.BlockSpec((D_MODEL, D_MODEL), lambda i: (0, 0)), half, half,
                pl.BlockSpec((tm, ATTN_WIDTH), lambda i: (i, COL_ZA // ATTN_WIDTH)),
                pl.BlockSpec((tm, ATTN_WIDTH), lambda i: (i, COL_ZR // ATTN_WIDTH))]
    out_specs = [half, half, half, half, half, row(D_MODEL), row(D_MODEL)]
    out_shape = [sds(ATTN_WIDTH, BF16), sds(ATTN_WIDTH, F32), sds(ATTN_WIDTH, F32), sds(ATTN_WIDTH, BF16),
                 sds(ATTN_WIDTH, BF16), sds(D_MODEL, BF16), sds(D_MODEL, BF16)]
    extra = ()
    if exchange is not None:
        in_specs += [HBM, HBM]
        out_specs += [HBM, HBM]
        out_shape += _landing_shapes(4)
        extra = _split_cores(exchange)
    return pl.pallas_call(
        body, name="outproj_bwd_reduce1" if exchange is not None else "outproj_bwd", grid=(s_len // tm,),
        in_specs=in_specs, out_specs=out_specs, out_shape=out_shape,
        scratch_shapes=REDUCE1_SEMS if exchange is not None else [],
        compiler_params=_params("arbitrary"),
    )(dxn, gate, w_out, o_a, y_r, proj, proj, *extra)


def wout_grad(y, dxb, gate, w_out):
    s_len = y.shape[0]
    tf, ts = 512, min(512, s_len)

    def body(y_ref, dx_ref, gate_ref, w_ref, dw_ref, dgate_ref, acc):
        f, s = pl.program_id(0), pl.program_id(1)

        @pl.when((f == 0) & (s == 0))
        def _():
            dgate_ref[...] = jnp.zeros_like(dgate_ref)

        @pl.when(s == 0)
        def _():
            acc[...] = jnp.zeros_like(acc)
        acc[...] += _dot_tn(y_ref[...], dx_ref[...])

        @pl.when(s == pl.num_programs(1) - 1)
        def _():
            m = acc[...]
            dw_ref[...] = (m * gate_ref[...]).astype(BF16).reshape(dw_ref.shape)
            dgate_ref[...] += jnp.sum(m * w_ref[...].astype(F32), axis=0, keepdims=True)

    per = tf // W_OUT_SHARD
    return pl.pallas_call(
        body, name="wout_grad", grid=(D_MODEL // tf, s_len // ts),
        in_specs=[pl.BlockSpec((ts, tf), lambda f, s: (s, f)), pl.BlockSpec((ts, D_MODEL), lambda f, s: (s, 0)),
                  pl.BlockSpec((1, D_MODEL), lambda f, s: (0, 0)), pl.BlockSpec((tf, D_MODEL), lambda f, s: (f, 0))],
        out_specs=[pl.BlockSpec((per, W_OUT_SHARD, D_MODEL), lambda f, s: (f, 0, 0)),
                   pl.BlockSpec((1, D_MODEL), lambda f, s: (0, 0))],
        out_shape=[jax.ShapeDtypeStruct((N_DEV, W_OUT_SHARD, D_MODEL), BF16), jax.ShapeDtypeStruct((1, D_MODEL), F32)],
        scratch_shapes=[pltpu.VMEM((tf, D_MODEL), F32)],
        compiler_params=_params("arbitrary", "arbitrary"),
    )(y, dxb, gate, w_out)


def assemble_dproj(dqkv_a, dz_a, dq_r, dk_r, dv_r, dz_r):
    s_len = dz_a.shape[0]
    tm = min(256, s_len)

    def body(*refs):
        pat, (dza, dqr, dkr, dvr, dzr, out) = refs[:9], refs[9:]
        for t in range(3):
            tot = pat[t][...].astype(F32) + pat[3 + t][...].astype(F32) + pat[6 + t][...].astype(F32)
            out[:, t * ATTN_WIDTH:(t + 1) * ATTN_WIDTH] = tot.astype(BF16)
        out[:, COL_ZA:COL_QR] = dza[...]
        out[:, COL_QR:COL_KR] = dqr[...].astype(BF16)
        out[:, COL_KR:COL_VR] = dkr[...].astype(BF16)
        out[:, COL_VR:COL_ZR] = dvr[...].astype(BF16)
        out[:, COL_ZR:IN_W] = dzr[...]

    row = lambda w: pl.BlockSpec((tm, w), lambda i: (i, 0))
    flat = [dqkv_a[p][t] for p in range(3) for t in range(3)]
    return pl.pallas_call(
        body, name="assemble_dproj", grid=(s_len // tm,),
        in_specs=[row(ATTN_WIDTH)] * 10 + [row(512), row(512), row(ATTN_WIDTH), row(ATTN_WIDTH)],
        out_specs=row(IN_W), out_shape=jax.ShapeDtypeStruct((s_len, IN_W), BF16),
        compiler_params=_params("arbitrary"),
    )(*flat, dz_a, dq_r, dk_r, dv_r, dz_r)


def inproj_bwd(dproj, w, x, g, scale1p, dxn, exchange=None):
    s_len = x.shape[0]
    tm, tk = min(512, s_len), 1024

    def body(dp_ref, w_ref, x_ref, g_ref, sc_ref, dxn_ref, *rest):
        if exchange is not None:
            pa_ref, pb_ref, dx_ref, st_ref, ra_ref, rb_ref, acc, send_sems, recv_sems = rest
            first, last = _grid_edge(2)
            copies = _reduce_phase2([pa_ref, pb_ref], [ra_ref, rb_ref], send_sems, recv_sems)
            pl.when(first)(lambda: _start_all(copies))
        else:
            dx_ref, st_ref, acc = rest
        i, k = pl.program_id(0), pl.program_id(1)

        @pl.when((i == 0) & (k == 0))
        def _():
            st_ref[...] = jnp.zeros_like(st_ref)

        @pl.when(k == 0)
        def _():
            acc[...] = jnp.zeros_like(acc)
        acc[...] += _dot_nt(dp_ref[...], w_ref[...])

        @pl.when(k == pl.num_programs(1) - 1)
        def _():
            dh, xv, g, sc = acc[...], x_ref[...], g_ref[...], sc_ref[...]
            r = lax.rsqrt(jnp.mean(xv * xv, axis=-1, keepdims=True) + NORM_EPS)
            xn = xv * r
            da = dh * sc
            st_ref[0:1, :] += jnp.sum(dh, axis=0, keepdims=True)
            st_ref[1:2, :] += jnp.sum(dh * (xn * g), axis=0, keepdims=True)
            st_ref[2:3, :] += jnp.sum(da * xn, axis=0, keepdims=True)
            dn = da * g
            dx_ref[...] = r * (dn - xn * jnp.mean(dn * xn, axis=-1, keepdims=True)) + dxn_ref[...]

        if exchange is not None:
            pl.when(last)(lambda: _wait_all(copies))

    row = pl.BlockSpec((tm, D_MODEL), lambda i, k: (i, 0))
    vec = pl.BlockSpec((1, D_MODEL), lambda i, k: (0, 0))
    in_specs = [pl.BlockSpec((tm, tk), lambda i, k: (i, k)), pl.BlockSpec((D_MODEL, tk), lambda i, k: (0, k)),
                row, vec, vec, row]
    out_specs = [row, pl.BlockSpec((8, D_MODEL), lambda i, k: (0, 0))]
    out_shape = [jax.ShapeDtypeStruct((s_len, D_MODEL), F32), jax.ShapeDtypeStruct((8, D_MODEL), F32)]
    scratch = [pltpu.VMEM((tm, D_MODEL), F32)]
    extra = ()
    if exchange is not None:
        in_specs += [HBM, HBM]
        out_specs += [HBM, HBM]
        out_shape += _landing_shapes(3)
        scratch += REDUCE2_SEMS
        extra = tuple(exchange)
    return pl.pallas_call(
        body, name="inproj_bwd_reduce2" if exchange is not None else "inproj_bwd", grid=(s_len // tm, IN_W // tk),
        in_specs=in_specs, out_specs=out_specs, out_shape=out_shape, scratch_shapes=scratch,
        compiler_params=_params("arbitrary", "arbitrary"),
    )(dproj, w, x, g, scale1p, dxn, *extra)


def win_grad(h, dproj):
    s_len = h.shape[0]
    ts = min(512, s_len)

    def body(h_ref, dp_ref, dw_ref, acc):
        s = pl.program_id(1)

        @pl.when(s == 0)
        def _():
            acc[...] = jnp.zeros_like(acc)
        acc[...] += _dot_tn(h_ref[...], dp_ref[...])

        @pl.when(s == pl.num_programs(1) - 1)
        def _():
            dw_ref[...] = acc[...].astype(BF16)

    return pl.pallas_call(
        body, name="win_grad", grid=(N_DEV, s_len // ts),
        in_specs=[pl.BlockSpec((ts, D_MODEL), lambda j, s: (s, 0)), pl.BlockSpec((ts, W_IN_SHARD), lambda j, s: (s, j))],
        out_specs=pl.BlockSpec((None, D_MODEL, W_IN_SHARD), lambda j, s: (j, 0, 0)),
        out_shape=jax.ShapeDtypeStruct((N_DEV, D_MODEL, W_IN_SHARD), BF16),
        scratch_shapes=[pltpu.VMEM((D_MODEL, W_IN_SHARD), F32)],
        compiler_params=_params("arbitrary", "arbitrary"),
    )(h, dproj)


def ada_fwd(c_all, w_ada):
    def body(c_ref, w_ref, act_ref, part_ref):
        cv = c_ref[...]
        act = cv * _sigmoid(cv)
        act_ref[...] = act
        part_ref[...] = _dot(act.astype(BF16), w_ref[...].astype(BF16))

    return pl.pallas_call(
        body, name="ada_fwd", grid=(DEPTH,),
        in_specs=[pl.BlockSpec((N_DEV, D_MODEL), lambda l: (0, 0)),
                  pl.BlockSpec((None, D_MODEL, W_ADA_SHARD), lambda l: (l, 0, 0))],
        out_specs=[pl.BlockSpec((N_DEV, D_MODEL), lambda l: (0, 0)),
                   pl.BlockSpec((None, N_DEV, W_ADA_SHARD), lambda l: (l, 0, 0))],
        out_shape=[jax.ShapeDtypeStruct((N_DEV, D_MODEL), F32), jax.ShapeDtypeStruct((DEPTH, N_DEV, W_ADA_SHARD), F32)],
        compiler_params=_params("arbitrary"),
    )(c_all, w_ada)


def _adamw(w, g, m, v):
    m = ADAM_B1 * m + (1.0 - ADAM_B1) * g
    v = ADAM_B2 * v + (1.0 - ADAM_B2) * (g * g)
    delta = -ADAM_LR * ((m * ADAM_C1) / (jnp.sqrt(v * ADAM_C2) + ADAM_EPS) + ADAM_WD * w)
    return delta, m, v


def ada_update(act_t, dmod, w, m, v):
    tr = 512

    def body(a_ref, d_ref, w_ref, m_ref, v_ref, g_out, dl_out, m_out, v_out):
        a = a_ref[...].astype(BF16).astype(F32)
        d = d_ref[...].astype(BF16).astype(F32)
        g = a[:, 0:1] * d[0:1, :]
        for b in range(1, N_DEV):
            g = g + a[:, b:b + 1] * d[b:b + 1, :]
        g_out[...] = g
        dl_out[...], m_out[...], v_out[...] = _adamw(w_ref[...], g, m_ref[...], v_ref[...])

    blk = pl.BlockSpec((None, tr, W_ADA_SHARD), lambda l, r: (l, r, 0))
    out = jax.ShapeDtypeStruct(w.shape, F32)
    return pl.pallas_call(
        body, name="ada_update", grid=(DEPTH, D_MODEL // tr),
        in_specs=[pl.BlockSpec((tr, N_DEV), lambda l, r: (r, 0)),
                  pl.BlockSpec((None, N_DEV, W_ADA_SHARD), lambda l, r: (l, 0, 0)), blk, blk, blk],
        out_specs=[blk] * 4, out_shape=[out] * 4, compiler_params=_params("arbitrary", "arbitrary"),
    )(act_t, dmod, w, m, v)


def chip_sum(pos, grads, landed, name):
    _, _, n_rows, n_cols = grads.shape
    tr = min(512, n_rows)

    def chip(k, pos_ref):
        return (pos_ref[0] ^ (k // 2)) * 2 + (pos_ref[1] ^ (k % 2))

    def body(pos_ref, g_ref, r_ref, out_ref):
        out_ref[...] = (g_ref[...].astype(F32) + r_ref[...].astype(F32)).astype(BF16)

    return pl.pallas_call(
        body, name=name,
        grid_spec=pltpu.PrefetchScalarGridSpec(
            num_scalar_prefetch=1, grid=(N_DEV // 2, n_rows // tr),
            in_specs=[pl.BlockSpec((None, None, tr, n_cols), lambda k, r, p: (chip(k, p), p[2], r, 0)),
                      pl.BlockSpec((None, tr, n_cols), lambda k, r, p: (chip(k, p), r, 0))],
            out_specs=pl.BlockSpec((None, tr, n_cols), lambda k, r, p: (k, r, 0))),
        out_shape=jax.ShapeDtypeStruct((N_DEV // 2, n_rows, n_cols), BF16),
        compiler_params=_params("arbitrary", "arbitrary"),
    )(pos, grads, landed)


def shard_update(own, others, w, m, v, name):
    n_rows, n_cols = w.shape
    tr = min(256, n_rows)

    def body(own_ref, oth_ref, w_ref, m_ref, v_ref, g_out, dl_out, m_out, v_out):
        g = own_ref[...].astype(F32)
        for k in range(3):
            g = g + oth_ref[k].astype(F32)
        g_out[...] = g
        dl_out[...], m_out[...], v_out[...] = _adamw(w_ref[...], g, m_ref[...], v_ref[...])

    blk = pl.BlockSpec((tr, n_cols), lambda r: (r, 0))
    out = jax.ShapeDtypeStruct(w.shape, F32)
    return pl.pallas_call(
        body, name=name, grid=(n_rows // tr,),
        in_specs=[pl.BlockSpec((None, tr, n_cols), lambda r: (0, r, 0)), pl.BlockSpec((3, tr, n_cols), lambda r: (0, r, 0)),
                  blk, blk, blk],
        out_specs=[blk] * 4, out_shape=[out] * 4, compiler_params=_params("arbitrary"),
    )(own, others, w, m, v)


def small_update(parts, w, m, v):
    def body(p_ref, w_ref, m_ref, v_ref, g_out, dl_out, m_out, v_out):
        g = p_ref[0]
        for k in range(1, N_DEV):
            g = g + p_ref[k]
        g_out[...] = g
        dl_out[...], m_out[...], v_out[...] = _adamw(w_ref[...], g, m_ref[...], v_ref[...])

    out = jax.ShapeDtypeStruct(w.shape, F32)
    return pl.pallas_call(body, name="small_update", out_shape=[out] * 4, compiler_params=_params())(parts, w, m, v)


def _two_level_allgather(srcs, dst_block, send_sems, recv_sems, local_sems):
    x, y, c = _position()
    me, sibling = (x, y, c), (x, y, 1 - c)
    chips = [(1 - x, y), (x, 1 - y), (1 - x, 1 - y)]
    n = len(srcs)

    def copy(a, k, block, to, src=None):
        dst = dst_block(a, _flat(*block))
        return pltpu.make_async_remote_copy(
            src_ref=dst if src is None else src, dst_ref=dst, send_sem=send_sems.at[a * 7 + k],
            recv_sem=recv_sems.at[a * 7 + k], device_id=to, device_id_type=MESH)

    mine = [pltpu.make_async_copy(srcs[a], dst_block(a, _flat(*me)), local_sems.at[a]) for a in range(n)]
    for cp in mine:
        cp.start()
    first = []
    for a in range(n):
        first.append(copy(a, 0, me, sibling, src=srcs[a]))
        first += [copy(a, 1 + j, me, (*chip, c), src=srcs[a]) for j, chip in enumerate(chips)]
    for cp in first:
        cp.start()
    passed = []
    for j, chip in enumerate(chips):
        for a in range(n):
            copy(a, 1 + j, (*chip, c), me).wait_recv()
            fwd = copy(a, 4 + j, (*chip, c), sibling)
            fwd.start()
            passed.append(fwd)
    for a in range(n):
        copy(a, 0, sibling, me).wait_recv()
        for j, chip in enumerate(chips):
            copy(a, 4 + j, (*chip, 1 - c), me).wait_recv()
    for cp in first + passed:
        cp.wait_send()
    for cp in mine:
        cp.wait()


def allgather_rows(x, name):
    def body(x_ref, out_ref, send_sems, recv_sems, local_sems):
        _two_level_allgather([x_ref], lambda a, idx: out_ref.at[idx], send_sems, recv_sems, local_sems)

    vmem = pl.BlockSpec(memory_space=pltpu.VMEM)
    return pl.pallas_call(
        body, name=name, in_specs=[vmem], out_specs=vmem,
        out_shape=jax.ShapeDtypeStruct((N_DEV,) + x.shape, x.dtype),
        scratch_shapes=[pltpu.SemaphoreType.DMA((7,)), pltpu.SemaphoreType.DMA((7,)), pltpu.SemaphoreType.DMA((1,))],
        compiler_params=_params(),
    )(x)


def _full_weight_shapes():
    return [jax.ShapeDtypeStruct((D_MODEL, IN_W), BF16), jax.ShapeDtypeStruct((D_MODEL, D_MODEL), BF16)]


def allgather_weights(w_in_b, w_out_b):
    def body(win_ref, wout_ref, fin_ref, fout_ref, s1, r1, l1, s2, r2):
        block = _weight_blocks(fin_ref, fout_ref)
        first = _gather_phase1([win_ref, wout_ref], block, s1, r1, l1)
        _start_all(first)
        _wait_all(first)
        second = _gather_phase2(block, s2, r2)
        _start_all(second)
        _wait_all(second)

    return pl.pallas_call(
        body, name="allgather_weights", in_specs=[HBM, HBM], out_specs=[HBM, HBM], out_shape=_full_weight_shapes(),
        scratch_shapes=GATHER1_SEMS + GATHER2_SEMS, compiler_params=_params(),
    )(w_in_b, w_out_b)


def gather_finish(full_in, full_out, name):
    def body(fin_in, fout_in, fin_ref, fout_ref, send_sems, recv_sems):
        copies = _gather_phase2(_weight_blocks(fin_ref, fout_ref), send_sems, recv_sems)
        _start_all(copies)
        _wait_all(copies)

    return pl.pallas_call(
        body, name=name, in_specs=[HBM, HBM], out_specs=[HBM, HBM], out_shape=_full_weight_shapes(),
        input_output_aliases={0: 0, 1: 1}, scratch_shapes=GATHER2_SEMS, compiler_params=_params(),
    )(full_in, full_out)


def reduce_first(dw_in, dw_out, name):
    def body(ga_ref, gb_ref, ra_ref, rb_ref, send_sems, recv_sems):
        copies = _reduce_phase1([ga_ref, gb_ref], [ra_ref, rb_ref], send_sems, recv_sems)
        _start_all(copies)
        _wait_all(copies)

    return pl.pallas_call(
        body, name=name, in_specs=[HBM, HBM], out_specs=[HBM, HBM], out_shape=_landing_shapes(4),
        scratch_shapes=REDUCE1_SEMS, compiler_params=_params(),
    )(*_split_cores((dw_in, dw_out)))


def reduce_second(sum_in, sum_out, name):
    def body(pa_ref, pb_ref, ra_ref, rb_ref, send_sems, recv_sems):
        copies = _reduce_phase2([pa_ref, pb_ref], [ra_ref, rb_ref], send_sems, recv_sems)
        _start_all(copies)
        _wait_all(copies)

    return pl.pallas_call(
        body, name=name, in_specs=[HBM, HBM], out_specs=[HBM, HBM], out_shape=_landing_shapes(3),
        scratch_shapes=REDUCE2_SEMS, compiler_params=_params(),
    )(sum_in, sum_out)


def _to_classes(a, dil):
    s_len, n_cols = a.shape
    if dil == 1:
        return a.reshape(1, s_len, n_cols)
    return a.reshape(s_len // dil, dil, n_cols).transpose(1, 0, 2)


def _from_classes(a):
    n_cls, sub_len, n_cols = a.shape
    if n_cls == 1:
        return a.reshape(sub_len, n_cols)
    return a.transpose(1, 0, 2).reshape(n_cls * sub_len, n_cols)


def _qkv_views(proj):
    views = [(_to_classes(proj, 1), (0, N_HEADS_ATTN, 2 * N_HEADS_ATTN))]
    qkv = proj[:, :QKV_A]
    for dil in DILATIONS[1:]:
        views.append((_to_classes(qkv, dil), (0, N_HEADS_ATTN, 2 * N_HEADS_ATTN)))
    return views


def layer_fwd(x, g, scale, shift, gate, w_in, w_out, lg, gather=None):
    proj, h, *began = inproj_fwd(x, g, 1.0 + scale, shift, w_in, gather)
    next_weights = gather_finish(*began, "gather_finish") if gather is not None else None
    views = _qkv_views(proj)
    attn_outs = []
    for dil, (arr, cols) in zip(DILATIONS, views):
        o_p, lse_p = attn_fwd(arr, dil, *cols)
        attn_outs.append((_from_classes(o_p), _from_classes(lse_p)))
    o_pre, y_r = ret_fwd(proj, lg)
    x_new, o_a, lse = outproj_fwd(x, gate, w_out, attn_outs, y_r, proj)
    saved = dict(x=x, proj=proj, h=h, o_a=o_a, lse=lse, o_pre=o_pre, y_r=y_r)
    return x_new, saved, next_weights


def layer_bwd(dxn, saved, g, scale, gate, w_in, w_out, lg, pos, later=None):
    proj = saved["proj"]
    do_a, delta, dyr, dz_a, dz_r, y, dxb, *landed = outproj_bwd(dxn, gate, w_out, saved["o_a"], saved["y_r"], proj, later)
    sums = None
    if later is not None:
        sums = [chip_sum(pos, g4, r, "chip_sum") for g4, r in zip(_split_cores(later), landed)]
    dw_out, dgate = wout_grad(y, dxb, gate, w_out)
    dq_r, dk_r, dv_r, glf, glb = ret_bwd(proj, lg, saved["o_pre"], dyr)
    dqkv_a = []
    for dil, (arr, cols) in zip(DILATIONS, _qkv_views(proj)):
        grads = attn_bwd(arr, _to_classes(do_a, dil), _to_classes(saved["lse"], dil), _to_classes(delta, dil), dil, *cols)
        dqkv_a.append([_from_classes(t) for t in grads])
    dproj = assemble_dproj(dqkv_a, dz_a, dq_r, dk_r, dv_r, dz_r)
    dx, stats, *others = inproj_bwd(dproj, w_in, saved["x"], g, 1.0 + scale, dxn, sums)
    dw_in = win_grad(saved["h"], dproj)
    dlg = jnp.concatenate([glf[:, 0, 0], glb[:, 0, 0]])
    reduced = (sums, others) if later is not None else None
    return dx, (dw_in, dw_out), stats[0:1], stats[1:2], dgate, stats[2:3], dlg, reduced


ROWS_B_ADA = DEPTH * 3 * D_MODEL // 128
ROWS_GAIN = DEPTH * D_MODEL // 128
ROWS_FINAL = D_MODEL // 128
ROWS_MISC = 8
ROWS_SMALL = ROWS_B_ADA + ROWS_GAIN + ROWS_FINAL + ROWS_MISC


def _pack_small(b_ada_like, gain_like, final_like, dec_f, dec_b, loss=None):
    misc = jnp.zeros((ROWS_MISC, 128), F32)
    misc = misc.at[0, :2 * DEPTH * RET_HEADS].set(jnp.concatenate([dec_f.reshape(-1), dec_b.reshape(-1)]))
    if loss is not None:
        misc = misc.at[1, 0].set(loss)
    return jnp.concatenate([b_ada_like.reshape(ROWS_B_ADA, 128), gain_like.reshape(ROWS_GAIN, 128),
                            final_like.reshape(ROWS_FINAL, 128), misc], axis=0)


def _unpack_small(p):
    r0, r1, r2 = ROWS_B_ADA, ROWS_B_ADA + ROWS_GAIN, ROWS_B_ADA + ROWS_GAIN + ROWS_FINAL
    n = DEPTH * RET_HEADS
    return (p[:r0].reshape(DEPTH, 3 * D_MODEL), p[r0:r1].reshape(DEPTH, D_MODEL), p[r1:r2].reshape(D_MODEL),
            p[r2, :n].reshape(DEPTH, RET_HEADS), p[r2, n:2 * n].reshape(DEPTH, RET_HEADS))


def kernel(x, c, norm_gain, w_ada, b_ada, w_in, w_out, ret_decay_logit_f, ret_decay_logit_b, final_gain, loss_target, m_norm_gain, m_w_ada, m_b_ada, m_w_in, m_w_out, m_ret_decay_logit_f, m_ret_decay_logit_b, m_final_gain, v_norm_gain, v_w_ada, v_b_ada, v_w_in, v_w_out, v_ret_decay_logit_f, v_ret_decay_logit_b, v_final_gain):
    px, py, pc = _position()
    me = _flat(px, py, pc)
    pos = jnp.stack([px, py, pc]).astype(jnp.int32)
    x2, target = x[0], loss_target[0]

    w_in_b, w_out_b = w_in.astype(BF16), w_out.astype(BF16)
    weights = allgather_weights(w_in_b[0], w_out_b[0])

    c_all = allgather_rows(c.reshape(D_MODEL // 128, 128), "allgather_c").reshape(N_DEV, D_MODEL)
    act, mod_part = ada_fwd(c_all, w_ada)
    mod_all = allgather_rows(mod_part.reshape(-1, 128), "allgather_mod").reshape(N_DEV, DEPTH, N_DEV, W_ADA_SHARD)
    mod = lax.dynamic_index_in_dim(mod_all, me, axis=2, keepdims=False)
    mod = mod.transpose(1, 0, 2).reshape(DEPTH, 3 * D_MODEL) + b_ada
    shift, scale, gate = mod[:, :D_MODEL], mod[:, D_MODEL:2 * D_MODEL], mod[:, 2 * D_MODEL:]

    lg = jnp.concatenate([jax.nn.log_sigmoid(ret_decay_logit_f), jax.nn.log_sigmoid(ret_decay_logit_b)], axis=1)

    h = x2
    saved, layer_weights = [], []
    for l in range(DEPTH):
        layer_weights.append(weights)
        gather = (w_in_b[l + 1], w_out_b[l + 1]) if l + 1 < DEPTH else None
        h, sv, weights = layer_fwd(h, norm_gain[l:l + 1], scale[l:l + 1], shift[l:l + 1], gate[l:l + 1],
                                   *layer_weights[l], lg[l], gather)
        saved.append(sv)
    dh, loss_part, dfinal = loss_head(h, final_gain.reshape(1, D_MODEL), target)

    dmod, dgain, dlg, reduced = [None] * DEPTH, [None] * DEPTH, [None] * DEPTH, [None] * DEPTH
    slabs = None
    for l in reversed(range(DEPTH)):
        dh, slabs, dshift, dscale, dgate, dg, dlg[l], done = layer_bwd(
            dh, saved[l], norm_gain[l:l + 1], scale[l:l + 1], gate[l:l + 1], *layer_weights[l], lg[l], pos, slabs)
        dmod[l] = jnp.concatenate([dshift, dscale, dgate], axis=1)
        dgain[l] = dg
        if done is not None:
            reduced[l + 1] = done
    landed = reduce_first(*slabs, "reduce_first")
    sums = [chip_sum(pos, g4, r, "chip_sum_last") for g4, r in zip(_split_cores(slabs), landed)]
    reduced[0] = (sums, reduce_second(*sums, "reduce_second"))

    dlg = jnp.stack(dlg)
    dlogit_f = dlg[:, :RET_HEADS] * jax.nn.sigmoid(-ret_decay_logit_f)
    dlogit_b = dlg[:, RET_HEADS:] * jax.nn.sigmoid(-ret_decay_logit_b)
    packed = _pack_small(jnp.concatenate(dmod, axis=0), jnp.concatenate(dgain, axis=0), dfinal, dlogit_f, dlogit_b,
                         loss=loss_part[0, 0])
    gathered = allgather_rows(packed, "allgather_small")
    small = small_update(gathered,
                         _pack_small(b_ada, norm_gain, final_gain, ret_decay_logit_f, ret_decay_logit_b),
                         _pack_small(m_b_ada, m_norm_gain, m_final_gain, m_ret_decay_logit_f, m_ret_decay_logit_b),
                         _pack_small(v_b_ada, v_norm_gain, v_final_gain, v_ret_decay_logit_f, v_ret_decay_logit_b))
    loss = small[0][ROWS_B_ADA + ROWS_GAIN + ROWS_FINAL + 1, 0]
    (g_b_ada, g_gain, g_final, g_dec_f, g_dec_b), (d_b_ada, d_gain, d_final, d_dec_f, d_dec_b), \
        (m_b_ada2, m_gain2, m_final2, m_dec_f2, m_dec_b2), (v_b_ada2, v_gain2, v_final2, v_dec_f2, v_dec_b2) = \
        [_unpack_small(p) for p in small]

    dmod_all = gathered[:, :ROWS_B_ADA].reshape(N_DEV, DEPTH, 3 * D_MODEL)
    dmod_mine = lax.dynamic_slice_in_dim(dmod_all, me * W_ADA_SHARD, W_ADA_SHARD, axis=2).transpose(1, 0, 2)
    g_w_ada, d_w_ada, m_w_ada2, v_w_ada2 = ada_update(act.T, dmod_mine, w_ada, m_w_ada, v_w_ada)

    upd_in = [shard_update(reduced[l][0][0], reduced[l][1][0], w_in[l], m_w_in[l], v_w_in[l], f"w_in_update_{l}")
              for l in range(DEPTH)]
    upd_out = [shard_update(reduced[l][0][1], reduced[l][1][1], w_out[l], m_w_out[l], v_w_out[l], f"w_out_update_{l}")
               for l in range(DEPTH)]
    g_w_in, d_w_in, m_w_in2, v_w_in2 = [jnp.stack([u[t] for u in upd_in]) for t in range(4)]
    g_w_out, d_w_out, m_w_out2, v_w_out2 = [jnp.stack([u[t] for u in upd_out]) for t in range(4)]

    return (loss, dh[None],
            g_gain, g_w_ada, g_b_ada, g_w_in, g_w_out, g_dec_f, g_dec_b, g_final,
            d_gain, d_w_ada, d_b_ada, d_w_in, d_w_out, d_dec_f, d_dec_b, d_final,
            m_gain2, m_w_ada2, m_b_ada2, m_w_in2, m_w_out2, m_dec_f2, m_dec_b2, m_final2,
            v_gain2, v_w_ada2, v_b_ada2, v_w_in2, v_w_out2, v_dec_f2, v_dec_b2, v_final2)
```

```python
import functools
import math

import jax
import jax.numpy as jnp
from jax import lax
from jax.experimental import pallas as pl
from jax.experimental.pallas import tpu as pltpu

F32, BF16 = jnp.float32, jnp.bfloat16

D_MODEL = 2048
DEPTH = 4
N_DEV = 8
ATTN_WIDTH = 1024
HEAD_DIM = 128
N_HEADS_ATTN = 8
DILATIONS = (1, 4, 16)
RADIUS = 64
RET_HEADS = 4
RET_QK = 128
RET_V = 256
RET_CHUNK = 128
IN_W = 7168
QKV_A = 3 * ATTN_WIDTH
COL_ZA, COL_QR, COL_KR, COL_VR, COL_ZR = 3072, 4096, 4608, 5120, 6144
W_IN_SHARD = IN_W // N_DEV
W_OUT_SHARD = D_MODEL // N_DEV
W_ADA_SHARD = 3 * D_MODEL // N_DEV
NORM_EPS = 1e-6
MASK_VALUE = -1e30
ATTN_SCALE = HEAD_DIM ** -0.5
RET_SCALE = RET_QK ** -0.5
LN2 = math.log(2.0)

ADAM_LR, ADAM_B1, ADAM_B2, ADAM_EPS, ADAM_WD, ADAM_STEP = 0.001, 0.9, 0.999, 1e-08, 0.01, 10
ADAM_C1 = 1.0 / (1.0 - ADAM_B1 ** ADAM_STEP)
ADAM_C2 = 1.0 / (1.0 - ADAM_B2 ** ADAM_STEP)

VMEM_LIMIT_BYTES = 56 * 1024 * 1024
MESH = pl.DeviceIdType.MESH


def _params(*sem):
    return pltpu.CompilerParams(dimension_semantics=sem if sem else None, vmem_limit_bytes=VMEM_LIMIT_BYTES)


def _dot(a, b):
    return jnp.dot(a, b, preferred_element_type=F32)


def _dot_nt(a, b):
    return lax.dot_general(a, b, (((1,), (1,)), ((), ())), preferred_element_type=F32)


def _dot_tn(a, b):
    return lax.dot_general(a, b, (((0,), (0,)), ((), ())), preferred_element_type=F32)


def _iota(shape, dim):
    return lax.broadcasted_iota(jnp.int32, shape, dim)


def _sigmoid(z):
    return 1.0 / (1.0 + jnp.exp(-z))


HBM = pl.BlockSpec(memory_space=pl.ANY)


def _position():
    return lax.axis_index("x"), lax.axis_index("y"), lax.axis_index("c")


def _flat(px, py, pc):
    return 4 * px + 2 * py + pc


def _remote(src, dst, send_sem, recv_sem, to):
    return pltpu.make_async_remote_copy(src_ref=src, dst_ref=dst, send_sem=send_sem, recv_sem=recv_sem,
                                        device_id=to, device_id_type=MESH)


def _weight_blocks(fin_ref, fout_ref):
    def block(a, idx):
        if a == 0:
            return fin_ref.at[:, pl.ds(pl.multiple_of(idx * W_IN_SHARD, 128), W_IN_SHARD)]
        return fout_ref.at[pl.ds(pl.multiple_of(idx * W_OUT_SHARD, W_OUT_SHARD), W_OUT_SHARD), :]
    return block


GATHER1_SEMS = [pltpu.SemaphoreType.DMA((8,)), pltpu.SemaphoreType.DMA((8,)), pltpu.SemaphoreType.DMA((2,))]
GATHER2_SEMS = [pltpu.SemaphoreType.DMA((6,)), pltpu.SemaphoreType.DMA((6,))]
REDUCE1_SEMS = [pltpu.SemaphoreType.DMA((2,)), pltpu.SemaphoreType.DMA((2,))]
REDUCE2_SEMS = [pltpu.SemaphoreType.DMA((6,)), pltpu.SemaphoreType.DMA((6,))]


def _gather_phase1(srcs, block, send_sems, recv_sems, local_sems):
    x, y, c = _position()
    mine = [block(a, _flat(x, y, c)) for a in range(2)]
    copies = [pltpu.make_async_copy(srcs[a], mine[a], local_sems.at[a]) for a in range(2)]
    for a in range(2):
        copies.append(_remote(srcs[a], mine[a], send_sems.at[4 * a], recv_sems.at[4 * a], (x, y, 1 - c)))
        for j, (px, py) in enumerate([(1 - x, y), (x, 1 - y), (1 - x, 1 - y)]):
            copies.append(_remote(srcs[a], mine[a], send_sems.at[4 * a + 1 + j], recv_sems.at[4 * a + 1 + j], (px, py, c)))
    return copies


def _gather_phase2(block, send_sems, recv_sems):
    x, y, c = _position()
    copies = []
    for a in range(2):
        for j, (px, py) in enumerate([(1 - x, y), (x, 1 - y), (1 - x, 1 - y)]):
            blk = block(a, _flat(px, py, c))
            copies.append(_remote(blk, blk, send_sems.at[3 * a + j], recv_sems.at[3 * a + j], (x, y, 1 - c)))
    return copies


def _reduce_phase1(grads, landings, send_sems, recv_sems):
    x, y, c = _position()
    return [_remote(g.at[:, 1 - c], r, send_sems.at[a], recv_sems.at[a], (x, y, 1 - c))
            for a, (g, r) in enumerate(zip(grads, landings))]


def _reduce_phase2(sums, landings, send_sems, recv_sems):
    x, y, c = _position()
    copies = []
    for a, (p, r) in enumerate(zip(sums, landings)):
        for k in (1, 2, 3):
            to = (1 - x if k & 2 else x, 1 - y if k & 1 else y, c)
            copies.append(_remote(p.at[k], r.at[k - 1], send_sems.at[3 * a + k - 1], recv_sems.at[3 * a + k - 1], to))
    return copies


def _landing_shapes(n):
    return [jax.ShapeDtypeStruct((n, D_MODEL, W_IN_SHARD), BF16), jax.ShapeDtypeStruct((n, W_OUT_SHARD, D_MODEL), BF16)]


def _split_cores(slabs):
    return tuple(s.reshape((N_DEV // 2, 2) + s.shape[1:]) for s in slabs)


def _start_all(copies):
    for cp in copies:
        cp.start()


def _wait_all(copies):
    for cp in copies:
        cp.wait()


def _grid_edge(n_axes):
    first = last = None
    for ax in range(n_axes):
        f = pl.program_id(ax) == 0
        e = pl.program_id(ax) == pl.num_programs(ax) - 1
        first = f if first is None else first & f
        last = e if last is None else last & e
    return first, last


LANES = 128


def _stage_shape(rows, width):
    return pltpu.VMEM((width // LANES, rows, LANES), F32)


def _fill_stage(stage_ref, value):
    for t in range(stage_ref.shape[0]):
        stage_ref[t] = value[:, t * LANES:(t + 1) * LANES]


def _read_stage(stage_ref):
    return jnp.concatenate([stage_ref[t] for t in range(stage_ref.shape[0])], axis=1)


def _split_classes(stage_ref, out_refs):
    n_t, rows, _ = stage_ref.shape
    for out_ref in out_refs:
        dil = out_ref.shape[0]
        for r in range(dil):
            for t in range(n_t):
                piece = stage_ref[t, pl.ds(r, rows // dil, stride=dil), :]
                out_ref[r, :, t * LANES:(t + 1) * LANES] = piece.astype(out_ref.dtype)


def _merge_classes(in_ref, stage_ref):
    dil, per = in_ref.shape[0], in_ref.shape[1]
    for r in range(dil):
        for t in range(stage_ref.shape[0]):
            stage_ref[t, pl.ds(r, per, stride=dil), :] = in_ref[r, :, t * LANES:(t + 1) * LANES].astype(F32)


def _class_block(dil, tm, width, index_map):
    return pl.BlockSpec((dil, tm // dil, width), index_map)


def inproj_fwd(x, g, scale1p, shift, w, gather=None):
    s_len = x.shape[0]
    tm, tn = min(512, s_len), 1024

    n_qkv = QKV_A // tn

    def body(x_ref, g_ref, sc_ref, sh_ref, w_ref, *rest):
        if gather is not None:
            (win_ref, wout_ref, proj_ref, h_ref, q4_ref, q16_ref, fin_ref, fout_ref, stage,
             send_sems, recv_sems, local_sems) = rest
            first, last = _grid_edge(2)
            copies = _gather_phase1([win_ref, wout_ref], _weight_blocks(fin_ref, fout_ref), send_sems, recv_sems, local_sems)
            pl.when(first)(lambda: _start_all(copies))
        else:
            proj_ref, h_ref, q4_ref, q16_ref, stage = rest

        @pl.when(pl.program_id(1) == 0)
        def _():
            xv = x_ref[...]
            r = lax.rsqrt(jnp.mean(xv * xv, axis=-1, keepdims=True) + NORM_EPS)
            h_ref[...] = ((xv * r * g_ref[...]) * sc_ref[...] + sh_ref[...]).astype(BF16)
        res = _dot(h_ref[...], w_ref[...])
        proj_ref[...] = res.astype(BF16)

        @pl.when(pl.program_id(1) < n_qkv)
        def _():
            _fill_stage(stage, res)
            _split_classes(stage, [q4_ref, q16_ref])
        if gather is not None:
            pl.when(last)(lambda: _wait_all(copies))

    vec = pl.BlockSpec((1, D_MODEL), lambda i, j: (0, 0))
    in_specs = [pl.BlockSpec((tm, D_MODEL), lambda i, j: (i, 0)), vec, vec, vec,
                pl.BlockSpec((D_MODEL, tn), lambda i, j: (0, j))]
    out_specs = [pl.BlockSpec((tm, tn), lambda i, j: (i, j)), pl.BlockSpec((tm, D_MODEL), lambda i, j: (i, 0))]
    out_shape = [jax.ShapeDtypeStruct((s_len, IN_W), BF16), jax.ShapeDtypeStruct((s_len, D_MODEL), BF16)]
    for dil in DILATIONS[1:]:
        out_specs.append(pl.BlockSpec((dil, tm // dil, tn), lambda i, j: (0, i, jnp.minimum(j, n_qkv - 1))))
        out_shape.append(jax.ShapeDtypeStruct((dil, s_len // dil, QKV_A), BF16))
    scratch = [_stage_shape(tm, tn)]
    extra = ()
    if gather is not None:
        in_specs += [HBM, HBM]
        out_specs += [HBM, HBM]
        out_shape += [jax.ShapeDtypeStruct((D_MODEL, IN_W), BF16), jax.ShapeDtypeStruct((D_MODEL, D_MODEL), BF16)]
        scratch += GATHER1_SEMS
        extra = tuple(gather)
    return pl.pallas_call(
        body, name="inproj_fwd_gather" if gather is not None else "inproj_fwd", grid=(s_len // tm, IN_W // tn),
        in_specs=in_specs, out_specs=out_specs, out_shape=out_shape, scratch_shapes=scratch,
        compiler_params=_params("arbitrary", "arbitrary"),
    )(x, g, scale1p, shift, w, *extra)


MASK_DISTANCE = 1e33


def _attn_plan(sub_len):
    tq = min(sub_len, 128)
    win = min(sub_len, tq + 2 * RADIUS)
    heads = 1 if sub_len > 1024 else (2 if sub_len > 256 else N_HEADS_ATTN)
    return tq, win, sub_len // tq, heads, 4


def _attn_tiles(sub_len, tq, win, n_tiles, unroll, tile):
    tile(0, 0, 0)
    if n_tiles > 2:
        def mid(i, carry):
            q0 = pl.multiple_of(i * tq, tq)
            tile(q0, pl.multiple_of(q0 - RADIUS, RADIUS), 1)
            return carry
        lax.fori_loop(1, n_tiles - 1, mid, 0, unroll=min(unroll, n_tiles - 2))
    if n_tiles > 1:
        tile(sub_len - tq, sub_len - win, 2)


def _attn_bias(bias_ref, head, heads, tq, win, dil):
    h = pl.program_id(1) * heads + head
    slope = jnp.exp(-(h + 1).astype(F32) * LN2 * jnp.ones((1, 1), F32))
    rel = _iota((tq, win), 1) - _iota((tq, win), 0)
    for v, off in enumerate((0, RADIUS, win - tq)):
        dist = jnp.abs(rel - off)
        bias_ref[v] = slope * jnp.where(dist <= RADIUS, (dist * dil).astype(F32), MASK_DISTANCE)


def _attn_specs(n_cls, sub_len, heads):
    width = heads * HEAD_DIM
    per = ATTN_WIDTH // width

    def col(part):
        return pl.BlockSpec((None, sub_len, width), lambda r, g: (r, 0, part * per + g))
    return col, (n_cls, N_HEADS_ATTN // heads)


def attn_fwd(qkv, dil):
    n_cls, sub_len, _ = qkv.shape
    tq, win, n_tiles, heads, unroll = _attn_plan(sub_len)

    def body(q_ref, k_ref, v_ref, o_ref, lse_ref, bias_ref):
        for head in range(heads):
            lanes = slice(head * HEAD_DIM, (head + 1) * HEAD_DIM)
            _attn_bias(bias_ref, head, heads, tq, win, dil)

            def tile(q0, start, variant):
                s = _dot_nt(q_ref[pl.ds(q0, tq), lanes], k_ref[pl.ds(start, win), lanes]) * ATTN_SCALE - bias_ref[variant]
                m = jnp.max(s, axis=1, keepdims=True)
                p = jnp.exp(s - m)
                den = jnp.sum(p, axis=1, keepdims=True)
                o_ref[pl.ds(q0, tq), lanes] = _dot(p.astype(BF16), v_ref[pl.ds(start, win), lanes]) / den
                lse_ref[pl.ds(q0, tq), lanes] = jnp.broadcast_to(m + jnp.log(den), (tq, HEAD_DIM))

            _attn_tiles(sub_len, tq, win, n_tiles, unroll, tile)

    col, grid = _attn_specs(n_cls, sub_len, heads)
    out = jax.ShapeDtypeStruct((n_cls, sub_len, ATTN_WIDTH), F32)
    return pl.pallas_call(
        body, name=f"attn_fwd_d{dil}", grid=grid,
        in_specs=[col(0), col(1), col(2)], out_specs=[col(0), col(0)], out_shape=[out, out],
        scratch_shapes=[pltpu.VMEM((3, tq, win), F32)],
        compiler_params=_params("arbitrary", "arbitrary"),
    )(qkv, qkv, qkv)


def attn_bwd(qkv, do, lse, delta, dil):
    n_cls, sub_len, _ = qkv.shape
    tq, win, n_tiles, heads, unroll = _attn_plan(sub_len)

    def body(q_ref, k_ref, v_ref, do_ref, lse_ref, dl_ref, dq_ref, dk_ref, dv_ref, bias_ref, dk_acc, dv_acc):
        for head in range(heads):
            lanes = slice(head * HEAD_DIM, (head + 1) * HEAD_DIM)
            _attn_bias(bias_ref, head, heads, tq, win, dil)
            dk_acc[...] = jnp.zeros_like(dk_acc)
            dv_acc[...] = jnp.zeros_like(dv_acc)

            def tile(q0, start, variant):
                q = q_ref[pl.ds(q0, tq), lanes]
                k = k_ref[pl.ds(start, win), lanes]
                v = v_ref[pl.ds(start, win), lanes]
                dov = do_ref[pl.ds(q0, tq), lanes]
                s = _dot_nt(q, k) * ATTN_SCALE - bias_ref[variant]
                p = jnp.exp(s - lse_ref[pl.ds(q0, tq), head * HEAD_DIM:head * HEAD_DIM + 1])
                ds = (p * (_dot_nt(dov, v) - dl_ref[pl.ds(q0, tq), head * HEAD_DIM:head * HEAD_DIM + 1])).astype(BF16)
                dq_ref[pl.ds(q0, tq), lanes] = (_dot(ds, k) * ATTN_SCALE).astype(BF16)
                dk_acc[pl.ds(start, win), :] += _dot_tn(ds, q) * ATTN_SCALE
                dv_acc[pl.ds(start, win), :] += _dot_tn(p.astype(BF16), dov)

            _attn_tiles(sub_len, tq, win, n_tiles, unroll, tile)
            dk_ref[:, lanes] = dk_acc[...].astype(BF16)
            dv_ref[:, lanes] = dv_acc[...].astype(BF16)

    col, grid = _attn_specs(n_cls, sub_len, heads)
    out = jax.ShapeDtypeStruct((n_cls, sub_len, ATTN_WIDTH), BF16)
    return pl.pallas_call(
        body, name=f"attn_bwd_d{dil}", grid=grid,
        in_specs=[col(0), col(1), col(2), col(0), col(0), col(0)],
        out_specs=[col(0), col(0), col(0)], out_shape=[out, out, out],
        scratch_shapes=[pltpu.VMEM((3, tq, win), F32), pltpu.VMEM((sub_len, HEAD_DIM), F32),
                        pltpu.VMEM((sub_len, HEAD_DIM), F32)],
        compiler_params=_params("arbitrary", "arbitrary"),
    )(qkv, qkv, qkv, do, lse, delta)


def _ret_tables(lg_ref):
    h = pl.program_id(0)
    one = jnp.ones((1, 1), F32)
    lgf, lgb = lg_ref[h] * one, lg_ref[RET_HEADS + h] * one
    c = RET_CHUNK
    rel = (_iota((c, c), 0) - _iota((c, c), 1)).astype(F32)
    dec_f = jnp.where(rel >= 0, jnp.exp(jnp.maximum(rel, 0.0) * lgf), 0.0)
    dec_b = jnp.where(rel <= 0, jnp.exp(jnp.maximum(-rel, 0.0) * lgb), 0.0)
    ci = _iota((c, 1), 0).astype(F32)
    tab = dict(rel=rel, dec_f=dec_f, dec_b=dec_b, ci=ci,
               xi_f=jnp.exp((ci + 1.0) * lgf), ze_f=jnp.exp((c - 1.0 - ci) * lgf), g_f=jnp.exp(c * lgf),
               xi_b=jnp.exp((c - ci) * lgb), ze_b=jnp.exp(ci * lgb), g_b=jnp.exp(c * lgb))
    return tab


def _ret_specs(s_len):
    q = pl.BlockSpec((s_len, RET_QK), lambda h: (0, COL_QR // RET_QK + h))
    k = pl.BlockSpec((s_len, RET_QK), lambda h: (0, COL_KR // RET_QK + h))
    v = pl.BlockSpec((s_len, RET_V), lambda h: (0, COL_VR // RET_V + h))
    wide = pl.BlockSpec((s_len, RET_V), lambda h: (0, h))
    narrow = pl.BlockSpec((s_len, RET_QK), lambda h: (0, h))
    smem = pl.BlockSpec(memory_space=pltpu.SMEM)
    return smem, q, k, v, wide, narrow


def ret_fwd(proj, lg):
    s_len = proj.shape[0]
    c, n_chunks = RET_CHUNK, proj.shape[0] // RET_CHUNK

    def body(lg_ref, q_ref, k_ref, v_ref, opre_ref, y_ref, st_f, st_b):
        t = _ret_tables(lg_ref)
        dec = t["dec_f"] + t["dec_b"]
        st_f[...] = jnp.zeros_like(st_f)
        st_b[...] = jnp.zeros_like(st_b)

        def load(n):
            r0 = pl.multiple_of(n * c, c)
            q, k, v = q_ref[pl.ds(r0, c), :], k_ref[pl.ds(r0, c), :], v_ref[pl.ds(r0, c), :]
            return r0, q, (k.astype(F32) * RET_SCALE), v

        def fwd(n, carry):
            r0, q, kf, v = load(n)
            inner = (_dot_nt(q, kf.astype(BF16)) * dec).astype(BF16)
            opre_ref[pl.ds(r0, c), :] = _dot(inner, v) + _dot(q, st_f[...].astype(BF16)) * t["xi_f"]
            st_f[...] = st_f[...] * t["g_f"] + _dot_tn((kf * t["ze_f"]).astype(BF16), v)
            return carry

        def bwd(i, carry):
            r0, q, kf, v = load(n_chunks - 1 - i)
            o = opre_ref[pl.ds(r0, c), :] + _dot(q, st_b[...].astype(BF16)) * t["xi_b"]
            st_b[...] = st_b[...] * t["g_b"] + _dot_tn((kf * t["ze_b"]).astype(BF16), v)
            opre_ref[pl.ds(r0, c), :] = o
            y_ref[pl.ds(r0, c), :] = o * lax.rsqrt(jnp.mean(o * o, axis=-1, keepdims=True) + NORM_EPS)
            return carry

        lax.fori_loop(0, n_chunks, fwd, 0)
        lax.fori_loop(0, n_chunks, bwd, 0)

    smem, q, k, v, wide, _ = _ret_specs(s_len)
    out = jax.ShapeDtypeStruct((s_len, RET_HEADS * RET_V), F32)
    return pl.pallas_call(
        body, name="ret_fwd", grid=(RET_HEADS,), in_specs=[smem, q, k, v], out_specs=[wide, wide],
        out_shape=[out, out], scratch_shapes=[pltpu.VMEM((RET_QK, RET_V), F32), pltpu.VMEM((RET_QK, RET_V), F32)],
        compiler_params=_params("arbitrary"),
    )(lg, proj, proj, proj)


def ret_bwd(proj, lg, o_pre, dy):
    s_len = proj.shape[0]
    c, n_chunks = RET_CHUNK, proj.shape[0] // RET_CHUNK
    cf = float(c)

    def body(lg_ref, q_ref, k_ref, v_ref, o_ref, dy_ref, dq_ref, dk_ref, dv_ref, glf_ref, glb_ref,
             st_f, dst_b, st_b, dst_f, keep_sf, keep_dtb):
        t = _ret_tables(lg_ref)
        dec = t["dec_f"] + t["dec_b"]
        e_f, e_b, ci = t["rel"] * t["dec_f"], -t["rel"] * t["dec_b"], t["ci"]
        for ref in (st_f, dst_b, st_b, dst_f):
            ref[...] = jnp.zeros_like(ref)

        def load(n):
            r0 = pl.multiple_of(n * c, c)
            q, k, v = q_ref[pl.ds(r0, c), :], k_ref[pl.ds(r0, c), :], v_ref[pl.ds(r0, c), :]
            o, dyv = o_ref[pl.ds(r0, c), :], dy_ref[pl.ds(r0, c), :]
            rr = lax.rsqrt(jnp.mean(o * o, axis=-1, keepdims=True) + NORM_EPS)
            y = o * rr
            do = (rr * (dyv - y * jnp.mean(dyv * y, axis=-1, keepdims=True))).astype(BF16)
            return r0, q, k.astype(F32) * RET_SCALE, v, do

        def total(x):
            return jnp.sum(jnp.sum(x, axis=1, keepdims=True), axis=0, keepdims=True)

        def fwd(n, carry):
            glf, glb = carry
            r0, q, kf, v, do = load(n)
            qf, kb = q.astype(F32), kf.astype(BF16)
            a = _dot_nt(q, kb)
            b = _dot_nt(do, v)
            da = (b * dec).astype(BF16)
            ab = a * b
            glf = glf + total(e_f * ab)
            glb = glb + total(e_b * ab)
            sf_b, dtb_b = st_f[...].astype(BF16), dst_b[...].astype(BF16)
            dq_inter = _dot_nt(do, sf_b) * t["xi_f"]
            dk_inter = _dot_nt(v, dtb_b) * t["ze_b"]
            glf = glf + total((ci + 1.0) * (qf * dq_inter))
            glb = glb + total(ci * (kf * dk_inter))
            dq_ref[pl.ds(r0, c), :] = _dot(da, kb) + dq_inter
            dk_ref[pl.ds(r0, c), :] = _dot_tn(da, q) + dk_inter
            dv_ref[pl.ds(r0, c), :] = _dot_tn((a * dec).astype(BF16), do) + _dot((kf * t["ze_b"]).astype(BF16), dtb_b)
            keep_sf[n] = sf_b
            keep_dtb[n] = dtb_b
            st_f[...] = st_f[...] * t["g_f"] + _dot_tn((kf * t["ze_f"]).astype(BF16), v)
            dst_b[...] = dst_b[...] * t["g_b"] + _dot_tn((qf * t["xi_b"]).astype(BF16), do)
            return glf, glb

        def bwd(i, carry):
            glf, glb = carry
            n = n_chunks - 1 - i
            r0, q, kf, v, do = load(n)
            qf = q.astype(F32)
            tb_b, dsf_b = st_b[...].astype(BF16), dst_f[...].astype(BF16)
            dq_inter = _dot_nt(do, tb_b) * t["xi_b"]
            dk_inter = _dot_nt(v, dsf_b) * t["ze_f"]
            glb = glb + total((cf - ci) * (qf * dq_inter)) + cf * t["g_b"] * total(keep_dtb[n].astype(F32) * st_b[...])
            glf = glf + total((cf - 1.0 - ci) * (kf * dk_inter)) + cf * t["g_f"] * total(dst_f[...] * keep_sf[n].astype(F32))
            dq_ref[pl.ds(r0, c), :] += dq_inter
            dk_ref[pl.ds(r0, c), :] = (dk_ref[pl.ds(r0, c), :] + dk_inter) * RET_SCALE
            dv_ref[pl.ds(r0, c), :] += _dot((kf * t["ze_f"]).astype(BF16), dsf_b)
            st_b[...] = st_b[...] * t["g_b"] + _dot_tn((kf * t["ze_b"]).astype(BF16), v)
            dst_f[...] = dst_f[...] * t["g_f"] + _dot_tn((qf * t["xi_f"]).astype(BF16), do)
            return glf, glb

        zero = jnp.zeros((1, 1), F32)
        carry = lax.fori_loop(0, n_chunks, fwd, (zero, zero))
        glf, glb = lax.fori_loop(0, n_chunks, bwd, carry)
        glf_ref[...] = jnp.broadcast_to(glf, (8, 128))
        glb_ref[...] = jnp.broadcast_to(glb, (8, 128))

    smem, q, k, v, wide, narrow = _ret_specs(s_len)
    scal = pl.BlockSpec((None, 8, 128), lambda h: (h, 0, 0))
    state = pltpu.VMEM((RET_QK, RET_V), F32)
    keep = pltpu.VMEM((n_chunks, RET_QK, RET_V), BF16)
    return pl.pallas_call(
        body, name="ret_bwd", grid=(RET_HEADS,), in_specs=[smem, q, k, v, wide, wide],
        out_specs=[narrow, narrow, wide, scal, scal],
        out_shape=[jax.ShapeDtypeStruct((s_len, RET_HEADS * RET_QK), F32), jax.ShapeDtypeStruct((s_len, RET_HEADS * RET_QK), F32),
                   jax.ShapeDtypeStruct((s_len, RET_HEADS * RET_V), F32),
                   jax.ShapeDtypeStruct((RET_HEADS, 8, 128), F32), jax.ShapeDtypeStruct((RET_HEADS, 8, 128), F32)],
        scratch_shapes=[state, state, state, state, keep, keep],
        compiler_params=_params("arbitrary"),
    )(lg, proj, proj, proj, o_pre, dy)


def _silu_parts(z):
    sig = _sigmoid(z)
    return z * sig, sig * (1.0 + z * (1.0 - sig))


def outproj_fwd(x, gate, w_out, attn_outs, y_r, proj):
    s_len = x.shape[0]
    tm = min(256, s_len)

    def body(x_ref, gate_ref, w_ref, o1, l1, o2, l2, o3, l3, yr_ref, za_ref, zr_ref,
             xn_ref, oa_ref, lse_ref, lse4_ref, lse16_ref, so2, sl2, so3, sl3):
        for src, dst in ((o2, so2), (l2, sl2), (o3, so3), (l3, sl3)):
            _merge_classes(src, dst)
        la, lb, lc = l1[...], _read_stage(sl2), _read_stage(sl3)
        m = jnp.maximum(jnp.maximum(la, lb), lc)
        lse = m + jnp.log(jnp.exp(la - m) + jnp.exp(lb - m) + jnp.exp(lc - m))
        o_a = jnp.exp(la - lse) * o1[...] + jnp.exp(lb - lse) * _read_stage(so2) + jnp.exp(lc - lse) * _read_stage(so3)
        oa_ref[...] = o_a
        lse_ref[...] = lse
        _fill_stage(sl2, lse)
        _split_classes(sl2, [lse4_ref, lse16_ref])
        silu_a, _ = _silu_parts(za_ref[...].astype(F32))
        silu_r, _ = _silu_parts(zr_ref[...].astype(F32))
        y = jnp.concatenate([(o_a * silu_a).astype(BF16), (yr_ref[...] * silu_r).astype(BF16)], axis=1)
        xn_ref[...] = x_ref[...] + gate_ref[...] * _dot(y, w_ref[...])

    row = lambda w: pl.BlockSpec((tm, w), lambda i: (i, 0))
    half = row(ATTN_WIDTH)
    cls = [_class_block(dil, tm, ATTN_WIDTH, lambda i: (0, i, 0)) for dil in DILATIONS[1:]]
    flat = [a for pair in attn_outs for a in pair]
    sds = jax.ShapeDtypeStruct
    return pl.pallas_call(
        body, name="outproj_fwd", grid=(s_len // tm,),
        in_specs=[row(D_MODEL), pl.BlockSpec((1, D_MODEL), lambda i: (0, 0)),
                  pl.BlockSpec((D_MODEL, D_MODEL), lambda i: (0, 0)), half, half, cls[0], cls[0], cls[1], cls[1], half,
                  pl.BlockSpec((tm, ATTN_WIDTH), lambda i: (i, COL_ZA // ATTN_WIDTH)),
                  pl.BlockSpec((tm, ATTN_WIDTH), lambda i: (i, COL_ZR // ATTN_WIDTH))],
        out_specs=[row(D_MODEL), half, half] + cls,
        out_shape=[sds((s_len, D_MODEL), F32), sds((s_len, ATTN_WIDTH), F32), sds((s_len, ATTN_WIDTH), F32)]
                  + [sds((dil, s_len // dil, ATTN_WIDTH), F32) for dil in DILATIONS[1:]],
        scratch_shapes=[_stage_shape(tm, ATTN_WIDTH)] * 4,
        compiler_params=_params("arbitrary"),
    )(x, gate, w_out, *flat, y_r, proj, proj)


def loss_head(x, gain, target):
    s_len = x.shape[0]
    tm = min(256, s_len)

    def body(x_ref, g_ref, t_ref, dx_ref, loss_ref, dg_ref):
        @pl.when(pl.program_id(0) == 0)
        def _():
            loss_ref[...] = jnp.zeros_like(loss_ref)
            dg_ref[...] = jnp.zeros_like(dg_ref)
        xv, g = x_ref[...], g_ref[...]
        r = lax.rsqrt(jnp.mean(xv * xv, axis=-1, keepdims=True) + NORM_EPS)
        xn = xv * r
        err = xn * g - t_ref[...]
        part = 0.5 * jnp.sum(jnp.mean(err * err, axis=-1, keepdims=True), axis=0, keepdims=True)
        loss_ref[...] += jnp.broadcast_to(part, loss_ref.shape)
        dy = err * (1.0 / D_MODEL)
        dg_ref[...] += jnp.sum(dy * xn, axis=0, keepdims=True)
        dxn = dy * g
        dx_ref[...] = r * (dxn - xn * jnp.mean(dxn * xn, axis=-1, keepdims=True))

    row = pl.BlockSpec((tm, D_MODEL), lambda i: (i, 0))
    vec = pl.BlockSpec((1, D_MODEL), lambda i: (0, 0))
    return pl.pallas_call(
        body, name="loss_head", grid=(s_len // tm,), in_specs=[row, vec, row],
        out_specs=[row, pl.BlockSpec((8, 128), lambda i: (0, 0)), vec],
        out_shape=[jax.ShapeDtypeStruct((s_len, D_MODEL), F32), jax.ShapeDtypeStruct((8, 128), F32),
                   jax.ShapeDtypeStruct((1, D_MODEL), F32)],
        compiler_params=_params("arbitrary"),
    )(x, gain, target)


def outproj_bwd(dxn, gate, w_out, o_a, y_r, proj, exchange=None):
    s_len = dxn.shape[0]
    tm = min(256, s_len)

    def body(dx_ref, gate_ref, w_ref, oa_ref, yr_ref, za_ref, zr_ref, *rest):
        if exchange is not None:
            (ga_ref, gb_ref, doa_ref, dl_ref, dyr_ref, dza_ref, dzr_ref, y_ref, dxb_ref, do4, do16, dl4, dl16,
             ra_ref, rb_ref, stage, send_sems, recv_sems) = rest
            first, last = _grid_edge(1)
            copies = _reduce_phase1([ga_ref, gb_ref], [ra_ref, rb_ref], send_sems, recv_sems)
            pl.when(first)(lambda: _start_all(copies))
        else:
            doa_ref, dl_ref, dyr_ref, dza_ref, dzr_ref, y_ref, dxb_ref, do4, do16, dl4, dl16, stage = rest
        dxv = dx_ref[...]
        dxb_ref[...] = dxv.astype(BF16)
        dy = _dot_nt((dxv * gate_ref[...]).astype(BF16), w_ref[...])
        dy_a, dy_r = dy[:, :ATTN_WIDTH], dy[:, ATTN_WIDTH:]
        o_a, y_rv = oa_ref[...], yr_ref[...]
        silu_a, dsilu_a = _silu_parts(za_ref[...].astype(F32))
        silu_r, dsilu_r = _silu_parts(zr_ref[...].astype(F32))
        do_a = dy_a * silu_a
        doa_ref[...] = do_a.astype(BF16)
        _fill_stage(stage, do_a)
        _split_classes(stage, [do4, do16])
        prod = do_a * o_a
        delta = jnp.concatenate(
            [jnp.broadcast_to(jnp.sum(prod[:, h * HEAD_DIM:(h + 1) * HEAD_DIM], axis=1, keepdims=True), (tm, HEAD_DIM))
             for h in range(N_HEADS_ATTN)], axis=1)
        dl_ref[...] = delta
        _fill_stage(stage, delta)
        _split_classes(stage, [dl4, dl16])
        dyr_ref[...] = dy_r * silu_r
        dza_ref[...] = (dy_a * o_a * dsilu_a).astype(BF16)
        dzr_ref[...] = (dy_r * y_rv * dsilu_r).astype(BF16)
        y_ref[...] = jnp.concatenate([(o_a * silu_a).astype(BF16), (y_rv * silu_r).astype(BF16)], axis=1)
        if exchange is not None:
            pl.when(last)(lambda: _wait_all(copies))

    row = lambda w: pl.BlockSpec((tm, w), lambda i: (i, 0))
    half = row(ATTN_WIDTH)
    sds = lambda w, dt: jax.ShapeDtypeStruct((s_len, w), dt)
    in_specs = [row(D_MODEL), pl.BlockSpec((1, D_MODEL), lambda i: (0, 0)),
                pl.BlockSpec((D_MODEL, D_MODEL), lambda i: (0, 0)), half, half,
                pl.BlockSpec((tm, ATTN_WIDTH), lambda i: (i, COL_ZA // ATTN_WIDTH)),
                pl.BlockSpec((tm, ATTN_WIDTH), lambda i: (i, COL_ZR // ATTN_WIDTH))]
    cls = [_class_block(dil, tm, ATTN_WIDTH, lambda i: (0, i, 0)) for dil in DILATIONS[1:]]
    out_specs = [half, half, half, half, half, row(D_MODEL), row(D_MODEL)] + cls + cls
    out_shape = [sds(ATTN_WIDTH, BF16), sds(ATTN_WIDTH, F32), sds(ATTN_WIDTH, F32), sds(ATTN_WIDTH, BF16),
                 sds(ATTN_WIDTH, BF16), sds(D_MODEL, BF16), sds(D_MODEL, BF16)]
    out_shape += [jax.ShapeDtypeStruct((dil, s_len // dil, ATTN_WIDTH), dt) for dt in (BF16, F32) for dil in DILATIONS[1:]]
    scratch = [_stage_shape(tm, ATTN_WIDTH)]
    extra = ()
    if exchange is not None:
        in_specs += [HBM, HBM]
        out_specs += [HBM, HBM]
        out_shape += _landing_shapes(4)
        scratch += REDUCE1_SEMS
        extra = _split_cores(exchange)
    return pl.pallas_call(
        body, name="outproj_bwd_reduce1" if exchange is not None else "outproj_bwd", grid=(s_len // tm,),
        in_specs=in_specs, out_specs=out_specs, out_shape=out_shape, scratch_shapes=scratch,
        compiler_params=_params("arbitrary"),
    )(dxn, gate, w_out, o_a, y_r, proj, proj, *extra)


def wout_grad(y, dxb, gate, w_out):
    s_len = y.shape[0]
    tf, ts = 512, min(512, s_len)

    def body(y_ref, dx_ref, gate_ref, w_ref, dw_ref, dgate_ref, acc):
        f, s = pl.program_id(0), pl.program_id(1)

        @pl.when((f == 0) & (s == 0))
        def _():
            dgate_ref[...] = jnp.zeros_like(dgate_ref)

        @pl.when(s == 0)
        def _():
            acc[...] = jnp.zeros_like(acc)
        acc[...] += _dot_tn(y_ref[...], dx_ref[...])

        @pl.when(s == pl.num_programs(1) - 1)
        def _():
            m = acc[...]
            dw_ref[...] = (m * gate_ref[...]).astype(BF16).reshape(dw_ref.shape)
            dgate_ref[...] += jnp.sum(m * w_ref[...].astype(F32), axis=0, keepdims=True)

    per = tf // W_OUT_SHARD
    return pl.pallas_call(
        body, name="wout_grad", grid=(D_MODEL // tf, s_len // ts),
        in_specs=[pl.BlockSpec((ts, tf), lambda f, s: (s, f)), pl.BlockSpec((ts, D_MODEL), lambda f, s: (s, 0)),
                  pl.BlockSpec((1, D_MODEL), lambda f, s: (0, 0)), pl.BlockSpec((tf, D_MODEL), lambda f, s: (f, 0))],
        out_specs=[pl.BlockSpec((per, W_OUT_SHARD, D_MODEL), lambda f, s: (f, 0, 0)),
                   pl.BlockSpec((1, D_MODEL), lambda f, s: (0, 0))],
        out_shape=[jax.ShapeDtypeStruct((N_DEV, W_OUT_SHARD, D_MODEL), BF16), jax.ShapeDtypeStruct((1, D_MODEL), F32)],
        scratch_shapes=[pltpu.VMEM((tf, D_MODEL), F32)],
        compiler_params=_params("arbitrary", "arbitrary"),
    )(y, dxb, gate, w_out)


def assemble_dproj(dqkv_a, dz_a, dq_r, dk_r, dv_r, dz_r):
    s_len = dz_a.shape[0]
    tm = min(256, s_len)

    def body(*refs):
        pat, (dza, dqr, dkr, dvr, dzr, out, stage) = refs[:9], refs[9:]
        for t in range(3):
            tot = pat[t][...].astype(F32)
            for p in (1, 2):
                _merge_classes(pat[3 * p + t], stage)
                tot = tot + _read_stage(stage)
            out[:, t * ATTN_WIDTH:(t + 1) * ATTN_WIDTH] = tot.astype(BF16)
        out[:, COL_ZA:COL_QR] = dza[...]
        out[:, COL_QR:COL_KR] = dqr[...].astype(BF16)
        out[:, COL_KR:COL_VR] = dkr[...].astype(BF16)
        out[:, COL_VR:COL_ZR] = dvr[...].astype(BF16)
        out[:, COL_ZR:IN_W] = dzr[...]

    row = lambda w: pl.BlockSpec((tm, w), lambda i: (i, 0))
    cls = [_class_block(dil, tm, ATTN_WIDTH, lambda i: (0, i, 0)) for dil in DILATIONS[1:]]
    flat = [dqkv_a[p][t] for p in range(3) for t in range(3)]
    return pl.pallas_call(
        body, name="assemble_dproj", grid=(s_len // tm,),
        in_specs=[row(ATTN_WIDTH)] * 3 + [cls[0]] * 3 + [cls[1]] * 3
                 + [row(ATTN_WIDTH), row(512), row(512), row(ATTN_WIDTH), row(ATTN_WIDTH)],
        out_specs=row(IN_W), out_shape=jax.ShapeDtypeStruct((s_len, IN_W), BF16),
        scratch_shapes=[_stage_shape(tm, ATTN_WIDTH)],
        compiler_params=_params("arbitrary"),
    )(*flat, dz_a, dq_r, dk_r, dv_r, dz_r)


def inproj_bwd(dproj, w, x, g, scale1p, dxn, exchange=None):
    s_len = x.shape[0]
    tm, tk = min(512, s_len), 1024

    def body(dp_ref, w_ref, x_ref, g_ref, sc_ref, dxn_ref, *rest):
        if exchange is not None:
            pa_ref, pb_ref, dx_ref, st_ref, ra_ref, rb_ref, acc, send_sems, recv_sems = rest
            first, last = _grid_edge(2)
            copies = _reduce_phase2([pa_ref, pb_ref], [ra_ref, rb_ref], send_sems, recv_sems)
            pl.when(first)(lambda: _start_all(copies))
        else:
            dx_ref, st_ref, acc = rest
        i, k = pl.program_id(0), pl.program_id(1)

        @pl.when((i == 0) & (k == 0))
        def _():
            st_ref[...] = jnp.zeros_like(st_ref)

        @pl.when(k == 0)
        def _():
            acc[...] = jnp.zeros_like(acc)
        acc[...] += _dot_nt(dp_ref[...], w_ref[...])

        @pl.when(k == pl.num_programs(1) - 1)
        def _():
            dh, xv, g, sc = acc[...], x_ref[...], g_ref[...], sc_ref[...]
            r = lax.rsqrt(jnp.mean(xv * xv, axis=-1, keepdims=True) + NORM_EPS)
            xn = xv * r
            da = dh * sc
            st_ref[0:1, :] += jnp.sum(dh, axis=0, keepdims=True)
            st_ref[1:2, :] += jnp.sum(dh * (xn * g), axis=0, keepdims=True)
            st_ref[2:3, :] += jnp.sum(da * xn, axis=0, keepdims=True)
            dn = da * g
            dx_ref[...] = r * (dn - xn * jnp.mean(dn * xn, axis=-1, keepdims=True)) + dxn_ref[...]

        if exchange is not None:
            pl.when(last)(lambda: _wait_all(copies))

    row = pl.BlockSpec((tm, D_MODEL), lambda i, k: (i, 0))
    vec = pl.BlockSpec((1, D_MODEL), lambda i, k: (0, 0))
    in_specs = [pl.BlockSpec((tm, tk), lambda i, k: (i, k)), pl.BlockSpec((D_MODEL, tk), lambda i, k: (0, k)),
                row, vec, vec, row]
    out_specs = [row, pl.BlockSpec((8, D_MODEL), lambda i, k: (0, 0))]
    out_shape = [jax.ShapeDtypeStruct((s_len, D_MODEL), F32), jax.ShapeDtypeStruct((8, D_MODEL), F32)]
    scratch = [pltpu.VMEM((tm, D_MODEL), F32)]
    extra = ()
    if exchange is not None:
        in_specs += [HBM, HBM]
        out_specs += [HBM, HBM]
        out_shape += _landing_shapes(3)
        scratch += REDUCE2_SEMS
        extra = tuple(exchange)
    return pl.pallas_call(
        body, name="inproj_bwd_reduce2" if exchange is not None else "inproj_bwd", grid=(s_len // tm, IN_W // tk),
        in_specs=in_specs, out_specs=out_specs, out_shape=out_shape, scratch_shapes=scratch,
        compiler_params=_params("arbitrary", "arbitrary"),
    )(dproj, w, x, g, scale1p, dxn, *extra)


def win_grad(h, dproj):
    s_len = h.shape[0]
    ts = min(512, s_len)

    def body(h_ref, dp_ref, dw_ref, acc):
        s = pl.program_id(1)

        @pl.when(s == 0)
        def _():
            acc[...] = jnp.zeros_like(acc)
        acc[...] += _dot_tn(h_ref[...], dp_ref[...])

        @pl.when(s == pl.num_programs(1) - 1)
        def _():
            dw_ref[...] = acc[...].astype(BF16)

    return pl.pallas_call(
        body, name="win_grad", grid=(N_DEV, s_len // ts),
        in_specs=[pl.BlockSpec((ts, D_MODEL), lambda j, s: (s, 0)), pl.BlockSpec((ts, W_IN_SHARD), lambda j, s: (s, j))],
        out_specs=pl.BlockSpec((None, D_MODEL, W_IN_SHARD), lambda j, s: (j, 0, 0)),
        out_shape=jax.ShapeDtypeStruct((N_DEV, D_MODEL, W_IN_SHARD), BF16),
        scratch_shapes=[pltpu.VMEM((D_MODEL, W_IN_SHARD), F32)],
        compiler_params=_params("arbitrary", "arbitrary"),
    )(h, dproj)


def ada_fwd(c_all, w_ada):
    def body(c_ref, w_ref, act_ref, part_ref):
        cv = c_ref[...]
        act = cv * _sigmoid(cv)
        act_ref[...] = act
        part_ref[...] = _dot(act.astype(BF16), w_ref[...].astype(BF16))

    return pl.pallas_call(
        body, name="ada_fwd", grid=(DEPTH,),
        in_specs=[pl.BlockSpec((N_DEV, D_MODEL), lambda l: (0, 0)),
                  pl.BlockSpec((None, D_MODEL, W_ADA_SHARD), lambda l: (l, 0, 0))],
        out_specs=[pl.BlockSpec((N_DEV, D_MODEL), lambda l: (0, 0)),
                   pl.BlockSpec((None, N_DEV, W_ADA_SHARD), lambda l: (l, 0, 0))],
        out_shape=[jax.ShapeDtypeStruct((N_DEV, D_MODEL), F32), jax.ShapeDtypeStruct((DEPTH, N_DEV, W_ADA_SHARD), F32)],
        compiler_params=_params("arbitrary"),
    )(c_all, w_ada)


def _adamw(w, g, m, v):
    m = ADAM_B1 * m + (1.0 - ADAM_B1) * g
    v = ADAM_B2 * v + (1.0 - ADAM_B2) * (g * g)
    delta = -ADAM_LR * ((m * ADAM_C1) / (jnp.sqrt(v * ADAM_C2) + ADAM_EPS) + ADAM_WD * w)
    return delta, m, v


def ada_update(act_t, dmod, w, m, v):
    tr = 512

    def body(a_ref, d_ref, w_ref, m_ref, v_ref, g_out, dl_out, m_out, v_out):
        a = a_ref[...].astype(BF16).astype(F32)
        d = d_ref[...].astype(BF16).astype(F32)
        g = a[:, 0:1] * d[0:1, :]
        for b in range(1, N_DEV):
            g = g + a[:, b:b + 1] * d[b:b + 1, :]
        g_out[...] = g
        dl_out[...], m_out[...], v_out[...] = _adamw(w_ref[...], g, m_ref[...], v_ref[...])

    blk = pl.BlockSpec((None, tr, W_ADA_SHARD), lambda l, r: (l, r, 0))
    out = jax.ShapeDtypeStruct(w.shape, F32)
    return pl.pallas_call(
        body, name="ada_update", grid=(DEPTH, D_MODEL // tr),
        in_specs=[pl.BlockSpec((tr, N_DEV), lambda l, r: (r, 0)),
                  pl.BlockSpec((None, N_DEV, W_ADA_SHARD), lambda l, r: (l, 0, 0)), blk, blk, blk],
        out_specs=[blk] * 4, out_shape=[out] * 4, compiler_params=_params("arbitrary", "arbitrary"),
    )(act_t, dmod, w, m, v)


def chip_sum(pos, grads, landed, name):
    _, _, n_rows, n_cols = grads.shape
    tr = min(512, n_rows)

    def chip(k, pos_ref):
        return (pos_ref[0] ^ (k // 2)) * 2 + (pos_ref[1] ^ (k % 2))

    def body(pos_ref, g_ref, r_ref, out_ref):
        out_ref[...] = (g_ref[...].astype(F32) + r_ref[...].astype(F32)).astype(BF16)

    return pl.pallas_call(
        body, name=name,
        grid_spec=pltpu.PrefetchScalarGridSpec(
            num_scalar_prefetch=1, grid=(N_DEV // 2, n_rows // tr),
            in_specs=[pl.BlockSpec((None, None, tr, n_cols), lambda k, r, p: (chip(k, p), p[2], r, 0)),
                      pl.BlockSpec((None, tr, n_cols), lambda k, r, p: (chip(k, p), r, 0))],
            out_specs=pl.BlockSpec((None, tr, n_cols), lambda k, r, p: (k, r, 0))),
        out_shape=jax.ShapeDtypeStruct((N_DEV // 2, n_rows, n_cols), BF16),
        compiler_params=_params("arbitrary", "arbitrary"),
    )(pos, grads, landed)


def shard_update(own, others, w, m, v, name):
    n_rows, n_cols = w.shape
    tr = min(256, n_rows)

    def body(own_ref, oth_ref, w_ref, m_ref, v_ref, g_out, dl_out, m_out, v_out):
        g = own_ref[...].astype(F32)
        for k in range(3):
            g = g + oth_ref[k].astype(F32)
        g_out[...] = g
        dl_out[...], m_out[...], v_out[...] = _adamw(w_ref[...], g, m_ref[...], v_ref[...])

    blk = pl.BlockSpec((tr, n_cols), lambda r: (r, 0))
    out = jax.ShapeDtypeStruct(w.shape, F32)
    return pl.pallas_call(
        body, name=name, grid=(n_rows // tr,),
        in_specs=[pl.BlockSpec((None, tr, n_cols), lambda r: (0, r, 0)), pl.BlockSpec((3, tr, n_cols), lambda r: (0, r, 0)),
                  blk, blk, blk],
        out_specs=[blk] * 4, out_shape=[out] * 4, compiler_params=_params("arbitrary"),
    )(own, others, w, m, v)


def small_update(parts, w, m, v):
    def body(p_ref, w_ref, m_ref, v_ref, g_out, dl_out, m_out, v_out):
        g = p_ref[0]
        for k in range(1, N_DEV):
            g = g + p_ref[k]
        g_out[...] = g
        dl_out[...], m_out[...], v_out[...] = _adamw(w_ref[...], g, m_ref[...], v_ref[...])

    out = jax.ShapeDtypeStruct(w.shape, F32)
    return pl.pallas_call(body, name="small_update", out_shape=[out] * 4, compiler_params=_params())(parts, w, m, v)


def _two_level_allgather(srcs, dst_block, send_sems, recv_sems, local_sems):
    x, y, c = _position()
    me, sibling = (x, y, c), (x, y, 1 - c)
    chips = [(1 - x, y), (x, 1 - y), (1 - x, 1 - y)]
    n = len(srcs)

    def copy(a, k, block, to, src=None):
        dst = dst_block(a, _flat(*block))
        return pltpu.make_async_remote_copy(
            src_ref=dst if src is None else src, dst_ref=dst, send_sem=send_sems.at[a * 7 + k],
            recv_sem=recv_sems.at[a * 7 + k], device_id=to, device_id_type=MESH)

    mine = [pltpu.make_async_copy(srcs[a], dst_block(a, _flat(*me)), local_sems.at[a]) for a in range(n)]
    for cp in mine:
        cp.start()
    first = []
    for a in range(n):
        first.append(copy(a, 0, me, sibling, src=srcs[a]))
        first += [copy(a, 1 + j, me, (*chip, c), src=srcs[a]) for j, chip in enumerate(chips)]
    for cp in first:
        cp.start()
    passed = []
    for j, chip in enumerate(chips):
        for a in range(n):
            copy(a, 1 + j, (*chip, c), me).wait_recv()
            fwd = copy(a, 4 + j, (*chip, c), sibling)
            fwd.start()
            passed.append(fwd)
    for a in range(n):
        copy(a, 0, sibling, me).wait_recv()
        for j, chip in enumerate(chips):
            copy(a, 4 + j, (*chip, 1 - c), me).wait_recv()
    for cp in first + passed:
        cp.wait_send()
    for cp in mine:
        cp.wait()


def allgather_rows(x, name):
    def body(x_ref, out_ref, send_sems, recv_sems, local_sems):
        _two_level_allgather([x_ref], lambda a, idx: out_ref.at[idx], send_sems, recv_sems, local_sems)

    vmem = pl.BlockSpec(memory_space=pltpu.VMEM)
    return pl.pallas_call(
        body, name=name, in_specs=[vmem], out_specs=vmem,
        out_shape=jax.ShapeDtypeStruct((N_DEV,) + x.shape, x.dtype),
        scratch_shapes=[pltpu.SemaphoreType.DMA((7,)), pltpu.SemaphoreType.DMA((7,)), pltpu.SemaphoreType.DMA((1,))],
        compiler_params=_params(),
    )(x)


def _full_weight_shapes():
    return [jax.ShapeDtypeStruct((D_MODEL, IN_W), BF16), jax.ShapeDtypeStruct((D_MODEL, D_MODEL), BF16)]


def allgather_weights(w_in_b, w_out_b):
    def body(win_ref, wout_ref, fin_ref, fout_ref, s1, r1, l1, s2, r2):
        block = _weight_blocks(fin_ref, fout_ref)
        first = _gather_phase1([win_ref, wout_ref], block, s1, r1, l1)
        _start_all(first)
        _wait_all(first)
        second = _gather_phase2(block, s2, r2)
        _start_all(second)
        _wait_all(second)

    return pl.pallas_call(
        body, name="allgather_weights", in_specs=[HBM, HBM], out_specs=[HBM, HBM], out_shape=_full_weight_shapes(),
        scratch_shapes=GATHER1_SEMS + GATHER2_SEMS, compiler_params=_params(),
    )(w_in_b, w_out_b)


def gather_finish(full_in, full_out, name):
    def body(fin_in, fout_in, fin_ref, fout_ref, send_sems, recv_sems):
        copies = _gather_phase2(_weight_blocks(fin_ref, fout_ref), send_sems, recv_sems)
        _start_all(copies)
        _wait_all(copies)

    return pl.pallas_call(
        body, name=name, in_specs=[HBM, HBM], out_specs=[HBM, HBM], out_shape=_full_weight_shapes(),
        input_output_aliases={0: 0, 1: 1}, scratch_shapes=GATHER2_SEMS, compiler_params=_params(),
    )(full_in, full_out)


def reduce_first(dw_in, dw_out, name):
    def body(ga_ref, gb_ref, ra_ref, rb_ref, send_sems, recv_sems):
        copies = _reduce_phase1([ga_ref, gb_ref], [ra_ref, rb_ref], send_sems, recv_sems)
        _start_all(copies)
        _wait_all(copies)

    return pl.pallas_call(
        body, name=name, in_specs=[HBM, HBM], out_specs=[HBM, HBM], out_shape=_landing_shapes(4),
        scratch_shapes=REDUCE1_SEMS, compiler_params=_params(),
    )(*_split_cores((dw_in, dw_out)))


def reduce_second(sum_in, sum_out, name):
    def body(pa_ref, pb_ref, ra_ref, rb_ref, send_sems, recv_sems):
        copies = _reduce_phase2([pa_ref, pb_ref], [ra_ref, rb_ref], send_sems, recv_sems)
        _start_all(copies)
        _wait_all(copies)

    return pl.pallas_call(
        body, name=name, in_specs=[HBM, HBM], out_specs=[HBM, HBM], out_shape=_landing_shapes(3),
        scratch_shapes=REDUCE2_SEMS, compiler_params=_params(),
    )(sum_in, sum_out)


def _one_class(a):
    return a.reshape((1,) + a.shape)


def layer_fwd(x, g, scale, shift, gate, w_in, w_out, lg, gather=None):
    proj, h, qkv4, qkv16, *began = inproj_fwd(x, g, 1.0 + scale, shift, w_in, gather)
    next_weights = gather_finish(*began, "gather_finish") if gather is not None else None
    qkv = (_one_class(proj), qkv4, qkv16)
    attn_outs = [attn_fwd(arr, dil) for dil, arr in zip(DILATIONS, qkv)]
    attn_outs[0] = tuple(a[0] for a in attn_outs[0])
    o_pre, y_r = ret_fwd(proj, lg)
    x_new, o_a, lse, lse4, lse16 = outproj_fwd(x, gate, w_out, attn_outs, y_r, proj)
    saved = dict(x=x, proj=proj, h=h, qkv=qkv, o_a=o_a, lse=(_one_class(lse), lse4, lse16), o_pre=o_pre, y_r=y_r)
    return x_new, saved, next_weights


def layer_bwd(dxn, saved, g, scale, gate, w_in, w_out, lg, pos, later=None):
    proj = saved["proj"]
    do_a, delta, dyr, dz_a, dz_r, y, dxb, do4, do16, dl4, dl16, *landed = outproj_bwd(
        dxn, gate, w_out, saved["o_a"], saved["y_r"], proj, later)
    sums = None
    if later is not None:
        sums = [chip_sum(pos, g4, r, "chip_sum") for g4, r in zip(_split_cores(later), landed)]
    dw_out, dgate = wout_grad(y, dxb, gate, w_out)
    dq_r, dk_r, dv_r, glf, glb = ret_bwd(proj, lg, saved["o_pre"], dyr)
    dos, deltas = (_one_class(do_a), do4, do16), (_one_class(delta), dl4, dl16)
    dqkv_a = [attn_bwd(arr, d_o, lse, dl, dil)
              for dil, arr, d_o, lse, dl in zip(DILATIONS, saved["qkv"], dos, saved["lse"], deltas)]
    dqkv_a[0] = [t[0] for t in dqkv_a[0]]
    dproj = assemble_dproj(dqkv_a, dz_a, dq_r, dk_r, dv_r, dz_r)
    dx, stats, *others = inproj_bwd(dproj, w_in, saved["x"], g, 1.0 + scale, dxn, sums)
    dw_in = win_grad(saved["h"], dproj)
    dlg = jnp.concatenate([glf[:, 0, 0], glb[:, 0, 0]])
    reduced = (sums, others) if later is not None else None
    return dx, (dw_in, dw_out), stats[0:1], stats[1:2], dgate, stats[2:3], dlg, reduced


ROWS_B_ADA = DEPTH * 3 * D_MODEL // 128
ROWS_GAIN = DEPTH * D_MODEL // 128
ROWS_FINAL = D_MODEL // 128
ROWS_MISC = 8
ROWS_SMALL = ROWS_B_ADA + ROWS_GAIN + ROWS_FINAL + ROWS_MISC


def _pack_small(b_ada_like, gain_like, final_like, dec_f, dec_b, loss=None):
    misc = jnp.zeros((ROWS_MISC, 128), F32)
    misc = misc.at[0, :2 * DEPTH * RET_HEADS].set(jnp.concatenate([dec_f.reshape(-1), dec_b.reshape(-1)]))
    if loss is not None:
        misc = misc.at[1, 0].set(loss)
    return jnp.concatenate([b_ada_like.reshape(ROWS_B_ADA, 128), gain_like.reshape(ROWS_GAIN, 128),
                            final_like.reshape(ROWS_FINAL, 128), misc], axis=0)


def _unpack_small(p):
    r0, r1, r2 = ROWS_B_ADA, ROWS_B_ADA + ROWS_GAIN, ROWS_B_ADA + ROWS_GAIN + ROWS_FINAL
    n = DEPTH * RET_HEADS
    return (p[:r0].reshape(DEPTH, 3 * D_MODEL), p[r0:r1].reshape(DEPTH, D_MODEL), p[r1:r2].reshape(D_MODEL),
            p[r2, :n].reshape(DEPTH, RET_HEADS), p[r2, n:2 * n].reshape(DEPTH, RET_HEADS))


def kernel(x, c, norm_gain, w_ada, b_ada, w_in, w_out, ret_decay_logit_f, ret_decay_logit_b, final_gain, loss_target, m_norm_gain, m_w_ada, m_b_ada, m_w_in, m_w_out, m_ret_decay_logit_f, m_ret_decay_logit_b, m_final_gain, v_norm_gain, v_w_ada, v_b_ada, v_w_in, v_w_out, v_ret_decay_logit_f, v_ret_decay_logit_b, v_final_gain):
    px, py, pc = _position()
    me = _flat(px, py, pc)
    pos = jnp.stack([px, py, pc]).astype(jnp.int32)
    x2, target = x[0], loss_target[0]

    w_in_b, w_out_b = w_in.astype(BF16), w_out.astype(BF16)
    weights = allgather_weights(w_in_b[0], w_out_b[0])

    c_all = allgather_rows(c.reshape(D_MODEL // 128, 128), "allgather_c").reshape(N_DEV, D_MODEL)
    act, mod_part = ada_fwd(c_all, w_ada)
    mod_all = allgather_rows(mod_part.reshape(-1, 128), "allgather_mod").reshape(N_DEV, DEPTH, N_DEV, W_ADA_SHARD)
    mod = lax.dynamic_index_in_dim(mod_all, me, axis=2, keepdims=False)
    mod = mod.transpose(1, 0, 2).reshape(DEPTH, 3 * D_MODEL) + b_ada
    shift, scale, gate = mod[:, :D_MODEL], mod[:, D_MODEL:2 * D_MODEL], mod[:, 2 * D_MODEL:]

    lg = jnp.concatenate([jax.nn.log_sigmoid(ret_decay_logit_f), jax.nn.log_sigmoid(ret_decay_logit_b)], axis=1)

    h = x2
    saved, layer_weights = [], []
    for l in range(DEPTH):
        layer_weights.append(weights)
        gather = (w_in_b[l + 1], w_out_b[l + 1]) if l + 1 < DEPTH else None
        h, sv, weights = layer_fwd(h, norm_gain[l:l + 1], scale[l:l + 1], shift[l:l + 1], gate[l:l + 1],
                                   *layer_weights[l], lg[l], gather)
        saved.append(sv)
    dh, loss_part, dfinal = loss_head(h, final_gain.reshape(1, D_MODEL), target)

    dmod, dgain, dlg, reduced = [None] * DEPTH, [None] * DEPTH, [None] * DEPTH, [None] * DEPTH
    slabs = None
    for l in reversed(range(DEPTH)):
        dh, slabs, dshift, dscale, dgate, dg, dlg[l], done = layer_bwd(
            dh, saved[l], norm_gain[l:l + 1], scale[l:l + 1], gate[l:l + 1], *layer_weights[l], lg[l], pos, slabs)
        dmod[l] = jnp.concatenate([dshift, dscale, dgate], axis=1)
        dgain[l] = dg
        if done is not None:
            reduced[l + 1] = done
    landed = reduce_first(*slabs, "reduce_first")
    sums = [chip_sum(pos, g4, r, "chip_sum_last") for g4, r in zip(_split_cores(slabs), landed)]
    reduced[0] = (sums, reduce_second(*sums, "reduce_second"))

    dlg = jnp.stack(dlg)
    dlogit_f = dlg[:, :RET_HEADS] * jax.nn.sigmoid(-ret_decay_logit_f)
    dlogit_b = dlg[:, RET_HEADS:] * jax.nn.sigmoid(-ret_decay_logit_b)
    packed = _pack_small(jnp.concatenate(dmod, axis=0), jnp.concatenate(dgain, axis=0), dfinal, dlogit_f, dlogit_b,
                         loss=loss_part[0, 0])
    gathered = allgather_rows(packed, "allgather_small")
    small = small_update(gathered,
                         _pack_small(b_ada, norm_gain, final_gain, ret_decay_logit_f, ret_decay_logit_b),
                         _pack_small(m_b_ada, m_norm_gain, m_final_gain, m_ret_decay_logit_f, m_ret_decay_logit_b),
                         _pack_small(v_b_ada, v_norm_gain, v_final_gain, v_ret_decay_logit_f, v_ret_decay_logit_b))
    loss = small[0][ROWS_B_ADA + ROWS_GAIN + ROWS_FINAL + 1, 0]
    (g_b_ada, g_gain, g_final, g_dec_f, g_dec_b), (d_b_ada, d_gain, d_final, d_dec_f, d_dec_b), \
        (m_b_ada2, m_gain2, m_final2, m_dec_f2, m_dec_b2), (v_b_ada2, v_gain2, v_final2, v_dec_f2, v_dec_b2) = \
        [_unpack_small(p) for p in small]

    dmod_all = gathered[:, :ROWS_B_ADA].reshape(N_DEV, DEPTH, 3 * D_MODEL)
    dmod_mine = lax.dynamic_slice_in_dim(dmod_all, me * W_ADA_SHARD, W_ADA_SHARD, axis=2).transpose(1, 0, 2)
    g_w_ada, d_w_ada, m_w_ada2, v_w_ada2 = ada_update(act.T, dmod_mine, w_ada, m_w_ada, v_w_ada)

    upd_in = [shard_update(reduced[l][0][0], reduced[l][1][0], w_in[l], m_w_in[l], v_w_in[l], f"w_in_update_{l}")
              for l in range(DEPTH)]
    upd_out = [shard_update(reduced[l][0][1], reduced[l][1][1], w_out[l], m_w_out[l], v_w_out[l], f"w_out_update_{l}")
               for l in range(DEPTH)]
    g_w_in, d_w_in, m_w_in2, v_w_in2 = [jnp.stack([u[t] for u in upd_in]) for t in range(4)]
    g_w_out, d_w_out, m_w_out2, v_w_out2 = [jnp.stack([u[t] for u in upd_out]) for t in range(4)]

    return (loss, dh[None],
            g_gain, g_w_ada, g_b_ada, g_w_in, g_w_out, g_dec_f, g_dec_b, g_final,
            d_gain, d_w_ada, d_b_ada, d_w_in, d_w_out, d_dec_f, d_dec_b, d_final,
            m_gain2, m_w_ada2, m_b_ada2, m_w_in2, m_w_out2, m_dec_f2, m_dec_b2, m_final2,
            v_gain2, v_w_ada2, v_b_ada2, v_w_in2, v_w_out2, v_dec_f2, v_dec_b2, v_final2)
```

```python
import functools
import math

import jax
import jax.numpy as jnp
from jax import lax
from jax.experimental import pallas as pl
from jax.experimental.pallas import tpu as pltpu

F32, BF16 = jnp.float32, jnp.bfloat16

D_MODEL = 2048
DEPTH = 4
N_DEV = 8
ATTN_WIDTH = 1024
HEAD_DIM = 128
N_HEADS_ATTN = 8
DILATIONS = (1, 4, 16)
RADIUS = 64
RET_HEADS = 4
RET_QK = 128
RET_V = 256
RET_CHUNK = 128
IN_W = 7168
QKV_A = 3 * ATTN_WIDTH
COL_ZA, COL_QR, COL_KR, COL_VR, COL_ZR = 3072, 4096, 4608, 5120, 6144
W_IN_SHARD = IN_W // N_DEV
W_OUT_SHARD = D_MODEL // N_DEV
W_ADA_SHARD = 3 * D_MODEL // N_DEV
NORM_EPS = 1e-6
MASK_VALUE = -1e30
ATTN_SCALE = HEAD_DIM ** -0.5
RET_SCALE = RET_QK ** -0.5
LN2 = math.log(2.0)

ADAM_LR, ADAM_B1, ADAM_B2, ADAM_EPS, ADAM_WD, ADAM_STEP = 0.001, 0.9, 0.999, 1e-08, 0.01, 10
ADAM_C1 = 1.0 / (1.0 - ADAM_B1 ** ADAM_STEP)
ADAM_C2 = 1.0 / (1.0 - ADAM_B2 ** ADAM_STEP)

VMEM_LIMIT_BYTES = 56 * 1024 * 1024
MESH = pl.DeviceIdType.MESH


def _params(*sem):
    return pltpu.CompilerParams(dimension_semantics=sem if sem else None, vmem_limit_bytes=VMEM_LIMIT_BYTES)


def _dot(a, b):
    return jnp.dot(a, b, preferred_element_type=F32)


def _dot_nt(a, b):
    return lax.dot_general(a, b, (((1,), (1,)), ((), ())), preferred_element_type=F32)


def _dot_tn(a, b):
    return lax.dot_general(a, b, (((0,), (0,)), ((), ())), preferred_element_type=F32)


def _iota(shape, dim):
    return lax.broadcasted_iota(jnp.int32, shape, dim)


def _sigmoid(z):
    return 1.0 / (1.0 + jnp.exp(-z))


HBM = pl.BlockSpec(memory_space=pl.ANY)


def _position():
    return lax.axis_index("x"), lax.axis_index("y"), lax.axis_index("c")


def _flat(px, py, pc):
    return 4 * px + 2 * py + pc


def _remote(src, dst, send_sem, recv_sem, to):
    return pltpu.make_async_remote_copy(src_ref=src, dst_ref=dst, send_sem=send_sem, recv_sem=recv_sem,
                                        device_id=to, device_id_type=MESH)


def _weight_blocks(fin_ref, fout_ref):
    def block(a, idx):
        if a == 0:
            return fin_ref.at[:, pl.ds(pl.multiple_of(idx * W_IN_SHARD, 128), W_IN_SHARD)]
        return fout_ref.at[pl.ds(pl.multiple_of(idx * W_OUT_SHARD, W_OUT_SHARD), W_OUT_SHARD), :]
    return block


GATHER1_SEMS = [pltpu.SemaphoreType.DMA((8,)), pltpu.SemaphoreType.DMA((8,)), pltpu.SemaphoreType.DMA((2,))]
GATHER2_SEMS = [pltpu.SemaphoreType.DMA((6,)), pltpu.SemaphoreType.DMA((6,))]
REDUCE1_SEMS = [pltpu.SemaphoreType.DMA((8,)), pltpu.SemaphoreType.DMA((8,)), pltpu.SemaphoreType.DMA((8,))]
REDUCE2_SEMS = [pltpu.SemaphoreType.DMA((6,)), pltpu.SemaphoreType.DMA((6,))]


def _gather_phase1(srcs, block, send_sems, recv_sems, local_sems):
    x, y, c = _position()
    mine = [block(a, _flat(x, y, c)) for a in range(2)]
    copies = [pltpu.make_async_copy(srcs[a], mine[a], local_sems.at[a]) for a in range(2)]
    for a in range(2):
        copies.append(_remote(srcs[a], mine[a], send_sems.at[4 * a], recv_sems.at[4 * a], (x, y, 1 - c)))
        for j, (px, py) in enumerate([(1 - x, y), (x, 1 - y), (1 - x, 1 - y)]):
            copies.append(_remote(srcs[a], mine[a], send_sems.at[4 * a + 1 + j], recv_sems.at[4 * a + 1 + j], (px, py, c)))
    return copies


def _gather_phase2(block, send_sems, recv_sems):
    x, y, c = _position()
    copies = []
    for a in range(2):
        for j, (px, py) in enumerate([(1 - x, y), (x, 1 - y), (1 - x, 1 - y)]):
            blk = block(a, _flat(px, py, c))
            copies.append(_remote(blk, blk, send_sems.at[3 * a + j], recv_sems.at[3 * a + j], (x, y, 1 - c)))
    return copies


def _reduce_phase1(grads, kept, landings, send_sems, recv_sems, local_sems):
    x, y, c = _position()
    copies = []
    for a, (g, own, land) in enumerate(zip(grads, kept, landings)):
        for k in range(4):
            chip = 2 * (1 - x if k & 2 else x) + (1 - y if k & 1 else y)
            copies.append(pltpu.make_async_copy(g.at[chip, c], own.at[k], local_sems.at[4 * a + k]))
            copies.append(_remote(g.at[chip, 1 - c], land.at[k], send_sems.at[4 * a + k], recv_sems.at[4 * a + k],
                                  (x, y, 1 - c)))
    return copies


def _reduce_phase2(sums, landings, send_sems, recv_sems):
    x, y, c = _position()
    copies = []
    for a, (p, r) in enumerate(zip(sums, landings)):
        for k in (1, 2, 3):
            to = (1 - x if k & 2 else x, 1 - y if k & 1 else y, c)
            copies.append(_remote(p.at[k], r.at[k - 1], send_sems.at[3 * a + k - 1], recv_sems.at[3 * a + k - 1], to))
    return copies


def _landing_shapes(n):
    return [jax.ShapeDtypeStruct((n, D_MODEL, W_IN_SHARD), BF16), jax.ShapeDtypeStruct((n, W_OUT_SHARD, D_MODEL), BF16)]


def _split_cores(slabs):
    return tuple(s.reshape((N_DEV // 2, 2) + s.shape[1:]) for s in slabs)


def _start_all(copies):
    for cp in copies:
        cp.start()


def _wait_all(copies):
    for cp in copies:
        cp.wait()


def _grid_edge(n_axes):
    first = last = None
    for ax in range(n_axes):
        f = pl.program_id(ax) == 0
        e = pl.program_id(ax) == pl.num_programs(ax) - 1
        first = f if first is None else first & f
        last = e if last is None else last & e
    return first, last


LANES = 128


def _stage_shape(rows, width):
    return pltpu.VMEM((width // LANES, rows, LANES), F32)


def _fill_stage(stage_ref, value):
    for t in range(stage_ref.shape[0]):
        stage_ref[t] = value[:, t * LANES:(t + 1) * LANES]


def _read_stage(stage_ref):
    return jnp.concatenate([stage_ref[t] for t in range(stage_ref.shape[0])], axis=1)


def _split_classes(stage_ref, out_refs):
    n_t, rows, _ = stage_ref.shape
    for out_ref in out_refs:
        dil = out_ref.shape[0]
        for r in range(dil):
            for t in range(n_t):
                piece = stage_ref[t, pl.ds(r, rows // dil, stride=dil), :]
                out_ref[r, :, t * LANES:(t + 1) * LANES] = piece.astype(out_ref.dtype)


def _merge_classes(in_ref, stage_ref):
    dil, per = in_ref.shape[0], in_ref.shape[1]
    for r in range(dil):
        for t in range(stage_ref.shape[0]):
            stage_ref[t, pl.ds(r, per, stride=dil), :] = in_ref[r, :, t * LANES:(t + 1) * LANES].astype(F32)


def _class_block(dil, tm, width, index_map):
    return pl.BlockSpec((dil, tm // dil, width), index_map)


def inproj_fwd(x, g, scale1p, shift, w, gather=None):
    s_len = x.shape[0]
    tm, tn = min(512, s_len), 1024

    n_qkv = QKV_A // tn

    def body(x_ref, g_ref, sc_ref, sh_ref, w_ref, *rest):
        if gather is not None:
            (win_ref, wout_ref, proj_ref, h_ref, q4_ref, q16_ref, fin_ref, fout_ref, stage,
             send_sems, recv_sems, local_sems) = rest
            first, last = _grid_edge(2)
            copies = _gather_phase1([win_ref.at[gather[2]], wout_ref.at[gather[2]]], _weight_blocks(fin_ref, fout_ref),
                                    send_sems, recv_sems, local_sems)
            pl.when(first)(lambda: _start_all(copies))
        else:
            proj_ref, h_ref, q4_ref, q16_ref, stage = rest

        @pl.when(pl.program_id(1) == 0)
        def _():
            xv = x_ref[...]
            r = lax.rsqrt(jnp.mean(xv * xv, axis=-1, keepdims=True) + NORM_EPS)
            h_ref[...] = ((xv * r * g_ref[...]) * sc_ref[...] + sh_ref[...]).astype(BF16)
        res = _dot(h_ref[...], w_ref[...])
        proj_ref[...] = res.astype(BF16)

        @pl.when(pl.program_id(1) < n_qkv)
        def _():
            _fill_stage(stage, res)
            _split_classes(stage, [q4_ref, q16_ref])
        if gather is not None:
            pl.when(last)(lambda: _wait_all(copies))

    vec = pl.BlockSpec((1, D_MODEL), lambda i, j: (0, 0))
    in_specs = [pl.BlockSpec((tm, D_MODEL), lambda i, j: (i, 0)), vec, vec, vec,
                pl.BlockSpec((D_MODEL, tn), lambda i, j: (0, j))]
    out_specs = [pl.BlockSpec((tm, tn), lambda i, j: (i, j)), pl.BlockSpec((tm, D_MODEL), lambda i, j: (i, 0))]
    out_shape = [jax.ShapeDtypeStruct((s_len, IN_W), BF16), jax.ShapeDtypeStruct((s_len, D_MODEL), BF16)]
    for dil in DILATIONS[1:]:
        out_specs.append(pl.BlockSpec((dil, tm // dil, tn), lambda i, j: (0, i, jnp.minimum(j, n_qkv - 1))))
        out_shape.append(jax.ShapeDtypeStruct((dil, s_len // dil, QKV_A), BF16))
    scratch = [_stage_shape(tm, tn)]
    extra = ()
    if gather is not None:
        in_specs += [HBM, HBM]
        out_specs += [HBM, HBM]
        out_shape += [jax.ShapeDtypeStruct((D_MODEL, IN_W), BF16), jax.ShapeDtypeStruct((D_MODEL, D_MODEL), BF16)]
        scratch += GATHER1_SEMS
        extra = tuple(gather[:2])
    return pl.pallas_call(
        body, name="inproj_fwd_gather" if gather is not None else "inproj_fwd", grid=(s_len // tm, IN_W // tn),
        in_specs=in_specs, out_specs=out_specs, out_shape=out_shape, scratch_shapes=scratch,
        compiler_params=_params("arbitrary", "arbitrary"),
    )(x, g, scale1p, shift, w, *extra)


MASK_DISTANCE = 1e33
ATTN_TILE = 128


ATTN_UNROLL_FWD, ATTN_UNROLL_BWD = 10, 6


def _attn_plan(sub_len, unroll):
    tq = min(sub_len, ATTN_TILE)
    win = min(sub_len, tq + 2 * RADIUS)
    heads = 1 if sub_len > 1024 else (2 if sub_len > 256 else N_HEADS_ATTN)
    return tq, win, sub_len // tq, heads, unroll


def _attn_tiles(sub_len, tq, win, n_tiles, unroll, tile):
    tile(0, 0, 0)
    if n_tiles > 2:
        def mid(i, carry):
            q0 = pl.multiple_of(i * tq, tq)
            tile(q0, pl.multiple_of(q0 - RADIUS, RADIUS), 1)
            return carry
        lax.fori_loop(1, n_tiles - 1, mid, 0, unroll=min(unroll, n_tiles - 2))
    if n_tiles > 1:
        tile(sub_len - tq, sub_len - win, 2)


def _attn_bias(bias_ref, head, heads, tq, win, dil):
    h = pl.program_id(1) * heads + head
    slope = jnp.exp(-(h + 1).astype(F32) * LN2 * jnp.ones((1, 1), F32))
    rel = _iota((tq, win), 1) - _iota((tq, win), 0)
    for v, off in enumerate((0, RADIUS, win - tq)):
        dist = jnp.abs(rel - off)
        bias_ref[v] = slope * jnp.where(dist <= RADIUS, (dist * dil).astype(F32), MASK_DISTANCE)


def _attn_specs(n_cls, sub_len, heads):
    width = heads * HEAD_DIM
    per = ATTN_WIDTH // width

    def col(part):
        return pl.BlockSpec((None, sub_len, width), lambda r, g: (r, 0, part * per + g))
    return col, (n_cls, N_HEADS_ATTN // heads)


def attn_fwd(qkv, dil):
    n_cls, sub_len, _ = qkv.shape
    tq, win, n_tiles, heads, unroll = _attn_plan(sub_len, ATTN_UNROLL_FWD)

    def body(q_ref, k_ref, v_ref, o_ref, lse_ref, bias_ref):
        for head in range(heads):
            lanes = slice(head * HEAD_DIM, (head + 1) * HEAD_DIM)
            _attn_bias(bias_ref, head, heads, tq, win, dil)

            def tile(q0, start, variant):
                s = _dot_nt(q_ref[pl.ds(q0, tq), lanes], k_ref[pl.ds(start, win), lanes]) * ATTN_SCALE - bias_ref[variant]
                m = jnp.max(s, axis=1, keepdims=True)
                p = jnp.exp(s - m)
                den = jnp.sum(p, axis=1, keepdims=True)
                o_ref[pl.ds(q0, tq), lanes] = _dot(p.astype(BF16), v_ref[pl.ds(start, win), lanes]) / den
                lse_ref[pl.ds(q0, tq), lanes] = jnp.broadcast_to(m + jnp.log(den), (tq, HEAD_DIM))

            _attn_tiles(sub_len, tq, win, n_tiles, unroll, tile)

    col, grid = _attn_specs(n_cls, sub_len, heads)
    out = jax.ShapeDtypeStruct((n_cls, sub_len, ATTN_WIDTH), F32)
    return pl.pallas_call(
        body, name=f"attn_fwd_d{dil}", grid=grid,
        in_specs=[col(0), col(1), col(2)], out_specs=[col(0), col(0)], out_shape=[out, out],
        scratch_shapes=[pltpu.VMEM((3, tq, win), F32)],
        compiler_params=_params("arbitrary", "arbitrary"),
    )(qkv, qkv, qkv)


def attn_bwd(qkv, do, lse, delta, dil):
    n_cls, sub_len, _ = qkv.shape
    tq, win, n_tiles, heads, unroll = _attn_plan(sub_len, ATTN_UNROLL_BWD)

    def body(q_ref, k_ref, v_ref, do_ref, lse_ref, dl_ref, dq_ref, dk_ref, dv_ref, bias_ref, dk_acc, dv_acc):
        for head in range(heads):
            lanes = slice(head * HEAD_DIM, (head + 1) * HEAD_DIM)
            _attn_bias(bias_ref, head, heads, tq, win, dil)
            dk_acc[...] = jnp.zeros_like(dk_acc)
            dv_acc[...] = jnp.zeros_like(dv_acc)

            def tile(q0, start, variant):
                q = q_ref[pl.ds(q0, tq), lanes]
                k = k_ref[pl.ds(start, win), lanes]
                v = v_ref[pl.ds(start, win), lanes]
                dov = do_ref[pl.ds(q0, tq), lanes]
                s = _dot_nt(q, k) * ATTN_SCALE - bias_ref[variant]
                p = jnp.exp(s - lse_ref[pl.ds(q0, tq), head * HEAD_DIM:head * HEAD_DIM + 1])
                ds = (p * (_dot_nt(dov, v) - dl_ref[pl.ds(q0, tq), head * HEAD_DIM:head * HEAD_DIM + 1])).astype(BF16)
                dq_ref[pl.ds(q0, tq), lanes] = (_dot(ds, k) * ATTN_SCALE).astype(BF16)
                dk_acc[pl.ds(start, win), :] += _dot_tn(ds, q) * ATTN_SCALE
                dv_acc[pl.ds(start, win), :] += _dot_tn(p.astype(BF16), dov)

            _attn_tiles(sub_len, tq, win, n_tiles, unroll, tile)
            dk_ref[:, lanes] = dk_acc[...].astype(BF16)
            dv_ref[:, lanes] = dv_acc[...].astype(BF16)

    col, grid = _attn_specs(n_cls, sub_len, heads)
    out = jax.ShapeDtypeStruct((n_cls, sub_len, ATTN_WIDTH), BF16)
    return pl.pallas_call(
        body, name=f"attn_bwd_d{dil}", grid=grid,
        in_specs=[col(0), col(1), col(2), col(0), col(0), col(0)],
        out_specs=[col(0), col(0), col(0)], out_shape=[out, out, out],
        scratch_shapes=[pltpu.VMEM((3, tq, win), F32), pltpu.VMEM((sub_len, HEAD_DIM), F32),
                        pltpu.VMEM((sub_len, HEAD_DIM), F32)],
        compiler_params=_params("arbitrary", "arbitrary"),
    )(qkv, qkv, qkv, do, lse, delta)


RET_UNROLL = 8
RET_UNROLL_BWD = 4


def _ret_tables(lg_ref):
    h = pl.program_id(0)
    one = jnp.ones((1, 1), F32)
    lgf, lgb = lg_ref[h] * one, lg_ref[RET_HEADS + h] * one
    c = RET_CHUNK
    rel = (_iota((c, c), 0) - _iota((c, c), 1)).astype(F32)
    dec_f = jnp.where(rel >= 0, jnp.exp(jnp.maximum(rel, 0.0) * lgf), 0.0)
    dec_b = jnp.where(rel <= 0, jnp.exp(jnp.maximum(-rel, 0.0) * lgb), 0.0)
    ci = _iota((c, 1), 0).astype(F32)
    tab = dict(rel=rel, dec_f=dec_f, dec_b=dec_b, ci=ci,
               xi_f=jnp.exp((ci + 1.0) * lgf), ze_f=jnp.exp((c - 1.0 - ci) * lgf), g_f=jnp.exp(c * lgf),
               xi_b=jnp.exp((c - ci) * lgb), ze_b=jnp.exp(ci * lgb), g_b=jnp.exp(c * lgb))
    return tab


def _ret_specs(s_len):
    q = pl.BlockSpec((s_len, RET_QK), lambda h: (0, COL_QR // RET_QK + h))
    k = pl.BlockSpec((s_len, RET_QK), lambda h: (0, COL_KR // RET_QK + h))
    v = pl.BlockSpec((s_len, RET_V), lambda h: (0, COL_VR // RET_V + h))
    wide = pl.BlockSpec((s_len, RET_V), lambda h: (0, h))
    narrow = pl.BlockSpec((s_len, RET_QK), lambda h: (0, h))
    smem = pl.BlockSpec(memory_space=pltpu.SMEM)
    return smem, q, k, v, wide, narrow


def ret_fwd(proj, lg):
    s_len = proj.shape[0]
    c, n_chunks = RET_CHUNK, proj.shape[0] // RET_CHUNK

    def body(lg_ref, q_ref, k_ref, v_ref, opre_ref, y_ref, st_f, st_b):
        t = _ret_tables(lg_ref)
        dec = t["dec_f"] + t["dec_b"]
        st_f[...] = jnp.zeros_like(st_f)
        st_b[...] = jnp.zeros_like(st_b)

        def load(n):
            r0 = pl.multiple_of(n * c, c)
            q, k, v = q_ref[pl.ds(r0, c), :], k_ref[pl.ds(r0, c), :], v_ref[pl.ds(r0, c), :]
            return r0, q, (k.astype(F32) * RET_SCALE), v

        def fwd(n, carry):
            r0, q, kf, v = load(n)
            inner = (_dot_nt(q, kf.astype(BF16)) * dec).astype(BF16)
            opre_ref[pl.ds(r0, c), :] = _dot(inner, v) + _dot(q, st_f[...].astype(BF16)) * t["xi_f"]
            st_f[...] = st_f[...] * t["g_f"] + _dot_tn((kf * t["ze_f"]).astype(BF16), v)
            return carry

        def bwd(i, carry):
            r0, q, kf, v = load(n_chunks - 1 - i)
            o = opre_ref[pl.ds(r0, c), :] + _dot(q, st_b[...].astype(BF16)) * t["xi_b"]
            st_b[...] = st_b[...] * t["g_b"] + _dot_tn((kf * t["ze_b"]).astype(BF16), v)
            opre_ref[pl.ds(r0, c), :] = o
            y_ref[pl.ds(r0, c), :] = o * lax.rsqrt(jnp.mean(o * o, axis=-1, keepdims=True) + NORM_EPS)
            return carry

        lax.fori_loop(0, n_chunks, fwd, 0, unroll=min(RET_UNROLL, n_chunks))
        lax.fori_loop(0, n_chunks, bwd, 0, unroll=min(RET_UNROLL, n_chunks))

    smem, q, k, v, wide, _ = _ret_specs(s_len)
    out = jax.ShapeDtypeStruct((s_len, RET_HEADS * RET_V), F32)
    return pl.pallas_call(
        body, name="ret_fwd", grid=(RET_HEADS,), in_specs=[smem, q, k, v], out_specs=[wide, wide],
        out_shape=[out, out], scratch_shapes=[pltpu.VMEM((RET_QK, RET_V), F32), pltpu.VMEM((RET_QK, RET_V), F32)],
        compiler_params=_params("arbitrary"),
    )(lg, proj, proj, proj)


def ret_bwd(proj, lg, o_pre, dy):
    s_len = proj.shape[0]
    c, n_chunks = RET_CHUNK, proj.shape[0] // RET_CHUNK
    cf = float(c)

    def body(lg_ref, q_ref, k_ref, v_ref, o_ref, dy_ref, dq_ref, dk_ref, dv_ref, glf_ref, glb_ref,
             st_f, dst_b, st_b, dst_f, keep_sf, keep_dtb, acc_f, acc_b, acc_sf, acc_sb):
        t = _ret_tables(lg_ref)
        dec = t["dec_f"] + t["dec_b"]
        e_f, e_b, ci = t["rel"] * t["dec_f"], -t["rel"] * t["dec_b"], t["ci"]
        for ref in (st_f, dst_b, st_b, dst_f, acc_f, acc_b, acc_sf, acc_sb):
            ref[...] = jnp.zeros_like(ref)

        def load(n):
            r0 = pl.multiple_of(n * c, c)
            q, k, v = q_ref[pl.ds(r0, c), :], k_ref[pl.ds(r0, c), :], v_ref[pl.ds(r0, c), :]
            o, dyv = o_ref[pl.ds(r0, c), :], dy_ref[pl.ds(r0, c), :]
            rr = lax.rsqrt(jnp.mean(o * o, axis=-1, keepdims=True) + NORM_EPS)
            y = o * rr
            do = (rr * (dyv - y * jnp.mean(dyv * y, axis=-1, keepdims=True))).astype(BF16)
            return r0, q, k.astype(F32) * RET_SCALE, v, do

        def fwd(n, carry):
            r0, q, kf, v, do = load(n)
            qf, kb = q.astype(F32), kf.astype(BF16)
            a = _dot_nt(q, kb)
            b = _dot_nt(do, v)
            da = (b * dec).astype(BF16)
            ab = a * b
            sf_b, dtb_b = st_f[...].astype(BF16), dst_b[...].astype(BF16)
            dq_inter = _dot_nt(do, sf_b) * t["xi_f"]
            dk_inter = _dot_nt(v, dtb_b) * t["ze_b"]
            acc_f[...] += e_f * ab + (ci + 1.0) * (qf * dq_inter)
            acc_b[...] += e_b * ab + ci * (kf * dk_inter)
            dq_ref[pl.ds(r0, c), :] = _dot(da, kb) + dq_inter
            dk_ref[pl.ds(r0, c), :] = _dot_tn(da, q) + dk_inter
            dv_ref[pl.ds(r0, c), :] = _dot_tn((a * dec).astype(BF16), do) + _dot((kf * t["ze_b"]).astype(BF16), dtb_b)
            keep_sf[n] = sf_b
            keep_dtb[n] = dtb_b
            st_f[...] = st_f[...] * t["g_f"] + _dot_tn((kf * t["ze_f"]).astype(BF16), v)
            dst_b[...] = dst_b[...] * t["g_b"] + _dot_tn((qf * t["xi_b"]).astype(BF16), do)
            return carry

        def bwd(i, carry):
            n = n_chunks - 1 - i
            r0, q, kf, v, do = load(n)
            qf = q.astype(F32)
            tb_b, dsf_b = st_b[...].astype(BF16), dst_f[...].astype(BF16)
            dq_inter = _dot_nt(do, tb_b) * t["xi_b"]
            dk_inter = _dot_nt(v, dsf_b) * t["ze_f"]
            acc_b[...] += (cf - ci) * (qf * dq_inter)
            acc_f[...] += (cf - 1.0 - ci) * (kf * dk_inter)
            acc_sb[...] += keep_dtb[n].astype(F32) * st_b[...]
            acc_sf[...] += dst_f[...] * keep_sf[n].astype(F32)
            dq_ref[pl.ds(r0, c), :] += dq_inter
            dk_ref[pl.ds(r0, c), :] = (dk_ref[pl.ds(r0, c), :] + dk_inter) * RET_SCALE
            dv_ref[pl.ds(r0, c), :] += _dot((kf * t["ze_f"]).astype(BF16), dsf_b)
            st_b[...] = st_b[...] * t["g_b"] + _dot_tn((kf * t["ze_b"]).astype(BF16), v)
            dst_f[...] = dst_f[...] * t["g_f"] + _dot_tn((qf * t["xi_f"]).astype(BF16), do)
            return carry

        lax.fori_loop(0, n_chunks, fwd, 0, unroll=min(RET_UNROLL_BWD, n_chunks))
        lax.fori_loop(0, n_chunks, bwd, 0, unroll=min(RET_UNROLL_BWD, n_chunks))

        def total(x):
            return jnp.sum(jnp.sum(x, axis=1, keepdims=True), axis=0, keepdims=True)

        glf_ref[...] = jnp.broadcast_to(total(acc_f[...]) + cf * t["g_f"] * total(acc_sf[...]), (8, 128))
        glb_ref[...] = jnp.broadcast_to(total(acc_b[...]) + cf * t["g_b"] * total(acc_sb[...]), (8, 128))

    smem, q, k, v, wide, narrow = _ret_specs(s_len)
    scal = pl.BlockSpec((None, 8, 128), lambda h: (h, 0, 0))
    state = pltpu.VMEM((RET_QK, RET_V), F32)
    square = pltpu.VMEM((RET_CHUNK, RET_QK), F32)
    keep = pltpu.VMEM((n_chunks, RET_QK, RET_V), BF16)
    return pl.pallas_call(
        body, name="ret_bwd", grid=(RET_HEADS,), in_specs=[smem, q, k, v, wide, wide],
        out_specs=[narrow, narrow, wide, scal, scal],
        out_shape=[jax.ShapeDtypeStruct((s_len, RET_HEADS * RET_QK), F32), jax.ShapeDtypeStruct((s_len, RET_HEADS * RET_QK), F32),
                   jax.ShapeDtypeStruct((s_len, RET_HEADS * RET_V), F32),
                   jax.ShapeDtypeStruct((RET_HEADS, 8, 128), F32), jax.ShapeDtypeStruct((RET_HEADS, 8, 128), F32)],
        scratch_shapes=[state, state, state, state, keep, keep, square, square, state, state],
        compiler_params=_params("arbitrary"),
    )(lg, proj, proj, proj, o_pre, dy)


def _silu_parts(z):
    sig = _sigmoid(z)
    return z * sig, sig * (1.0 + z * (1.0 - sig))


def outproj_fwd(x, gate, w_out, attn_outs, y_r, proj):
    s_len = x.shape[0]
    tm = min(256, s_len)

    def body(x_ref, gate_ref, w_ref, o1, l1, o2, l2, o3, l3, yr_ref, za_ref, zr_ref,
             xn_ref, oa_ref, lse_ref, lse4_ref, lse16_ref, so2, sl2, so3, sl3):
        for src, dst in ((o2, so2), (l2, sl2), (o3, so3), (l3, sl3)):
            _merge_classes(src, dst)
        la, lb, lc = l1[...], _read_stage(sl2), _read_stage(sl3)
        m = jnp.maximum(jnp.maximum(la, lb), lc)
        lse = m + jnp.log(jnp.exp(la - m) + jnp.exp(lb - m) + jnp.exp(lc - m))
        o_a = jnp.exp(la - lse) * o1[...] + jnp.exp(lb - lse) * _read_stage(so2) + jnp.exp(lc - lse) * _read_stage(so3)
        oa_ref[...] = o_a
        lse_ref[...] = lse
        _fill_stage(sl2, lse)
        _split_classes(sl2, [lse4_ref, lse16_ref])
        silu_a, _ = _silu_parts(za_ref[...].astype(F32))
        silu_r, _ = _silu_parts(zr_ref[...].astype(F32))
        y = jnp.concatenate([(o_a * silu_a).astype(BF16), (yr_ref[...] * silu_r).astype(BF16)], axis=1)
        xn_ref[...] = x_ref[...] + gate_ref[...] * _dot(y, w_ref[...])

    row = lambda w: pl.BlockSpec((tm, w), lambda i: (i, 0))
    half = row(ATTN_WIDTH)
    cls = [_class_block(dil, tm, ATTN_WIDTH, lambda i: (0, i, 0)) for dil in DILATIONS[1:]]
    flat = [a for pair in attn_outs for a in pair]
    sds = jax.ShapeDtypeStruct
    return pl.pallas_call(
        body, name="outproj_fwd", grid=(s_len // tm,),
        in_specs=[row(D_MODEL), pl.BlockSpec((1, D_MODEL), lambda i: (0, 0)),
                  pl.BlockSpec((D_MODEL, D_MODEL), lambda i: (0, 0)), half, half, cls[0], cls[0], cls[1], cls[1], half,
                  pl.BlockSpec((tm, ATTN_WIDTH), lambda i: (i, COL_ZA // ATTN_WIDTH)),
                  pl.BlockSpec((tm, ATTN_WIDTH), lambda i: (i, COL_ZR // ATTN_WIDTH))],
        out_specs=[row(D_MODEL), half, half] + cls,
        out_shape=[sds((s_len, D_MODEL), F32), sds((s_len, ATTN_WIDTH), F32), sds((s_len, ATTN_WIDTH), F32)]
                  + [sds((dil, s_len // dil, ATTN_WIDTH), F32) for dil in DILATIONS[1:]],
        scratch_shapes=[_stage_shape(tm, ATTN_WIDTH)] * 4,
        compiler_params=_params("arbitrary"),
    )(x, gate, w_out, *flat, y_r, proj, proj)


def loss_head(x, gain, target):
    s_len = x.shape[0]
    tm = min(256, s_len)

    def body(x_ref, g_ref, t_ref, dx_ref, loss_ref, dg_ref):
        @pl.when(pl.program_id(0) == 0)
        def _():
            loss_ref[...] = jnp.zeros_like(loss_ref)
            dg_ref[...] = jnp.zeros_like(dg_ref)
        xv, g = x_ref[...], g_ref[...]
        r = lax.rsqrt(jnp.mean(xv * xv, axis=-1, keepdims=True) + NORM_EPS)
        xn = xv * r
        err = xn * g - t_ref[...]
        part = 0.5 * jnp.sum(jnp.mean(err * err, axis=-1, keepdims=True), axis=0, keepdims=True)
        loss_ref[...] += jnp.broadcast_to(part, loss_ref.shape)
        dy = err * (1.0 / D_MODEL)
        dg_ref[...] += jnp.sum(dy * xn, axis=0, keepdims=True)
        dxn = dy * g
        dx_ref[...] = r * (dxn - xn * jnp.mean(dxn * xn, axis=-1, keepdims=True))

    row = pl.BlockSpec((tm, D_MODEL), lambda i: (i, 0))
    vec = pl.BlockSpec((1, D_MODEL), lambda i: (0, 0))
    return pl.pallas_call(
        body, name="loss_head", grid=(s_len // tm,), in_specs=[row, vec, row],
        out_specs=[row, pl.BlockSpec((8, 128), lambda i: (0, 0)), vec],
        out_shape=[jax.ShapeDtypeStruct((s_len, D_MODEL), F32), jax.ShapeDtypeStruct((8, 128), F32),
                   jax.ShapeDtypeStruct((1, D_MODEL), F32)],
        compiler_params=_params("arbitrary"),
    )(x, gain, target)


def outproj_bwd(dxn, gate, w_out, o_a, y_r, proj, exchange=None):
    s_len = dxn.shape[0]
    tm = min(256, s_len)

    def body(dx_ref, gate_ref, w_ref, oa_ref, yr_ref, za_ref, zr_ref, *rest):
        if exchange is not None:
            (ga_ref, gb_ref, doa_ref, dl_ref, dyr_ref, dza_ref, dzr_ref, y_ref, dxb_ref, do4, do16, dl4, dl16,
             ka_ref, kb_ref, ra_ref, rb_ref, stage, send_sems, recv_sems, local_sems) = rest
            first, last = _grid_edge(1)
            copies = _reduce_phase1([ga_ref, gb_ref], [ka_ref, kb_ref], [ra_ref, rb_ref], send_sems, recv_sems, local_sems)
            pl.when(first)(lambda: _start_all(copies))
        else:
            doa_ref, dl_ref, dyr_ref, dza_ref, dzr_ref, y_ref, dxb_ref, do4, do16, dl4, dl16, stage = rest
        dxv = dx_ref[...]
        dxb_ref[...] = dxv.astype(BF16)
        dy = _dot_nt((dxv * gate_ref[...]).astype(BF16), w_ref[...])
        dy_a, dy_r = dy[:, :ATTN_WIDTH], dy[:, ATTN_WIDTH:]
        o_a, y_rv = oa_ref[...], yr_ref[...]
        silu_a, dsilu_a = _silu_parts(za_ref[...].astype(F32))
        silu_r, dsilu_r = _silu_parts(zr_ref[...].astype(F32))
        do_a = dy_a * silu_a
        doa_ref[...] = do_a.astype(BF16)
        _fill_stage(stage, do_a)
        _split_classes(stage, [do4, do16])
        prod = do_a * o_a
        delta = jnp.concatenate(
            [jnp.broadcast_to(jnp.sum(prod[:, h * HEAD_DIM:(h + 1) * HEAD_DIM], axis=1, keepdims=True), (tm, HEAD_DIM))
             for h in range(N_HEADS_ATTN)], axis=1)
        dl_ref[...] = delta
        _fill_stage(stage, delta)
        _split_classes(stage, [dl4, dl16])
        dyr_ref[...] = dy_r * silu_r
        dza_ref[...] = (dy_a * o_a * dsilu_a).astype(BF16)
        dzr_ref[...] = (dy_r * y_rv * dsilu_r).astype(BF16)
        y_ref[...] = jnp.concatenate([(o_a * silu_a).astype(BF16), (y_rv * silu_r).astype(BF16)], axis=1)
        if exchange is not None:
            pl.when(last)(lambda: _wait_all(copies))

    row = lambda w: pl.BlockSpec((tm, w), lambda i: (i, 0))
    half = row(ATTN_WIDTH)
    sds = lambda w, dt: jax.ShapeDtypeStruct((s_len, w), dt)
    in_specs = [row(D_MODEL), pl.BlockSpec((1, D_MODEL), lambda i: (0, 0)),
                pl.BlockSpec((D_MODEL, D_MODEL), lambda i: (0, 0)), half, half,
                pl.BlockSpec((tm, ATTN_WIDTH), lambda i: (i, COL_ZA // ATTN_WIDTH)),
                pl.BlockSpec((tm, ATTN_WIDTH), lambda i: (i, COL_ZR // ATTN_WIDTH))]
    cls = [_class_block(dil, tm, ATTN_WIDTH, lambda i: (0, i, 0)) for dil in DILATIONS[1:]]
    out_specs = [half, half, half, half, half, row(D_MODEL), row(D_MODEL)] + cls + cls
    out_shape = [sds(ATTN_WIDTH, BF16), sds(ATTN_WIDTH, F32), sds(ATTN_WIDTH, F32), sds(ATTN_WIDTH, BF16),
                 sds(ATTN_WIDTH, BF16), sds(D_MODEL, BF16), sds(D_MODEL, BF16)]
    out_shape += [jax.ShapeDtypeStruct((dil, s_len // dil, ATTN_WIDTH), dt) for dt in (BF16, F32) for dil in DILATIONS[1:]]
    scratch = [_stage_shape(tm, ATTN_WIDTH)]
    extra = ()
    if exchange is not None:
        in_specs += [HBM, HBM]
        out_specs += [HBM] * 4
        out_shape += _landing_shapes(4) + _landing_shapes(4)
        scratch += REDUCE1_SEMS
        extra = _split_cores(exchange)
    return pl.pallas_call(
        body, name="outproj_bwd_reduce1" if exchange is not None else "outproj_bwd", grid=(s_len // tm,),
        in_specs=in_specs, out_specs=out_specs, out_shape=out_shape, scratch_shapes=scratch,
        compiler_params=_params("arbitrary"),
    )(dxn, gate, w_out, o_a, y_r, proj, proj, *extra)


def wout_grad(y, dxb, gate, w_out):
    s_len = y.shape[0]
    tf, ts = 512, min(512, s_len)

    def body(y_ref, dx_ref, gate_ref, w_ref, dw_ref, dgate_ref, acc):
        f, s = pl.program_id(0), pl.program_id(1)

        @pl.when((f == 0) & (s == 0))
        def _():
            dgate_ref[...] = jnp.zeros_like(dgate_ref)

        @pl.when(s == 0)
        def _():
            acc[...] = jnp.zeros_like(acc)
        acc[...] += _dot_tn(y_ref[...], dx_ref[...])

        @pl.when(s == pl.num_programs(1) - 1)
        def _():
            m = acc[...]
            dw_ref[...] = (m * gate_ref[...]).astype(BF16).reshape(dw_ref.shape)
            dgate_ref[...] += jnp.sum(m * w_ref[...].astype(F32), axis=0, keepdims=True)

    per = tf // W_OUT_SHARD
    return pl.pallas_call(
        body, name="wout_grad", grid=(D_MODEL // tf, s_len // ts),
        in_specs=[pl.BlockSpec((ts, tf), lambda f, s: (s, f)), pl.BlockSpec((ts, D_MODEL), lambda f, s: (s, 0)),
                  pl.BlockSpec((1, D_MODEL), lambda f, s: (0, 0)), pl.BlockSpec((tf, D_MODEL), lambda f, s: (f, 0))],
        out_specs=[pl.BlockSpec((per, W_OUT_SHARD, D_MODEL), lambda f, s: (f, 0, 0)),
                   pl.BlockSpec((1, D_MODEL), lambda f, s: (0, 0))],
        out_shape=[jax.ShapeDtypeStruct((N_DEV, W_OUT_SHARD, D_MODEL), BF16), jax.ShapeDtypeStruct((1, D_MODEL), F32)],
        scratch_shapes=[pltpu.VMEM((tf, D_MODEL), F32)],
        compiler_params=_params("arbitrary", "arbitrary"),
    )(y, dxb, gate, w_out)


def assemble_dproj(dqkv_a, dz_a, dq_r, dk_r, dv_r, dz_r):
    s_len = dz_a.shape[0]
    tm = min(256, s_len)

    def body(*refs):
        pat, (dza, dqr, dkr, dvr, dzr, out, stage) = refs[:9], refs[9:]
        for t in range(3):
            tot = pat[t][...].astype(F32)
            for p in (1, 2):
                _merge_classes(pat[3 * p + t], stage)
                tot = tot + _read_stage(stage)
            out[:, t * ATTN_WIDTH:(t + 1) * ATTN_WIDTH] = tot.astype(BF16)
        out[:, COL_ZA:COL_QR] = dza[...]
        out[:, COL_QR:COL_KR] = dqr[...].astype(BF16)
        out[:, COL_KR:COL_VR] = dkr[...].astype(BF16)
        out[:, COL_VR:COL_ZR] = dvr[...].astype(BF16)
        out[:, COL_ZR:IN_W] = dzr[...]

    row = lambda w: pl.BlockSpec((tm, w), lambda i: (i, 0))
    cls = [_class_block(dil, tm, ATTN_WIDTH, lambda i: (0, i, 0)) for dil in DILATIONS[1:]]
    flat = [dqkv_a[p][t] for p in range(3) for t in range(3)]
    return pl.pallas_call(
        body, name="assemble_dproj", grid=(s_len // tm,),
        in_specs=[row(ATTN_WIDTH)] * 3 + [cls[0]] * 3 + [cls[1]] * 3
                 + [row(ATTN_WIDTH), row(512), row(512), row(ATTN_WIDTH), row(ATTN_WIDTH)],
        out_specs=row(IN_W), out_shape=jax.ShapeDtypeStruct((s_len, IN_W), BF16),
        scratch_shapes=[_stage_shape(tm, ATTN_WIDTH)],
        compiler_params=_params("arbitrary"),
    )(*flat, dz_a, dq_r, dk_r, dv_r, dz_r)


def inproj_bwd(dproj, w, x, g, scale1p, dxn, exchange=None):
    s_len = x.shape[0]
    tm, tk = min(512, s_len), 1024

    def body(dp_ref, w_ref, x_ref, g_ref, sc_ref, dxn_ref, *rest):
        if exchange is not None:
            pa_ref, pb_ref, dx_ref, st_ref, ra_ref, rb_ref, acc, send_sems, recv_sems = rest
            first, last = _grid_edge(2)
            copies = _reduce_phase2([pa_ref, pb_ref], [ra_ref, rb_ref], send_sems, recv_sems)
            pl.when(first)(lambda: _start_all(copies))
        else:
            dx_ref, st_ref, acc = rest
        i, k = pl.program_id(0), pl.program_id(1)

        @pl.when((i == 0) & (k == 0))
        def _():
            st_ref[...] = jnp.zeros_like(st_ref)

        @pl.when(k == 0)
        def _():
            acc[...] = jnp.zeros_like(acc)
        acc[...] += _dot_nt(dp_ref[...], w_ref[...])

        @pl.when(k == pl.num_programs(1) - 1)
        def _():
            dh, xv, g, sc = acc[...], x_ref[...], g_ref[...], sc_ref[...]
            r = lax.rsqrt(jnp.mean(xv * xv, axis=-1, keepdims=True) + NORM_EPS)
            xn = xv * r
            da = dh * sc
            st_ref[0:1, :] += jnp.sum(dh, axis=0, keepdims=True)
            st_ref[1:2, :] += jnp.sum(dh * (xn * g), axis=0, keepdims=True)
            st_ref[2:3, :] += jnp.sum(da * xn, axis=0, keepdims=True)
            dn = da * g
            dx_ref[...] = r * (dn - xn * jnp.mean(dn * xn, axis=-1, keepdims=True)) + dxn_ref[...]

        if exchange is not None:
            pl.when(last)(lambda: _wait_all(copies))

    row = pl.BlockSpec((tm, D_MODEL), lambda i, k: (i, 0))
    vec = pl.BlockSpec((1, D_MODEL), lambda i, k: (0, 0))
    in_specs = [pl.BlockSpec((tm, tk), lambda i, k: (i, k)), pl.BlockSpec((D_MODEL, tk), lambda i, k: (0, k)),
                row, vec, vec, row]
    out_specs = [row, pl.BlockSpec((8, D_MODEL), lambda i, k: (0, 0))]
    out_shape = [jax.ShapeDtypeStruct((s_len, D_MODEL), F32), jax.ShapeDtypeStruct((8, D_MODEL), F32)]
    scratch = [pltpu.VMEM((tm, D_MODEL), F32)]
    extra = ()
    if exchange is not None:
        in_specs += [HBM, HBM]
        out_specs += [HBM, HBM]
        out_shape += _landing_shapes(3)
        scratch += REDUCE2_SEMS
        extra = tuple(exchange)
    return pl.pallas_call(
        body, name="inproj_bwd_reduce2" if exchange is not None else "inproj_bwd", grid=(s_len // tm, IN_W // tk),
        in_specs=in_specs, out_specs=out_specs, out_shape=out_shape, scratch_shapes=scratch,
        compiler_params=_params("arbitrary", "arbitrary"),
    )(dproj, w, x, g, scale1p, dxn, *extra)


def win_grad(h, dproj):
    s_len = h.shape[0]
    ts = min(1024, s_len)

    def body(h_ref, dp_ref, dw_ref, acc):
        s = pl.program_id(1)

        @pl.when(s == 0)
        def _():
            acc[...] = jnp.zeros_like(acc)
        acc[...] += _dot_tn(h_ref[...], dp_ref[...])

        @pl.when(s == pl.num_programs(1) - 1)
        def _():
            dw_ref[...] = acc[...].astype(BF16)

    return pl.pallas_call(
        body, name="win_grad", grid=(N_DEV, s_len // ts),
        in_specs=[pl.BlockSpec((ts, D_MODEL), lambda j, s: (s, 0)), pl.BlockSpec((ts, W_IN_SHARD), lambda j, s: (s, j))],
        out_specs=pl.BlockSpec((None, D_MODEL, W_IN_SHARD), lambda j, s: (j, 0, 0)),
        out_shape=jax.ShapeDtypeStruct((N_DEV, D_MODEL, W_IN_SHARD), BF16),
        scratch_shapes=[pltpu.VMEM((D_MODEL, W_IN_SHARD), F32)],
        compiler_params=_params("arbitrary", "arbitrary"),
    )(h, dproj)


def ada_fwd(c_all, w_ada):
    def body(c_ref, w_ref, act_ref, part_ref):
        cv = c_ref[...]
        act = cv * _sigmoid(cv)
        act_ref[...] = act
        part_ref[...] = _dot(act.astype(BF16), w_ref[...].astype(BF16))

    return pl.pallas_call(
        body, name="ada_fwd", grid=(DEPTH,),
        in_specs=[pl.BlockSpec((N_DEV, D_MODEL), lambda l: (0, 0)),
                  pl.BlockSpec((None, D_MODEL, W_ADA_SHARD), lambda l: (l, 0, 0))],
        out_specs=[pl.BlockSpec((N_DEV, D_MODEL), lambda l: (0, 0)),
                   pl.BlockSpec((None, N_DEV, W_ADA_SHARD), lambda l: (l, 0, 0))],
        out_shape=[jax.ShapeDtypeStruct((N_DEV, D_MODEL), F32), jax.ShapeDtypeStruct((DEPTH, N_DEV, W_ADA_SHARD), F32)],
        compiler_params=_params("arbitrary"),
    )(c_all, w_ada)


def _adamw(w, g, m, v):
    m = ADAM_B1 * m + (1.0 - ADAM_B1) * g
    v = ADAM_B2 * v + (1.0 - ADAM_B2) * (g * g)
    delta = -ADAM_LR * ((m * ADAM_C1) / (jnp.sqrt(v * ADAM_C2) + ADAM_EPS) + ADAM_WD * w)
    return delta, m, v


def ada_update(act_t, dmod, w, m, v):
    tr = 512

    def body(a_ref, d_ref, w_ref, m_ref, v_ref, g_out, dl_out, m_out, v_out):
        a = a_ref[...].astype(BF16).astype(F32)
        d = d_ref[...].astype(BF16).astype(F32)
        g = a[:, 0:1] * d[0:1, :]
        for b in range(1, N_DEV):
            g = g + a[:, b:b + 1] * d[b:b + 1, :]
        g_out[...] = g
        dl_out[...], m_out[...], v_out[...] = _adamw(w_ref[...], g, m_ref[...], v_ref[...])

    blk = pl.BlockSpec((None, tr, W_ADA_SHARD), lambda l, r: (l, r, 0))
    out = jax.ShapeDtypeStruct(w.shape, F32)
    return pl.pallas_call(
        body, name="ada_update", grid=(DEPTH, D_MODEL // tr),
        in_specs=[pl.BlockSpec((tr, N_DEV), lambda l, r: (r, 0)),
                  pl.BlockSpec((None, N_DEV, W_ADA_SHARD), lambda l, r: (l, 0, 0)), blk, blk, blk],
        out_specs=[blk] * 4, out_shape=[out] * 4, compiler_params=_params("arbitrary", "arbitrary"),
    )(act_t, dmod, w, m, v)


def chip_sum(kept, landed, name):
    _, n_rows, n_cols = kept.shape
    tr = min(512, n_rows)

    def body(g_ref, r_ref, out_ref):
        out_ref[...] = (g_ref[...].astype(F32) + r_ref[...].astype(F32)).astype(BF16)

    blk = pl.BlockSpec((None, tr, n_cols), lambda k, r: (k, r, 0))
    return pl.pallas_call(
        body, name=name, grid=(N_DEV // 2, n_rows // tr), in_specs=[blk, blk], out_specs=blk,
        out_shape=jax.ShapeDtypeStruct(kept.shape, BF16), compiler_params=_params("arbitrary", "arbitrary"),
    )(kept, landed)


def shard_update(layer, own, others, w, m, v, prev, name):
    _, n_rows, n_cols = w.shape
    tr = min(256, n_rows)

    def body(own_ref, oth_ref, w_ref, m_ref, v_ref, *rest):
        g_out, dl_out, m_out, v_out = rest[-4:]
        g = own_ref[...].astype(F32)
        for k in range(3):
            g = g + oth_ref[k].astype(F32)
        g_out[...] = g
        dl_out[...], m_out[...], v_out[...] = _adamw(w_ref[...], g, m_ref[...], v_ref[...])

    blk = pl.BlockSpec((None, tr, n_cols), lambda r: (layer, r, 0))
    out = jax.ShapeDtypeStruct(w.shape, F32)
    in_specs = [pl.BlockSpec((None, tr, n_cols), lambda r: (0, r, 0)), pl.BlockSpec((3, tr, n_cols), lambda r: (0, r, 0)),
                blk, blk, blk]
    aliases, extra = {}, ()
    if prev is not None:
        in_specs += [HBM] * 4
        aliases = {5 + t: t for t in range(4)}
        extra = tuple(prev)
    return pl.pallas_call(
        body, name=name, grid=(n_rows // tr,), in_specs=in_specs, out_specs=[blk] * 4, out_shape=[out] * 4,
        input_output_aliases=aliases, compiler_params=_params("arbitrary"),
    )(own, others, w, m, v, *extra)


def small_update(parts, w, m, v):
    def body(p_ref, w_ref, m_ref, v_ref, g_out, dl_out, m_out, v_out):
        g = p_ref[0]
        for k in range(1, N_DEV):
            g = g + p_ref[k]
        g_out[...] = g
        dl_out[...], m_out[...], v_out[...] = _adamw(w_ref[...], g, m_ref[...], v_ref[...])

    out = jax.ShapeDtypeStruct(w.shape, F32)
    return pl.pallas_call(body, name="small_update", out_shape=[out] * 4, compiler_params=_params())(parts, w, m, v)


def _two_level_allgather(srcs, dst_block, send_sems, recv_sems, local_sems):
    x, y, c = _position()
    me, sibling = (x, y, c), (x, y, 1 - c)
    chips = [(1 - x, y), (x, 1 - y), (1 - x, 1 - y)]
    n = len(srcs)

    def copy(a, k, block, to, src=None):
        dst = dst_block(a, _flat(*block))
        return pltpu.make_async_remote_copy(
            src_ref=dst if src is None else src, dst_ref=dst, send_sem=send_sems.at[a * 7 + k],
            recv_sem=recv_sems.at[a * 7 + k], device_id=to, device_id_type=MESH)

    mine = [pltpu.make_async_copy(srcs[a], dst_block(a, _flat(*me)), local_sems.at[a]) for a in range(n)]
    for cp in mine:
        cp.start()
    first = []
    for a in range(n):
        first.append(copy(a, 0, me, sibling, src=srcs[a]))
        first += [copy(a, 1 + j, me, (*chip, c), src=srcs[a]) for j, chip in enumerate(chips)]
    for cp in first:
        cp.start()
    passed = []
    for j, chip in enumerate(chips):
        for a in range(n):
            copy(a, 1 + j, (*chip, c), me).wait_recv()
            fwd = copy(a, 4 + j, (*chip, c), sibling)
            fwd.start()
            passed.append(fwd)
    for a in range(n):
        copy(a, 0, sibling, me).wait_recv()
        for j, chip in enumerate(chips):
            copy(a, 4 + j, (*chip, 1 - c), me).wait_recv()
    for cp in first + passed:
        cp.wait_send()
    for cp in mine:
        cp.wait()


def allgather_rows(x, name):
    def body(x_ref, out_ref, send_sems, recv_sems, local_sems):
        _two_level_allgather([x_ref], lambda a, idx: out_ref.at[idx], send_sems, recv_sems, local_sems)

    vmem = pl.BlockSpec(memory_space=pltpu.VMEM)
    return pl.pallas_call(
        body, name=name, in_specs=[vmem], out_specs=vmem,
        out_shape=jax.ShapeDtypeStruct((N_DEV,) + x.shape, x.dtype),
        scratch_shapes=[pltpu.SemaphoreType.DMA((7,)), pltpu.SemaphoreType.DMA((7,)), pltpu.SemaphoreType.DMA((1,))],
        compiler_params=_params(),
    )(x)


def _full_weight_shapes():
    return [jax.ShapeDtypeStruct((D_MODEL, IN_W), BF16), jax.ShapeDtypeStruct((D_MODEL, D_MODEL), BF16)]


def allgather_weights(w_in_b, w_out_b):
    def body(win_ref, wout_ref, fin_ref, fout_ref, s1, r1, l1, s2, r2):
        block = _weight_blocks(fin_ref, fout_ref)
        first = _gather_phase1([win_ref.at[0], wout_ref.at[0]], block, s1, r1, l1)
        _start_all(first)
        _wait_all(first)
        second = _gather_phase2(block, s2, r2)
        _start_all(second)
        _wait_all(second)

    return pl.pallas_call(
        body, name="allgather_weights", in_specs=[HBM, HBM], out_specs=[HBM, HBM], out_shape=_full_weight_shapes(),
        scratch_shapes=GATHER1_SEMS + GATHER2_SEMS, compiler_params=_params(),
    )(w_in_b, w_out_b)


def gather_finish(full_in, full_out, name):
    def body(fin_in, fout_in, fin_ref, fout_ref, send_sems, recv_sems):
        copies = _gather_phase2(_weight_blocks(fin_ref, fout_ref), send_sems, recv_sems)
        _start_all(copies)
        _wait_all(copies)

    return pl.pallas_call(
        body, name=name, in_specs=[HBM, HBM], out_specs=[HBM, HBM], out_shape=_full_weight_shapes(),
        input_output_aliases={0: 0, 1: 1}, scratch_shapes=GATHER2_SEMS, compiler_params=_params(),
    )(full_in, full_out)


def reduce_first(dw_in, dw_out, name):
    def body(ga_ref, gb_ref, ka_ref, kb_ref, ra_ref, rb_ref, send_sems, recv_sems, local_sems):
        copies = _reduce_phase1([ga_ref, gb_ref], [ka_ref, kb_ref], [ra_ref, rb_ref], send_sems, recv_sems, local_sems)
        _start_all(copies)
        _wait_all(copies)

    return pl.pallas_call(
        body, name=name, in_specs=[HBM, HBM], out_specs=[HBM] * 4, out_shape=_landing_shapes(4) + _landing_shapes(4),
        scratch_shapes=REDUCE1_SEMS, compiler_params=_params(),
    )(*_split_cores((dw_in, dw_out)))


def reduce_second(sum_in, sum_out, name):
    def body(pa_ref, pb_ref, ra_ref, rb_ref, send_sems, recv_sems):
        copies = _reduce_phase2([pa_ref, pb_ref], [ra_ref, rb_ref], send_sems, recv_sems)
        _start_all(copies)
        _wait_all(copies)

    return pl.pallas_call(
        body, name=name, in_specs=[HBM, HBM], out_specs=[HBM, HBM], out_shape=_landing_shapes(3),
        scratch_shapes=REDUCE2_SEMS, compiler_params=_params(),
    )(sum_in, sum_out)


def _one_class(a):
    return a.reshape((1,) + a.shape)


def layer_fwd(x, g, scale, shift, gate, w_in, w_out, lg, gather=None):
    proj, h, qkv4, qkv16, *began = inproj_fwd(x, g, 1.0 + scale, shift, w_in, gather)
    next_weights = gather_finish(*began, "gather_finish") if gather is not None else None
    qkv = (_one_class(proj), qkv4, qkv16)
    attn_outs = [attn_fwd(arr, dil) for dil, arr in zip(DILATIONS, qkv)]
    attn_outs[0] = tuple(a[0] for a in attn_outs[0])
    o_pre, y_r = ret_fwd(proj, lg)
    x_new, o_a, lse, lse4, lse16 = outproj_fwd(x, gate, w_out, attn_outs, y_r, proj)
    saved = dict(x=x, proj=proj, h=h, qkv=qkv, o_a=o_a, lse=(_one_class(lse), lse4, lse16), o_pre=o_pre, y_r=y_r)
    return x_new, saved, next_weights


def layer_bwd(dxn, saved, g, scale, gate, w_in, w_out, lg, later=None):
    proj = saved["proj"]
    do_a, delta, dyr, dz_a, dz_r, y, dxb, do4, do16, dl4, dl16, *landed = outproj_bwd(
        dxn, gate, w_out, saved["o_a"], saved["y_r"], proj, later)
    sums = None
    if later is not None:
        sums = [chip_sum(landed[a], landed[2 + a], "chip_sum") for a in range(2)]
    dw_out, dgate = wout_grad(y, dxb, gate, w_out)
    dq_r, dk_r, dv_r, glf, glb = ret_bwd(proj, lg, saved["o_pre"], dyr)
    dos, deltas = (_one_class(do_a), do4, do16), (_one_class(delta), dl4, dl16)
    dqkv_a = [attn_bwd(arr, d_o, lse, dl, dil)
              for dil, arr, d_o, lse, dl in zip(DILATIONS, saved["qkv"], dos, saved["lse"], deltas)]
    dqkv_a[0] = [t[0] for t in dqkv_a[0]]
    dproj = assemble_dproj(dqkv_a, dz_a, dq_r, dk_r, dv_r, dz_r)
    dx, stats, *others = inproj_bwd(dproj, w_in, saved["x"], g, 1.0 + scale, dxn, sums)
    dw_in = win_grad(saved["h"], dproj)
    dlg = jnp.concatenate([glf[:, 0, 0], glb[:, 0, 0]])
    reduced = (sums, others) if later is not None else None
    return dx, (dw_in, dw_out), stats[0:1], stats[1:2], dgate, stats[2:3], dlg, reduced


ROWS_B_ADA = DEPTH * 3 * D_MODEL // 128
ROWS_GAIN = DEPTH * D_MODEL // 128
ROWS_FINAL = D_MODEL // 128
ROWS_MISC = 8
ROWS_SMALL = ROWS_B_ADA + ROWS_GAIN + ROWS_FINAL + ROWS_MISC


def _pack_small(b_ada_like, gain_like, final_like, dec_f, dec_b, loss=None):
    misc = jnp.zeros((ROWS_MISC, 128), F32)
    misc = misc.at[0, :2 * DEPTH * RET_HEADS].set(jnp.concatenate([dec_f.reshape(-1), dec_b.reshape(-1)]))
    if loss is not None:
        misc = misc.at[1, 0].set(loss)
    return jnp.concatenate([b_ada_like.reshape(ROWS_B_ADA, 128), gain_like.reshape(ROWS_GAIN, 128),
                            final_like.reshape(ROWS_FINAL, 128), misc], axis=0)


def _unpack_small(p):
    r0, r1, r2 = ROWS_B_ADA, ROWS_B_ADA + ROWS_GAIN, ROWS_B_ADA + ROWS_GAIN + ROWS_FINAL
    n = DEPTH * RET_HEADS
    return (p[:r0].reshape(DEPTH, 3 * D_MODEL), p[r0:r1].reshape(DEPTH, D_MODEL), p[r1:r2].reshape(D_MODEL),
            p[r2, :n].reshape(DEPTH, RET_HEADS), p[r2, n:2 * n].reshape(DEPTH, RET_HEADS))


def kernel(x, c, norm_gain, w_ada, b_ada, w_in, w_out, ret_decay_logit_f, ret_decay_logit_b, final_gain, loss_target, m_norm_gain, m_w_ada, m_b_ada, m_w_in, m_w_out, m_ret_decay_logit_f, m_ret_decay_logit_b, m_final_gain, v_norm_gain, v_w_ada, v_b_ada, v_w_in, v_w_out, v_ret_decay_logit_f, v_ret_decay_logit_b, v_final_gain):
    px, py, pc = _position()
    me = _flat(px, py, pc)
    x2, target = x[0], loss_target[0]

    w_in_b, w_out_b = w_in.astype(BF16), w_out.astype(BF16)
    weights = allgather_weights(w_in_b, w_out_b)

    c_all = allgather_rows(c.reshape(D_MODEL // 128, 128), "allgather_c").reshape(N_DEV, D_MODEL)
    act, mod_part = ada_fwd(c_all, w_ada)
    mod_all = allgather_rows(mod_part.reshape(-1, 128), "allgather_mod").reshape(N_DEV, DEPTH, N_DEV, W_ADA_SHARD)
    mod = lax.dynamic_index_in_dim(mod_all, me, axis=2, keepdims=False)
    mod = mod.transpose(1, 0, 2).reshape(DEPTH, 3 * D_MODEL) + b_ada
    shift, scale, gate = mod[:, :D_MODEL], mod[:, D_MODEL:2 * D_MODEL], mod[:, 2 * D_MODEL:]

    lg = jnp.concatenate([jax.nn.log_sigmoid(ret_decay_logit_f), jax.nn.log_sigmoid(ret_decay_logit_b)], axis=1)

    h = x2
    saved, layer_weights = [], []
    for l in range(DEPTH):
        layer_weights.append(weights)
        gather = (w_in_b, w_out_b, l + 1) if l + 1 < DEPTH else None
        h, sv, weights = layer_fwd(h, norm_gain[l:l + 1], scale[l:l + 1], shift[l:l + 1], gate[l:l + 1],
                                   *layer_weights[l], lg[l], gather)
        saved.append(sv)
    dh, loss_part, dfinal = loss_head(h, final_gain.reshape(1, D_MODEL), target)

    dmod, dgain, dlg, reduced = [None] * DEPTH, [None] * DEPTH, [None] * DEPTH, [None] * DEPTH
    slabs = None
    for l in reversed(range(DEPTH)):
        dh, slabs, dshift, dscale, dgate, dg, dlg[l], done = layer_bwd(
            dh, saved[l], norm_gain[l:l + 1], scale[l:l + 1], gate[l:l + 1], *layer_weights[l], lg[l], slabs)
        dmod[l] = jnp.concatenate([dshift, dscale, dgate], axis=1)
        dgain[l] = dg
        if done is not None:
            reduced[l + 1] = done
    landed = reduce_first(*slabs, "reduce_first")
    sums = [chip_sum(landed[a], landed[2 + a], "chip_sum_last") for a in range(2)]
    reduced[0] = (sums, reduce_second(*sums, "reduce_second"))

    dlg = jnp.stack(dlg)
    dlogit_f = dlg[:, :RET_HEADS] * jax.nn.sigmoid(-ret_decay_logit_f)
    dlogit_b = dlg[:, RET_HEADS:] * jax.nn.sigmoid(-ret_decay_logit_b)
    packed = _pack_small(jnp.concatenate(dmod, axis=0), jnp.concatenate(dgain, axis=0), dfinal, dlogit_f, dlogit_b,
                         loss=loss_part[0, 0])
    gathered = allgather_rows(packed, "allgather_small")
    small = small_update(gathered,
                         _pack_small(b_ada, norm_gain, final_gain, ret_decay_logit_f, ret_decay_logit_b),
                         _pack_small(m_b_ada, m_norm_gain, m_final_gain, m_ret_decay_logit_f, m_ret_decay_logit_b),
                         _pack_small(v_b_ada, v_norm_gain, v_final_gain, v_ret_decay_logit_f, v_ret_decay_logit_b))
    loss = small[0][ROWS_B_ADA + ROWS_GAIN + ROWS_FINAL + 1, 0]
    (g_b_ada, g_gain, g_final, g_dec_f, g_dec_b), (d_b_ada, d_gain, d_final, d_dec_f, d_dec_b), \
        (m_b_ada2, m_gain2, m_final2, m_dec_f2, m_dec_b2), (v_b_ada2, v_gain2, v_final2, v_dec_f2, v_dec_b2) = \
        [_unpack_small(p) for p in small]

    dmod_all = gathered[:, :ROWS_B_ADA].reshape(N_DEV, DEPTH, 3 * D_MODEL)
    dmod_mine = lax.dynamic_slice_in_dim(dmod_all, me * W_ADA_SHARD, W_ADA_SHARD, axis=2).transpose(1, 0, 2)
    g_w_ada, d_w_ada, m_w_ada2, v_w_ada2 = ada_update(act.T, dmod_mine, w_ada, m_w_ada, v_w_ada)

    upd_in = upd_out = None
    for l in reversed(range(DEPTH)):
        upd_in = shard_update(l, reduced[l][0][0], reduced[l][1][0], w_in, m_w_in, v_w_in, upd_in, f"w_in_update_{l}")
        upd_out = shard_update(l, reduced[l][0][1], reduced[l][1][1], w_out, m_w_out, v_w_out, upd_out, f"w_out_update_{l}")
    g_w_in, d_w_in, m_w_in2, v_w_in2 = upd_in
    g_w_out, d_w_out, m_w_out2, v_w_out2 = upd_out

    return (loss, dh[None],
            g_gain, g_w_ada, g_b_ada, g_w_in, g_w_out, g_dec_f, g_dec_b, g_final,
            d_gain, d_w_ada, d_b_ada, d_w_in, d_w_out, d_dec_f, d_dec_b, d_final,
            m_gain2, m_w_ada2, m_b_ada2, m_w_in2, m_w_out2, m_dec_f2, m_dec_b2, m_final2,
            v_gain2, v_w_ada2, v_b_ada2, v_w_in2, v_w_out2, v_dec_f2, v_dec_b2, v_final2)
```

```python
import functools
import math

import jax
import jax.numpy as jnp
from jax import lax
from jax.experimental import pallas as pl
from jax.experimental.pallas import tpu as pltpu

F32, BF16 = jnp.float32, jnp.bfloat16

D_MODEL = 2048
DEPTH = 4
N_DEV = 8
ATTN_WIDTH = 1024
HEAD_DIM = 128
N_HEADS_ATTN = 8
DILATIONS = (1, 4, 16)
RADIUS = 64
RET_HEADS = 4
RET_QK = 128
RET_V = 256
RET_CHUNK = 128
IN_W = 7168
QKV_A = 3 * ATTN_WIDTH
COL_ZA, COL_QR, COL_KR, COL_VR, COL_ZR = 3072, 4096, 4608, 5120, 6144
W_IN_SHARD = IN_W // N_DEV
W_OUT_SHARD = D_MODEL // N_DEV
W_ADA_SHARD = 3 * D_MODEL // N_DEV
NORM_EPS = 1e-6
MASK_VALUE = -1e30
ATTN_SCALE = HEAD_DIM ** -0.5
RET_SCALE = RET_QK ** -0.5
LN2 = math.log(2.0)

ADAM_LR, ADAM_B1, ADAM_B2, ADAM_EPS, ADAM_WD, ADAM_STEP = 0.001, 0.9, 0.999, 1e-08, 0.01, 10
ADAM_C1 = 1.0 / (1.0 - ADAM_B1 ** ADAM_STEP)
ADAM_C2 = 1.0 / (1.0 - ADAM_B2 ** ADAM_STEP)

VMEM_LIMIT_BYTES = 56 * 1024 * 1024
MESH = pl.DeviceIdType.MESH


def _params(*sem):
    return pltpu.CompilerParams(dimension_semantics=sem if sem else None, vmem_limit_bytes=VMEM_LIMIT_BYTES)


def _dot(a, b):
    return jnp.dot(a, b, preferred_element_type=F32)


def _dot_nt(a, b):
    return lax.dot_general(a, b, (((1,), (1,)), ((), ())), preferred_element_type=F32)


def _dot_tn(a, b):
    return lax.dot_general(a, b, (((0,), (0,)), ((), ())), preferred_element_type=F32)


def _iota(shape, dim):
    return lax.broadcasted_iota(jnp.int32, shape, dim)


def _sigmoid(z):
    return 1.0 / (1.0 + jnp.exp(-z))


HBM = pl.BlockSpec(memory_space=pl.ANY)


def _position():
    return lax.axis_index("x"), lax.axis_index("y"), lax.axis_index("c")


def _flat(px, py, pc):
    return 4 * px + 2 * py + pc


def _remote(src, dst, send_sem, recv_sem, to):
    return pltpu.make_async_remote_copy(src_ref=src, dst_ref=dst, send_sem=send_sem, recv_sem=recv_sem,
                                        device_id=to, device_id_type=MESH)


def _weight_blocks(fin_ref, fout_ref):
    def block(a, idx):
        if a == 0:
            return fin_ref.at[:, pl.ds(pl.multiple_of(idx * W_IN_SHARD, 128), W_IN_SHARD)]
        return fout_ref.at[pl.ds(pl.multiple_of(idx * W_OUT_SHARD, W_OUT_SHARD), W_OUT_SHARD), :]
    return block


GATHER1_SEMS = [pltpu.SemaphoreType.DMA((8,)), pltpu.SemaphoreType.DMA((8,)), pltpu.SemaphoreType.DMA((2,))]
GATHER2_SEMS = [pltpu.SemaphoreType.DMA((6,)), pltpu.SemaphoreType.DMA((6,))]
REDUCE1_SEMS = [pltpu.SemaphoreType.DMA((2,)), pltpu.SemaphoreType.DMA((2,))]
REDUCE2_SEMS = [pltpu.SemaphoreType.DMA((6,)), pltpu.SemaphoreType.DMA((6,))]


def _gather_phase1(srcs, block, send_sems, recv_sems, local_sems):
    x, y, c = _position()
    mine = [block(a, _flat(x, y, c)) for a in range(2)]
    copies = [pltpu.make_async_copy(srcs[a], mine[a], local_sems.at[a]) for a in range(2)]
    for a in range(2):
        copies.append(_remote(srcs[a], mine[a], send_sems.at[4 * a], recv_sems.at[4 * a], (x, y, 1 - c)))
        for j, (px, py) in enumerate([(1 - x, y), (x, 1 - y), (1 - x, 1 - y)]):
            copies.append(_remote(srcs[a], mine[a], send_sems.at[4 * a + 1 + j], recv_sems.at[4 * a + 1 + j], (px, py, c)))
    return copies


def _gather_phase2(block, send_sems, recv_sems):
    x, y, c = _position()
    copies = []
    for a in range(2):
        for j, (px, py) in enumerate([(1 - x, y), (x, 1 - y), (1 - x, 1 - y)]):
            blk = block(a, _flat(px, py, c))
            copies.append(_remote(blk, blk, send_sems.at[3 * a + j], recv_sems.at[3 * a + j], (x, y, 1 - c)))
    return copies


def _reduce_phase1(grads, landings, send_sems, recv_sems):
    x, y, c = _position()
    return [_remote(g.at[:, 1 - c], r, send_sems.at[a], recv_sems.at[a], (x, y, 1 - c))
            for a, (g, r) in enumerate(zip(grads, landings))]


def _reduce_phase2(sums, landings, send_sems, recv_sems):
    x, y, c = _position()
    copies = []
    for a, (p, r) in enumerate(zip(sums, landings)):
        for k in (1, 2, 3):
            to = (1 - x if k & 2 else x, 1 - y if k & 1 else y, c)
            copies.append(_remote(p.at[k], r.at[k - 1], send_sems.at[3 * a + k - 1], recv_sems.at[3 * a + k - 1], to))
    return copies


def _landing_shapes(n):
    return [jax.ShapeDtypeStruct((n, D_MODEL, W_IN_SHARD), BF16), jax.ShapeDtypeStruct((n, W_OUT_SHARD, D_MODEL), BF16)]


def _split_cores(slabs):
    return tuple(s.reshape((N_DEV // 2, 2) + s.shape[1:]) for s in slabs)


def _start_all(copies):
    for cp in copies:
        cp.start()


def _wait_all(copies):
    for cp in copies:
        cp.wait()


def _grid_edge(n_axes):
    first = last = None
    for ax in range(n_axes):
        f = pl.program_id(ax) == 0
        e = pl.program_id(ax) == pl.num_programs(ax) - 1
        first = f if first is None else first & f
        last = e if last is None else last & e
    return first, last


LANES = 128


def _stage_shape(rows, width):
    return pltpu.VMEM((width // LANES, rows, LANES), F32)


def _fill_stage(stage_ref, value):
    for t in range(stage_ref.shape[0]):
        stage_ref[t] = value[:, t * LANES:(t + 1) * LANES]


def _read_stage(stage_ref):
    return jnp.concatenate([stage_ref[t] for t in range(stage_ref.shape[0])], axis=1)


def _split_classes(stage_ref, out_refs):
    n_t, rows, _ = stage_ref.shape
    for out_ref in out_refs:
        dil = out_ref.shape[0]
        for r in range(dil):
            for t in range(n_t):
                piece = stage_ref[t, pl.ds(r, rows // dil, stride=dil), :]
                out_ref[r, :, t * LANES:(t + 1) * LANES] = piece.astype(out_ref.dtype)


def _merge_classes(in_ref, stage_ref):
    dil, per = in_ref.shape[0], in_ref.shape[1]
    for r in range(dil):
        for t in range(stage_ref.shape[0]):
            stage_ref[t, pl.ds(r, per, stride=dil), :] = in_ref[r, :, t * LANES:(t + 1) * LANES].astype(F32)


def _class_block(dil, tm, width, index_map):
    return pl.BlockSpec((dil, tm // dil, width), index_map)


def inproj_fwd(x, g, scale1p, shift, w, gather=None):
    s_len = x.shape[0]
    tm, tn = min(512, s_len), 1024

    n_qkv = QKV_A // tn

    def body(x_ref, g_ref, sc_ref, sh_ref, w_ref, *rest):
        if gather is not None:
            (win_ref, wout_ref, proj_ref, h_ref, q4_ref, q16_ref, fin_ref, fout_ref, stage,
             send_sems, recv_sems, local_sems) = rest
            first, last = _grid_edge(2)
            copies = _gather_phase1([win_ref.at[gather[2]], wout_ref.at[gather[2]]], _weight_blocks(fin_ref, fout_ref),
                                    send_sems, recv_sems, local_sems)
            pl.when(first)(lambda: _start_all(copies))
        else:
            proj_ref, h_ref, q4_ref, q16_ref, stage = rest

        @pl.when(pl.program_id(1) == 0)
        def _():
            xv = x_ref[...]
            r = lax.rsqrt(jnp.mean(xv * xv, axis=-1, keepdims=True) + NORM_EPS)
            h_ref[...] = ((xv * r * g_ref[...]) * sc_ref[...] + sh_ref[...]).astype(BF16)
        res = _dot(h_ref[...], w_ref[...])
        proj_ref[...] = res.astype(BF16)

        @pl.when(pl.program_id(1) < n_qkv)
        def _():
            _fill_stage(stage, res)
            _split_classes(stage, [q4_ref, q16_ref])
        if gather is not None:
            pl.when(last)(lambda: _wait_all(copies))

    vec = pl.BlockSpec((1, D_MODEL), lambda i, j: (0, 0))
    in_specs = [pl.BlockSpec((tm, D_MODEL), lambda i, j: (i, 0)), vec, vec, vec,
                pl.BlockSpec((D_MODEL, tn), lambda i, j: (0, j))]
    out_specs = [pl.BlockSpec((tm, tn), lambda i, j: (i, j)), pl.BlockSpec((tm, D_MODEL), lambda i, j: (i, 0))]
    out_shape = [jax.ShapeDtypeStruct((s_len, IN_W), BF16), jax.ShapeDtypeStruct((s_len, D_MODEL), BF16)]
    for dil in DILATIONS[1:]:
        out_specs.append(pl.BlockSpec((dil, tm // dil, tn), lambda i, j: (0, i, jnp.minimum(j, n_qkv - 1))))
        out_shape.append(jax.ShapeDtypeStruct((dil, s_len // dil, QKV_A), BF16))
    scratch = [_stage_shape(tm, tn)]
    extra = ()
    if gather is not None:
        in_specs += [HBM, HBM]
        out_specs += [HBM, HBM]
        out_shape += [jax.ShapeDtypeStruct((D_MODEL, IN_W), BF16), jax.ShapeDtypeStruct((D_MODEL, D_MODEL), BF16)]
        scratch += GATHER1_SEMS
        extra = tuple(gather[:2])
    return pl.pallas_call(
        body, name="inproj_fwd_gather" if gather is not None else "inproj_fwd", grid=(s_len // tm, IN_W // tn),
        in_specs=in_specs, out_specs=out_specs, out_shape=out_shape, scratch_shapes=scratch,
        compiler_params=_params("arbitrary", "arbitrary"),
    )(x, g, scale1p, shift, w, *extra)


MASK_DISTANCE = 1e33
ATTN_TILE = 128


ATTN_UNROLL_FWD, ATTN_UNROLL_BWD = 10, 6


def _attn_plan(sub_len, unroll):
    tq = min(sub_len, ATTN_TILE)
    win = min(sub_len, tq + 2 * RADIUS)
    heads = 1 if sub_len > 1024 else (2 if sub_len > 256 else N_HEADS_ATTN)
    return tq, win, sub_len // tq, heads, unroll


def _attn_tiles(sub_len, tq, win, n_tiles, unroll, tile):
    tile(0, 0, 0)
    if n_tiles > 2:
        def mid(i, carry):
            q0 = pl.multiple_of(i * tq, tq)
            tile(q0, pl.multiple_of(q0 - RADIUS, RADIUS), 1)
            return carry
        lax.fori_loop(1, n_tiles - 1, mid, 0, unroll=min(unroll, n_tiles - 2))
    if n_tiles > 1:
        tile(sub_len - tq, sub_len - win, 2)


def _attn_bias(bias_ref, head, heads, tq, win, dil):
    h = pl.program_id(1) * heads + head
    slope = jnp.exp(-(h + 1).astype(F32) * LN2 * jnp.ones((1, 1), F32))
    rel = _iota((tq, win), 1) - _iota((tq, win), 0)
    for v, off in enumerate((0, RADIUS, win - tq)):
        dist = jnp.abs(rel - off)
        bias_ref[v] = slope * jnp.where(dist <= RADIUS, (dist * dil).astype(F32), MASK_DISTANCE)


def _attn_specs(n_cls, sub_len, heads):
    width = heads * HEAD_DIM
    per = ATTN_WIDTH // width

    def col(part):
        return pl.BlockSpec((None, sub_len, width), lambda r, g: (r, 0, part * per + g))
    return col, (n_cls, N_HEADS_ATTN // heads)


def attn_fwd(qkv, dil):
    n_cls, sub_len, _ = qkv.shape
    tq, win, n_tiles, heads, unroll = _attn_plan(sub_len, ATTN_UNROLL_FWD)

    def body(q_ref, k_ref, v_ref, o_ref, lse_ref, bias_ref):
        for head in range(heads):
            lanes = slice(head * HEAD_DIM, (head + 1) * HEAD_DIM)
            _attn_bias(bias_ref, head, heads, tq, win, dil)

            def tile(q0, start, variant):
                s = _dot_nt(q_ref[pl.ds(q0, tq), lanes], k_ref[pl.ds(start, win), lanes]) * ATTN_SCALE - bias_ref[variant]
                m = jnp.max(s, axis=1, keepdims=True)
                p = jnp.exp(s - m)
                den = jnp.sum(p, axis=1, keepdims=True)
                o_ref[pl.ds(q0, tq), lanes] = _dot(p.astype(BF16), v_ref[pl.ds(start, win), lanes]) / den
                lse_ref[pl.ds(q0, tq), lanes] = jnp.broadcast_to(m + jnp.log(den), (tq, HEAD_DIM))

            _attn_tiles(sub_len, tq, win, n_tiles, unroll, tile)

    col, grid = _attn_specs(n_cls, sub_len, heads)
    out = jax.ShapeDtypeStruct((n_cls, sub_len, ATTN_WIDTH), F32)
    return pl.pallas_call(
        body, name=f"attn_fwd_d{dil}", grid=grid,
        in_specs=[col(0), col(1), col(2)], out_specs=[col(0), col(0)], out_shape=[out, out],
        scratch_shapes=[pltpu.VMEM((3, tq, win), F32)],
        compiler_params=_params("arbitrary", "arbitrary"),
    )(qkv, qkv, qkv)


def attn_bwd(qkv, do, lse, delta, dil):
    n_cls, sub_len, _ = qkv.shape
    tq, win, n_tiles, heads, unroll = _attn_plan(sub_len, ATTN_UNROLL_BWD)

    def body(q_ref, k_ref, v_ref, do_ref, lse_ref, dl_ref, dq_ref, dk_ref, dv_ref, bias_ref, dk_acc, dv_acc):
        for head in range(heads):
            lanes = slice(head * HEAD_DIM, (head + 1) * HEAD_DIM)
            _attn_bias(bias_ref, head, heads, tq, win, dil)
            dk_acc[...] = jnp.zeros_like(dk_acc)
            dv_acc[...] = jnp.zeros_like(dv_acc)

            def tile(q0, start, variant):
                q = q_ref[pl.ds(q0, tq), lanes]
                k = k_ref[pl.ds(start, win), lanes]
                v = v_ref[pl.ds(start, win), lanes]
                dov = do_ref[pl.ds(q0, tq), lanes]
                s = _dot_nt(q, k) * ATTN_SCALE - bias_ref[variant]
                p = jnp.exp(s - lse_ref[pl.ds(q0, tq), head * HEAD_DIM:head * HEAD_DIM + 1])
                ds = (p * (_dot_nt(dov, v) - dl_ref[pl.ds(q0, tq), head * HEAD_DIM:head * HEAD_DIM + 1])).astype(BF16)
                dq_ref[pl.ds(q0, tq), lanes] = (_dot(ds, k) * ATTN_SCALE).astype(BF16)
                dk_acc[pl.ds(start, win), :] += _dot_tn(ds, q) * ATTN_SCALE
                dv_acc[pl.ds(start, win), :] += _dot_tn(p.astype(BF16), dov)

            _attn_tiles(sub_len, tq, win, n_tiles, unroll, tile)
            dk_ref[:, lanes] = dk_acc[...].astype(BF16)
            dv_ref[:, lanes] = dv_acc[...].astype(BF16)

    col, grid = _attn_specs(n_cls, sub_len, heads)
    out = jax.ShapeDtypeStruct((n_cls, sub_len, ATTN_WIDTH), BF16)
    return pl.pallas_call(
        body, name=f"attn_bwd_d{dil}", grid=grid,
        in_specs=[col(0), col(1), col(2), col(0), col(0), col(0)],
        out_specs=[col(0), col(0), col(0)], out_shape=[out, out, out],
        scratch_shapes=[pltpu.VMEM((3, tq, win), F32), pltpu.VMEM((sub_len, HEAD_DIM), F32),
                        pltpu.VMEM((sub_len, HEAD_DIM), F32)],
        compiler_params=_params("arbitrary", "arbitrary"),
    )(qkv, qkv, qkv, do, lse, delta)


RET_UNROLL = 8
RET_UNROLL_BWD = 4


def _ret_tables(lg_ref):
    h = pl.program_id(0)
    one = jnp.ones((1, 1), F32)
    lgf, lgb = lg_ref[h] * one, lg_ref[RET_HEADS + h] * one
    c = RET_CHUNK
    rel = (_iota((c, c), 0) - _iota((c, c), 1)).astype(F32)
    dec_f = jnp.where(rel >= 0, jnp.exp(jnp.maximum(rel, 0.0) * lgf), 0.0)
    dec_b = jnp.where(rel <= 0, jnp.exp(jnp.maximum(-rel, 0.0) * lgb), 0.0)
    ci = _iota((c, 1), 0).astype(F32)
    tab = dict(rel=rel, dec_f=dec_f, dec_b=dec_b, ci=ci,
               xi_f=jnp.exp((ci + 1.0) * lgf), ze_f=jnp.exp((c - 1.0 - ci) * lgf), g_f=jnp.exp(c * lgf),
               xi_b=jnp.exp((c - ci) * lgb), ze_b=jnp.exp(ci * lgb), g_b=jnp.exp(c * lgb))
    return tab


def _ret_specs(s_len):
    q = pl.BlockSpec((s_len, RET_QK), lambda h: (0, COL_QR // RET_QK + h))
    k = pl.BlockSpec((s_len, RET_QK), lambda h: (0, COL_KR // RET_QK + h))
    v = pl.BlockSpec((s_len, RET_V), lambda h: (0, COL_VR // RET_V + h))
    wide = pl.BlockSpec((s_len, RET_V), lambda h: (0, h))
    narrow = pl.BlockSpec((s_len, RET_QK), lambda h: (0, h))
    smem = pl.BlockSpec(memory_space=pltpu.SMEM)
    return smem, q, k, v, wide, narrow


def ret_fwd(proj, lg):
    s_len = proj.shape[0]
    c, n_chunks = RET_CHUNK, proj.shape[0] // RET_CHUNK

    def body(lg_ref, q_ref, k_ref, v_ref, opre_ref, y_ref, st_f, st_b):
        t = _ret_tables(lg_ref)
        dec = t["dec_f"] + t["dec_b"]
        st_f[...] = jnp.zeros_like(st_f)
        st_b[...] = jnp.zeros_like(st_b)

        def load(n):
            r0 = pl.multiple_of(n * c, c)
            q, k, v = q_ref[pl.ds(r0, c), :], k_ref[pl.ds(r0, c), :], v_ref[pl.ds(r0, c), :]
            return r0, q, (k.astype(F32) * RET_SCALE), v

        def fwd(n, carry):
            r0, q, kf, v = load(n)
            inner = (_dot_nt(q, kf.astype(BF16)) * dec).astype(BF16)
            opre_ref[pl.ds(r0, c), :] = _dot(inner, v) + _dot(q, st_f[...].astype(BF16)) * t["xi_f"]
            st_f[...] = st_f[...] * t["g_f"] + _dot_tn((kf * t["ze_f"]).astype(BF16), v)
            return carry

        def bwd(i, carry):
            r0, q, kf, v = load(n_chunks - 1 - i)
            o = opre_ref[pl.ds(r0, c), :] + _dot(q, st_b[...].astype(BF16)) * t["xi_b"]
            st_b[...] = st_b[...] * t["g_b"] + _dot_tn((kf * t["ze_b"]).astype(BF16), v)
            opre_ref[pl.ds(r0, c), :] = o
            y_ref[pl.ds(r0, c), :] = o * lax.rsqrt(jnp.mean(o * o, axis=-1, keepdims=True) + NORM_EPS)
            return carry

        lax.fori_loop(0, n_chunks, fwd, 0, unroll=min(RET_UNROLL, n_chunks))
        lax.fori_loop(0, n_chunks, bwd, 0, unroll=min(RET_UNROLL, n_chunks))

    smem, q, k, v, wide, _ = _ret_specs(s_len)
    out = jax.ShapeDtypeStruct((s_len, RET_HEADS * RET_V), F32)
    return pl.pallas_call(
        body, name="ret_fwd", grid=(RET_HEADS,), in_specs=[smem, q, k, v], out_specs=[wide, wide],
        out_shape=[out, out], scratch_shapes=[pltpu.VMEM((RET_QK, RET_V), F32), pltpu.VMEM((RET_QK, RET_V), F32)],
        compiler_params=_params("arbitrary"),
    )(lg, proj, proj, proj)


def ret_bwd(proj, lg, o_pre, dy):
    s_len = proj.shape[0]
    c, n_chunks = RET_CHUNK, proj.shape[0] // RET_CHUNK
    cf = float(c)

    def body(lg_ref, q_ref, k_ref, v_ref, o_ref, dy_ref, dq_ref, dk_ref, dv_ref, glf_ref, glb_ref,
             st_f, dst_b, st_b, dst_f, keep_sf, keep_dtb, acc_f, acc_b, acc_sf, acc_sb):
        t = _ret_tables(lg_ref)
        dec = t["dec_f"] + t["dec_b"]
        e_f, e_b, ci = t["rel"] * t["dec_f"], -t["rel"] * t["dec_b"], t["ci"]
        for ref in (st_f, dst_b, st_b, dst_f, acc_f, acc_b, acc_sf, acc_sb):
            ref[...] = jnp.zeros_like(ref)

        def load(n):
            r0 = pl.multiple_of(n * c, c)
            q, k, v = q_ref[pl.ds(r0, c), :], k_ref[pl.ds(r0, c), :], v_ref[pl.ds(r0, c), :]
            o, dyv = o_ref[pl.ds(r0, c), :], dy_ref[pl.ds(r0, c), :]
            rr = lax.rsqrt(jnp.mean(o * o, axis=-1, keepdims=True) + NORM_EPS)
            y = o * rr
            do = (rr * (dyv - y * jnp.mean(dyv * y, axis=-1, keepdims=True))).astype(BF16)
            return r0, q, k.astype(F32) * RET_SCALE, v, do

        def fwd(n, carry):
            r0, q, kf, v, do = load(n)
            qf, kb = q.astype(F32), kf.astype(BF16)
            a = _dot_nt(q, kb)
            b = _dot_nt(do, v)
            da = (b * dec).astype(BF16)
            ab = a * b
            sf_b, dtb_b = st_f[...].astype(BF16), dst_b[...].astype(BF16)
            dq_inter = _dot_nt(do, sf_b) * t["xi_f"]
            dk_inter = _dot_nt(v, dtb_b) * t["ze_b"]
            acc_f[...] += e_f * ab + (ci + 1.0) * (qf * dq_inter)
            acc_b[...] += e_b * ab + ci * (kf * dk_inter)
            dq_ref[pl.ds(r0, c), :] = _dot(da, kb) + dq_inter
            dk_ref[pl.ds(r0, c), :] = _dot_tn(da, q) + dk_inter
            dv_ref[pl.ds(r0, c), :] = _dot_tn((a * dec).astype(BF16), do) + _dot((kf * t["ze_b"]).astype(BF16), dtb_b)
            keep_sf[n] = sf_b
            keep_dtb[n] = dtb_b
            st_f[...] = st_f[...] * t["g_f"] + _dot_tn((kf * t["ze_f"]).astype(BF16), v)
            dst_b[...] = dst_b[...] * t["g_b"] + _dot_tn((qf * t["xi_b"]).astype(BF16), do)
            return carry

        def bwd(i, carry):
            n = n_chunks - 1 - i
            r0, q, kf, v, do = load(n)
            qf = q.astype(F32)
            tb_b, dsf_b = st_b[...].astype(BF16), dst_f[...].astype(BF16)
            dq_inter = _dot_nt(do, tb_b) * t["xi_b"]
            dk_inter = _dot_nt(v, dsf_b) * t["ze_f"]
            acc_b[...] += (cf - ci) * (qf * dq_inter)
            acc_f[...] += (cf - 1.0 - ci) * (kf * dk_inter)
            acc_sb[...] += keep_dtb[n].astype(F32) * st_b[...]
            acc_sf[...] += dst_f[...] * keep_sf[n].astype(F32)
            dq_ref[pl.ds(r0, c), :] += dq_inter
            dk_ref[pl.ds(r0, c), :] = (dk_ref[pl.ds(r0, c), :] + dk_inter) * RET_SCALE
            dv_ref[pl.ds(r0, c), :] += _dot((kf * t["ze_f"]).astype(BF16), dsf_b)
            st_b[...] = st_b[...] * t["g_b"] + _dot_tn((kf * t["ze_b"]).astype(BF16), v)
            dst_f[...] = dst_f[...] * t["g_f"] + _dot_tn((qf * t["xi_f"]).astype(BF16), do)
            return carry

        lax.fori_loop(0, n_chunks, fwd, 0, unroll=min(RET_UNROLL_BWD, n_chunks))
        lax.fori_loop(0, n_chunks, bwd, 0, unroll=min(RET_UNROLL_BWD, n_chunks))

        def total(x):
            return jnp.sum(jnp.sum(x, axis=1, keepdims=True), axis=0, keepdims=True)

        glf_ref[...] = jnp.broadcast_to(total(acc_f[...]) + cf * t["g_f"] * total(acc_sf[...]), (8, 128))
        glb_ref[...] = jnp.broadcast_to(total(acc_b[...]) + cf * t["g_b"] * total(acc_sb[...]), (8, 128))

    smem, q, k, v, wide, narrow = _ret_specs(s_len)
    scal = pl.BlockSpec((None, 8, 128), lambda h: (h, 0, 0))
    state = pltpu.VMEM((RET_QK, RET_V), F32)
    square = pltpu.VMEM((RET_CHUNK, RET_QK), F32)
    keep = pltpu.VMEM((n_chunks, RET_QK, RET_V), BF16)
    return pl.pallas_call(
        body, name="ret_bwd", grid=(RET_HEADS,), in_specs=[smem, q, k, v, wide, wide],
        out_specs=[narrow, narrow, wide, scal, scal],
        out_shape=[jax.ShapeDtypeStruct((s_len, RET_HEADS * RET_QK), F32), jax.ShapeDtypeStruct((s_len, RET_HEADS * RET_QK), F32),
                   jax.ShapeDtypeStruct((s_len, RET_HEADS * RET_V), F32),
                   jax.ShapeDtypeStruct((RET_HEADS, 8, 128), F32), jax.ShapeDtypeStruct((RET_HEADS, 8, 128), F32)],
        scratch_shapes=[state, state, state, state, keep, keep, square, square, state, state],
        compiler_params=_params("arbitrary"),
    )(lg, proj, proj, proj, o_pre, dy)


def _silu_parts(z):
    sig = _sigmoid(z)
    return z * sig, sig * (1.0 + z * (1.0 - sig))


def outproj_fwd(x, gate, w_out, attn_outs, y_r, proj):
    s_len = x.shape[0]
    tm = min(256, s_len)

    def body(x_ref, gate_ref, w_ref, o1, l1, o2, l2, o3, l3, yr_ref, za_ref, zr_ref,
             xn_ref, oa_ref, lse_ref, lse4_ref, lse16_ref, so2, sl2, so3, sl3):
        for src, dst in ((o2, so2), (l2, sl2), (o3, so3), (l3, sl3)):
            _merge_classes(src, dst)
        la, lb, lc = l1[...], _read_stage(sl2), _read_stage(sl3)
        m = jnp.maximum(jnp.maximum(la, lb), lc)
        lse = m + jnp.log(jnp.exp(la - m) + jnp.exp(lb - m) + jnp.exp(lc - m))
        o_a = jnp.exp(la - lse) * o1[...] + jnp.exp(lb - lse) * _read_stage(so2) + jnp.exp(lc - lse) * _read_stage(so3)
        oa_ref[...] = o_a
        lse_ref[...] = lse
        _fill_stage(sl2, lse)
        _split_classes(sl2, [lse4_ref, lse16_ref])
        silu_a, _ = _silu_parts(za_ref[...].astype(F32))
        silu_r, _ = _silu_parts(zr_ref[...].astype(F32))
        y = jnp.concatenate([(o_a * silu_a).astype(BF16), (yr_ref[...] * silu_r).astype(BF16)], axis=1)
        xn_ref[...] = x_ref[...] + gate_ref[...] * _dot(y, w_ref[...])

    row = lambda w: pl.BlockSpec((tm, w), lambda i: (i, 0))
    half = row(ATTN_WIDTH)
    cls = [_class_block(dil, tm, ATTN_WIDTH, lambda i: (0, i, 0)) for dil in DILATIONS[1:]]
    flat = [a for pair in attn_outs for a in pair]
    sds = jax.ShapeDtypeStruct
    return pl.pallas_call(
        body, name="outproj_fwd", grid=(s_len // tm,),
        in_specs=[row(D_MODEL), pl.BlockSpec((1, D_MODEL), lambda i: (0, 0)),
                  pl.BlockSpec((D_MODEL, D_MODEL), lambda i: (0, 0)), half, half, cls[0], cls[0], cls[1], cls[1], half,
                  pl.BlockSpec((tm, ATTN_WIDTH), lambda i: (i, COL_ZA // ATTN_WIDTH)),
                  pl.BlockSpec((tm, ATTN_WIDTH), lambda i: (i, COL_ZR // ATTN_WIDTH))],
        out_specs=[row(D_MODEL), half, half] + cls,
        out_shape=[sds((s_len, D_MODEL), F32), sds((s_len, ATTN_WIDTH), F32), sds((s_len, ATTN_WIDTH), F32)]
                  + [sds((dil, s_len // dil, ATTN_WIDTH), F32) for dil in DILATIONS[1:]],
        scratch_shapes=[_stage_shape(tm, ATTN_WIDTH)] * 4,
        compiler_params=_params("arbitrary"),
    )(x, gate, w_out, *flat, y_r, proj, proj)


def loss_head(x, gain, target):
    s_len = x.shape[0]
    tm = min(256, s_len)

    def body(x_ref, g_ref, t_ref, dx_ref, loss_ref, dg_ref):
        @pl.when(pl.program_id(0) == 0)
        def _():
            loss_ref[...] = jnp.zeros_like(loss_ref)
            dg_ref[...] = jnp.zeros_like(dg_ref)
        xv, g = x_ref[...], g_ref[...]
        r = lax.rsqrt(jnp.mean(xv * xv, axis=-1, keepdims=True) + NORM_EPS)
        xn = xv * r
        err = xn * g - t_ref[...]
        part = 0.5 * jnp.sum(jnp.mean(err * err, axis=-1, keepdims=True), axis=0, keepdims=True)
        loss_ref[...] += jnp.broadcast_to(part, loss_ref.shape)
        dy = err * (1.0 / D_MODEL)
        dg_ref[...] += jnp.sum(dy * xn, axis=0, keepdims=True)
        dxn = dy * g
        dx_ref[...] = r * (dxn - xn * jnp.mean(dxn * xn, axis=-1, keepdims=True))

    row = pl.BlockSpec((tm, D_MODEL), lambda i: (i, 0))
    vec = pl.BlockSpec((1, D_MODEL), lambda i: (0, 0))
    return pl.pallas_call(
        body, name="loss_head", grid=(s_len // tm,), in_specs=[row, vec, row],
        out_specs=[row, pl.BlockSpec((8, 128), lambda i: (0, 0)), vec],
        out_shape=[jax.ShapeDtypeStruct((s_len, D_MODEL), F32), jax.ShapeDtypeStruct((8, 128), F32),
                   jax.ShapeDtypeStruct((1, D_MODEL), F32)],
        compiler_params=_params("arbitrary"),
    )(x, gain, target)


def outproj_bwd(dxn, gate, w_out, o_a, y_r, proj, exchange=None):
    s_len = dxn.shape[0]
    tm = min(256, s_len)

    def body(dx_ref, gate_ref, w_ref, oa_ref, yr_ref, za_ref, zr_ref, *rest):
        if exchange is not None:
            (ga_ref, gb_ref, doa_ref, dl_ref, dyr_ref, dza_ref, dzr_ref, y_ref, dxb_ref, do4, do16, dl4, dl16,
             ra_ref, rb_ref, stage, send_sems, recv_sems) = rest
            first, last = _grid_edge(1)
            copies = _reduce_phase1([ga_ref, gb_ref], [ra_ref, rb_ref], send_sems, recv_sems)
            pl.when(first)(lambda: _start_all(copies))
        else:
            doa_ref, dl_ref, dyr_ref, dza_ref, dzr_ref, y_ref, dxb_ref, do4, do16, dl4, dl16, stage = rest
        dxv = dx_ref[...]
        dxb_ref[...] = dxv.astype(BF16)
        dy = _dot_nt((dxv * gate_ref[...]).astype(BF16), w_ref[...])
        dy_a, dy_r = dy[:, :ATTN_WIDTH], dy[:, ATTN_WIDTH:]
        o_a, y_rv = oa_ref[...], yr_ref[...]
        silu_a, dsilu_a = _silu_parts(za_ref[...].astype(F32))
        silu_r, dsilu_r = _silu_parts(zr_ref[...].astype(F32))
        do_a = dy_a * silu_a
        doa_ref[...] = do_a.astype(BF16)
        _fill_stage(stage, do_a)
        _split_classes(stage, [do4, do16])
        prod = do_a * o_a
        delta = jnp.concatenate(
            [jnp.broadcast_to(jnp.sum(prod[:, h * HEAD_DIM:(h + 1) * HEAD_DIM], axis=1, keepdims=True), (tm, HEAD_DIM))
             for h in range(N_HEADS_ATTN)], axis=1)
        dl_ref[...] = delta
        _fill_stage(stage, delta)
        _split_classes(stage, [dl4, dl16])
        dyr_ref[...] = dy_r * silu_r
        dza_ref[...] = (dy_a * o_a * dsilu_a).astype(BF16)
        dzr_ref[...] = (dy_r * y_rv * dsilu_r).astype(BF16)
        y_ref[...] = jnp.concatenate([(o_a * silu_a).astype(BF16), (y_rv * silu_r).astype(BF16)], axis=1)
        if exchange is not None:
            pl.when(last)(lambda: _wait_all(copies))

    row = lambda w: pl.BlockSpec((tm, w), lambda i: (i, 0))
    half = row(ATTN_WIDTH)
    sds = lambda w, dt: jax.ShapeDtypeStruct((s_len, w), dt)
    in_specs = [row(D_MODEL), pl.BlockSpec((1, D_MODEL), lambda i: (0, 0)),
                pl.BlockSpec((D_MODEL, D_MODEL), lambda i: (0, 0)), half, half,
                pl.BlockSpec((tm, ATTN_WIDTH), lambda i: (i, COL_ZA // ATTN_WIDTH)),
                pl.BlockSpec((tm, ATTN_WIDTH), lambda i: (i, COL_ZR // ATTN_WIDTH))]
    cls = [_class_block(dil, tm, ATTN_WIDTH, lambda i: (0, i, 0)) for dil in DILATIONS[1:]]
    out_specs = [half, half, half, half, half, row(D_MODEL), row(D_MODEL)] + cls + cls
    out_shape = [sds(ATTN_WIDTH, BF16), sds(ATTN_WIDTH, F32), sds(ATTN_WIDTH, F32), sds(ATTN_WIDTH, BF16),
                 sds(ATTN_WIDTH, BF16), sds(D_MODEL, BF16), sds(D_MODEL, BF16)]
    out_shape += [jax.ShapeDtypeStruct((dil, s_len // dil, ATTN_WIDTH), dt) for dt in (BF16, F32) for dil in DILATIONS[1:]]
    scratch = [_stage_shape(tm, ATTN_WIDTH)]
    extra = ()
    if exchange is not None:
        in_specs += [HBM, HBM]
        out_specs += [HBM, HBM]
        out_shape += _landing_shapes(4)
        scratch += REDUCE1_SEMS
        extra = _split_cores(exchange)
    return pl.pallas_call(
        body, name="outproj_bwd_reduce1" if exchange is not None else "outproj_bwd", grid=(s_len // tm,),
        in_specs=in_specs, out_specs=out_specs, out_shape=out_shape, scratch_shapes=scratch,
        compiler_params=_params("arbitrary"),
    )(dxn, gate, w_out, o_a, y_r, proj, proj, *extra)


def wout_grad(y, dxb, gate, w_out):
    s_len = y.shape[0]
    tf, ts = 512, min(512, s_len)

    def body(y_ref, dx_ref, gate_ref, w_ref, dw_ref, dgate_ref, acc):
        f, s = pl.program_id(0), pl.program_id(1)

        @pl.when((f == 0) & (s == 0))
        def _():
            dgate_ref[...] = jnp.zeros_like(dgate_ref)

        @pl.when(s == 0)
        def _():
            acc[...] = jnp.zeros_like(acc)
        acc[...] += _dot_tn(y_ref[...], dx_ref[...])

        @pl.when(s == pl.num_programs(1) - 1)
        def _():
            m = acc[...]
            dw_ref[...] = (m * gate_ref[...]).astype(BF16).reshape(dw_ref.shape)
            dgate_ref[...] += jnp.sum(m * w_ref[...].astype(F32), axis=0, keepdims=True)

    per = tf // W_OUT_SHARD
    return pl.pallas_call(
        body, name="wout_grad", grid=(D_MODEL // tf, s_len // ts),
        in_specs=[pl.BlockSpec((ts, tf), lambda f, s: (s, f)), pl.BlockSpec((ts, D_MODEL), lambda f, s: (s, 0)),
                  pl.BlockSpec((1, D_MODEL), lambda f, s: (0, 0)), pl.BlockSpec((tf, D_MODEL), lambda f, s: (f, 0))],
        out_specs=[pl.BlockSpec((per, W_OUT_SHARD, D_MODEL), lambda f, s: (f, 0, 0)),
                   pl.BlockSpec((1, D_MODEL), lambda f, s: (0, 0))],
        out_shape=[jax.ShapeDtypeStruct((N_DEV, W_OUT_SHARD, D_MODEL), BF16), jax.ShapeDtypeStruct((1, D_MODEL), F32)],
        scratch_shapes=[pltpu.VMEM((tf, D_MODEL), F32)],
        compiler_params=_params("arbitrary", "arbitrary"),
    )(y, dxb, gate, w_out)


def assemble_dproj(dqkv_a, dz_a, dq_r, dk_r, dv_r, dz_r):
    s_len = dz_a.shape[0]
    tm = min(256, s_len)

    def body(*refs):
        pat, (dza, dqr, dkr, dvr, dzr, out, stage) = refs[:9], refs[9:]
        for t in range(3):
            tot = pat[t][...].astype(F32)
            for p in (1, 2):
                _merge_classes(pat[3 * p + t], stage)
                tot = tot + _read_stage(stage)
            out[:, t * ATTN_WIDTH:(t + 1) * ATTN_WIDTH] = tot.astype(BF16)
        out[:, COL_ZA:COL_QR] = dza[...]
        out[:, COL_QR:COL_KR] = dqr[...].astype(BF16)
        out[:, COL_KR:COL_VR] = dkr[...].astype(BF16)
        out[:, COL_VR:COL_ZR] = dvr[...].astype(BF16)
        out[:, COL_ZR:IN_W] = dzr[...]

    row = lambda w: pl.BlockSpec((tm, w), lambda i: (i, 0))
    cls = [_class_block(dil, tm, ATTN_WIDTH, lambda i: (0, i, 0)) for dil in DILATIONS[1:]]
    flat = [dqkv_a[p][t] for p in range(3) for t in range(3)]
    return pl.pallas_call(
        body, name="assemble_dproj", grid=(s_len // tm,),
        in_specs=[row(ATTN_WIDTH)] * 3 + [cls[0]] * 3 + [cls[1]] * 3
                 + [row(ATTN_WIDTH), row(512), row(512), row(ATTN_WIDTH), row(ATTN_WIDTH)],
        out_specs=row(IN_W), out_shape=jax.ShapeDtypeStruct((s_len, IN_W), BF16),
        scratch_shapes=[_stage_shape(tm, ATTN_WIDTH)],
        compiler_params=_params("arbitrary"),
    )(*flat, dz_a, dq_r, dk_r, dv_r, dz_r)


def inproj_bwd(dproj, w, x, g, scale1p, dxn, exchange=None):
    s_len = x.shape[0]
    tm, tk = min(512, s_len), 1024

    def body(dp_ref, w_ref, x_ref, g_ref, sc_ref, dxn_ref, *rest):
        if exchange is not None:
            pa_ref, pb_ref, dx_ref, st_ref, ra_ref, rb_ref, acc, send_sems, recv_sems = rest
            first, last = _grid_edge(2)
            copies = _reduce_phase2([pa_ref, pb_ref], [ra_ref, rb_ref], send_sems, recv_sems)
            pl.when(first)(lambda: _start_all(copies))
        else:
            dx_ref, st_ref, acc = rest
        i, k = pl.program_id(0), pl.program_id(1)

        @pl.when((i == 0) & (k == 0))
        def _():
            st_ref[...] = jnp.zeros_like(st_ref)

        @pl.when(k == 0)
        def _():
            acc[...] = jnp.zeros_like(acc)
        acc[...] += _dot_nt(dp_ref[...], w_ref[...])

        @pl.when(k == pl.num_programs(1) - 1)
        def _():
            dh, xv, g, sc = acc[...], x_ref[...], g_ref[...], sc_ref[...]
            r = lax.rsqrt(jnp.mean(xv * xv, axis=-1, keepdims=True) + NORM_EPS)
            xn = xv * r
            da = dh * sc
            st_ref[0:1, :] += jnp.sum(dh, axis=0, keepdims=True)
            st_ref[1:2, :] += jnp.sum(dh * (xn * g), axis=0, keepdims=True)
            st_ref[2:3, :] += jnp.sum(da * xn, axis=0, keepdims=True)
            dn = da * g
            dx_ref[...] = r * (dn - xn * jnp.mean(dn * xn, axis=-1, keepdims=True)) + dxn_ref[...]

        if exchange is not None:
            pl.when(last)(lambda: _wait_all(copies))

    row = pl.BlockSpec((tm, D_MODEL), lambda i, k: (i, 0))
    vec = pl.BlockSpec((1, D_MODEL), lambda i, k: (0, 0))
    in_specs = [pl.BlockSpec((tm, tk), lambda i, k: (i, k)), pl.BlockSpec((D_MODEL, tk), lambda i, k: (0, k)),
                row, vec, vec, row]
    out_specs = [row, pl.BlockSpec((8, D_MODEL), lambda i, k: (0, 0))]
    out_shape = [jax.ShapeDtypeStruct((s_len, D_MODEL), F32), jax.ShapeDtypeStruct((8, D_MODEL), F32)]
    scratch = [pltpu.VMEM((tm, D_MODEL), F32)]
    extra = ()
    if exchange is not None:
        in_specs += [HBM, HBM]
        out_specs += [HBM, HBM]
        out_shape += _landing_shapes(3)
        scratch += REDUCE2_SEMS
        extra = tuple(exchange)
    return pl.pallas_call(
        body, name="inproj_bwd_reduce2" if exchange is not None else "inproj_bwd", grid=(s_len // tm, IN_W // tk),
        in_specs=in_specs, out_specs=out_specs, out_shape=out_shape, scratch_shapes=scratch,
        compiler_params=_params("arbitrary", "arbitrary"),
    )(dproj, w, x, g, scale1p, dxn, *extra)


def win_grad(h, dproj):
    s_len = h.shape[0]
    ts = min(1024, s_len)

    def body(h_ref, dp_ref, dw_ref, acc):
        s = pl.program_id(1)

        @pl.when(s == 0)
        def _():
            acc[...] = jnp.zeros_like(acc)
        acc[...] += _dot_tn(h_ref[...], dp_ref[...])

        @pl.when(s == pl.num_programs(1) - 1)
        def _():
            dw_ref[...] = acc[...].astype(BF16)

    return pl.pallas_call(
        body, name="win_grad", grid=(N_DEV, s_len // ts),
        in_specs=[pl.BlockSpec((ts, D_MODEL), lambda j, s: (s, 0)), pl.BlockSpec((ts, W_IN_SHARD), lambda j, s: (s, j))],
        out_specs=pl.BlockSpec((None, D_MODEL, W_IN_SHARD), lambda j, s: (j, 0, 0)),
        out_shape=jax.ShapeDtypeStruct((N_DEV, D_MODEL, W_IN_SHARD), BF16),
        scratch_shapes=[pltpu.VMEM((D_MODEL, W_IN_SHARD), F32)],
        compiler_params=_params("arbitrary", "arbitrary"),
    )(h, dproj)


def ada_fwd(c_all, w_ada):
    def body(c_ref, w_ref, act_ref, part_ref):
        cv = c_ref[...]
        act = cv * _sigmoid(cv)
        act_ref[...] = act
        part_ref[...] = _dot(act.astype(BF16), w_ref[...].astype(BF16))

    return pl.pallas_call(
        body, name="ada_fwd", grid=(DEPTH,),
        in_specs=[pl.BlockSpec((N_DEV, D_MODEL), lambda l: (0, 0)),
                  pl.BlockSpec((None, D_MODEL, W_ADA_SHARD), lambda l: (l, 0, 0))],
        out_specs=[pl.BlockSpec((N_DEV, D_MODEL), lambda l: (0, 0)),
                   pl.BlockSpec((None, N_DEV, W_ADA_SHARD), lambda l: (l, 0, 0))],
        out_shape=[jax.ShapeDtypeStruct((N_DEV, D_MODEL), F32), jax.ShapeDtypeStruct((DEPTH, N_DEV, W_ADA_SHARD), F32)],
        compiler_params=_params("arbitrary"),
    )(c_all, w_ada)


def _adamw(w, g, m, v):
    m = ADAM_B1 * m + (1.0 - ADAM_B1) * g
    v = ADAM_B2 * v + (1.0 - ADAM_B2) * (g * g)
    delta = -ADAM_LR * ((m * ADAM_C1) / (jnp.sqrt(v * ADAM_C2) + ADAM_EPS) + ADAM_WD * w)
    return delta, m, v


def ada_update(act_t, dmod, w, m, v):
    tr = 512

    def body(a_ref, d_ref, w_ref, m_ref, v_ref, g_out, dl_out, m_out, v_out):
        a = a_ref[...].astype(BF16).astype(F32)
        d = d_ref[...].astype(BF16).astype(F32)
        g = a[:, 0:1] * d[0:1, :]
        for b in range(1, N_DEV):
            g = g + a[:, b:b + 1] * d[b:b + 1, :]
        g_out[...] = g
        dl_out[...], m_out[...], v_out[...] = _adamw(w_ref[...], g, m_ref[...], v_ref[...])

    blk = pl.BlockSpec((None, tr, W_ADA_SHARD), lambda l, r: (l, r, 0))
    out = jax.ShapeDtypeStruct(w.shape, F32)
    return pl.pallas_call(
        body, name="ada_update", grid=(DEPTH, D_MODEL // tr),
        in_specs=[pl.BlockSpec((tr, N_DEV), lambda l, r: (r, 0)),
                  pl.BlockSpec((None, N_DEV, W_ADA_SHARD), lambda l, r: (l, 0, 0)), blk, blk, blk],
        out_specs=[blk] * 4, out_shape=[out] * 4, compiler_params=_params("arbitrary", "arbitrary"),
    )(act_t, dmod, w, m, v)


def chip_sum(pos, grads, landed, name):
    _, _, n_rows, n_cols = grads.shape
    tr = min(512, n_rows)

    def chip(k, pos_ref):
        return (pos_ref[0] ^ (k // 2)) * 2 + (pos_ref[1] ^ (k % 2))

    def body(pos_ref, g_ref, r_ref, out_ref):
        out_ref[...] = (g_ref[...].astype(F32) + r_ref[...].astype(F32)).astype(BF16)

    return pl.pallas_call(
        body, name=name,
        grid_spec=pltpu.PrefetchScalarGridSpec(
            num_scalar_prefetch=1, grid=(N_DEV // 2, n_rows // tr),
            in_specs=[pl.BlockSpec((None, None, tr, n_cols), lambda k, r, p: (chip(k, p), p[2], r, 0)),
                      pl.BlockSpec((None, tr, n_cols), lambda k, r, p: (chip(k, p), r, 0))],
            out_specs=pl.BlockSpec((None, tr, n_cols), lambda k, r, p: (k, r, 0))),
        out_shape=jax.ShapeDtypeStruct((N_DEV // 2, n_rows, n_cols), BF16),
        compiler_params=_params("arbitrary", "arbitrary"),
    )(pos, grads, landed)


def shard_update(layer, own, others, w, m, v, prev, name):
    _, n_rows, n_cols = w.shape
    tr = min(256, n_rows)

    def body(own_ref, oth_ref, w_ref, m_ref, v_ref, *rest):
        g_out, dl_out, m_out, v_out = rest[-4:]
        g = own_ref[...].astype(F32)
        for k in range(3):
            g = g + oth_ref[k].astype(F32)
        g_out[...] = g
        dl_out[...], m_out[...], v_out[...] = _adamw(w_ref[...], g, m_ref[...], v_ref[...])

    blk = pl.BlockSpec((None, tr, n_cols), lambda r: (layer, r, 0))
    out = jax.ShapeDtypeStruct(w.shape, F32)
    in_specs = [pl.BlockSpec((None, tr, n_cols), lambda r: (0, r, 0)), pl.BlockSpec((3, tr, n_cols), lambda r: (0, r, 0)),
                blk, blk, blk]
    aliases, extra = {}, ()
    if prev is not None:
        in_specs += [HBM] * 4
        aliases = {5 + t: t for t in range(4)}
        extra = tuple(prev)
    return pl.pallas_call(
        body, name=name, grid=(n_rows // tr,), in_specs=in_specs, out_specs=[blk] * 4, out_shape=[out] * 4,
        input_output_aliases=aliases, compiler_params=_params("arbitrary"),
    )(own, others, w, m, v, *extra)


def small_update(parts, w, m, v):
    def body(p_ref, w_ref, m_ref, v_ref, g_out, dl_out, m_out, v_out):
        g = p_ref[0]
        for k in range(1, N_DEV):
            g = g + p_ref[k]
        g_out[...] = g
        dl_out[...], m_out[...], v_out[...] = _adamw(w_ref[...], g, m_ref[...], v_ref[...])

    out = jax.ShapeDtypeStruct(w.shape, F32)
    return pl.pallas_call(body, name="small_update", out_shape=[out] * 4, compiler_params=_params())(parts, w, m, v)


def _two_level_allgather(srcs, dst_block, send_sems, recv_sems, local_sems):
    x, y, c = _position()
    me, sibling = (x, y, c), (x, y, 1 - c)
    chips = [(1 - x, y), (x, 1 - y), (1 - x, 1 - y)]
    n = len(srcs)

    def copy(a, k, block, to, src=None):
        dst = dst_block(a, _flat(*block))
        return pltpu.make_async_remote_copy(
            src_ref=dst if src is None else src, dst_ref=dst, send_sem=send_sems.at[a * 7 + k],
            recv_sem=recv_sems.at[a * 7 + k], device_id=to, device_id_type=MESH)

    mine = [pltpu.make_async_copy(srcs[a], dst_block(a, _flat(*me)), local_sems.at[a]) for a in range(n)]
    for cp in mine:
        cp.start()
    first = []
    for a in range(n):
        first.append(copy(a, 0, me, sibling, src=srcs[a]))
        first += [copy(a, 1 + j, me, (*chip, c), src=srcs[a]) for j, chip in enumerate(chips)]
    for cp in first:
        cp.start()
    passed = []
    for j, chip in enumerate(chips):
        for a in range(n):
            copy(a, 1 + j, (*chip, c), me).wait_recv()
            fwd = copy(a, 4 + j, (*chip, c), sibling)
            fwd.start()
            passed.append(fwd)
    for a in range(n):
        copy(a, 0, sibling, me).wait_recv()
        for j, chip in enumerate(chips):
            copy(a, 4 + j, (*chip, 1 - c), me).wait_recv()
    for cp in first + passed:
        cp.wait_send()
    for cp in mine:
        cp.wait()


def allgather_rows(x, name):
    def body(x_ref, out_ref, send_sems, recv_sems, local_sems):
        _two_level_allgather([x_ref], lambda a, idx: out_ref.at[idx], send_sems, recv_sems, local_sems)

    vmem = pl.BlockSpec(memory_space=pltpu.VMEM)
    return pl.pallas_call(
        body, name=name, in_specs=[vmem], out_specs=vmem,
        out_shape=jax.ShapeDtypeStruct((N_DEV,) + x.shape, x.dtype),
        scratch_shapes=[pltpu.SemaphoreType.DMA((7,)), pltpu.SemaphoreType.DMA((7,)), pltpu.SemaphoreType.DMA((1,))],
        compiler_params=_params(),
    )(x)


def _full_weight_shapes():
    return [jax.ShapeDtypeStruct((D_MODEL, IN_W), BF16), jax.ShapeDtypeStruct((D_MODEL, D_MODEL), BF16)]


def allgather_weights(w_in_b, w_out_b):
    def body(win_ref, wout_ref, fin_ref, fout_ref, s1, r1, l1, s2, r2):
        block = _weight_blocks(fin_ref, fout_ref)
        first = _gather_phase1([win_ref.at[0], wout_ref.at[0]], block, s1, r1, l1)
        _start_all(first)
        _wait_all(first)
        second = _gather_phase2(block, s2, r2)
        _start_all(second)
        _wait_all(second)

    return pl.pallas_call(
        body, name="allgather_weights", in_specs=[HBM, HBM], out_specs=[HBM, HBM], out_shape=_full_weight_shapes(),
        scratch_shapes=GATHER1_SEMS + GATHER2_SEMS, compiler_params=_params(),
    )(w_in_b, w_out_b)


def gather_finish(full_in, full_out, name):
    def body(fin_in, fout_in, fin_ref, fout_ref, send_sems, recv_sems):
        copies = _gather_phase2(_weight_blocks(fin_ref, fout_ref), send_sems, recv_sems)
        _start_all(copies)
        _wait_all(copies)

    return pl.pallas_call(
        body, name=name, in_specs=[HBM, HBM], out_specs=[HBM, HBM], out_shape=_full_weight_shapes(),
        input_output_aliases={0: 0, 1: 1}, scratch_shapes=GATHER2_SEMS, compiler_params=_params(),
    )(full_in, full_out)


def reduce_first(dw_in, dw_out, name):
    def body(ga_ref, gb_ref, ra_ref, rb_ref, send_sems, recv_sems):
        copies = _reduce_phase1([ga_ref, gb_ref], [ra_ref, rb_ref], send_sems, recv_sems)
        _start_all(copies)
        _wait_all(copies)

    return pl.pallas_call(
        body, name=name, in_specs=[HBM, HBM], out_specs=[HBM, HBM], out_shape=_landing_shapes(4),
        scratch_shapes=REDUCE1_SEMS, compiler_params=_params(),
    )(*_split_cores((dw_in, dw_out)))


def reduce_second(sum_in, sum_out, name):
    def body(pa_ref, pb_ref, ra_ref, rb_ref, send_sems, recv_sems):
        copies = _reduce_phase2([pa_ref, pb_ref], [ra_ref, rb_ref], send_sems, recv_sems)
        _start_all(copies)
        _wait_all(copies)

    return pl.pallas_call(
        body, name=name, in_specs=[HBM, HBM], out_specs=[HBM, HBM], out_shape=_landing_shapes(3),
        scratch_shapes=REDUCE2_SEMS, compiler_params=_params(),
    )(sum_in, sum_out)


def _one_class(a):
    return a.reshape((1,) + a.shape)


def layer_fwd(x, g, scale, shift, gate, w_in, w_out, lg, gather=None):
    proj, h, qkv4, qkv16, *began = inproj_fwd(x, g, 1.0 + scale, shift, w_in, gather)
    next_weights = gather_finish(*began, "gather_finish") if gather is not None else None
    qkv = (_one_class(proj), qkv4, qkv16)
    attn_outs = [attn_fwd(arr, dil) for dil, arr in zip(DILATIONS, qkv)]
    attn_outs[0] = tuple(a[0] for a in attn_outs[0])
    o_pre, y_r = ret_fwd(proj, lg)
    x_new, o_a, lse, lse4, lse16 = outproj_fwd(x, gate, w_out, attn_outs, y_r, proj)
    saved = dict(x=x, proj=proj, h=h, qkv=qkv, o_a=o_a, lse=(_one_class(lse), lse4, lse16), o_pre=o_pre, y_r=y_r)
    return x_new, saved, next_weights


def layer_bwd(dxn, saved, g, scale, gate, w_in, w_out, lg, pos, later=None):
    proj = saved["proj"]
    do_a, delta, dyr, dz_a, dz_r, y, dxb, do4, do16, dl4, dl16, *landed = outproj_bwd(
        dxn, gate, w_out, saved["o_a"], saved["y_r"], proj, later)
    sums = None
    if later is not None:
        sums = [chip_sum(pos, g4, r, "chip_sum") for g4, r in zip(_split_cores(later), landed)]
    dw_out, dgate = wout_grad(y, dxb, gate, w_out)
    dq_r, dk_r, dv_r, glf, glb = ret_bwd(proj, lg, saved["o_pre"], dyr)
    dos, deltas = (_one_class(do_a), do4, do16), (_one_class(delta), dl4, dl16)
    dqkv_a = [attn_bwd(arr, d_o, lse, dl, dil)
              for dil, arr, d_o, lse, dl in zip(DILATIONS, saved["qkv"], dos, saved["lse"], deltas)]
    dqkv_a[0] = [t[0] for t in dqkv_a[0]]
    dproj = assemble_dproj(dqkv_a, dz_a, dq_r, dk_r, dv_r, dz_r)
    dx, stats, *others = inproj_bwd(dproj, w_in, saved["x"], g, 1.0 + scale, dxn, sums)
    dw_in = win_grad(saved["h"], dproj)
    dlg = jnp.concatenate([glf[:, 0, 0], glb[:, 0, 0]])
    reduced = (sums, others) if later is not None else None
    return dx, (dw_in, dw_out), stats[0:1], stats[1:2], dgate, stats[2:3], dlg, reduced


ROWS_B_ADA = DEPTH * 3 * D_MODEL // 128
ROWS_GAIN = DEPTH * D_MODEL // 128
ROWS_FINAL = D_MODEL // 128
ROWS_MISC = 8
ROWS_SMALL = ROWS_B_ADA + ROWS_GAIN + ROWS_FINAL + ROWS_MISC


def _pack_small(b_ada_like, gain_like, final_like, dec_f, dec_b, loss=None):
    misc = jnp.zeros((ROWS_MISC, 128), F32)
    misc = misc.at[0, :2 * DEPTH * RET_HEADS].set(jnp.concatenate([dec_f.reshape(-1), dec_b.reshape(-1)]))
    if loss is not None:
        misc = misc.at[1, 0].set(loss)
    return jnp.concatenate([b_ada_like.reshape(ROWS_B_ADA, 128), gain_like.reshape(ROWS_GAIN, 128),
                            final_like.reshape(ROWS_FINAL, 128), misc], axis=0)


def _unpack_small(p):
    r0, r1, r2 = ROWS_B_ADA, ROWS_B_ADA + ROWS_GAIN, ROWS_B_ADA + ROWS_GAIN + ROWS_FINAL
    n = DEPTH * RET_HEADS
    return (p[:r0].reshape(DEPTH, 3 * D_MODEL), p[r0:r1].reshape(DEPTH, D_MODEL), p[r1:r2].reshape(D_MODEL),
            p[r2, :n].reshape(DEPTH, RET_HEADS), p[r2, n:2 * n].reshape(DEPTH, RET_HEADS))


def kernel(x, c, norm_gain, w_ada, b_ada, w_in, w_out, ret_decay_logit_f, ret_decay_logit_b, final_gain, loss_target, m_norm_gain, m_w_ada, m_b_ada, m_w_in, m_w_out, m_ret_decay_logit_f, m_ret_decay_logit_b, m_final_gain, v_norm_gain, v_w_ada, v_b_ada, v_w_in, v_w_out, v_ret_decay_logit_f, v_ret_decay_logit_b, v_final_gain):
    px, py, pc = _position()
    me = _flat(px, py, pc)
    pos = jnp.stack([px, py, pc]).astype(jnp.int32)
    x2, target = x[0], loss_target[0]

    w_in_b, w_out_b = w_in.astype(BF16), w_out.astype(BF16)
    weights = allgather_weights(w_in_b, w_out_b)

    c_all = allgather_rows(c.reshape(D_MODEL // 128, 128), "allgather_c").reshape(N_DEV, D_MODEL)
    act, mod_part = ada_fwd(c_all, w_ada)
    mod_all = allgather_rows(mod_part.reshape(-1, 128), "allgather_mod").reshape(N_DEV, DEPTH, N_DEV, W_ADA_SHARD)
    mod = lax.dynamic_index_in_dim(mod_all, me, axis=2, keepdims=False)
    mod = mod.transpose(1, 0, 2).reshape(DEPTH, 3 * D_MODEL) + b_ada
    shift, scale, gate = mod[:, :D_MODEL], mod[:, D_MODEL:2 * D_MODEL], mod[:, 2 * D_MODEL:]

    lg = jnp.concatenate([jax.nn.log_sigmoid(ret_decay_logit_f), jax.nn.log_sigmoid(ret_decay_logit_b)], axis=1)

    h = x2
    saved, layer_weights = [], []
    for l in range(DEPTH):
        layer_weights.append(weights)
        gather = (w_in_b, w_out_b, l + 1) if l + 1 < DEPTH else None
        h, sv, weights = layer_fwd(h, norm_gain[l:l + 1], scale[l:l + 1], shift[l:l + 1], gate[l:l + 1],
                                   *layer_weights[l], lg[l], gather)
        saved.append(sv)
    dh, loss_part, dfinal = loss_head(h, final_gain.reshape(1, D_MODEL), target)

    dmod, dgain, dlg, reduced = [None] * DEPTH, [None] * DEPTH, [None] * DEPTH, [None] * DEPTH
    slabs = None
    for l in reversed(range(DEPTH)):
        dh, slabs, dshift, dscale, dgate, dg, dlg[l], done = layer_bwd(
            dh, saved[l], norm_gain[l:l + 1], scale[l:l + 1], gate[l:l + 1], *layer_weights[l], lg[l], pos, slabs)
        dmod[l] = jnp.concatenate([dshift, dscale, dgate], axis=1)
        dgain[l] = dg
        if done is not None:
            reduced[l + 1] = done
    landed = reduce_first(*slabs, "reduce_first")
    sums = [chip_sum(pos, g4, r, "chip_sum_last") for g4, r in zip(_split_cores(slabs), landed)]
    reduced[0] = (sums, reduce_second(*sums, "reduce_second"))

    dlg = jnp.stack(dlg)
    dlogit_f = dlg[:, :RET_HEADS] * jax.nn.sigmoid(-ret_decay_logit_f)
    dlogit_b = dlg[:, RET_HEADS:] * jax.nn.sigmoid(-ret_decay_logit_b)
    packed = _pack_small(jnp.concatenate(dmod, axis=0), jnp.concatenate(dgain, axis=0), dfinal, dlogit_f, dlogit_b,
                         loss=loss_part[0, 0])
    gathered = allgather_rows(packed, "allgather_small")
    small = small_update(gathered,
                         _pack_small(b_ada, norm_gain, final_gain, ret_decay_logit_f, ret_decay_logit_b),
                         _pack_small(m_b_ada, m_norm_gain, m_final_gain, m_ret_decay_logit_f, m_ret_decay_logit_b),
                         _pack_small(v_b_ada, v_norm_gain, v_final_gain, v_ret_decay_logit_f, v_ret_decay_logit_b))
    loss = small[0][ROWS_B_ADA + ROWS_GAIN + ROWS_FINAL + 1, 0]
    (g_b_ada, g_gain, g_final, g_dec_f, g_dec_b), (d_b_ada, d_gain, d_final, d_dec_f, d_dec_b), \
        (m_b_ada2, m_gain2, m_final2, m_dec_f2, m_dec_b2), (v_b_ada2, v_gain2, v_final2, v_dec_f2, v_dec_b2) = \
        [_unpack_small(p) for p in small]

    dmod_all = gathered[:, :ROWS_B_ADA].reshape(N_DEV, DEPTH, 3 * D_MODEL)
    dmod_mine = lax.dynamic_slice_in_dim(dmod_all, me * W_ADA_SHARD, W_ADA_SHARD, axis=2).transpose(1, 0, 2)
    g_w_ada, d_w_ada, m_w_ada2, v_w_ada2 = ada_update(act.T, dmod_mine, w_ada, m_w_ada, v_w_ada)

    upd_in = upd_out = None
    for l in reversed(range(DEPTH)):
        upd_in = shard_update(l, reduced[l][0][0], reduced[l][1][0], w_in, m_w_in, v_w_in, upd_in, f"w_in_update_{l}")
        upd_out = shard_update(l, reduced[l][0][1], reduced[l][1][1], w_out, m_w_out, v_w_out, upd_out, f"w_out_update_{l}")
    g_w_in, d_w_in, m_w_in2, v_w_in2 = upd_in
    g_w_out, d_w_out, m_w_out2, v_w_out2 = upd_out

    return (loss, dh[None],
            g_gain, g_w_ada, g_b_ada, g_w_in, g_w_out, g_dec_f, g_dec_b, g_final,
            d_gain, d_w_ada, d_b_ada, d_w_in, d_w_out, d_dec_f, d_dec_b, d_final,
            m_gain2, m_w_ada2, m_b_ada2, m_w_in2, m_w_out2, m_dec_f2, m_dec_b2, m_final2,
            v_gain2, v_w_ada2, v_b_ada2, v_w_in2, v_w_out2, v_dec_f2, v_dec_b2, v_final2)
```

```python
import functools
import math

import jax
import jax.numpy as jnp
from jax import lax
from jax.experimental import pallas as pl
from jax.experimental.pallas import tpu as pltpu

F32, BF16 = jnp.float32, jnp.bfloat16

D_MODEL = 2048
DEPTH = 4
N_DEV = 8
ATTN_WIDTH = 1024
HEAD_DIM = 128
N_HEADS_ATTN = 8
DILATIONS = (1, 4, 16)
RADIUS = 64
RET_HEADS = 4
RET_QK = 128
RET_V = 256
RET_CHUNK = 128
IN_W = 7168
QKV_A = 3 * ATTN_WIDTH
COL_ZA, COL_QR, COL_KR, COL_VR, COL_ZR = 3072, 4096, 4608, 5120, 6144
W_IN_SHARD = IN_W // N_DEV
W_OUT_SHARD = D_MODEL // N_DEV
W_ADA_SHARD = 3 * D_MODEL // N_DEV
NORM_EPS = 1e-6
MASK_VALUE = -1e30
ATTN_SCALE = HEAD_DIM ** -0.5
RET_SCALE = RET_QK ** -0.5
LN2 = math.log(2.0)

ADAM_LR, ADAM_B1, ADAM_B2, ADAM_EPS, ADAM_WD, ADAM_STEP = 0.001, 0.9, 0.999, 1e-08, 0.01, 10
ADAM_C1 = 1.0 / (1.0 - ADAM_B1 ** ADAM_STEP)
ADAM_C2 = 1.0 / (1.0 - ADAM_B2 ** ADAM_STEP)

VMEM_LIMIT_BYTES = 56 * 1024 * 1024
MESH = pl.DeviceIdType.MESH


def _params(*sem):
    return pltpu.CompilerParams(dimension_semantics=sem if sem else None, vmem_limit_bytes=VMEM_LIMIT_BYTES)


def _dot(a, b):
    return jnp.dot(a, b, preferred_element_type=F32)


def _dot_nt(a, b):
    return lax.dot_general(a, b, (((1,), (1,)), ((), ())), preferred_element_type=F32)


def _dot_tn(a, b):
    return lax.dot_general(a, b, (((0,), (0,)), ((), ())), preferred_element_type=F32)


def _iota(shape, dim):
    return lax.broadcasted_iota(jnp.int32, shape, dim)


def _sigmoid(z):
    return 1.0 / (1.0 + jnp.exp(-z))


HBM = pl.BlockSpec(memory_space=pl.ANY)


def _position():
    return lax.axis_index("x"), lax.axis_index("y"), lax.axis_index("c")


def _flat(px, py, pc):
    return 4 * px + 2 * py + pc


def _remote(src, dst, send_sem, recv_sem, to):
    return pltpu.make_async_remote_copy(src_ref=src, dst_ref=dst, send_sem=send_sem, recv_sem=recv_sem,
                                        device_id=to, device_id_type=MESH)


def _weight_blocks(fin_ref, fout_ref):
    def block(a, idx):
        if a == 0:
            return fin_ref.at[:, pl.ds(pl.multiple_of(idx * W_IN_SHARD, 128), W_IN_SHARD)]
        return fout_ref.at[pl.ds(pl.multiple_of(idx * W_OUT_SHARD, W_OUT_SHARD), W_OUT_SHARD), :]
    return block


GATHER1_SEMS = [pltpu.SemaphoreType.DMA((8,)), pltpu.SemaphoreType.DMA((8,)), pltpu.SemaphoreType.DMA((2,))]
GATHER2_SEMS = [pltpu.SemaphoreType.DMA((6,)), pltpu.SemaphoreType.DMA((6,))]
REDUCE1_SEMS = [pltpu.SemaphoreType.DMA((2,)), pltpu.SemaphoreType.DMA((2,))]
REDUCE2_SEMS = [pltpu.SemaphoreType.DMA((6,)), pltpu.SemaphoreType.DMA((6,))]


def _gather_phase1(srcs, block, send_sems, recv_sems, local_sems):
    x, y, c = _position()
    mine = [block(a, _flat(x, y, c)) for a in range(2)]
    copies = [pltpu.make_async_copy(srcs[a], mine[a], local_sems.at[a]) for a in range(2)]
    for a in range(2):
        copies.append(_remote(srcs[a], mine[a], send_sems.at[4 * a], recv_sems.at[4 * a], (x, y, 1 - c)))
        for j, (px, py) in enumerate([(1 - x, y), (x, 1 - y), (1 - x, 1 - y)]):
            copies.append(_remote(srcs[a], mine[a], send_sems.at[4 * a + 1 + j], recv_sems.at[4 * a + 1 + j], (px, py, c)))
    return copies


def _gather_phase2(block, send_sems, recv_sems):
    x, y, c = _position()
    copies = []
    for a in range(2):
        for j, (px, py) in enumerate([(1 - x, y), (x, 1 - y), (1 - x, 1 - y)]):
            blk = block(a, _flat(px, py, c))
            copies.append(_remote(blk, blk, send_sems.at[3 * a + j], recv_sems.at[3 * a + j], (x, y, 1 - c)))
    return copies


def _reduce_phase1(grads, landings, send_sems, recv_sems):
    x, y, c = _position()
    return [_remote(g.at[:, 1 - c], r, send_sems.at[a], recv_sems.at[a], (x, y, 1 - c))
            for a, (g, r) in enumerate(zip(grads, landings))]


def _reduce_phase2(sums, landings, send_sems, recv_sems):
    x, y, c = _position()
    copies = []
    for a, (p, r) in enumerate(zip(sums, landings)):
        for k in (1, 2, 3):
            to = (1 - x if k & 2 else x, 1 - y if k & 1 else y, c)
            copies.append(_remote(p.at[k], r.at[k - 1], send_sems.at[3 * a + k - 1], recv_sems.at[3 * a + k - 1], to))
    return copies


def _landing_shapes(n):
    return [jax.ShapeDtypeStruct((n, D_MODEL, W_IN_SHARD), BF16), jax.ShapeDtypeStruct((n, W_OUT_SHARD, D_MODEL), BF16)]


def _split_cores(slabs):
    return tuple(s.reshape((N_DEV // 2, 2) + s.shape[1:]) for s in slabs)


def _start_all(copies):
    for cp in copies:
        cp.start()


def _wait_all(copies):
    for cp in copies:
        cp.wait()


def _grid_edge(n_axes):
    first = last = None
    for ax in range(n_axes):
        f = pl.program_id(ax) == 0
        e = pl.program_id(ax) == pl.num_programs(ax) - 1
        first = f if first is None else first & f
        last = e if last is None else last & e
    return first, last


LANES = 128


def _stage_shape(rows, width):
    return pltpu.VMEM((width // LANES, rows, LANES), F32)


def _fill_stage(stage_ref, value):
    for t in range(stage_ref.shape[0]):
        stage_ref[t] = value[:, t * LANES:(t + 1) * LANES]


def _read_stage(stage_ref):
    return jnp.concatenate([stage_ref[t] for t in range(stage_ref.shape[0])], axis=1)


def _split_classes(stage_ref, out_refs):
    n_t, rows, _ = stage_ref.shape
    for out_ref in out_refs:
        dil = out_ref.shape[0]
        for r in range(dil):
            for t in range(n_t):
                piece = stage_ref[t, pl.ds(r, rows // dil, stride=dil), :]
                out_ref[r, :, t * LANES:(t + 1) * LANES] = piece.astype(out_ref.dtype)


def _merge_classes(in_ref, stage_ref):
    dil, per = in_ref.shape[0], in_ref.shape[1]
    for r in range(dil):
        for t in range(stage_ref.shape[0]):
            stage_ref[t, pl.ds(r, per, stride=dil), :] = in_ref[r, :, t * LANES:(t + 1) * LANES].astype(F32)


def _class_block(dil, tm, width, index_map):
    return pl.BlockSpec((dil, tm // dil, width), index_map)


def inproj_fwd(x, g, scale1p, shift, w, gather=None):
    s_len = x.shape[0]
    tm, tn = min(1024, s_len), 512

    n_qkv = QKV_A // tn

    def body(x_ref, g_ref, sc_ref, sh_ref, w_ref, *rest):
        if gather is not None:
            (win_ref, wout_ref, proj_ref, h_ref, q4_ref, q16_ref, fin_ref, fout_ref, stage,
             send_sems, recv_sems, local_sems) = rest
            first, last = _grid_edge(2)
            copies = _gather_phase1([win_ref.at[gather[2]], wout_ref.at[gather[2]]], _weight_blocks(fin_ref, fout_ref),
                                    send_sems, recv_sems, local_sems)
            pl.when(first)(lambda: _start_all(copies))
        else:
            proj_ref, h_ref, q4_ref, q16_ref, stage = rest

        @pl.when(pl.program_id(1) == 0)
        def _():
            xv = x_ref[...]
            r = lax.rsqrt(jnp.mean(xv * xv, axis=-1, keepdims=True) + NORM_EPS)
            h_ref[...] = ((xv * r * g_ref[...]) * sc_ref[...] + sh_ref[...]).astype(BF16)
        res = _dot(h_ref[...], w_ref[...])
        proj_ref[...] = res.astype(BF16)

        @pl.when(pl.program_id(1) < n_qkv)
        def _():
            _fill_stage(stage, res)
            _split_classes(stage, [q4_ref, q16_ref])
        if gather is not None:
            pl.when(last)(lambda: _wait_all(copies))

    vec = pl.BlockSpec((1, D_MODEL), lambda i, j: (0, 0))
    in_specs = [pl.BlockSpec((tm, D_MODEL), lambda i, j: (i, 0)), vec, vec, vec,
                pl.BlockSpec((D_MODEL, tn), lambda i, j: (0, j))]
    out_specs = [pl.BlockSpec((tm, tn), lambda i, j: (i, j)), pl.BlockSpec((tm, D_MODEL), lambda i, j: (i, 0))]
    out_shape = [jax.ShapeDtypeStruct((s_len, IN_W), BF16), jax.ShapeDtypeStruct((s_len, D_MODEL), BF16)]
    for dil in DILATIONS[1:]:
        out_specs.append(pl.BlockSpec((dil, tm // dil, tn), lambda i, j: (0, i, jnp.minimum(j, n_qkv - 1))))
        out_shape.append(jax.ShapeDtypeStruct((dil, s_len // dil, QKV_A), BF16))
    scratch = [_stage_shape(tm, tn)]
    extra = ()
    if gather is not None:
        in_specs += [HBM, HBM]
        out_specs += [HBM, HBM]
        out_shape += [jax.ShapeDtypeStruct((D_MODEL, IN_W), BF16), jax.ShapeDtypeStruct((D_MODEL, D_MODEL), BF16)]
        scratch += GATHER1_SEMS
        extra = tuple(gather[:2])
    return pl.pallas_call(
        body, name="inproj_fwd_gather" if gather is not None else "inproj_fwd", grid=(s_len // tm, IN_W // tn),
        in_specs=in_specs, out_specs=out_specs, out_shape=out_shape, scratch_shapes=scratch,
        compiler_params=_params("arbitrary", "arbitrary"),
    )(x, g, scale1p, shift, w, *extra)


MASK_DISTANCE = 1e33
ATTN_TILE = 128


ATTN_UNROLL_FWD, ATTN_UNROLL_BWD = 10, 6


def _attn_plan(sub_len, unroll):
    tq = min(sub_len, ATTN_TILE)
    win = min(sub_len, tq + 2 * RADIUS)
    heads = 1 if sub_len > 1024 else (2 if sub_len > 256 else N_HEADS_ATTN)
    return tq, win, sub_len // tq, heads, unroll


def _attn_tiles(sub_len, tq, win, n_tiles, unroll, tile):
    tile(0, 0, 0)
    if n_tiles > 2:
        def mid(i, carry):
            q0 = pl.multiple_of(i * tq, tq)
            tile(q0, pl.multiple_of(q0 - RADIUS, RADIUS), 1)
            return carry
        lax.fori_loop(1, n_tiles - 1, mid, 0, unroll=min(unroll, n_tiles - 2))
    if n_tiles > 1:
        tile(sub_len - tq, sub_len - win, 2)


def _attn_bias(bias_ref, head, heads, tq, win, dil):
    h = pl.program_id(1) * heads + head
    slope = jnp.exp(-(h + 1).astype(F32) * LN2 * jnp.ones((1, 1), F32))
    rel = _iota((tq, win), 1) - _iota((tq, win), 0)
    for v, off in enumerate((0, RADIUS, win - tq)):
        dist = jnp.abs(rel - off)
        bias_ref[v] = slope * jnp.where(dist <= RADIUS, (dist * dil).astype(F32), MASK_DISTANCE)


def _attn_specs(n_cls, sub_len, heads):
    width = heads * HEAD_DIM
    per = ATTN_WIDTH // width

    def col(part):
        return pl.BlockSpec((None, sub_len, width), lambda r, g: (r, 0, part * per + g))
    return col, (n_cls, N_HEADS_ATTN // heads)


def attn_fwd(qkv, dil):
    n_cls, sub_len, _ = qkv.shape
    tq, win, n_tiles, heads, unroll = _attn_plan(sub_len, ATTN_UNROLL_FWD)

    def body(q_ref, k_ref, v_ref, o_ref, lse_ref, bias_ref):
        for head in range(heads):
            lanes = slice(head * HEAD_DIM, (head + 1) * HEAD_DIM)
            _attn_bias(bias_ref, head, heads, tq, win, dil)

            def tile(q0, start, variant):
                s = _dot_nt(q_ref[pl.ds(q0, tq), lanes], k_ref[pl.ds(start, win), lanes]) * ATTN_SCALE - bias_ref[variant]
                m = jnp.max(s, axis=1, keepdims=True)
                p = jnp.exp(s - m)
                den = jnp.sum(p, axis=1, keepdims=True)
                o_ref[pl.ds(q0, tq), lanes] = _dot(p.astype(BF16), v_ref[pl.ds(start, win), lanes]) / den
                lse_ref[pl.ds(q0, tq), lanes] = jnp.broadcast_to(m + jnp.log(den), (tq, HEAD_DIM))

            _attn_tiles(sub_len, tq, win, n_tiles, unroll, tile)

    col, grid = _attn_specs(n_cls, sub_len, heads)
    out = jax.ShapeDtypeStruct((n_cls, sub_len, ATTN_WIDTH), F32)
    return pl.pallas_call(
        body, name=f"attn_fwd_d{dil}", grid=grid,
        in_specs=[col(0), col(1), col(2)], out_specs=[col(0), col(0)], out_shape=[out, out],
        scratch_shapes=[pltpu.VMEM((3, tq, win), F32)],
        compiler_params=_params("arbitrary", "arbitrary"),
    )(qkv, qkv, qkv)


def attn_bwd(qkv, do, lse, delta, dil):
    n_cls, sub_len, _ = qkv.shape
    tq, win, n_tiles, heads, unroll = _attn_plan(sub_len, ATTN_UNROLL_BWD)

    def body(q_ref, k_ref, v_ref, do_ref, lse_ref, dl_ref, dq_ref, dk_ref, dv_ref, bias_ref, dk_acc, dv_acc):
        for head in range(heads):
            lanes = slice(head * HEAD_DIM, (head + 1) * HEAD_DIM)
            _attn_bias(bias_ref, head, heads, tq, win, dil)
            dk_acc[...] = jnp.zeros_like(dk_acc)
            dv_acc[...] = jnp.zeros_like(dv_acc)

            def tile(q0, start, variant):
                q = q_ref[pl.ds(q0, tq), lanes]
                k = k_ref[pl.ds(start, win), lanes]
                v = v_ref[pl.ds(start, win), lanes]
                dov = do_ref[pl.ds(q0, tq), lanes]
                s = _dot_nt(q, k) * ATTN_SCALE - bias_ref[variant]
                p = jnp.exp(s - lse_ref[pl.ds(q0, tq), head * HEAD_DIM:head * HEAD_DIM + 1])
                ds = (p * (_dot_nt(dov, v) - dl_ref[pl.ds(q0, tq), head * HEAD_DIM:head * HEAD_DIM + 1])).astype(BF16)
                dq_ref[pl.ds(q0, tq), lanes] = (_dot(ds, k) * ATTN_SCALE).astype(BF16)
                dk_acc[pl.ds(start, win), :] += _dot_tn(ds, q) * ATTN_SCALE
                dv_acc[pl.ds(start, win), :] += _dot_tn(p.astype(BF16), dov)

            _attn_tiles(sub_len, tq, win, n_tiles, unroll, tile)
            dk_ref[:, lanes] = dk_acc[...].astype(BF16)
            dv_ref[:, lanes] = dv_acc[...].astype(BF16)

    col, grid = _attn_specs(n_cls, sub_len, heads)
    out = jax.ShapeDtypeStruct((n_cls, sub_len, ATTN_WIDTH), BF16)
    return pl.pallas_call(
        body, name=f"attn_bwd_d{dil}", grid=grid,
        in_specs=[col(0), col(1), col(2), col(0), col(0), col(0)],
        out_specs=[col(0), col(0), col(0)], out_shape=[out, out, out],
        scratch_shapes=[pltpu.VMEM((3, tq, win), F32), pltpu.VMEM((sub_len, HEAD_DIM), F32),
                        pltpu.VMEM((sub_len, HEAD_DIM), F32)],
        compiler_params=_params("arbitrary", "arbitrary"),
    )(qkv, qkv, qkv, do, lse, delta)


RET_UNROLL = 8
RET_UNROLL_BWD = 4


def _ret_tables(lg_ref):
    h = pl.program_id(0)
    one = jnp.ones((1, 1), F32)
    lgf, lgb = lg_ref[h] * one, lg_ref[RET_HEADS + h] * one
    c = RET_CHUNK
    rel = (_iota((c, c), 0) - _iota((c, c), 1)).astype(F32)
    dec_f = jnp.where(rel >= 0, jnp.exp(jnp.maximum(rel, 0.0) * lgf), 0.0)
    dec_b = jnp.where(rel <= 0, jnp.exp(jnp.maximum(-rel, 0.0) * lgb), 0.0)
    ci = _iota((c, 1), 0).astype(F32)
    tab = dict(rel=rel, dec_f=dec_f, dec_b=dec_b, ci=ci,
               xi_f=jnp.exp((ci + 1.0) * lgf), ze_f=jnp.exp((c - 1.0 - ci) * lgf), g_f=jnp.exp(c * lgf),
               xi_b=jnp.exp((c - ci) * lgb), ze_b=jnp.exp(ci * lgb), g_b=jnp.exp(c * lgb))
    return tab


def _ret_specs(s_len):
    q = pl.BlockSpec((s_len, RET_QK), lambda h: (0, COL_QR // RET_QK + h))
    k = pl.BlockSpec((s_len, RET_QK), lambda h: (0, COL_KR // RET_QK + h))
    v = pl.BlockSpec((s_len, RET_V), lambda h: (0, COL_VR // RET_V + h))
    wide = pl.BlockSpec((s_len, RET_V), lambda h: (0, h))
    narrow = pl.BlockSpec((s_len, RET_QK), lambda h: (0, h))
    smem = pl.BlockSpec(memory_space=pltpu.SMEM)
    return smem, q, k, v, wide, narrow


def ret_fwd(proj, lg, finish=None):
    s_len = proj.shape[0]
    c, n_chunks = RET_CHUNK, proj.shape[0] // RET_CHUNK

    def body(lg_ref, q_ref, k_ref, v_ref, *rest):
        if finish is not None:
            _, _, opre_ref, y_ref, fin_ref, fout_ref, st_f, st_b, send_sems, recv_sems = rest
            first, last = _grid_edge(1)
            copies = _gather_phase2(_weight_blocks(fin_ref, fout_ref), send_sems, recv_sems)
            pl.when(first)(lambda: _start_all(copies))
        else:
            opre_ref, y_ref, st_f, st_b = rest
        t = _ret_tables(lg_ref)
        dec = t["dec_f"] + t["dec_b"]
        st_f[...] = jnp.zeros_like(st_f)
        st_b[...] = jnp.zeros_like(st_b)

        def load(n):
            r0 = pl.multiple_of(n * c, c)
            q, k, v = q_ref[pl.ds(r0, c), :], k_ref[pl.ds(r0, c), :], v_ref[pl.ds(r0, c), :]
            return r0, q, (k.astype(F32) * RET_SCALE), v

        def fwd(n, carry):
            r0, q, kf, v = load(n)
            inner = (_dot_nt(q, kf.astype(BF16)) * dec).astype(BF16)
            opre_ref[pl.ds(r0, c), :] = _dot(inner, v) + _dot(q, st_f[...].astype(BF16)) * t["xi_f"]
            st_f[...] = st_f[...] * t["g_f"] + _dot_tn((kf * t["ze_f"]).astype(BF16), v)
            return carry

        def bwd(i, carry):
            r0, q, kf, v = load(n_chunks - 1 - i)
            o = opre_ref[pl.ds(r0, c), :] + _dot(q, st_b[...].astype(BF16)) * t["xi_b"]
            st_b[...] = st_b[...] * t["g_b"] + _dot_tn((kf * t["ze_b"]).astype(BF16), v)
            opre_ref[pl.ds(r0, c), :] = o
            y_ref[pl.ds(r0, c), :] = o * lax.rsqrt(jnp.mean(o * o, axis=-1, keepdims=True) + NORM_EPS)
            return carry

        lax.fori_loop(0, n_chunks, fwd, 0, unroll=min(RET_UNROLL, n_chunks))
        lax.fori_loop(0, n_chunks, bwd, 0, unroll=min(RET_UNROLL, n_chunks))
        if finish is not None:
            pl.when(last)(lambda: _wait_all(copies))

    smem, q, k, v, wide, _ = _ret_specs(s_len)
    out = jax.ShapeDtypeStruct((s_len, RET_HEADS * RET_V), F32)
    in_specs, out_specs, out_shape = [smem, q, k, v], [wide, wide], [out, out]
    scratch = [pltpu.VMEM((RET_QK, RET_V), F32), pltpu.VMEM((RET_QK, RET_V), F32)]
    aliases, extra = {}, ()
    if finish is not None:
        in_specs += [HBM, HBM]
        out_specs += [HBM, HBM]
        out_shape += _full_weight_shapes()
        scratch += GATHER2_SEMS
        aliases, extra = {4: 2, 5: 3}, tuple(finish)
    return pl.pallas_call(
        body, name="ret_fwd_gather2" if finish is not None else "ret_fwd", grid=(RET_HEADS,), in_specs=in_specs,
        out_specs=out_specs, out_shape=out_shape, scratch_shapes=scratch, input_output_aliases=aliases,
        compiler_params=_params("arbitrary"),
    )(lg, proj, proj, proj, *extra)


def ret_bwd(proj, lg, o_pre, dy):
    s_len = proj.shape[0]
    c, n_chunks = RET_CHUNK, proj.shape[0] // RET_CHUNK
    cf = float(c)

    def body(lg_ref, q_ref, k_ref, v_ref, o_ref, dy_ref, dq_ref, dk_ref, dv_ref, glf_ref, glb_ref,
             st_f, dst_b, st_b, dst_f, keep_sf, keep_dtb, acc_f, acc_b, acc_sf, acc_sb):
        t = _ret_tables(lg_ref)
        dec = t["dec_f"] + t["dec_b"]
        e_f, e_b, ci = t["rel"] * t["dec_f"], -t["rel"] * t["dec_b"], t["ci"]
        for ref in (st_f, dst_b, st_b, dst_f, acc_f, acc_b, acc_sf, acc_sb):
            ref[...] = jnp.zeros_like(ref)

        def load(n):
            r0 = pl.multiple_of(n * c, c)
            q, k, v = q_ref[pl.ds(r0, c), :], k_ref[pl.ds(r0, c), :], v_ref[pl.ds(r0, c), :]
            o, dyv = o_ref[pl.ds(r0, c), :], dy_ref[pl.ds(r0, c), :]
            rr = lax.rsqrt(jnp.mean(o * o, axis=-1, keepdims=True) + NORM_EPS)
            y = o * rr
            do = (rr * (dyv - y * jnp.mean(dyv * y, axis=-1, keepdims=True))).astype(BF16)
            return r0, q, k.astype(F32) * RET_SCALE, v, do

        def fwd(n, carry):
            r0, q, kf, v, do = load(n)
            qf, kb = q.astype(F32), kf.astype(BF16)
            a = _dot_nt(q, kb)
            b = _dot_nt(do, v)
            da = (b * dec).astype(BF16)
            ab = a * b
            sf_b, dtb_b = st_f[...].astype(BF16), dst_b[...].astype(BF16)
            dq_inter = _dot_nt(do, sf_b) * t["xi_f"]
            dk_inter = _dot_nt(v, dtb_b) * t["ze_b"]
            acc_f[...] += e_f * ab + (ci + 1.0) * (qf * dq_inter)
            acc_b[...] += e_b * ab + ci * (kf * dk_inter)
            dq_ref[pl.ds(r0, c), :] = _dot(da, kb) + dq_inter
            dk_ref[pl.ds(r0, c), :] = _dot_tn(da, q) + dk_inter
            dv_ref[pl.ds(r0, c), :] = _dot_tn((a * dec).astype(BF16), do) + _dot((kf * t["ze_b"]).astype(BF16), dtb_b)
            keep_sf[n] = sf_b
            keep_dtb[n] = dtb_b
            st_f[...] = st_f[...] * t["g_f"] + _dot_tn((kf * t["ze_f"]).astype(BF16), v)
            dst_b[...] = dst_b[...] * t["g_b"] + _dot_tn((qf * t["xi_b"]).astype(BF16), do)
            return carry

        def bwd(i, carry):
            n = n_chunks - 1 - i
            r0, q, kf, v, do = load(n)
            qf = q.astype(F32)
            tb_b, dsf_b = st_b[...].astype(BF16), dst_f[...].astype(BF16)
            dq_inter = _dot_nt(do, tb_b) * t["xi_b"]
            dk_inter = _dot_nt(v, dsf_b) * t["ze_f"]
            acc_b[...] += (cf - ci) * (qf * dq_inter)
            acc_f[...] += (cf - 1.0 - ci) * (kf * dk_inter)
            acc_sb[...] += keep_dtb[n].astype(F32) * st_b[...]
            acc_sf[...] += dst_f[...] * keep_sf[n].astype(F32)
            dq_ref[pl.ds(r0, c), :] += dq_inter
            dk_ref[pl.ds(r0, c), :] = (dk_ref[pl.ds(r0, c), :] + dk_inter) * RET_SCALE
            dv_ref[pl.ds(r0, c), :] += _dot((kf * t["ze_f"]).astype(BF16), dsf_b)
            st_b[...] = st_b[...] * t["g_b"] + _dot_tn((kf * t["ze_b"]).astype(BF16), v)
            dst_f[...] = dst_f[...] * t["g_f"] + _dot_tn((qf * t["xi_f"]).astype(BF16), do)
            return carry

        lax.fori_loop(0, n_chunks, fwd, 0, unroll=min(RET_UNROLL_BWD, n_chunks))
        lax.fori_loop(0, n_chunks, bwd, 0, unroll=min(RET_UNROLL_BWD, n_chunks))

        def total(x):
            return jnp.sum(jnp.sum(x, axis=1, keepdims=True), axis=0, keepdims=True)

        glf_ref[...] = jnp.broadcast_to(total(acc_f[...]) + cf * t["g_f"] * total(acc_sf[...]), (8, 128))
        glb_ref[...] = jnp.broadcast_to(total(acc_b[...]) + cf * t["g_b"] * total(acc_sb[...]), (8, 128))

    smem, q, k, v, wide, narrow = _ret_specs(s_len)
    scal = pl.BlockSpec((None, 8, 128), lambda h: (h, 0, 0))
    state = pltpu.VMEM((RET_QK, RET_V), F32)
    square = pltpu.VMEM((RET_CHUNK, RET_QK), F32)
    keep = pltpu.VMEM((n_chunks, RET_QK, RET_V), BF16)
    return pl.pallas_call(
        body, name="ret_bwd", grid=(RET_HEADS,), in_specs=[smem, q, k, v, wide, wide],
        out_specs=[narrow, narrow, wide, scal, scal],
        out_shape=[jax.ShapeDtypeStruct((s_len, RET_HEADS * RET_QK), F32), jax.ShapeDtypeStruct((s_len, RET_HEADS * RET_QK), F32),
                   jax.ShapeDtypeStruct((s_len, RET_HEADS * RET_V), F32),
                   jax.ShapeDtypeStruct((RET_HEADS, 8, 128), F32), jax.ShapeDtypeStruct((RET_HEADS, 8, 128), F32)],
        scratch_shapes=[state, state, state, state, keep, keep, square, square, state, state],
        compiler_params=_params("arbitrary"),
    )(lg, proj, proj, proj, o_pre, dy)


def _silu_parts(z):
    sig = _sigmoid(z)
    return z * sig, sig * (1.0 + z * (1.0 - sig))


def outproj_fwd(x, gate, w_out, attn_outs, y_r, proj):
    s_len = x.shape[0]
    tm = min(256, s_len)

    def body(x_ref, gate_ref, w_ref, o1, l1, o2, l2, o3, l3, yr_ref, za_ref, zr_ref,
             xn_ref, oa_ref, lse_ref, lse4_ref, lse16_ref, so2, sl2, so3, sl3):
        for src, dst in ((o2, so2), (l2, sl2), (o3, so3), (l3, sl3)):
            _merge_classes(src, dst)
        la, lb, lc = l1[...], _read_stage(sl2), _read_stage(sl3)
        m = jnp.maximum(jnp.maximum(la, lb), lc)
        lse = m + jnp.log(jnp.exp(la - m) + jnp.exp(lb - m) + jnp.exp(lc - m))
        o_a = jnp.exp(la - lse) * o1[...] + jnp.exp(lb - lse) * _read_stage(so2) + jnp.exp(lc - lse) * _read_stage(so3)
        oa_ref[...] = o_a
        lse_ref[...] = lse
        _fill_stage(sl2, lse)
        _split_classes(sl2, [lse4_ref, lse16_ref])
        silu_a, _ = _silu_parts(za_ref[...].astype(F32))
        silu_r, _ = _silu_parts(zr_ref[...].astype(F32))
        y = jnp.concatenate([(o_a * silu_a).astype(BF16), (yr_ref[...] * silu_r).astype(BF16)], axis=1)
        xn_ref[...] = x_ref[...] + gate_ref[...] * _dot(y, w_ref[...])

    row = lambda w: pl.BlockSpec((tm, w), lambda i: (i, 0))
    half = row(ATTN_WIDTH)
    cls = [_class_block(dil, tm, ATTN_WIDTH, lambda i: (0, i, 0)) for dil in DILATIONS[1:]]
    flat = [a for pair in attn_outs for a in pair]
    sds = jax.ShapeDtypeStruct
    return pl.pallas_call(
        body, name="outproj_fwd", grid=(s_len // tm,),
        in_specs=[row(D_MODEL), pl.BlockSpec((1, D_MODEL), lambda i: (0, 0)),
                  pl.BlockSpec((D_MODEL, D_MODEL), lambda i: (0, 0)), half, half, cls[0], cls[0], cls[1], cls[1], half,
                  pl.BlockSpec((tm, ATTN_WIDTH), lambda i: (i, COL_ZA // ATTN_WIDTH)),
                  pl.BlockSpec((tm, ATTN_WIDTH), lambda i: (i, COL_ZR // ATTN_WIDTH))],
        out_specs=[row(D_MODEL), half, half] + cls,
        out_shape=[sds((s_len, D_MODEL), F32), sds((s_len, ATTN_WIDTH), F32), sds((s_len, ATTN_WIDTH), F32)]
                  + [sds((dil, s_len // dil, ATTN_WIDTH), F32) for dil in DILATIONS[1:]],
        scratch_shapes=[_stage_shape(tm, ATTN_WIDTH)] * 4,
        compiler_params=_params("arbitrary"),
    )(x, gate, w_out, *flat, y_r, proj, proj)


def loss_head(x, gain, target):
    s_len = x.shape[0]
    tm = min(256, s_len)

    def body(x_ref, g_ref, t_ref, dx_ref, loss_ref, dg_ref):
        @pl.when(pl.program_id(0) == 0)
        def _():
            loss_ref[...] = jnp.zeros_like(loss_ref)
            dg_ref[...] = jnp.zeros_like(dg_ref)
        xv, g = x_ref[...], g_ref[...]
        r = lax.rsqrt(jnp.mean(xv * xv, axis=-1, keepdims=True) + NORM_EPS)
        xn = xv * r
        err = xn * g - t_ref[...]
        part = 0.5 * jnp.sum(jnp.mean(err * err, axis=-1, keepdims=True), axis=0, keepdims=True)
        loss_ref[...] += jnp.broadcast_to(part, loss_ref.shape)
        dy = err * (1.0 / D_MODEL)
        dg_ref[...] += jnp.sum(dy * xn, axis=0, keepdims=True)
        dxn = dy * g
        dx_ref[...] = r * (dxn - xn * jnp.mean(dxn * xn, axis=-1, keepdims=True))

    row = pl.BlockSpec((tm, D_MODEL), lambda i: (i, 0))
    vec = pl.BlockSpec((1, D_MODEL), lambda i: (0, 0))
    return pl.pallas_call(
        body, name="loss_head", grid=(s_len // tm,), in_specs=[row, vec, row],
        out_specs=[row, pl.BlockSpec((8, 128), lambda i: (0, 0)), vec],
        out_shape=[jax.ShapeDtypeStruct((s_len, D_MODEL), F32), jax.ShapeDtypeStruct((8, 128), F32),
                   jax.ShapeDtypeStruct((1, D_MODEL), F32)],
        compiler_params=_params("arbitrary"),
    )(x, gain, target)


def outproj_bwd(dxn, gate, w_out, o_a, y_r, proj):
    s_len = dxn.shape[0]
    tm = min(256, s_len)

    def body(dx_ref, gate_ref, w_ref, oa_ref, yr_ref, za_ref, zr_ref,
             doa_ref, dl_ref, dyr_ref, dza_ref, dzr_ref, y_ref, dxb_ref, do4, do16, dl4, dl16, stage):
        dxv = dx_ref[...]
        dxb_ref[...] = dxv.astype(BF16)
        dy = _dot_nt((dxv * gate_ref[...]).astype(BF16), w_ref[...])
        dy_a, dy_r = dy[:, :ATTN_WIDTH], dy[:, ATTN_WIDTH:]
        o_a, y_rv = oa_ref[...], yr_ref[...]
        silu_a, dsilu_a = _silu_parts(za_ref[...].astype(F32))
        silu_r, dsilu_r = _silu_parts(zr_ref[...].astype(F32))
        do_a = dy_a * silu_a
        doa_ref[...] = do_a.astype(BF16)
        _fill_stage(stage, do_a)
        _split_classes(stage, [do4, do16])
        prod = do_a * o_a
        delta = jnp.concatenate(
            [jnp.broadcast_to(jnp.sum(prod[:, h * HEAD_DIM:(h + 1) * HEAD_DIM], axis=1, keepdims=True), (tm, HEAD_DIM))
             for h in range(N_HEADS_ATTN)], axis=1)
        dl_ref[...] = delta
        _fill_stage(stage, delta)
        _split_classes(stage, [dl4, dl16])
        dyr_ref[...] = dy_r * silu_r
        dza_ref[...] = (dy_a * o_a * dsilu_a).astype(BF16)
        dzr_ref[...] = (dy_r * y_rv * dsilu_r).astype(BF16)
        y_ref[...] = jnp.concatenate([(o_a * silu_a).astype(BF16), (y_rv * silu_r).astype(BF16)], axis=1)

    row = lambda w: pl.BlockSpec((tm, w), lambda i: (i, 0))
    half = row(ATTN_WIDTH)
    sds = lambda w, dt: jax.ShapeDtypeStruct((s_len, w), dt)
    in_specs = [row(D_MODEL), pl.BlockSpec((1, D_MODEL), lambda i: (0, 0)),
                pl.BlockSpec((D_MODEL, D_MODEL), lambda i: (0, 0)), half, half,
                pl.BlockSpec((tm, ATTN_WIDTH), lambda i: (i, COL_ZA // ATTN_WIDTH)),
                pl.BlockSpec((tm, ATTN_WIDTH), lambda i: (i, COL_ZR // ATTN_WIDTH))]
    cls = [_class_block(dil, tm, ATTN_WIDTH, lambda i: (0, i, 0)) for dil in DILATIONS[1:]]
    out_specs = [half, half, half, half, half, row(D_MODEL), row(D_MODEL)] + cls + cls
    out_shape = [sds(ATTN_WIDTH, BF16), sds(ATTN_WIDTH, F32), sds(ATTN_WIDTH, F32), sds(ATTN_WIDTH, BF16),
                 sds(ATTN_WIDTH, BF16), sds(D_MODEL, BF16), sds(D_MODEL, BF16)]
    out_shape += [jax.ShapeDtypeStruct((dil, s_len // dil, ATTN_WIDTH), dt) for dt in (BF16, F32) for dil in DILATIONS[1:]]
    return pl.pallas_call(
        body, name="outproj_bwd", grid=(s_len // tm,),
        in_specs=in_specs, out_specs=out_specs, out_shape=out_shape, scratch_shapes=[_stage_shape(tm, ATTN_WIDTH)],
        compiler_params=_params("arbitrary"),
    )(dxn, gate, w_out, o_a, y_r, proj, proj)


def wout_grad(y, dxb, gate, w_out):
    s_len = y.shape[0]
    tf, ts = 512, min(512, s_len)

    def body(y_ref, dx_ref, gate_ref, w_ref, dw_ref, dgate_ref, acc):
        f, s = pl.program_id(0), pl.program_id(1)

        @pl.when((f == 0) & (s == 0))
        def _():
            dgate_ref[...] = jnp.zeros_like(dgate_ref)

        @pl.when(s == 0)
        def _():
            acc[...] = jnp.zeros_like(acc)
        acc[...] += _dot_tn(y_ref[...], dx_ref[...])

        @pl.when(s == pl.num_programs(1) - 1)
        def _():
            m = acc[...]
            dw_ref[...] = (m * gate_ref[...]).astype(BF16).reshape(dw_ref.shape)
            dgate_ref[...] += jnp.sum(m * w_ref[...].astype(F32), axis=0, keepdims=True)

    per = tf // W_OUT_SHARD
    return pl.pallas_call(
        body, name="wout_grad", grid=(D_MODEL // tf, s_len // ts),
        in_specs=[pl.BlockSpec((ts, tf), lambda f, s: (s, f)), pl.BlockSpec((ts, D_MODEL), lambda f, s: (s, 0)),
                  pl.BlockSpec((1, D_MODEL), lambda f, s: (0, 0)), pl.BlockSpec((tf, D_MODEL), lambda f, s: (f, 0))],
        out_specs=[pl.BlockSpec((per, W_OUT_SHARD, D_MODEL), lambda f, s: (f, 0, 0)),
                   pl.BlockSpec((1, D_MODEL), lambda f, s: (0, 0))],
        out_shape=[jax.ShapeDtypeStruct((N_DEV, W_OUT_SHARD, D_MODEL), BF16), jax.ShapeDtypeStruct((1, D_MODEL), F32)],
        scratch_shapes=[pltpu.VMEM((tf, D_MODEL), F32)],
        compiler_params=_params("arbitrary", "arbitrary"),
    )(y, dxb, gate, w_out)


def assemble_dproj(dqkv_a, dz_a, dq_r, dk_r, dv_r, dz_r):
    s_len = dz_a.shape[0]
    tm = min(256, s_len)

    def body(*refs):
        pat, (dza, dqr, dkr, dvr, dzr, out, stage) = refs[:9], refs[9:]
        for t in range(3):
            tot = pat[t][...].astype(F32)
            for p in (1, 2):
                _merge_classes(pat[3 * p + t], stage)
                tot = tot + _read_stage(stage)
            out[:, t * ATTN_WIDTH:(t + 1) * ATTN_WIDTH] = tot.astype(BF16)
        out[:, COL_ZA:COL_QR] = dza[...]
        out[:, COL_QR:COL_KR] = dqr[...].astype(BF16)
        out[:, COL_KR:COL_VR] = dkr[...].astype(BF16)
        out[:, COL_VR:COL_ZR] = dvr[...].astype(BF16)
        out[:, COL_ZR:IN_W] = dzr[...]

    row = lambda w: pl.BlockSpec((tm, w), lambda i: (i, 0))
    cls = [_class_block(dil, tm, ATTN_WIDTH, lambda i: (0, i, 0)) for dil in DILATIONS[1:]]
    flat = [dqkv_a[p][t] for p in range(3) for t in range(3)]
    return pl.pallas_call(
        body, name="assemble_dproj", grid=(s_len // tm,),
        in_specs=[row(ATTN_WIDTH)] * 3 + [cls[0]] * 3 + [cls[1]] * 3
                 + [row(ATTN_WIDTH), row(512), row(512), row(ATTN_WIDTH), row(ATTN_WIDTH)],
        out_specs=row(IN_W), out_shape=jax.ShapeDtypeStruct((s_len, IN_W), BF16),
        scratch_shapes=[_stage_shape(tm, ATTN_WIDTH)],
        compiler_params=_params("arbitrary"),
    )(*flat, dz_a, dq_r, dk_r, dv_r, dz_r)


def inproj_bwd(dproj, w, x, g, scale1p, dxn, exchange):
    s_len = x.shape[0]
    tm, tk = min(512, s_len), 1024

    def body(dp_ref, w_ref, x_ref, g_ref, sc_ref, dxn_ref, pa_ref, pb_ref, dx_ref, st_ref, ra_ref, rb_ref,
             acc, send_sems, recv_sems):
        first, last = _grid_edge(2)
        copies = _reduce_phase2([pa_ref, pb_ref], [ra_ref, rb_ref], send_sems, recv_sems)
        pl.when(first)(lambda: _start_all(copies))
        i, k = pl.program_id(0), pl.program_id(1)

        @pl.when((i == 0) & (k == 0))
        def _():
            st_ref[...] = jnp.zeros_like(st_ref)

        @pl.when(k == 0)
        def _():
            acc[...] = jnp.zeros_like(acc)
        acc[...] += _dot_nt(dp_ref[...], w_ref[...])

        @pl.when(k == pl.num_programs(1) - 1)
        def _():
            dh, xv = acc[...], x_ref[...]
            r = lax.rsqrt(jnp.mean(xv * xv, axis=-1, keepdims=True) + NORM_EPS)
            xn = xv * r
            st_ref[0:1, :] += jnp.sum(dh, axis=0, keepdims=True)
            st_ref[3:4, :] += jnp.sum(dh * xn, axis=0, keepdims=True)
            dn = dh * (sc_ref[...] * g_ref[...])
            dx_ref[...] = r * (dn - xn * jnp.mean(dn * xn, axis=-1, keepdims=True)) + dxn_ref[...]

        @pl.when(last)
        def _():
            st_ref[1:2, :] = st_ref[3:4, :] * g_ref[...]
            st_ref[2:3, :] = st_ref[3:4, :] * sc_ref[...]
            _wait_all(copies)

    row = pl.BlockSpec((tm, D_MODEL), lambda i, k: (i, 0))
    vec = pl.BlockSpec((1, D_MODEL), lambda i, k: (0, 0))
    return pl.pallas_call(
        body, name="inproj_bwd_reduce2", grid=(s_len // tm, IN_W // tk),
        in_specs=[pl.BlockSpec((tm, tk), lambda i, k: (i, k)), pl.BlockSpec((D_MODEL, tk), lambda i, k: (0, k)),
                  row, vec, vec, row, HBM, HBM],
        out_specs=[row, pl.BlockSpec((8, D_MODEL), lambda i, k: (0, 0)), HBM, HBM],
        out_shape=[jax.ShapeDtypeStruct((s_len, D_MODEL), F32), jax.ShapeDtypeStruct((8, D_MODEL), F32)]
                  + _landing_shapes(3),
        scratch_shapes=[pltpu.VMEM((tm, D_MODEL), F32)] + REDUCE2_SEMS,
        compiler_params=_params("arbitrary", "arbitrary"),
    )(dproj, w, x, g, scale1p, dxn, *exchange)


def win_grad(h, dproj):
    s_len = h.shape[0]
    ts = min(1024, s_len)

    def body(h_ref, dp_ref, dw_ref, acc):
        s = pl.program_id(1)

        @pl.when(s == 0)
        def _():
            acc[...] = jnp.zeros_like(acc)
        acc[...] += _dot_tn(h_ref[...], dp_ref[...])

        @pl.when(s == pl.num_programs(1) - 1)
        def _():
            dw_ref[...] = acc[...].astype(BF16)

    return pl.pallas_call(
        body, name="win_grad", grid=(N_DEV, s_len // ts),
        in_specs=[pl.BlockSpec((ts, D_MODEL), lambda j, s: (s, 0)), pl.BlockSpec((ts, W_IN_SHARD), lambda j, s: (s, j))],
        out_specs=pl.BlockSpec((None, D_MODEL, W_IN_SHARD), lambda j, s: (j, 0, 0)),
        out_shape=jax.ShapeDtypeStruct((N_DEV, D_MODEL, W_IN_SHARD), BF16),
        scratch_shapes=[pltpu.VMEM((D_MODEL, W_IN_SHARD), F32)],
        compiler_params=_params("arbitrary", "arbitrary"),
    )(h, dproj)


def ada_fwd(c_all, w_ada):
    def body(c_ref, w_ref, act_ref, part_ref):
        cv = c_ref[...]
        act = cv * _sigmoid(cv)
        act_ref[...] = act
        part_ref[...] = _dot(act.astype(BF16), w_ref[...].astype(BF16))

    return pl.pallas_call(
        body, name="ada_fwd", grid=(DEPTH,),
        in_specs=[pl.BlockSpec((N_DEV, D_MODEL), lambda l: (0, 0)),
                  pl.BlockSpec((None, D_MODEL, W_ADA_SHARD), lambda l: (l, 0, 0))],
        out_specs=[pl.BlockSpec((N_DEV, D_MODEL), lambda l: (0, 0)),
                   pl.BlockSpec((None, N_DEV, W_ADA_SHARD), lambda l: (l, 0, 0))],
        out_shape=[jax.ShapeDtypeStruct((N_DEV, D_MODEL), F32), jax.ShapeDtypeStruct((DEPTH, N_DEV, W_ADA_SHARD), F32)],
        compiler_params=_params("arbitrary"),
    )(c_all, w_ada)


def _adamw(w, g, m, v):
    m = ADAM_B1 * m + (1.0 - ADAM_B1) * g
    v = ADAM_B2 * v + (1.0 - ADAM_B2) * (g * g)
    delta = -ADAM_LR * ((m * ADAM_C1) / (jnp.sqrt(v * ADAM_C2) + ADAM_EPS) + ADAM_WD * w)
    return delta, m, v


def ada_update(act_t, dmod, w, m, v):
    tr = 512

    def body(a_ref, d_ref, w_ref, m_ref, v_ref, g_out, dl_out, m_out, v_out):
        a = a_ref[...].astype(BF16).astype(F32)
        d = d_ref[...].astype(BF16).astype(F32)
        g = a[:, 0:1] * d[0:1, :]
        for b in range(1, N_DEV):
            g = g + a[:, b:b + 1] * d[b:b + 1, :]
        g_out[...] = g
        dl_out[...], m_out[...], v_out[...] = _adamw(w_ref[...], g, m_ref[...], v_ref[...])

    blk = pl.BlockSpec((None, tr, W_ADA_SHARD), lambda l, r: (l, r, 0))
    out = jax.ShapeDtypeStruct(w.shape, F32)
    return pl.pallas_call(
        body, name="ada_update", grid=(DEPTH, D_MODEL // tr),
        in_specs=[pl.BlockSpec((tr, N_DEV), lambda l, r: (r, 0)),
                  pl.BlockSpec((None, N_DEV, W_ADA_SHARD), lambda l, r: (l, 0, 0)), blk, blk, blk],
        out_specs=[blk] * 4, out_shape=[out] * 4, compiler_params=_params("arbitrary", "arbitrary"),
    )(act_t, dmod, w, m, v)


def chip_sum(pos, grads, landed, name):
    _, _, n_rows, n_cols = grads.shape
    tr = min(512, n_rows)

    def chip(k, pos_ref):
        return (pos_ref[0] ^ (k // 2)) * 2 + (pos_ref[1] ^ (k % 2))

    def body(pos_ref, g_ref, r_ref, out_ref):
        out_ref[...] = (g_ref[...].astype(F32) + r_ref[...].astype(F32)).astype(BF16)

    return pl.pallas_call(
        body, name=name,
        grid_spec=pltpu.PrefetchScalarGridSpec(
            num_scalar_prefetch=1, grid=(N_DEV // 2, n_rows // tr),
            in_specs=[pl.BlockSpec((None, None, tr, n_cols), lambda k, r, p: (chip(k, p), p[2], r, 0)),
                      pl.BlockSpec((None, tr, n_cols), lambda k, r, p: (chip(k, p), r, 0))],
            out_specs=pl.BlockSpec((None, tr, n_cols), lambda k, r, p: (k, r, 0))),
        out_shape=jax.ShapeDtypeStruct((N_DEV // 2, n_rows, n_cols), BF16),
        compiler_params=_params("arbitrary", "arbitrary"),
    )(pos, grads, landed)


def shard_update(layer, own, others, w, m, v, prev, name):
    _, n_rows, n_cols = w.shape
    tr = min(256, n_rows)

    def body(own_ref, oth_ref, w_ref, m_ref, v_ref, *rest):
        g_out, dl_out, m_out, v_out = rest[-4:]
        g = own_ref[...].astype(F32)
        for k in range(3):
            g = g + oth_ref[k].astype(F32)
        g_out[...] = g
        dl_out[...], m_out[...], v_out[...] = _adamw(w_ref[...], g, m_ref[...], v_ref[...])

    blk = pl.BlockSpec((None, tr, n_cols), lambda r: (layer, r, 0))
    out = jax.ShapeDtypeStruct(w.shape, F32)
    in_specs = [pl.BlockSpec((None, tr, n_cols), lambda r: (0, r, 0)), pl.BlockSpec((3, tr, n_cols), lambda r: (0, r, 0)),
                blk, blk, blk]
    aliases, extra = {}, ()
    if prev is not None:
        in_specs += [HBM] * 4
        aliases = {5 + t: t for t in range(4)}
        extra = tuple(prev)
    return pl.pallas_call(
        body, name=name, grid=(n_rows // tr,), in_specs=in_specs, out_specs=[blk] * 4, out_shape=[out] * 4,
        input_output_aliases=aliases, compiler_params=_params("arbitrary"),
    )(own, others, w, m, v, *extra)


def small_update(parts, w, m, v):
    def body(p_ref, w_ref, m_ref, v_ref, g_out, dl_out, m_out, v_out):
        g = p_ref[0]
        for k in range(1, N_DEV):
            g = g + p_ref[k]
        g_out[...] = g
        dl_out[...], m_out[...], v_out[...] = _adamw(w_ref[...], g, m_ref[...], v_ref[...])

    out = jax.ShapeDtypeStruct(w.shape, F32)
    return pl.pallas_call(body, name="small_update", out_shape=[out] * 4, compiler_params=_params())(parts, w, m, v)


def _two_level_allgather(srcs, dst_block, send_sems, recv_sems, local_sems):
    x, y, c = _position()
    me, sibling = (x, y, c), (x, y, 1 - c)
    chips = [(1 - x, y), (x, 1 - y), (1 - x, 1 - y)]
    n = len(srcs)

    def copy(a, k, block, to, src=None):
        dst = dst_block(a, _flat(*block))
        return pltpu.make_async_remote_copy(
            src_ref=dst if src is None else src, dst_ref=dst, send_sem=send_sems.at[a * 7 + k],
            recv_sem=recv_sems.at[a * 7 + k], device_id=to, device_id_type=MESH)

    mine = [pltpu.make_async_copy(srcs[a], dst_block(a, _flat(*me)), local_sems.at[a]) for a in range(n)]
    for cp in mine:
        cp.start()
    first = []
    for a in range(n):
        first.append(copy(a, 0, me, sibling, src=srcs[a]))
        first += [copy(a, 1 + j, me, (*chip, c), src=srcs[a]) for j, chip in enumerate(chips)]
    for cp in first:
        cp.start()
    passed = []
    for j, chip in enumerate(chips):
        for a in range(n):
            copy(a, 1 + j, (*chip, c), me).wait_recv()
            fwd = copy(a, 4 + j, (*chip, c), sibling)
            fwd.start()
            passed.append(fwd)
    for a in range(n):
        copy(a, 0, sibling, me).wait_recv()
        for j, chip in enumerate(chips):
            copy(a, 4 + j, (*chip, 1 - c), me).wait_recv()
    for cp in first + passed:
        cp.wait_send()
    for cp in mine:
        cp.wait()


def allgather_rows(x, name):
    def body(x_ref, out_ref, send_sems, recv_sems, local_sems):
        _two_level_allgather([x_ref], lambda a, idx: out_ref.at[idx], send_sems, recv_sems, local_sems)

    vmem = pl.BlockSpec(memory_space=pltpu.VMEM)
    return pl.pallas_call(
        body, name=name, in_specs=[vmem], out_specs=vmem,
        out_shape=jax.ShapeDtypeStruct((N_DEV,) + x.shape, x.dtype),
        scratch_shapes=[pltpu.SemaphoreType.DMA((7,)), pltpu.SemaphoreType.DMA((7,)), pltpu.SemaphoreType.DMA((1,))],
        compiler_params=_params(),
    )(x)


def _full_weight_shapes():
    return [jax.ShapeDtypeStruct((D_MODEL, IN_W), BF16), jax.ShapeDtypeStruct((D_MODEL, D_MODEL), BF16)]


def allgather_weights(w_in_b, w_out_b):
    def body(win_ref, wout_ref, fin_ref, fout_ref, s1, r1, l1, s2, r2):
        block = _weight_blocks(fin_ref, fout_ref)
        first = _gather_phase1([win_ref.at[0], wout_ref.at[0]], block, s1, r1, l1)
        _start_all(first)
        _wait_all(first)
        second = _gather_phase2(block, s2, r2)
        _start_all(second)
        _wait_all(second)

    return pl.pallas_call(
        body, name="allgather_weights", in_specs=[HBM, HBM], out_specs=[HBM, HBM], out_shape=_full_weight_shapes(),
        scratch_shapes=GATHER1_SEMS + GATHER2_SEMS, compiler_params=_params(),
    )(w_in_b, w_out_b)


def reduce_first(dw_in, dw_out, name):
    def body(ga_ref, gb_ref, ra_ref, rb_ref, send_sems, recv_sems):
        copies = _reduce_phase1([ga_ref, gb_ref], [ra_ref, rb_ref], send_sems, recv_sems)
        _start_all(copies)
        _wait_all(copies)

    return pl.pallas_call(
        body, name=name, in_specs=[HBM, HBM], out_specs=[HBM, HBM], out_shape=_landing_shapes(4),
        scratch_shapes=REDUCE1_SEMS, compiler_params=_params(),
    )(*_split_cores((dw_in, dw_out)))


def _one_class(a):
    return a.reshape((1,) + a.shape)


def layer_fwd(x, g, scale, shift, gate, w_in, w_out, lg, gather=None):
    proj, h, qkv4, qkv16, *began = inproj_fwd(x, g, 1.0 + scale, shift, w_in, gather)
    qkv = (_one_class(proj), qkv4, qkv16)
    attn_outs = [attn_fwd(arr, dil) for dil, arr in zip(DILATIONS, qkv)]
    attn_outs[0] = tuple(a[0] for a in attn_outs[0])
    o_pre, y_r, *next_weights = ret_fwd(proj, lg, began if gather is not None else None)
    next_weights = tuple(next_weights) if gather is not None else None
    x_new, o_a, lse, lse4, lse16 = outproj_fwd(x, gate, w_out, attn_outs, y_r, proj)
    saved = dict(x=x, proj=proj, h=h, qkv=qkv, o_a=o_a, lse=(_one_class(lse), lse4, lse16), o_pre=o_pre, y_r=y_r)
    return x_new, saved, next_weights


def layer_bwd(dxn, saved, g, scale, gate, w_in, w_out, lg, pos):
    proj = saved["proj"]
    do_a, delta, dyr, dz_a, dz_r, y, dxb, do4, do16, dl4, dl16 = outproj_bwd(
        dxn, gate, w_out, saved["o_a"], saved["y_r"], proj)
    dw_out, dgate = wout_grad(y, dxb, gate, w_out)
    dq_r, dk_r, dv_r, glf, glb = ret_bwd(proj, lg, saved["o_pre"], dyr)
    dos, deltas = (_one_class(do_a), do4, do16), (_one_class(delta), dl4, dl16)
    dqkv_a = [attn_bwd(arr, d_o, lse, dl, dil)
              for dil, arr, d_o, lse, dl in zip(DILATIONS, saved["qkv"], dos, saved["lse"], deltas)]
    dqkv_a[0] = [t[0] for t in dqkv_a[0]]
    dproj = assemble_dproj(dqkv_a, dz_a, dq_r, dk_r, dv_r, dz_r)
    dw_in = win_grad(saved["h"], dproj)
    landed = reduce_first(dw_in, dw_out, "reduce_first")
    sums = [chip_sum(pos, g4, r, "chip_sum") for g4, r in zip(_split_cores((dw_in, dw_out)), landed)]
    dx, stats, *others = inproj_bwd(dproj, w_in, saved["x"], g, 1.0 + scale, dxn, sums)
    dlg = jnp.concatenate([glf[:, 0, 0], glb[:, 0, 0]])
    return dx, stats[0:1], stats[1:2], dgate, stats[2:3], dlg, (sums, others)


ROWS_B_ADA = DEPTH * 3 * D_MODEL // 128
ROWS_GAIN = DEPTH * D_MODEL // 128
ROWS_FINAL = D_MODEL // 128
ROWS_MISC = 8
ROWS_SMALL = ROWS_B_ADA + ROWS_GAIN + ROWS_FINAL + ROWS_MISC


def _pack_small(b_ada_like, gain_like, final_like, dec_f, dec_b, loss=None):
    misc = jnp.zeros((ROWS_MISC, 128), F32)
    misc = misc.at[0, :2 * DEPTH * RET_HEADS].set(jnp.concatenate([dec_f.reshape(-1), dec_b.reshape(-1)]))
    if loss is not None:
        misc = misc.at[1, 0].set(loss)
    return jnp.concatenate([b_ada_like.reshape(ROWS_B_ADA, 128), gain_like.reshape(ROWS_GAIN, 128),
                            final_like.reshape(ROWS_FINAL, 128), misc], axis=0)


def _unpack_small(p):
    r0, r1, r2 = ROWS_B_ADA, ROWS_B_ADA + ROWS_GAIN, ROWS_B_ADA + ROWS_GAIN + ROWS_FINAL
    n = DEPTH * RET_HEADS
    return (p[:r0].reshape(DEPTH, 3 * D_MODEL), p[r0:r1].reshape(DEPTH, D_MODEL), p[r1:r2].reshape(D_MODEL),
            p[r2, :n].reshape(DEPTH, RET_HEADS), p[r2, n:2 * n].reshape(DEPTH, RET_HEADS))


def kernel(x, c, norm_gain, w_ada, b_ada, w_in, w_out, ret_decay_logit_f, ret_decay_logit_b, final_gain, loss_target, m_norm_gain, m_w_ada, m_b_ada, m_w_in, m_w_out, m_ret_decay_logit_f, m_ret_decay_logit_b, m_final_gain, v_norm_gain, v_w_ada, v_b_ada, v_w_in, v_w_out, v_ret_decay_logit_f, v_ret_decay_logit_b, v_final_gain):
    px, py, pc = _position()
    me = _flat(px, py, pc)
    pos = jnp.stack([px, py, pc]).astype(jnp.int32)
    x2, target = x[0], loss_target[0]

    w_in_b, w_out_b = w_in.astype(BF16), w_out.astype(BF16)
    weights = allgather_weights(w_in_b, w_out_b)

    c_all = allgather_rows(c.reshape(D_MODEL // 128, 128), "allgather_c").reshape(N_DEV, D_MODEL)
    act, mod_part = ada_fwd(c_all, w_ada)
    mod_all = allgather_rows(mod_part.reshape(-1, 128), "allgather_mod").reshape(N_DEV, DEPTH, N_DEV, W_ADA_SHARD)
    mod = lax.dynamic_index_in_dim(mod_all, me, axis=2, keepdims=False)
    mod = mod.transpose(1, 0, 2).reshape(DEPTH, 3 * D_MODEL) + b_ada
    shift, scale, gate = mod[:, :D_MODEL], mod[:, D_MODEL:2 * D_MODEL], mod[:, 2 * D_MODEL:]

    lg = jnp.concatenate([jax.nn.log_sigmoid(ret_decay_logit_f), jax.nn.log_sigmoid(ret_decay_logit_b)], axis=1)

    h = x2
    saved, layer_weights = [], []
    for l in range(DEPTH):
        layer_weights.append(weights)
        gather = (w_in_b, w_out_b, l + 1) if l + 1 < DEPTH else None
        h, sv, weights = layer_fwd(h, norm_gain[l:l + 1], scale[l:l + 1], shift[l:l + 1], gate[l:l + 1],
                                   *layer_weights[l], lg[l], gather)
        saved.append(sv)
    dh, loss_part, dfinal = loss_head(h, final_gain.reshape(1, D_MODEL), target)

    dmod, dgain, dlg, reduced = [None] * DEPTH, [None] * DEPTH, [None] * DEPTH, [None] * DEPTH
    for l in reversed(range(DEPTH)):
        dh, dshift, dscale, dgate, dg, dlg[l], reduced[l] = layer_bwd(
            dh, saved[l], norm_gain[l:l + 1], scale[l:l + 1], gate[l:l + 1], *layer_weights[l], lg[l], pos)
        dmod[l] = jnp.concatenate([dshift, dscale, dgate], axis=1)
        dgain[l] = dg

    dlg = jnp.stack(dlg)
    dlogit_f = dlg[:, :RET_HEADS] * jax.nn.sigmoid(-ret_decay_logit_f)
    dlogit_b = dlg[:, RET_HEADS:] * jax.nn.sigmoid(-ret_decay_logit_b)
    packed = _pack_small(jnp.concatenate(dmod, axis=0), jnp.concatenate(dgain, axis=0), dfinal, dlogit_f, dlogit_b,
                         loss=loss_part[0, 0])
    gathered = allgather_rows(packed, "allgather_small")
    small = small_update(gathered,
                         _pack_small(b_ada, norm_gain, final_gain, ret_decay_logit_f, ret_decay_logit_b),
                         _pack_small(m_b_ada, m_norm_gain, m_final_gain, m_ret_decay_logit_f, m_ret_decay_logit_b),
                         _pack_small(v_b_ada, v_norm_gain, v_final_gain, v_ret_decay_logit_f, v_ret_decay_logit_b))
    loss = small[0][ROWS_B_ADA + ROWS_GAIN + ROWS_FINAL + 1, 0]
    (g_b_ada, g_gain, g_final, g_dec_f, g_dec_b), (d_b_ada, d_gain, d_final, d_dec_f, d_dec_b), \
        (m_b_ada2, m_gain2, m_final2, m_dec_f2, m_dec_b2), (v_b_ada2, v_gain2, v_final2, v_dec_f2, v_dec_b2) = \
        [_unpack_small(p) for p in small]

    dmod_all = gathered[:, :ROWS_B_ADA].reshape(N_DEV, DEPTH, 3 * D_MODEL)
    dmod_mine = lax.dynamic_slice_in_dim(dmod_all, me * W_ADA_SHARD, W_ADA_SHARD, axis=2).transpose(1, 0, 2)
    g_w_ada, d_w_ada, m_w_ada2, v_w_ada2 = ada_update(act.T, dmod_mine, w_ada, m_w_ada, v_w_ada)

    upd_in = upd_out = None
    for l in reversed(range(DEPTH)):
        upd_in = shard_update(l, reduced[l][0][0], reduced[l][1][0], w_in, m_w_in, v_w_in, upd_in, f"w_in_update_{l}")
        upd_out = shard_update(l, reduced[l][0][1], reduced[l][1][1], w_out, m_w_out, v_w_out, upd_out, f"w_out_update_{l}")
    g_w_in, d_w_in, m_w_in2, v_w_in2 = upd_in
    g_w_out, d_w_out, m_w_out2, v_w_out2 = upd_out

    return (loss, dh[None],
            g_gain, g_w_ada, g_b_ada, g_w_in, g_w_out, g_dec_f, g_dec_b, g_final,
            d_gain, d_w_ada, d_b_ada, d_w_in, d_w_out, d_dec_f, d_dec_b, d_final,
            m_gain2, m_w_ada2, m_b_ada2, m_w_in2, m_w_out2, m_dec_f2, m_dec_b2, m_final2,
            v_gain2, v_w_ada2, v_b_ada2, v_w_in2, v_w_out2, v_dec_f2, v_dec_b2, v_final2)
```

```python
import functools
import math

import jax
import jax.numpy as jnp
from jax import lax
from jax.experimental import pallas as pl
from jax.experimental.pallas import tpu as pltpu

F32, BF16 = jnp.float32, jnp.bfloat16

D_MODEL = 2048
DEPTH = 4
N_DEV = 8
ATTN_WIDTH = 1024
HEAD_DIM = 128
N_HEADS_ATTN = 8
DILATIONS = (1, 4, 16)
RADIUS = 64
RET_HEADS = 4
RET_QK = 128
RET_V = 256
RET_CHUNK = 128
IN_W = 7168
QKV_A = 3 * ATTN_WIDTH
COL_ZA, COL_QR, COL_KR, COL_VR, COL_ZR = 3072, 4096, 4608, 5120, 6144
W_IN_SHARD = IN_W // N_DEV
W_OUT_SHARD = D_MODEL // N_DEV
W_ADA_SHARD = 3 * D_MODEL // N_DEV
NORM_EPS = 1e-6
MASK_VALUE = -1e30
ATTN_SCALE = HEAD_DIM ** -0.5
RET_SCALE = RET_QK ** -0.5
LN2 = math.log(2.0)

ADAM_LR, ADAM_B1, ADAM_B2, ADAM_EPS, ADAM_WD, ADAM_STEP = 0.001, 0.9, 0.999, 1e-08, 0.01, 10
ADAM_C1 = 1.0 / (1.0 - ADAM_B1 ** ADAM_STEP)
ADAM_C2 = 1.0 / (1.0 - ADAM_B2 ** ADAM_STEP)

VMEM_LIMIT_BYTES = 56 * 1024 * 1024
MESH = pl.DeviceIdType.MESH


def _params(*sem):
    return pltpu.CompilerParams(dimension_semantics=sem if sem else None, vmem_limit_bytes=VMEM_LIMIT_BYTES)


def _dot(a, b):
    return jnp.dot(a, b, preferred_element_type=F32)


def _dot_nt(a, b):
    return lax.dot_general(a, b, (((1,), (1,)), ((), ())), preferred_element_type=F32)


def _dot_tn(a, b):
    return lax.dot_general(a, b, (((0,), (0,)), ((), ())), preferred_element_type=F32)


def _iota(shape, dim):
    return lax.broadcasted_iota(jnp.int32, shape, dim)


def _sigmoid(z):
    return 1.0 / (1.0 + jnp.exp(-z))


HBM = pl.BlockSpec(memory_space=pl.ANY)


def _position():
    return lax.axis_index("x"), lax.axis_index("y"), lax.axis_index("c")


def _flat(px, py, pc):
    return 4 * px + 2 * py + pc


def _remote(src, dst, send_sem, recv_sem, to):
    return pltpu.make_async_remote_copy(src_ref=src, dst_ref=dst, send_sem=send_sem, recv_sem=recv_sem,
                                        device_id=to, device_id_type=MESH)


def _weight_blocks(fin_ref, fout_ref):
    def block(a, idx):
        if a == 0:
            return fin_ref.at[:, pl.ds(pl.multiple_of(idx * W_IN_SHARD, 128), W_IN_SHARD)]
        return fout_ref.at[pl.ds(pl.multiple_of(idx * W_OUT_SHARD, W_OUT_SHARD), W_OUT_SHARD), :]
    return block


GATHER1_SEMS = [pltpu.SemaphoreType.DMA((8,)), pltpu.SemaphoreType.DMA((8,)), pltpu.SemaphoreType.DMA((2,))]
GATHER2_SEMS = [pltpu.SemaphoreType.DMA((6,)), pltpu.SemaphoreType.DMA((6,))]
REDUCE1_SEMS = [pltpu.SemaphoreType.DMA((2,)), pltpu.SemaphoreType.DMA((2,))]
REDUCE2_SEMS = [pltpu.SemaphoreType.DMA((6,)), pltpu.SemaphoreType.DMA((6,))]


def _gather_phase1(srcs, block, send_sems, recv_sems, local_sems):
    x, y, c = _position()
    mine = [block(a, _flat(x, y, c)) for a in range(2)]
    copies = [pltpu.make_async_copy(srcs[a], mine[a], local_sems.at[a]) for a in range(2)]
    for a in range(2):
        copies.append(_remote(srcs[a], mine[a], send_sems.at[4 * a], recv_sems.at[4 * a], (x, y, 1 - c)))
        for j, (px, py) in enumerate([(1 - x, y), (x, 1 - y), (1 - x, 1 - y)]):
            copies.append(_remote(srcs[a], mine[a], send_sems.at[4 * a + 1 + j], recv_sems.at[4 * a + 1 + j], (px, py, c)))
    return copies


def _gather_phase2(block, send_sems, recv_sems):
    x, y, c = _position()
    copies = []
    for a in range(2):
        for j, (px, py) in enumerate([(1 - x, y), (x, 1 - y), (1 - x, 1 - y)]):
            blk = block(a, _flat(px, py, c))
            copies.append(_remote(blk, blk, send_sems.at[3 * a + j], recv_sems.at[3 * a + j], (x, y, 1 - c)))
    return copies


def _reduce_phase1(grads, landings, send_sems, recv_sems):
    x, y, c = _position()
    return [_remote(g.at[:, 1 - c], r, send_sems.at[a], recv_sems.at[a], (x, y, 1 - c))
            for a, (g, r) in enumerate(zip(grads, landings))]


def _reduce_phase2(sums, landings, send_sems, recv_sems):
    x, y, c = _position()
    copies = []
    for a, (p, r) in enumerate(zip(sums, landings)):
        for k in (1, 2, 3):
            to = (1 - x if k & 2 else x, 1 - y if k & 1 else y, c)
            copies.append(_remote(p.at[k], r.at[k - 1], send_sems.at[3 * a + k - 1], recv_sems.at[3 * a + k - 1], to))
    return copies


def _landing_shapes(n):
    return [jax.ShapeDtypeStruct((n, D_MODEL, W_IN_SHARD), BF16), jax.ShapeDtypeStruct((n, W_OUT_SHARD, D_MODEL), BF16)]


def _split_cores(slabs):
    return tuple(s.reshape((N_DEV // 2, 2) + s.shape[1:]) for s in slabs)


def _start_all(copies):
    for cp in copies:
        cp.start()


def _wait_all(copies):
    for cp in copies:
        cp.wait()


def _grid_edge(n_axes):
    first = last = None
    for ax in range(n_axes):
        f = pl.program_id(ax) == 0
        e = pl.program_id(ax) == pl.num_programs(ax) - 1
        first = f if first is None else first & f
        last = e if last is None else last & e
    return first, last


LANES = 128


def _stage_shape(rows, width):
    return pltpu.VMEM((width // LANES, rows, LANES), F32)


def _fill_stage(stage_ref, value):
    for t in range(stage_ref.shape[0]):
        stage_ref[t] = value[:, t * LANES:(t + 1) * LANES]


def _read_stage(stage_ref):
    return jnp.concatenate([stage_ref[t] for t in range(stage_ref.shape[0])], axis=1)


def _split_classes(stage_ref, out_refs):
    n_t, rows, _ = stage_ref.shape
    for out_ref in out_refs:
        dil = out_ref.shape[0]
        for r in range(dil):
            for t in range(n_t):
                piece = stage_ref[t, pl.ds(r, rows // dil, stride=dil), :]
                out_ref[r, :, t * LANES:(t + 1) * LANES] = piece.astype(out_ref.dtype)


def _merge_classes(in_ref, stage_ref):
    dil, per = in_ref.shape[0], in_ref.shape[1]
    for r in range(dil):
        for t in range(stage_ref.shape[0]):
            stage_ref[t, pl.ds(r, per, stride=dil), :] = in_ref[r, :, t * LANES:(t + 1) * LANES].astype(F32)


def _class_block(dil, tm, width, index_map):
    return pl.BlockSpec((dil, tm // dil, width), index_map)


def inproj_fwd(x, g, scale1p, shift, w, gather=None):
    s_len = x.shape[0]
    tm, tn = min(1024, s_len), 512

    n_qkv = QKV_A // tn

    def body(x_ref, g_ref, sc_ref, sh_ref, w_ref, *rest):
        if gather is not None:
            (win_ref, wout_ref, proj_ref, h_ref, q4_ref, q16_ref, fin_ref, fout_ref, stage,
             send_sems, recv_sems, local_sems) = rest
            first, last = _grid_edge(2)
            copies = _gather_phase1([win_ref.at[gather[2]], wout_ref.at[gather[2]]], _weight_blocks(fin_ref, fout_ref),
                                    send_sems, recv_sems, local_sems)
            pl.when(first)(lambda: _start_all(copies))
        else:
            proj_ref, h_ref, q4_ref, q16_ref, stage = rest

        @pl.when(pl.program_id(1) == 0)
        def _():
            xv = x_ref[...]
            r = lax.rsqrt(jnp.mean(xv * xv, axis=-1, keepdims=True) + NORM_EPS)
            h_ref[...] = ((xv * r * g_ref[...]) * sc_ref[...] + sh_ref[...]).astype(BF16)
        res = _dot(h_ref[...], w_ref[...])
        proj_ref[...] = res.astype(BF16)

        @pl.when(pl.program_id(1) < n_qkv)
        def _():
            _fill_stage(stage, res)
            _split_classes(stage, [q4_ref, q16_ref])
        if gather is not None:
            pl.when(last)(lambda: _wait_all(copies))

    vec = pl.BlockSpec((1, D_MODEL), lambda i, j: (0, 0))
    in_specs = [pl.BlockSpec((tm, D_MODEL), lambda i, j: (i, 0)), vec, vec, vec,
                pl.BlockSpec((D_MODEL, tn), lambda i, j: (0, j))]
    out_specs = [pl.BlockSpec((tm, tn), lambda i, j: (i, j)), pl.BlockSpec((tm, D_MODEL), lambda i, j: (i, 0))]
    out_shape = [jax.ShapeDtypeStruct((s_len, IN_W), BF16), jax.ShapeDtypeStruct((s_len, D_MODEL), BF16)]
    for dil in DILATIONS[1:]:
        out_specs.append(pl.BlockSpec((dil, tm // dil, tn), lambda i, j: (0, i, jnp.minimum(j, n_qkv - 1))))
        out_shape.append(jax.ShapeDtypeStruct((dil, s_len // dil, QKV_A), BF16))
    scratch = [_stage_shape(tm, tn)]
    extra = ()
    if gather is not None:
        in_specs += [HBM, HBM]
        out_specs += [HBM, HBM]
        out_shape += [jax.ShapeDtypeStruct((D_MODEL, IN_W), BF16), jax.ShapeDtypeStruct((D_MODEL, D_MODEL), BF16)]
        scratch += GATHER1_SEMS
        extra = tuple(gather[:2])
    return pl.pallas_call(
        body, name="inproj_fwd_gather" if gather is not None else "inproj_fwd", grid=(s_len // tm, IN_W // tn),
        in_specs=in_specs, out_specs=out_specs, out_shape=out_shape, scratch_shapes=scratch,
        compiler_params=_params("arbitrary", "arbitrary"),
    )(x, g, scale1p, shift, w, *extra)


MASK_DISTANCE = 1e33
ATTN_TILE = 128


ATTN_UNROLL_FWD, ATTN_UNROLL_BWD = 15, 10


def _attn_plan(sub_len, unroll):
    tq = min(sub_len, ATTN_TILE)
    win = min(sub_len, tq + 2 * RADIUS)
    heads = 1 if sub_len > 1024 else (2 if sub_len > 256 else N_HEADS_ATTN)
    return tq, win, sub_len // tq, heads, unroll


def _attn_tiles(sub_len, tq, win, n_tiles, unroll, tile):
    tile(0, 0, 0)
    if n_tiles > 2:
        def mid(i, carry):
            q0 = pl.multiple_of(i * tq, tq)
            tile(q0, pl.multiple_of(q0 - RADIUS, RADIUS), 1)
            return carry
        lax.fori_loop(1, n_tiles - 1, mid, 0, unroll=min(unroll, n_tiles - 2))
    if n_tiles > 1:
        tile(sub_len - tq, sub_len - win, 2)


def _attn_bias(bias_ref, head, heads, tq, win, dil):
    h = pl.program_id(1) * heads + head
    slope = jnp.exp(-(h + 1).astype(F32) * LN2 * jnp.ones((1, 1), F32))
    rel = _iota((tq, win), 1) - _iota((tq, win), 0)
    for v, off in enumerate((0, RADIUS, win - tq)):
        dist = jnp.abs(rel - off)
        bias_ref[v] = slope * jnp.where(dist <= RADIUS, (dist * dil).astype(F32), MASK_DISTANCE)


def _attn_specs(n_cls, sub_len, heads):
    width = heads * HEAD_DIM
    per = ATTN_WIDTH // width

    def col(part):
        return pl.BlockSpec((None, sub_len, width), lambda r, g: (r, 0, part * per + g))
    return col, (n_cls, N_HEADS_ATTN // heads)


def attn_fwd(qkv, dil):
    n_cls, sub_len, _ = qkv.shape
    tq, win, n_tiles, heads, unroll = _attn_plan(sub_len, ATTN_UNROLL_FWD)

    def body(q_ref, k_ref, v_ref, o_ref, lse_ref, bias_ref):
        for head in range(heads):
            lanes = slice(head * HEAD_DIM, (head + 1) * HEAD_DIM)
            _attn_bias(bias_ref, head, heads, tq, win, dil)

            def tile(q0, start, variant):
                s = _dot_nt(q_ref[pl.ds(q0, tq), lanes], k_ref[pl.ds(start, win), lanes]) * ATTN_SCALE - bias_ref[variant]
                m = jnp.max(s, axis=1, keepdims=True)
                p = jnp.exp(s - m)
                den = jnp.sum(p, axis=1, keepdims=True)
                o_ref[pl.ds(q0, tq), lanes] = _dot(p.astype(BF16), v_ref[pl.ds(start, win), lanes]) / den
                lse_ref[pl.ds(q0, tq), lanes] = jnp.broadcast_to(m + jnp.log(den), (tq, HEAD_DIM))

            _attn_tiles(sub_len, tq, win, n_tiles, unroll, tile)

    col, grid = _attn_specs(n_cls, sub_len, heads)
    out = jax.ShapeDtypeStruct((n_cls, sub_len, ATTN_WIDTH), F32)
    return pl.pallas_call(
        body, name=f"attn_fwd_d{dil}", grid=grid,
        in_specs=[col(0), col(1), col(2)], out_specs=[col(0), col(0)], out_shape=[out, out],
        scratch_shapes=[pltpu.VMEM((3, tq, win), F32)],
        compiler_params=_params("arbitrary", "arbitrary"),
    )(qkv, qkv, qkv)


def attn_bwd(qkv, do, lse, delta, dil):
    n_cls, sub_len, _ = qkv.shape
    tq, win, n_tiles, heads, unroll = _attn_plan(sub_len, ATTN_UNROLL_BWD)

    def body(q_ref, k_ref, v_ref, do_ref, lse_ref, dl_ref, dq_ref, dk_ref, dv_ref, bias_ref, dk_acc, dv_acc):
        for head in range(heads):
            lanes = slice(head * HEAD_DIM, (head + 1) * HEAD_DIM)
            _attn_bias(bias_ref, head, heads, tq, win, dil)
            dk_acc[...] = jnp.zeros_like(dk_acc)
            dv_acc[...] = jnp.zeros_like(dv_acc)

            def tile(q0, start, variant):
                q = q_ref[pl.ds(q0, tq), lanes]
                k = k_ref[pl.ds(start, win), lanes]
                v = v_ref[pl.ds(start, win), lanes]
                dov = do_ref[pl.ds(q0, tq), lanes]
                s = _dot_nt(q, k) * ATTN_SCALE - bias_ref[variant]
                p = jnp.exp(s - lse_ref[pl.ds(q0, tq), head * HEAD_DIM:head * HEAD_DIM + 1])
                ds = (p * (_dot_nt(dov, v) - dl_ref[pl.ds(q0, tq), head * HEAD_DIM:head * HEAD_DIM + 1])).astype(BF16)
                dq_ref[pl.ds(q0, tq), lanes] = (_dot(ds, k) * ATTN_SCALE).astype(BF16)
                dk_acc[pl.ds(start, win), :] += _dot_tn(ds, q) * ATTN_SCALE
                dv_acc[pl.ds(start, win), :] += _dot_tn(p.astype(BF16), dov)

            _attn_tiles(sub_len, tq, win, n_tiles, unroll, tile)
            dk_ref[:, lanes] = dk_acc[...].astype(BF16)
            dv_ref[:, lanes] = dv_acc[...].astype(BF16)

    col, grid = _attn_specs(n_cls, sub_len, heads)
    out = jax.ShapeDtypeStruct((n_cls, sub_len, ATTN_WIDTH), BF16)
    return pl.pallas_call(
        body, name=f"attn_bwd_d{dil}", grid=grid,
        in_specs=[col(0), col(1), col(2), col(0), col(0), col(0)],
        out_specs=[col(0), col(0), col(0)], out_shape=[out, out, out],
        scratch_shapes=[pltpu.VMEM((3, tq, win), F32), pltpu.VMEM((sub_len, HEAD_DIM), F32),
                        pltpu.VMEM((sub_len, HEAD_DIM), F32)],
        compiler_params=_params("arbitrary", "arbitrary"),
    )(qkv, qkv, qkv, do, lse, delta)


RET_UNROLL = 8
RET_UNROLL_BWD = 8


def _ret_tables(lg_ref):
    h = pl.program_id(0)
    one = jnp.ones((1, 1), F32)
    lgf, lgb = lg_ref[h] * one, lg_ref[RET_HEADS + h] * one
    c = RET_CHUNK
    rel = (_iota((c, c), 0) - _iota((c, c), 1)).astype(F32)
    dec_f = jnp.where(rel >= 0, jnp.exp(jnp.maximum(rel, 0.0) * lgf), 0.0)
    dec_b = jnp.where(rel <= 0, jnp.exp(jnp.maximum(-rel, 0.0) * lgb), 0.0)
    ci = _iota((c, 1), 0).astype(F32)
    tab = dict(rel=rel, dec_f=dec_f, dec_b=dec_b, ci=ci,
               xi_f=jnp.exp((ci + 1.0) * lgf), ze_f=jnp.exp((c - 1.0 - ci) * lgf), g_f=jnp.exp(c * lgf),
               xi_b=jnp.exp((c - ci) * lgb), ze_b=jnp.exp(ci * lgb), g_b=jnp.exp(c * lgb))
    return tab


def _ret_specs(s_len):
    q = pl.BlockSpec((s_len, RET_QK), lambda h: (0, COL_QR // RET_QK + h))
    k = pl.BlockSpec((s_len, RET_QK), lambda h: (0, COL_KR // RET_QK + h))
    v = pl.BlockSpec((s_len, RET_V), lambda h: (0, COL_VR // RET_V + h))
    wide = pl.BlockSpec((s_len, RET_V), lambda h: (0, h))
    narrow = pl.BlockSpec((s_len, RET_QK), lambda h: (0, h))
    smem = pl.BlockSpec(memory_space=pltpu.SMEM)
    return smem, q, k, v, wide, narrow


def ret_fwd(proj, lg, finish=None):
    s_len = proj.shape[0]
    c, n_chunks = RET_CHUNK, proj.shape[0] // RET_CHUNK

    def body(lg_ref, q_ref, k_ref, v_ref, *rest):
        if finish is not None:
            _, _, opre_ref, y_ref, fin_ref, fout_ref, st_f, st_b, send_sems, recv_sems = rest
            first, last = _grid_edge(1)
            copies = _gather_phase2(_weight_blocks(fin_ref, fout_ref), send_sems, recv_sems)
            pl.when(first)(lambda: _start_all(copies))
        else:
            opre_ref, y_ref, st_f, st_b = rest
        t = _ret_tables(lg_ref)
        dec = t["dec_f"] + t["dec_b"]
        st_f[...] = jnp.zeros_like(st_f)
        st_b[...] = jnp.zeros_like(st_b)

        def load(n):
            r0 = pl.multiple_of(n * c, c)
            q, k, v = q_ref[pl.ds(r0, c), :], k_ref[pl.ds(r0, c), :], v_ref[pl.ds(r0, c), :]
            return r0, q, (k.astype(F32) * RET_SCALE), v

        def fwd(n, carry):
            r0, q, kf, v = load(n)
            inner = (_dot_nt(q, kf.astype(BF16)) * dec).astype(BF16)
            opre_ref[pl.ds(r0, c), :] = _dot(inner, v) + _dot(q, st_f[...].astype(BF16)) * t["xi_f"]
            st_f[...] = st_f[...] * t["g_f"] + _dot_tn((kf * t["ze_f"]).astype(BF16), v)
            return carry

        def bwd(i, carry):
            r0, q, kf, v = load(n_chunks - 1 - i)
            o = opre_ref[pl.ds(r0, c), :] + _dot(q, st_b[...].astype(BF16)) * t["xi_b"]
            st_b[...] = st_b[...] * t["g_b"] + _dot_tn((kf * t["ze_b"]).astype(BF16), v)
            opre_ref[pl.ds(r0, c), :] = o
            y_ref[pl.ds(r0, c), :] = o * lax.rsqrt(jnp.mean(o * o, axis=-1, keepdims=True) + NORM_EPS)
            return carry

        lax.fori_loop(0, n_chunks, fwd, 0, unroll=min(RET_UNROLL, n_chunks))
        lax.fori_loop(0, n_chunks, bwd, 0, unroll=min(RET_UNROLL, n_chunks))
        if finish is not None:
            pl.when(last)(lambda: _wait_all(copies))

    smem, q, k, v, wide, _ = _ret_specs(s_len)
    out = jax.ShapeDtypeStruct((s_len, RET_HEADS * RET_V), F32)
    in_specs, out_specs, out_shape = [smem, q, k, v], [wide, wide], [out, out]
    scratch = [pltpu.VMEM((RET_QK, RET_V), F32), pltpu.VMEM((RET_QK, RET_V), F32)]
    aliases, extra = {}, ()
    if finish is not None:
        in_specs += [HBM, HBM]
        out_specs += [HBM, HBM]
        out_shape += _full_weight_shapes()
        scratch += GATHER2_SEMS
        aliases, extra = {4: 2, 5: 3}, tuple(finish)
    return pl.pallas_call(
        body, name="ret_fwd_gather2" if finish is not None else "ret_fwd", grid=(RET_HEADS,), in_specs=in_specs,
        out_specs=out_specs, out_shape=out_shape, scratch_shapes=scratch, input_output_aliases=aliases,
        compiler_params=_params("arbitrary"),
    )(lg, proj, proj, proj, *extra)


def ret_bwd(proj, lg, o_pre, dy):
    s_len = proj.shape[0]
    c, n_chunks = RET_CHUNK, proj.shape[0] // RET_CHUNK
    cf = float(c)

    def body(lg_ref, q_ref, k_ref, v_ref, o_ref, dy_ref, dq_ref, dk_ref, dv_ref, glf_ref, glb_ref,
             st_f, dst_b, st_b, dst_f, keep_sf, keep_dtb, acc_f, acc_b, acc_sf, acc_sb):
        t = _ret_tables(lg_ref)
        dec = t["dec_f"] + t["dec_b"]
        e_f, e_b, ci = t["rel"] * t["dec_f"], -t["rel"] * t["dec_b"], t["ci"]
        for ref in (st_f, dst_b, st_b, dst_f, acc_f, acc_b, acc_sf, acc_sb):
            ref[...] = jnp.zeros_like(ref)

        def load(n):
            r0 = pl.multiple_of(n * c, c)
            q, k, v = q_ref[pl.ds(r0, c), :], k_ref[pl.ds(r0, c), :], v_ref[pl.ds(r0, c), :]
            o, dyv = o_ref[pl.ds(r0, c), :], dy_ref[pl.ds(r0, c), :]
            rr = lax.rsqrt(jnp.mean(o * o, axis=-1, keepdims=True) + NORM_EPS)
            y = o * rr
            do = (rr * (dyv - y * jnp.mean(dyv * y, axis=-1, keepdims=True))).astype(BF16)
            return r0, q, k.astype(F32) * RET_SCALE, v, do

        def fwd(n, carry):
            r0, q, kf, v, do = load(n)
            qf, kb = q.astype(F32), kf.astype(BF16)
            a = _dot_nt(q, kb)
            b = _dot_nt(do, v)
            da = (b * dec).astype(BF16)
            ab = a * b
            sf_b, dtb_b = st_f[...].astype(BF16), dst_b[...].astype(BF16)
            dq_inter = _dot_nt(do, sf_b) * t["xi_f"]
            dk_inter = _dot_nt(v, dtb_b) * t["ze_b"]
            acc_f[...] += e_f * ab + (ci + 1.0) * (qf * dq_inter)
            acc_b[...] += e_b * ab + ci * (kf * dk_inter)
            dq_ref[pl.ds(r0, c), :] = _dot(da, kb) + dq_inter
            dk_ref[pl.ds(r0, c), :] = _dot_tn(da, q) + dk_inter
            dv_ref[pl.ds(r0, c), :] = _dot_tn((a * dec).astype(BF16), do) + _dot((kf * t["ze_b"]).astype(BF16), dtb_b)
            keep_sf[n] = sf_b
            keep_dtb[n] = dtb_b
            st_f[...] = st_f[...] * t["g_f"] + _dot_tn((kf * t["ze_f"]).astype(BF16), v)
            dst_b[...] = dst_b[...] * t["g_b"] + _dot_tn((qf * t["xi_b"]).astype(BF16), do)
            return carry

        def bwd(i, carry):
            n = n_chunks - 1 - i
            r0, q, kf, v, do = load(n)
            qf = q.astype(F32)
            tb_b, dsf_b = st_b[...].astype(BF16), dst_f[...].astype(BF16)
            dq_inter = _dot_nt(do, tb_b) * t["xi_b"]
            dk_inter = _dot_nt(v, dsf_b) * t["ze_f"]
            acc_b[...] += (cf - ci) * (qf * dq_inter)
            acc_f[...] += (cf - 1.0 - ci) * (kf * dk_inter)
            acc_sb[...] += keep_dtb[n].astype(F32) * st_b[...]
            acc_sf[...] += dst_f[...] * keep_sf[n].astype(F32)
            dq_ref[pl.ds(r0, c), :] += dq_inter
            dk_ref[pl.ds(r0, c), :] = (dk_ref[pl.ds(r0, c), :] + dk_inter) * RET_SCALE
            dv_ref[pl.ds(r0, c), :] += _dot((kf * t["ze_f"]).astype(BF16), dsf_b)
            st_b[...] = st_b[...] * t["g_b"] + _dot_tn((kf * t["ze_b"]).astype(BF16), v)
            dst_f[...] = dst_f[...] * t["g_f"] + _dot_tn((qf * t["xi_f"]).astype(BF16), do)
            return carry

        lax.fori_loop(0, n_chunks, fwd, 0, unroll=min(RET_UNROLL_BWD, n_chunks))
        lax.fori_loop(0, n_chunks, bwd, 0, unroll=min(RET_UNROLL_BWD, n_chunks))

        def total(x):
            return jnp.sum(jnp.sum(x, axis=1, keepdims=True), axis=0, keepdims=True)

        glf_ref[...] = jnp.broadcast_to(total(acc_f[...]) + cf * t["g_f"] * total(acc_sf[...]), (8, 128))
        glb_ref[...] = jnp.broadcast_to(total(acc_b[...]) + cf * t["g_b"] * total(acc_sb[...]), (8, 128))

    smem, q, k, v, wide, narrow = _ret_specs(s_len)
    scal = pl.BlockSpec((None, 8, 128), lambda h: (h, 0, 0))
    state = pltpu.VMEM((RET_QK, RET_V), F32)
    square = pltpu.VMEM((RET_CHUNK, RET_QK), F32)
    keep = pltpu.VMEM((n_chunks, RET_QK, RET_V), BF16)
    return pl.pallas_call(
        body, name="ret_bwd", grid=(RET_HEADS,), in_specs=[smem, q, k, v, wide, wide],
        out_specs=[narrow, narrow, wide, scal, scal],
        out_shape=[jax.ShapeDtypeStruct((s_len, RET_HEADS * RET_QK), F32), jax.ShapeDtypeStruct((s_len, RET_HEADS * RET_QK), F32),
                   jax.ShapeDtypeStruct((s_len, RET_HEADS * RET_V), F32),
                   jax.ShapeDtypeStruct((RET_HEADS, 8, 128), F32), jax.ShapeDtypeStruct((RET_HEADS, 8, 128), F32)],
        scratch_shapes=[state, state, state, state, keep, keep, square, square, state, state],
        compiler_params=_params("arbitrary"),
    )(lg, proj, proj, proj, o_pre, dy)


def _silu_parts(z):
    sig = _sigmoid(z)
    return z * sig, sig * (1.0 + z * (1.0 - sig))


def outproj_fwd(x, gate, w_out, attn_outs, y_r, proj):
    s_len = x.shape[0]
    tm = min(256, s_len)

    def body(x_ref, gate_ref, w_ref, o1, l1, o2, l2, o3, l3, yr_ref, za_ref, zr_ref,
             xn_ref, oa_ref, lse_ref, lse4_ref, lse16_ref, so2, sl2, so3, sl3):
        silu_r, _ = _silu_parts(zr_ref[...].astype(F32))
        out_r = _dot((yr_ref[...] * silu_r).astype(BF16), w_ref[ATTN_WIDTH:, :])
        for src, dst in ((o2, so2), (l2, sl2), (o3, so3), (l3, sl3)):
            _merge_classes(src, dst)
        la, lb, lc = l1[...], _read_stage(sl2), _read_stage(sl3)
        m = jnp.maximum(jnp.maximum(la, lb), lc)
        lse = m + jnp.log(jnp.exp(la - m) + jnp.exp(lb - m) + jnp.exp(lc - m))
        o_a = jnp.exp(la - lse) * o1[...] + jnp.exp(lb - lse) * _read_stage(so2) + jnp.exp(lc - lse) * _read_stage(so3)
        oa_ref[...] = o_a
        lse_ref[...] = lse
        _fill_stage(sl2, lse)
        _split_classes(sl2, [lse4_ref, lse16_ref])
        silu_a, _ = _silu_parts(za_ref[...].astype(F32))
        out_a = _dot((o_a * silu_a).astype(BF16), w_ref[:ATTN_WIDTH, :])
        xn_ref[...] = x_ref[...] + gate_ref[...] * (out_a + out_r)

    row = lambda w: pl.BlockSpec((tm, w), lambda i: (i, 0))
    half = row(ATTN_WIDTH)
    cls = [_class_block(dil, tm, ATTN_WIDTH, lambda i: (0, i, 0)) for dil in DILATIONS[1:]]
    flat = [a for pair in attn_outs for a in pair]
    sds = jax.ShapeDtypeStruct
    return pl.pallas_call(
        body, name="outproj_fwd", grid=(s_len // tm,),
        in_specs=[row(D_MODEL), pl.BlockSpec((1, D_MODEL), lambda i: (0, 0)),
                  pl.BlockSpec((D_MODEL, D_MODEL), lambda i: (0, 0)), half, half, cls[0], cls[0], cls[1], cls[1], half,
                  pl.BlockSpec((tm, ATTN_WIDTH), lambda i: (i, COL_ZA // ATTN_WIDTH)),
                  pl.BlockSpec((tm, ATTN_WIDTH), lambda i: (i, COL_ZR // ATTN_WIDTH))],
        out_specs=[row(D_MODEL), half, half] + cls,
        out_shape=[sds((s_len, D_MODEL), F32), sds((s_len, ATTN_WIDTH), F32), sds((s_len, ATTN_WIDTH), F32)]
                  + [sds((dil, s_len // dil, ATTN_WIDTH), F32) for dil in DILATIONS[1:]],
        scratch_shapes=[_stage_shape(tm, ATTN_WIDTH)] * 4,
        compiler_params=_params("arbitrary"),
    )(x, gate, w_out, *flat, y_r, proj, proj)


def loss_head(x, gain, target):
    s_len = x.shape[0]
    tm = min(256, s_len)

    def body(x_ref, g_ref, t_ref, dx_ref, loss_ref, dg_ref):
        @pl.when(pl.program_id(0) == 0)
        def _():
            loss_ref[...] = jnp.zeros_like(loss_ref)
            dg_ref[...] = jnp.zeros_like(dg_ref)
        xv, g = x_ref[...], g_ref[...]
        r = lax.rsqrt(jnp.mean(xv * xv, axis=-1, keepdims=True) + NORM_EPS)
        xn = xv * r
        err = xn * g - t_ref[...]
        part = 0.5 * jnp.sum(jnp.mean(err * err, axis=-1, keepdims=True), axis=0, keepdims=True)
        loss_ref[...] += jnp.broadcast_to(part, loss_ref.shape)
        dy = err * (1.0 / D_MODEL)
        dg_ref[...] += jnp.sum(dy * xn, axis=0, keepdims=True)
        dxn = dy * g
        dx_ref[...] = r * (dxn - xn * jnp.mean(dxn * xn, axis=-1, keepdims=True))

    row = pl.BlockSpec((tm, D_MODEL), lambda i: (i, 0))
    vec = pl.BlockSpec((1, D_MODEL), lambda i: (0, 0))
    return pl.pallas_call(
        body, name="loss_head", grid=(s_len // tm,), in_specs=[row, vec, row],
        out_specs=[row, pl.BlockSpec((8, 128), lambda i: (0, 0)), vec],
        out_shape=[jax.ShapeDtypeStruct((s_len, D_MODEL), F32), jax.ShapeDtypeStruct((8, 128), F32),
                   jax.ShapeDtypeStruct((1, D_MODEL), F32)],
        compiler_params=_params("arbitrary"),
    )(x, gain, target)


def outproj_bwd(dxn, gate, w_out, o_a, y_r, proj):
    s_len = dxn.shape[0]
    tm = min(256, s_len)

    def body(dx_ref, gate_ref, w_ref, oa_ref, yr_ref, za_ref, zr_ref,
             doa_ref, dl_ref, dyr_ref, dza_ref, dzr_ref, y_ref, dxb_ref, do4, do16, dl4, dl16, stage):
        dxv = dx_ref[...]
        dxb_ref[...] = dxv.astype(BF16)
        dy = _dot_nt((dxv * gate_ref[...]).astype(BF16), w_ref[...])
        dy_a, dy_r = dy[:, :ATTN_WIDTH], dy[:, ATTN_WIDTH:]
        o_a, y_rv = oa_ref[...], yr_ref[...]
        silu_a, dsilu_a = _silu_parts(za_ref[...].astype(F32))
        silu_r, dsilu_r = _silu_parts(zr_ref[...].astype(F32))
        do_a = dy_a * silu_a
        doa_ref[...] = do_a.astype(BF16)
        _fill_stage(stage, do_a)
        _split_classes(stage, [do4, do16])
        prod = do_a * o_a
        delta = jnp.concatenate(
            [jnp.broadcast_to(jnp.sum(prod[:, h * HEAD_DIM:(h + 1) * HEAD_DIM], axis=1, keepdims=True), (tm, HEAD_DIM))
             for h in range(N_HEADS_ATTN)], axis=1)
        dl_ref[...] = delta
        _fill_stage(stage, delta)
        _split_classes(stage, [dl4, dl16])
        dyr_ref[...] = dy_r * silu_r
        dza_ref[...] = (dy_a * o_a * dsilu_a).astype(BF16)
        dzr_ref[...] = (dy_r * y_rv * dsilu_r).astype(BF16)
        y_ref[...] = jnp.concatenate([(o_a * silu_a).astype(BF16), (y_rv * silu_r).astype(BF16)], axis=1)

    row = lambda w: pl.BlockSpec((tm, w), lambda i: (i, 0))
    half = row(ATTN_WIDTH)
    sds = lambda w, dt: jax.ShapeDtypeStruct((s_len, w), dt)
    in_specs = [row(D_MODEL), pl.BlockSpec((1, D_MODEL), lambda i: (0, 0)),
                pl.BlockSpec((D_MODEL, D_MODEL), lambda i: (0, 0)), half, half,
                pl.BlockSpec((tm, ATTN_WIDTH), lambda i: (i, COL_ZA // ATTN_WIDTH)),
                pl.BlockSpec((tm, ATTN_WIDTH), lambda i: (i, COL_ZR // ATTN_WIDTH))]
    cls = [_class_block(dil, tm, ATTN_WIDTH, lambda i: (0, i, 0)) for dil in DILATIONS[1:]]
    out_specs = [half, half, half, half, half, row(D_MODEL), row(D_MODEL)] + cls + cls
    out_shape = [sds(ATTN_WIDTH, BF16), sds(ATTN_WIDTH, F32), sds(ATTN_WIDTH, F32), sds(ATTN_WIDTH, BF16),
                 sds(ATTN_WIDTH, BF16), sds(D_MODEL, BF16), sds(D_MODEL, BF16)]
    out_shape += [jax.ShapeDtypeStruct((dil, s_len // dil, ATTN_WIDTH), dt) for dt in (BF16, F32) for dil in DILATIONS[1:]]
    return pl.pallas_call(
        body, name="outproj_bwd", grid=(s_len // tm,),
        in_specs=in_specs, out_specs=out_specs, out_shape=out_shape, scratch_shapes=[_stage_shape(tm, ATTN_WIDTH)],
        compiler_params=_params("arbitrary"),
    )(dxn, gate, w_out, o_a, y_r, proj, proj)


def wout_grad(y, dxb, gate, w_out):
    s_len = y.shape[0]
    tf, ts = 512, min(512, s_len)

    def body(y_ref, dx_ref, gate_ref, w_ref, dw_ref, dgate_ref, acc):
        f, s = pl.program_id(0), pl.program_id(1)

        @pl.when((f == 0) & (s == 0))
        def _():
            dgate_ref[...] = jnp.zeros_like(dgate_ref)

        @pl.when(s == 0)
        def _():
            acc[...] = jnp.zeros_like(acc)
        acc[...] += _dot_tn(y_ref[...], dx_ref[...])

        @pl.when(s == pl.num_programs(1) - 1)
        def _():
            m = acc[...]
            dw_ref[...] = (m * gate_ref[...]).astype(BF16).reshape(dw_ref.shape)
            dgate_ref[...] += jnp.sum(m * w_ref[...].astype(F32), axis=0, keepdims=True)

    per = tf // W_OUT_SHARD
    return pl.pallas_call(
        body, name="wout_grad", grid=(D_MODEL // tf, s_len // ts),
        in_specs=[pl.BlockSpec((ts, tf), lambda f, s: (s, f)), pl.BlockSpec((ts, D_MODEL), lambda f, s: (s, 0)),
                  pl.BlockSpec((1, D_MODEL), lambda f, s: (0, 0)), pl.BlockSpec((tf, D_MODEL), lambda f, s: (f, 0))],
        out_specs=[pl.BlockSpec((per, W_OUT_SHARD, D_MODEL), lambda f, s: (f, 0, 0)),
                   pl.BlockSpec((1, D_MODEL), lambda f, s: (0, 0))],
        out_shape=[jax.ShapeDtypeStruct((N_DEV, W_OUT_SHARD, D_MODEL), BF16), jax.ShapeDtypeStruct((1, D_MODEL), F32)],
        scratch_shapes=[pltpu.VMEM((tf, D_MODEL), F32)],
        compiler_params=_params("arbitrary", "arbitrary"),
    )(y, dxb, gate, w_out)


def assemble_dproj(dqkv_a, dz_a, dq_r, dk_r, dv_r, dz_r):
    s_len = dz_a.shape[0]
    tm = min(256, s_len)

    def body(*refs):
        pat, (dza, dqr, dkr, dvr, dzr, out, stage) = refs[:9], refs[9:]
        for t in range(3):
            tot = pat[t][...].astype(F32)
            for p in (1, 2):
                _merge_classes(pat[3 * p + t], stage)
                tot = tot + _read_stage(stage)
            out[:, t * ATTN_WIDTH:(t + 1) * ATTN_WIDTH] = tot.astype(BF16)
        out[:, COL_ZA:COL_QR] = dza[...]
        out[:, COL_QR:COL_KR] = dqr[...].astype(BF16)
        out[:, COL_KR:COL_VR] = dkr[...].astype(BF16)
        out[:, COL_VR:COL_ZR] = dvr[...].astype(BF16)
        out[:, COL_ZR:IN_W] = dzr[...]

    row = lambda w: pl.BlockSpec((tm, w), lambda i: (i, 0))
    cls = [_class_block(dil, tm, ATTN_WIDTH, lambda i: (0, i, 0)) for dil in DILATIONS[1:]]
    flat = [dqkv_a[p][t] for p in range(3) for t in range(3)]
    return pl.pallas_call(
        body, name="assemble_dproj", grid=(s_len // tm,),
        in_specs=[row(ATTN_WIDTH)] * 3 + [cls[0]] * 3 + [cls[1]] * 3
                 + [row(ATTN_WIDTH), row(512), row(512), row(ATTN_WIDTH), row(ATTN_WIDTH)],
        out_specs=row(IN_W), out_shape=jax.ShapeDtypeStruct((s_len, IN_W), BF16),
        scratch_shapes=[_stage_shape(tm, ATTN_WIDTH)],
        compiler_params=_params("arbitrary"),
    )(*flat, dz_a, dq_r, dk_r, dv_r, dz_r)


def inproj_bwd(dproj, w, x, g, scale1p, dxn, exchange):
    s_len = x.shape[0]
    tm, tk = min(512, s_len), 1024

    def body(dp_ref, w_ref, x_ref, g_ref, sc_ref, dxn_ref, pa_ref, pb_ref, dx_ref, st_ref, ra_ref, rb_ref,
             acc, send_sems, recv_sems):
        first, last = _grid_edge(2)
        copies = _reduce_phase2([pa_ref, pb_ref], [ra_ref, rb_ref], send_sems, recv_sems)
        pl.when(first)(lambda: _start_all(copies))
        i, k = pl.program_id(0), pl.program_id(1)

        @pl.when((i == 0) & (k == 0))
        def _():
            st_ref[...] = jnp.zeros_like(st_ref)

        @pl.when(k == 0)
        def _():
            acc[...] = jnp.zeros_like(acc)
        acc[...] += _dot_nt(dp_ref[...], w_ref[...])

        @pl.when(k == pl.num_programs(1) - 1)
        def _():
            dh, xv = acc[...], x_ref[...]
            r = lax.rsqrt(jnp.mean(xv * xv, axis=-1, keepdims=True) + NORM_EPS)
            xn = xv * r
            st_ref[0:1, :] += jnp.sum(dh, axis=0, keepdims=True)
            st_ref[3:4, :] += jnp.sum(dh * xn, axis=0, keepdims=True)
            dn = dh * (sc_ref[...] * g_ref[...])
            dx_ref[...] = r * (dn - xn * jnp.mean(dn * xn, axis=-1, keepdims=True)) + dxn_ref[...]

        @pl.when(last)
        def _():
            st_ref[1:2, :] = st_ref[3:4, :] * g_ref[...]
            st_ref[2:3, :] = st_ref[3:4, :] * sc_ref[...]
            _wait_all(copies)

    row = pl.BlockSpec((tm, D_MODEL), lambda i, k: (i, 0))
    vec = pl.BlockSpec((1, D_MODEL), lambda i, k: (0, 0))
    return pl.pallas_call(
        body, name="inproj_bwd_reduce2", grid=(s_len // tm, IN_W // tk),
        in_specs=[pl.BlockSpec((tm, tk), lambda i, k: (i, k)), pl.BlockSpec((D_MODEL, tk), lambda i, k: (0, k)),
                  row, vec, vec, row, HBM, HBM],
        out_specs=[row, pl.BlockSpec((8, D_MODEL), lambda i, k: (0, 0)), HBM, HBM],
        out_shape=[jax.ShapeDtypeStruct((s_len, D_MODEL), F32), jax.ShapeDtypeStruct((8, D_MODEL), F32)]
                  + _landing_shapes(3),
        scratch_shapes=[pltpu.VMEM((tm, D_MODEL), F32)] + REDUCE2_SEMS,
        compiler_params=_params("arbitrary", "arbitrary"),
    )(dproj, w, x, g, scale1p, dxn, *exchange)


def win_grad(h, dproj):
    s_len = h.shape[0]
    ts = min(1024, s_len)

    def body(h_ref, dp_ref, dw_ref, acc):
        s = pl.program_id(1)

        @pl.when(s == 0)
        def _():
            acc[...] = jnp.zeros_like(acc)
        acc[...] += _dot_tn(h_ref[...], dp_ref[...])

        @pl.when(s == pl.num_programs(1) - 1)
        def _():
            dw_ref[...] = acc[...].astype(BF16)

    return pl.pallas_call(
        body, name="win_grad", grid=(N_DEV, s_len // ts),
        in_specs=[pl.BlockSpec((ts, D_MODEL), lambda j, s: (s, 0)), pl.BlockSpec((ts, W_IN_SHARD), lambda j, s: (s, j))],
        out_specs=pl.BlockSpec((None, D_MODEL, W_IN_SHARD), lambda j, s: (j, 0, 0)),
        out_shape=jax.ShapeDtypeStruct((N_DEV, D_MODEL, W_IN_SHARD), BF16),
        scratch_shapes=[pltpu.VMEM((D_MODEL, W_IN_SHARD), F32)],
        compiler_params=_params("arbitrary", "arbitrary"),
    )(h, dproj)


def ada_fwd(c_all, w_ada):
    def body(c_ref, w_ref, act_ref, part_ref):
        cv = c_ref[...]
        act = cv * _sigmoid(cv)
        act_ref[...] = act
        part_ref[...] = _dot(act.astype(BF16), w_ref[...].astype(BF16))

    return pl.pallas_call(
        body, name="ada_fwd", grid=(DEPTH,),
        in_specs=[pl.BlockSpec((N_DEV, D_MODEL), lambda l: (0, 0)),
                  pl.BlockSpec((None, D_MODEL, W_ADA_SHARD), lambda l: (l, 0, 0))],
        out_specs=[pl.BlockSpec((N_DEV, D_MODEL), lambda l: (0, 0)),
                   pl.BlockSpec((None, N_DEV, W_ADA_SHARD), lambda l: (l, 0, 0))],
        out_shape=[jax.ShapeDtypeStruct((N_DEV, D_MODEL), F32), jax.ShapeDtypeStruct((DEPTH, N_DEV, W_ADA_SHARD), F32)],
        compiler_params=_params("arbitrary"),
    )(c_all, w_ada)


def _adamw(w, g, m, v):
    m = ADAM_B1 * m + (1.0 - ADAM_B1) * g
    v = ADAM_B2 * v + (1.0 - ADAM_B2) * (g * g)
    delta = -ADAM_LR * ((m * ADAM_C1) / (jnp.sqrt(v * ADAM_C2) + ADAM_EPS) + ADAM_WD * w)
    return delta, m, v


def ada_update(act_t, dmod, w, m, v):
    tr = 512

    def body(a_ref, d_ref, w_ref, m_ref, v_ref, g_out, dl_out, m_out, v_out):
        a = a_ref[...].astype(BF16).astype(F32)
        d = d_ref[...].astype(BF16).astype(F32)
        g = a[:, 0:1] * d[0:1, :]
        for b in range(1, N_DEV):
            g = g + a[:, b:b + 1] * d[b:b + 1, :]
        g_out[...] = g
        dl_out[...], m_out[...], v_out[...] = _adamw(w_ref[...], g, m_ref[...], v_ref[...])

    blk = pl.BlockSpec((None, tr, W_ADA_SHARD), lambda l, r: (l, r, 0))
    out = jax.ShapeDtypeStruct(w.shape, F32)
    return pl.pallas_call(
        body, name="ada_update", grid=(DEPTH, D_MODEL // tr),
        in_specs=[pl.BlockSpec((tr, N_DEV), lambda l, r: (r, 0)),
                  pl.BlockSpec((None, N_DEV, W_ADA_SHARD), lambda l, r: (l, 0, 0)), blk, blk, blk],
        out_specs=[blk] * 4, out_shape=[out] * 4, compiler_params=_params("arbitrary", "arbitrary"),
    )(act_t, dmod, w, m, v)


def chip_sum(pos, grads, landed, name):
    _, _, n_rows, n_cols = grads.shape
    tr = min(512, n_rows)

    def chip(k, pos_ref):
        return (pos_ref[0] ^ (k // 2)) * 2 + (pos_ref[1] ^ (k % 2))

    def body(pos_ref, g_ref, r_ref, out_ref):
        out_ref[...] = (g_ref[...].astype(F32) + r_ref[...].astype(F32)).astype(BF16)

    return pl.pallas_call(
        body, name=name,
        grid_spec=pltpu.PrefetchScalarGridSpec(
            num_scalar_prefetch=1, grid=(N_DEV // 2, n_rows // tr),
            in_specs=[pl.BlockSpec((None, None, tr, n_cols), lambda k, r, p: (chip(k, p), p[2], r, 0)),
                      pl.BlockSpec((None, tr, n_cols), lambda k, r, p: (chip(k, p), r, 0))],
            out_specs=pl.BlockSpec((None, tr, n_cols), lambda k, r, p: (k, r, 0))),
        out_shape=jax.ShapeDtypeStruct((N_DEV // 2, n_rows, n_cols), BF16),
        compiler_params=_params("arbitrary", "arbitrary"),
    )(pos, grads, landed)


def shard_update(layer, own, others, w, m, v, prev, name):
    _, n_rows, n_cols = w.shape
    tr = min(256, n_rows)

    def body(own_ref, oth_ref, w_ref, m_ref, v_ref, *rest):
        g_out, dl_out, m_out, v_out = rest[-4:]
        g = own_ref[...].astype(F32)
        for k in range(3):
            g = g + oth_ref[k].astype(F32)
        g_out[...] = g
        dl_out[...], m_out[...], v_out[...] = _adamw(w_ref[...], g, m_ref[...], v_ref[...])

    blk = pl.BlockSpec((None, tr, n_cols), lambda r: (layer, r, 0))
    out = jax.ShapeDtypeStruct(w.shape, F32)
    in_specs = [pl.BlockSpec((None, tr, n_cols), lambda r: (0, r, 0)), pl.BlockSpec((3, tr, n_cols), lambda r: (0, r, 0)),
                blk, blk, blk]
    aliases, extra = {}, ()
    if prev is not None:
        in_specs += [HBM] * 4
        aliases = {5 + t: t for t in range(4)}
        extra = tuple(prev)
    return pl.pallas_call(
        body, name=name, grid=(n_rows // tr,), in_specs=in_specs, out_specs=[blk] * 4, out_shape=[out] * 4,
        input_output_aliases=aliases, compiler_params=_params("arbitrary"),
    )(own, others, w, m, v, *extra)


def small_update(parts, w, m, v):
    def body(p_ref, w_ref, m_ref, v_ref, g_out, dl_out, m_out, v_out):
        g = p_ref[0]
        for k in range(1, N_DEV):
            g = g + p_ref[k]
        g_out[...] = g
        dl_out[...], m_out[...], v_out[...] = _adamw(w_ref[...], g, m_ref[...], v_ref[...])

    out = jax.ShapeDtypeStruct(w.shape, F32)
    return pl.pallas_call(body, name="small_update", out_shape=[out] * 4, compiler_params=_params())(parts, w, m, v)


def _two_level_allgather(srcs, dst_block, send_sems, recv_sems, local_sems):
    x, y, c = _position()
    me, sibling = (x, y, c), (x, y, 1 - c)
    chips = [(1 - x, y), (x, 1 - y), (1 - x, 1 - y)]
    n = len(srcs)

    def copy(a, k, block, to, src=None):
        dst = dst_block(a, _flat(*block))
        return pltpu.make_async_remote_copy(
            src_ref=dst if src is None else src, dst_ref=dst, send_sem=send_sems.at[a * 7 + k],
            recv_sem=recv_sems.at[a * 7 + k], device_id=to, device_id_type=MESH)

    mine = [pltpu.make_async_copy(srcs[a], dst_block(a, _flat(*me)), local_sems.at[a]) for a in range(n)]
    for cp in mine:
        cp.start()
    first = []
    for a in range(n):
        first.append(copy(a, 0, me, sibling, src=srcs[a]))
        first += [copy(a, 1 + j, me, (*chip, c), src=srcs[a]) for j, chip in enumerate(chips)]
    for cp in first:
        cp.start()
    passed = []
    for j, chip in enumerate(chips):
        for a in range(n):
            copy(a, 1 + j, (*chip, c), me).wait_recv()
            fwd = copy(a, 4 + j, (*chip, c), sibling)
            fwd.start()
            passed.append(fwd)
    for a in range(n):
        copy(a, 0, sibling, me).wait_recv()
        for j, chip in enumerate(chips):
            copy(a, 4 + j, (*chip, 1 - c), me).wait_recv()
    for cp in first + passed:
        cp.wait_send()
    for cp in mine:
        cp.wait()


def allgather_rows(x, name):
    def body(x_ref, out_ref, send_sems, recv_sems, local_sems):
        _two_level_allgather([x_ref], lambda a, idx: out_ref.at[idx], send_sems, recv_sems, local_sems)

    vmem = pl.BlockSpec(memory_space=pltpu.VMEM)
    return pl.pallas_call(
        body, name=name, in_specs=[vmem], out_specs=vmem,
        out_shape=jax.ShapeDtypeStruct((N_DEV,) + x.shape, x.dtype),
        scratch_shapes=[pltpu.SemaphoreType.DMA((7,)), pltpu.SemaphoreType.DMA((7,)), pltpu.SemaphoreType.DMA((1,))],
        compiler_params=_params(),
    )(x)


def _full_weight_shapes():
    return [jax.ShapeDtypeStruct((D_MODEL, IN_W), BF16), jax.ShapeDtypeStruct((D_MODEL, D_MODEL), BF16)]


def allgather_weights(w_in_b, w_out_b):
    def body(win_ref, wout_ref, fin_ref, fout_ref, s1, r1, l1, s2, r2):
        block = _weight_blocks(fin_ref, fout_ref)
        first = _gather_phase1([win_ref.at[0], wout_ref.at[0]], block, s1, r1, l1)
        _start_all(first)
        _wait_all(first)
        second = _gather_phase2(block, s2, r2)
        _start_all(second)
        _wait_all(second)

    return pl.pallas_call(
        body, name="allgather_weights", in_specs=[HBM, HBM], out_specs=[HBM, HBM], out_shape=_full_weight_shapes(),
        scratch_shapes=GATHER1_SEMS + GATHER2_SEMS, compiler_params=_params(),
    )(w_in_b, w_out_b)


def reduce_first(dw_in, dw_out, name):
    def body(ga_ref, gb_ref, ra_ref, rb_ref, send_sems, recv_sems):
        copies = _reduce_phase1([ga_ref, gb_ref], [ra_ref, rb_ref], send_sems, recv_sems)
        _start_all(copies)
        _wait_all(copies)

    return pl.pallas_call(
        body, name=name, in_specs=[HBM, HBM], out_specs=[HBM, HBM], out_shape=_landing_shapes(4),
        scratch_shapes=REDUCE1_SEMS, compiler_params=_params(),
    )(*_split_cores((dw_in, dw_out)))


def _one_class(a):
    return a.reshape((1,) + a.shape)


def layer_fwd(x, g, scale, shift, gate, w_in, w_out, lg, gather=None):
    proj, h, qkv4, qkv16, *began = inproj_fwd(x, g, 1.0 + scale, shift, w_in, gather)
    qkv = (_one_class(proj), qkv4, qkv16)
    attn_outs = [attn_fwd(arr, dil) for dil, arr in zip(DILATIONS, qkv)]
    attn_outs[0] = tuple(a[0] for a in attn_outs[0])
    o_pre, y_r, *next_weights = ret_fwd(proj, lg, began if gather is not None else None)
    next_weights = tuple(next_weights) if gather is not None else None
    x_new, o_a, lse, lse4, lse16 = outproj_fwd(x, gate, w_out, attn_outs, y_r, proj)
    saved = dict(x=x, proj=proj, h=h, qkv=qkv, o_a=o_a, lse=(_one_class(lse), lse4, lse16), o_pre=o_pre, y_r=y_r)
    return x_new, saved, next_weights


def layer_bwd(dxn, saved, g, scale, gate, w_in, w_out, lg, pos):
    proj = saved["proj"]
    do_a, delta, dyr, dz_a, dz_r, y, dxb, do4, do16, dl4, dl16 = outproj_bwd(
        dxn, gate, w_out, saved["o_a"], saved["y_r"], proj)
    dw_out, dgate = wout_grad(y, dxb, gate, w_out)
    dq_r, dk_r, dv_r, glf, glb = ret_bwd(proj, lg, saved["o_pre"], dyr)
    dos, deltas = (_one_class(do_a), do4, do16), (_one_class(delta), dl4, dl16)
    dqkv_a = [attn_bwd(arr, d_o, lse, dl, dil)
              for dil, arr, d_o, lse, dl in zip(DILATIONS, saved["qkv"], dos, saved["lse"], deltas)]
    dqkv_a[0] = [t[0] for t in dqkv_a[0]]
    dproj = assemble_dproj(dqkv_a, dz_a, dq_r, dk_r, dv_r, dz_r)
    dw_in = win_grad(saved["h"], dproj)
    landed = reduce_first(dw_in, dw_out, "reduce_first")
    sums = [chip_sum(pos, g4, r, "chip_sum") for g4, r in zip(_split_cores((dw_in, dw_out)), landed)]
    dx, stats, *others = inproj_bwd(dproj, w_in, saved["x"], g, 1.0 + scale, dxn, sums)
    dlg = jnp.concatenate([glf[:, 0, 0], glb[:, 0, 0]])
    return dx, stats[0:1], stats[1:2], dgate, stats[2:3], dlg, (sums, others)


ROWS_B_ADA = DEPTH * 3 * D_MODEL // 128
ROWS_GAIN = DEPTH * D_MODEL // 128
ROWS_FINAL = D_MODEL // 128
ROWS_MISC = 8
ROWS_SMALL = ROWS_B_ADA + ROWS_GAIN + ROWS_FINAL + ROWS_MISC


def _pack_small(b_ada_like, gain_like, final_like, dec_f, dec_b, loss=None):
    misc = jnp.zeros((ROWS_MISC, 128), F32)
    misc = misc.at[0, :2 * DEPTH * RET_HEADS].set(jnp.concatenate([dec_f.reshape(-1), dec_b.reshape(-1)]))
    if loss is not None:
        misc = misc.at[1, 0].set(loss)
    return jnp.concatenate([b_ada_like.reshape(ROWS_B_ADA, 128), gain_like.reshape(ROWS_GAIN, 128),
                            final_like.reshape(ROWS_FINAL, 128), misc], axis=0)


def _unpack_small(p):
    r0, r1, r2 = ROWS_B_ADA, ROWS_B_ADA + ROWS_GAIN, ROWS_B_ADA + ROWS_GAIN + ROWS_FINAL
    n = DEPTH * RET_HEADS
    return (p[:r0].reshape(DEPTH, 3 * D_MODEL), p[r0:r1].reshape(DEPTH, D_MODEL), p[r1:r2].reshape(D_MODEL),
            p[r2, :n].reshape(DEPTH, RET_HEADS), p[r2, n:2 * n].reshape(DEPTH, RET_HEADS))


def kernel(x, c, norm_gain, w_ada, b_ada, w_in, w_out, ret_decay_logit_f, ret_decay_logit_b, final_gain, loss_target, m_norm_gain, m_w_ada, m_b_ada, m_w_in, m_w_out, m_ret_decay_logit_f, m_ret_decay_logit_b, m_final_gain, v_norm_gain, v_w_ada, v_b_ada, v_w_in, v_w_out, v_ret_decay_logit_f, v_ret_decay_logit_b, v_final_gain):
    px, py, pc = _position()
    me = _flat(px, py, pc)
    pos = jnp.stack([px, py, pc]).astype(jnp.int32)
    x2, target = x[0], loss_target[0]

    w_in_b, w_out_b = w_in.astype(BF16), w_out.astype(BF16)
    weights = allgather_weights(w_in_b, w_out_b)

    c_all = allgather_rows(c.reshape(D_MODEL // 128, 128), "allgather_c").reshape(N_DEV, D_MODEL)
    act, mod_part = ada_fwd(c_all, w_ada)
    mod_all = allgather_rows(mod_part.reshape(-1, 128), "allgather_mod").reshape(N_DEV, DEPTH, N_DEV, W_ADA_SHARD)
    mod = lax.dynamic_index_in_dim(mod_all, me, axis=2, keepdims=False)
    mod = mod.transpose(1, 0, 2).reshape(DEPTH, 3 * D_MODEL) + b_ada
    shift, scale, gate = mod[:, :D_MODEL], mod[:, D_MODEL:2 * D_MODEL], mod[:, 2 * D_MODEL:]

    lg = jnp.concatenate([jax.nn.log_sigmoid(ret_decay_logit_f), jax.nn.log_sigmoid(ret_decay_logit_b)], axis=1)

    h = x2
    saved, layer_weights = [], []
    for l in range(DEPTH):
        layer_weights.append(weights)
        gather = (w_in_b, w_out_b, l + 1) if l + 1 < DEPTH else None
        h, sv, weights = layer_fwd(h, norm_gain[l:l + 1], scale[l:l + 1], shift[l:l + 1], gate[l:l + 1],
                                   *layer_weights[l], lg[l], gather)
        saved.append(sv)
    dh, loss_part, dfinal = loss_head(h, final_gain.reshape(1, D_MODEL), target)

    dmod, dgain, dlg, reduced = [None] * DEPTH, [None] * DEPTH, [None] * DEPTH, [None] * DEPTH
    for l in reversed(range(DEPTH)):
        dh, dshift, dscale, dgate, dg, dlg[l], reduced[l] = layer_bwd(
            dh, saved[l], norm_gain[l:l + 1], scale[l:l + 1], gate[l:l + 1], *layer_weights[l], lg[l], pos)
        dmod[l] = jnp.concatenate([dshift, dscale, dgate], axis=1)
        dgain[l] = dg

    dlg = jnp.stack(dlg)
    dlogit_f = dlg[:, :RET_HEADS] * jax.nn.sigmoid(-ret_decay_logit_f)
    dlogit_b = dlg[:, RET_HEADS:] * jax.nn.sigmoid(-ret_decay_logit_b)
    packed = _pack_small(jnp.concatenate(dmod, axis=0), jnp.concatenate(dgain, axis=0), dfinal, dlogit_f, dlogit_b,
                         loss=loss_part[0, 0])
    gathered = allgather_rows(packed, "allgather_small")
    small = small_update(gathered,
                         _pack_small(b_ada, norm_gain, final_gain, ret_decay_logit_f, ret_decay_logit_b),
                         _pack_small(m_b_ada, m_norm_gain, m_final_gain, m_ret_decay_logit_f, m_ret_decay_logit_b),
                         _pack_small(v_b_ada, v_norm_gain, v_final_gain, v_ret_decay_logit_f, v_ret_decay_logit_b))
    loss = small[0][ROWS_B_ADA + ROWS_GAIN + ROWS_FINAL + 1, 0]
    (g_b_ada, g_gain, g_final, g_dec_f, g_dec_b), (d_b_ada, d_gain, d_final, d_dec_f, d_dec_b), \
        (m_b_ada2, m_gain2, m_final2, m_dec_f2, m_dec_b2), (v_b_ada2, v_gain2, v_final2, v_dec_f2, v_dec_b2) = \
        [_unpack_small(p) for p in small]

    dmod_all = gathered[:, :ROWS_B_ADA].reshape(N_DEV, DEPTH, 3 * D_MODEL)
    dmod_mine = lax.dynamic_slice_in_dim(dmod_all, me * W_ADA_SHARD, W_ADA_SHARD, axis=2).transpose(1, 0, 2)
    g_w_ada, d_w_ada, m_w_ada2, v_w_ada2 = ada_update(act.T, dmod_mine, w_ada, m_w_ada, v_w_ada)

    upd_in = upd_out = None
    for l in reversed(range(DEPTH)):
        upd_in = shard_update(l, reduced[l][0][0], reduced[l][1][0], w_in, m_w_in, v_w_in, upd_in, f"w_in_update_{l}")
        upd_out = shard_update(l, reduced[l][0][1], reduced[l][1][1], w_out, m_w_out, v_w_out, upd_out, f"w_out_update_{l}")
    g_w_in, d_w_in, m_w_in2, v_w_in2 = upd_in
    g_w_out, d_w_out, m_w_out2, v_w_out2 = upd_out

    return (loss, dh[None],
            g_gain, g_w_ada, g_b_ada, g_w_in, g_w_out, g_dec_f, g_dec_b, g_final,
            d_gain, d_w_ada, d_b_ada, d_w_in, d_w_out, d_dec_f, d_dec_b, d_final,
            m_gain2, m_w_ada2, m_b_ada2, m_w_in2, m_w_out2, m_dec_f2, m_dec_b2, m_final2,
            v_gain2, v_w_ada2, v_b_ada2, v_w_in2, v_w_out2, v_dec_f2, v_dec_b2, v_final2)
```

```python
import functools
import math

import jax
import jax.numpy as jnp
from jax import lax
from jax.experimental import pallas as pl
from jax.experimental.pallas import tpu as pltpu

F32, BF16 = jnp.float32, jnp.bfloat16

D_MODEL = 2048
DEPTH = 4
N_DEV = 8
ATTN_WIDTH = 1024
HEAD_DIM = 128
N_HEADS_ATTN = 8
DILATIONS = (1, 4, 16)
RADIUS = 64
RET_HEADS = 4
RET_QK = 128
RET_V = 256
RET_CHUNK = 128
IN_W = 7168
QKV_A = 3 * ATTN_WIDTH
COL_ZA, COL_QR, COL_KR, COL_VR, COL_ZR = 3072, 4096, 4608, 5120, 6144
W_IN_SHARD = IN_W // N_DEV
W_OUT_SHARD = D_MODEL // N_DEV
W_ADA_SHARD = 3 * D_MODEL // N_DEV
NORM_EPS = 1e-6
MASK_VALUE = -1e30
ATTN_SCALE = HEAD_DIM ** -0.5
RET_SCALE = RET_QK ** -0.5
LN2 = math.log(2.0)

ADAM_LR, ADAM_B1, ADAM_B2, ADAM_EPS, ADAM_WD, ADAM_STEP = 0.001, 0.9, 0.999, 1e-08, 0.01, 10
ADAM_C1 = 1.0 / (1.0 - ADAM_B1 ** ADAM_STEP)
ADAM_C2 = 1.0 / (1.0 - ADAM_B2 ** ADAM_STEP)

VMEM_LIMIT_BYTES = 56 * 1024 * 1024
MESH = pl.DeviceIdType.MESH


def _params(*sem):
    return pltpu.CompilerParams(dimension_semantics=sem if sem else None, vmem_limit_bytes=VMEM_LIMIT_BYTES)


def _dot(a, b):
    return jnp.dot(a, b, preferred_element_type=F32)


def _dot_nt(a, b):
    return lax.dot_general(a, b, (((1,), (1,)), ((), ())), preferred_element_type=F32)


def _dot_tn(a, b):
    return lax.dot_general(a, b, (((0,), (0,)), ((), ())), preferred_element_type=F32)


def _iota(shape, dim):
    return lax.broadcasted_iota(jnp.int32, shape, dim)


def _sigmoid(z):
    return 1.0 / (1.0 + jnp.exp(-z))


HBM = pl.BlockSpec(memory_space=pl.ANY)


def _position():
    return lax.axis_index("x"), lax.axis_index("y"), lax.axis_index("c")


def _flat(px, py, pc):
    return 4 * px + 2 * py + pc


def _remote(src, dst, send_sem, recv_sem, to):
    return pltpu.make_async_remote_copy(src_ref=src, dst_ref=dst, send_sem=send_sem, recv_sem=recv_sem,
                                        device_id=to, device_id_type=MESH)


def _weight_blocks(fin_ref, fout_ref):
    def block(a, idx):
        if a == 0:
            return fin_ref.at[:, pl.ds(pl.multiple_of(idx * W_IN_SHARD, 128), W_IN_SHARD)]
        return fout_ref.at[pl.ds(pl.multiple_of(idx * W_OUT_SHARD, W_OUT_SHARD), W_OUT_SHARD), :]
    return block


GATHER1_SEMS = [pltpu.SemaphoreType.DMA((8,)), pltpu.SemaphoreType.DMA((8,)), pltpu.SemaphoreType.DMA((2,))]
GATHER2_SEMS = [pltpu.SemaphoreType.DMA((6,)), pltpu.SemaphoreType.DMA((6,))]
REDUCE1_SEMS = [pltpu.SemaphoreType.DMA((2,)), pltpu.SemaphoreType.DMA((2,))]
REDUCE2_SEMS = [pltpu.SemaphoreType.DMA((6,)), pltpu.SemaphoreType.DMA((6,))]


def _gather_phase1(srcs, block, send_sems, recv_sems, local_sems):
    x, y, c = _position()
    mine = [block(a, _flat(x, y, c)) for a in range(2)]
    copies = [pltpu.make_async_copy(srcs[a], mine[a], local_sems.at[a]) for a in range(2)]
    for a in range(2):
        copies.append(_remote(srcs[a], mine[a], send_sems.at[4 * a], recv_sems.at[4 * a], (x, y, 1 - c)))
        for j, (px, py) in enumerate([(1 - x, y), (x, 1 - y), (1 - x, 1 - y)]):
            copies.append(_remote(srcs[a], mine[a], send_sems.at[4 * a + 1 + j], recv_sems.at[4 * a + 1 + j], (px, py, c)))
    return copies


def _gather_phase2(block, send_sems, recv_sems):
    x, y, c = _position()
    copies = []
    for a in range(2):
        for j, (px, py) in enumerate([(1 - x, y), (x, 1 - y), (1 - x, 1 - y)]):
            blk = block(a, _flat(px, py, c))
            copies.append(_remote(blk, blk, send_sems.at[3 * a + j], recv_sems.at[3 * a + j], (x, y, 1 - c)))
    return copies


def _reduce_phase1(grads, landings, send_sems, recv_sems):
    x, y, c = _position()
    return [_remote(g.at[:, 1 - c], r, send_sems.at[a], recv_sems.at[a], (x, y, 1 - c))
            for a, (g, r) in enumerate(zip(grads, landings))]


def _reduce_phase2(sums, landings, send_sems, recv_sems):
    x, y, c = _position()
    copies = []
    for a, (p, r) in enumerate(zip(sums, landings)):
        for k in (1, 2, 3):
            to = (1 - x if k & 2 else x, 1 - y if k & 1 else y, c)
            copies.append(_remote(p.at[k], r.at[k - 1], send_sems.at[3 * a + k - 1], recv_sems.at[3 * a + k - 1], to))
    return copies


def _landing_shapes(n):
    return [jax.ShapeDtypeStruct((n, D_MODEL, W_IN_SHARD), BF16), jax.ShapeDtypeStruct((n, W_OUT_SHARD, D_MODEL), BF16)]


def _split_cores(slabs):
    return tuple(s.reshape((N_DEV // 2, 2) + s.shape[1:]) for s in slabs)


def _start_all(copies):
    for cp in copies:
        cp.start()


def _wait_all(copies):
    for cp in copies:
        cp.wait()


def _grid_edge(n_axes):
    first = last = None
    for ax in range(n_axes):
        f = pl.program_id(ax) == 0
        e = pl.program_id(ax) == pl.num_programs(ax) - 1
        first = f if first is None else first & f
        last = e if last is None else last & e
    return first, last


LANES = 128


def _stage_shape(rows, width):
    return pltpu.VMEM((width // LANES, rows, LANES), F32)


def _fill_stage(stage_ref, value):
    for t in range(stage_ref.shape[0]):
        stage_ref[t] = value[:, t * LANES:(t + 1) * LANES]


def _read_stage(stage_ref):
    return jnp.concatenate([stage_ref[t] for t in range(stage_ref.shape[0])], axis=1)


def _split_classes(stage_ref, out_refs):
    n_t, rows, _ = stage_ref.shape
    for out_ref in out_refs:
        dil = out_ref.shape[0]
        for r in range(dil):
            for t in range(n_t):
                piece = stage_ref[t, pl.ds(r, rows // dil, stride=dil), :]
                out_ref[r, :, t * LANES:(t + 1) * LANES] = piece.astype(out_ref.dtype)


def _merge_classes(in_ref, stage_ref):
    dil, per = in_ref.shape[0], in_ref.shape[1]
    for r in range(dil):
        for t in range(stage_ref.shape[0]):
            stage_ref[t, pl.ds(r, per, stride=dil), :] = in_ref[r, :, t * LANES:(t + 1) * LANES].astype(F32)


def _class_block(dil, tm, width, index_map):
    return pl.BlockSpec((dil, tm // dil, width), index_map)


def inproj_fwd(x, g, scale1p, shift, w, gather=None):
    s_len = x.shape[0]
    tm, tn = min(1024, s_len), 512

    n_qkv = QKV_A // tn

    def body(x_ref, g_ref, sc_ref, sh_ref, w_ref, *rest):
        if gather is not None:
            (win_ref, wout_ref, proj_ref, h_ref, q4_ref, q16_ref, fin_ref, fout_ref, stage,
             send_sems, recv_sems, local_sems) = rest
            first, last = _grid_edge(2)
            copies = _gather_phase1([win_ref.at[gather[2]], wout_ref.at[gather[2]]], _weight_blocks(fin_ref, fout_ref),
                                    send_sems, recv_sems, local_sems)
            pl.when(first)(lambda: _start_all(copies))
        else:
            proj_ref, h_ref, q4_ref, q16_ref, stage = rest

        @pl.when(pl.program_id(1) == 0)
        def _():
            xv = x_ref[...]
            r = lax.rsqrt(jnp.mean(xv * xv, axis=-1, keepdims=True) + NORM_EPS)
            h_ref[...] = ((xv * r * g_ref[...]) * sc_ref[...] + sh_ref[...]).astype(BF16)
        res = _dot(h_ref[...], w_ref[...])
        proj_ref[...] = res.astype(BF16)

        @pl.when(pl.program_id(1) < n_qkv)
        def _():
            _fill_stage(stage, res)
            _split_classes(stage, [q4_ref, q16_ref])
        if gather is not None:
            pl.when(last)(lambda: _wait_all(copies))

    vec = pl.BlockSpec((1, D_MODEL), lambda i, j: (0, 0))
    in_specs = [pl.BlockSpec((tm, D_MODEL), lambda i, j: (i, 0)), vec, vec, vec,
                pl.BlockSpec((D_MODEL, tn), lambda i, j: (0, j))]
    out_specs = [pl.BlockSpec((tm, tn), lambda i, j: (i, j)), pl.BlockSpec((tm, D_MODEL), lambda i, j: (i, 0))]
    out_shape = [jax.ShapeDtypeStruct((s_len, IN_W), BF16), jax.ShapeDtypeStruct((s_len, D_MODEL), BF16)]
    for dil in DILATIONS[1:]:
        out_specs.append(pl.BlockSpec((dil, tm // dil, tn), lambda i, j: (0, i, jnp.minimum(j, n_qkv - 1))))
        out_shape.append(jax.ShapeDtypeStruct((dil, s_len // dil, QKV_A), BF16))
    scratch = [_stage_shape(tm, tn)]
    extra = ()
    if gather is not None:
        in_specs += [HBM, HBM]
        out_specs += [HBM, HBM]
        out_shape += [jax.ShapeDtypeStruct((D_MODEL, IN_W), BF16), jax.ShapeDtypeStruct((D_MODEL, D_MODEL), BF16)]
        scratch += GATHER1_SEMS
        extra = tuple(gather[:2])
    return pl.pallas_call(
        body, name="inproj_fwd_gather" if gather is not None else "inproj_fwd", grid=(s_len // tm, IN_W // tn),
        in_specs=in_specs, out_specs=out_specs, out_shape=out_shape, scratch_shapes=scratch,
        compiler_params=_params("arbitrary", "arbitrary"),
    )(x, g, scale1p, shift, w, *extra)


MASK_DISTANCE = 1e33
ATTN_TILE = 128


ATTN_UNROLL_FWD, ATTN_UNROLL_BWD = 15, 10


def _attn_plan(sub_len, unroll):
    tq = min(sub_len, ATTN_TILE)
    win = min(sub_len, tq + 2 * RADIUS)
    heads = 1 if sub_len > 1024 else (2 if sub_len > 256 else N_HEADS_ATTN)
    return tq, win, sub_len // tq, heads, unroll


def _attn_tiles(sub_len, tq, win, n_tiles, unroll, tile):
    tile(0, 0, 0)
    if n_tiles > 2:
        def mid(i, carry):
            q0 = pl.multiple_of(i * tq, tq)
            tile(q0, pl.multiple_of(q0 - RADIUS, RADIUS), 1)
            return carry
        lax.fori_loop(1, n_tiles - 1, mid, 0, unroll=min(unroll, n_tiles - 2))
    if n_tiles > 1:
        tile(sub_len - tq, sub_len - win, 2)


def _attn_bias(bias_ref, head, heads, tq, win, dil):
    h = pl.program_id(1) * heads + head
    slope = jnp.exp(-(h + 1).astype(F32) * LN2 * jnp.ones((1, 1), F32))
    rel = _iota((tq, win), 1) - _iota((tq, win), 0)
    for v, off in enumerate((0, RADIUS, win - tq)):
        dist = jnp.abs(rel - off)
        bias_ref[v] = slope * jnp.where(dist <= RADIUS, (dist * dil).astype(F32), MASK_DISTANCE)


def _attn_specs(n_cls, sub_len, heads):
    width = heads * HEAD_DIM
    per = ATTN_WIDTH // width

    def col(part):
        return pl.BlockSpec((None, sub_len, width), lambda r, g: (r, 0, part * per + g))
    return col, (n_cls, N_HEADS_ATTN // heads)


def _stat_spec(sub_len):
    return pl.BlockSpec((None, sub_len, LANES), lambda r, g: (r, 0, 0))


def _stat_column(ref, rows, lane, h_abs):
    return jnp.sum(jnp.where(lane == h_abs, ref[rows, :], 0.0), axis=1, keepdims=True)


def attn_fwd(qkv, dil):
    n_cls, sub_len, _ = qkv.shape
    tq, win, n_tiles, heads, unroll = _attn_plan(sub_len, ATTN_UNROLL_FWD)

    def body(q_ref, k_ref, v_ref, o_ref, lse_ref, bias_ref):
        @pl.when(pl.program_id(1) == 0)
        def _():
            lse_ref[...] = jnp.zeros_like(lse_ref)
        lane = _iota((tq, LANES), 1)
        for head in range(heads):
            lanes = slice(head * HEAD_DIM, (head + 1) * HEAD_DIM)
            h_abs = pl.program_id(1) * heads + head
            _attn_bias(bias_ref, head, heads, tq, win, dil)

            def tile(q0, start, variant):
                s = _dot_nt(q_ref[pl.ds(q0, tq), lanes], k_ref[pl.ds(start, win), lanes]) * ATTN_SCALE - bias_ref[variant]
                m = jnp.max(s, axis=1, keepdims=True)
                p = jnp.exp(s - m)
                den = jnp.sum(p, axis=1, keepdims=True)
                o_ref[pl.ds(q0, tq), lanes] = _dot(p.astype(BF16), v_ref[pl.ds(start, win), lanes]) / den
                rows = pl.ds(q0, tq)
                lse_ref[rows, :] = jnp.where(lane == h_abs, m + jnp.log(den), lse_ref[rows, :])

            _attn_tiles(sub_len, tq, win, n_tiles, unroll, tile)

    col, grid = _attn_specs(n_cls, sub_len, heads)
    return pl.pallas_call(
        body, name=f"attn_fwd_d{dil}", grid=grid,
        in_specs=[col(0), col(1), col(2)], out_specs=[col(0), _stat_spec(sub_len)],
        out_shape=[jax.ShapeDtypeStruct((n_cls, sub_len, ATTN_WIDTH), F32), jax.ShapeDtypeStruct((n_cls, sub_len, LANES), F32)],
        scratch_shapes=[pltpu.VMEM((3, tq, win), F32)],
        compiler_params=_params("arbitrary", "arbitrary"),
    )(qkv, qkv, qkv)


def attn_bwd(qkv, do, lse, delta, dil):
    n_cls, sub_len, _ = qkv.shape
    tq, win, n_tiles, heads, unroll = _attn_plan(sub_len, ATTN_UNROLL_BWD)

    def body(q_ref, k_ref, v_ref, do_ref, lse_ref, dl_ref, dq_ref, dk_ref, dv_ref, bias_ref, dk_acc, dv_acc):
        lane = _iota((tq, LANES), 1)
        for head in range(heads):
            lanes = slice(head * HEAD_DIM, (head + 1) * HEAD_DIM)
            h_abs = pl.program_id(1) * heads + head
            _attn_bias(bias_ref, head, heads, tq, win, dil)
            dk_acc[...] = jnp.zeros_like(dk_acc)
            dv_acc[...] = jnp.zeros_like(dv_acc)

            def tile(q0, start, variant):
                q = q_ref[pl.ds(q0, tq), lanes]
                k = k_ref[pl.ds(start, win), lanes]
                v = v_ref[pl.ds(start, win), lanes]
                dov = do_ref[pl.ds(q0, tq), lanes]
                s = _dot_nt(q, k) * ATTN_SCALE - bias_ref[variant]
                rows = pl.ds(q0, tq)
                p = jnp.exp(s - _stat_column(lse_ref, rows, lane, h_abs))
                ds = (p * (_dot_nt(dov, v) - _stat_column(dl_ref, rows, lane, h_abs))).astype(BF16)
                dq_ref[pl.ds(q0, tq), lanes] = (_dot(ds, k) * ATTN_SCALE).astype(BF16)
                dk_acc[pl.ds(start, win), :] += _dot_tn(ds, q) * ATTN_SCALE
                dv_acc[pl.ds(start, win), :] += _dot_tn(p.astype(BF16), dov)

            _attn_tiles(sub_len, tq, win, n_tiles, unroll, tile)
            dk_ref[:, lanes] = dk_acc[...].astype(BF16)
            dv_ref[:, lanes] = dv_acc[...].astype(BF16)

    col, grid = _attn_specs(n_cls, sub_len, heads)
    out = jax.ShapeDtypeStruct((n_cls, sub_len, ATTN_WIDTH), BF16)
    return pl.pallas_call(
        body, name=f"attn_bwd_d{dil}", grid=grid,
        in_specs=[col(0), col(1), col(2), col(0), _stat_spec(sub_len), _stat_spec(sub_len)],
        out_specs=[col(0), col(0), col(0)], out_shape=[out, out, out],
        scratch_shapes=[pltpu.VMEM((3, tq, win), F32), pltpu.VMEM((sub_len, HEAD_DIM), F32),
                        pltpu.VMEM((sub_len, HEAD_DIM), F32)],
        compiler_params=_params("arbitrary", "arbitrary"),
    )(qkv, qkv, qkv, do, lse, delta)


RET_UNROLL = 8
RET_UNROLL_BWD = 8


def _ret_tables(lg_ref):
    h = pl.program_id(0)
    one = jnp.ones((1, 1), F32)
    lgf, lgb = lg_ref[h] * one, lg_ref[RET_HEADS + h] * one
    c = RET_CHUNK
    rel = (_iota((c, c), 0) - _iota((c, c), 1)).astype(F32)
    dec_f = jnp.where(rel >= 0, jnp.exp(jnp.maximum(rel, 0.0) * lgf), 0.0)
    dec_b = jnp.where(rel <= 0, jnp.exp(jnp.maximum(-rel, 0.0) * lgb), 0.0)
    ci = _iota((c, 1), 0).astype(F32)
    tab = dict(rel=rel, dec_f=dec_f, dec_b=dec_b, ci=ci,
               xi_f=jnp.exp((ci + 1.0) * lgf), ze_f=jnp.exp((c - 1.0 - ci) * lgf), g_f=jnp.exp(c * lgf),
               xi_b=jnp.exp((c - ci) * lgb), ze_b=jnp.exp(ci * lgb), g_b=jnp.exp(c * lgb))
    return tab


def _ret_specs(s_len):
    q = pl.BlockSpec((s_len, RET_QK), lambda h: (0, COL_QR // RET_QK + h))
    k = pl.BlockSpec((s_len, RET_QK), lambda h: (0, COL_KR // RET_QK + h))
    v = pl.BlockSpec((s_len, RET_V), lambda h: (0, COL_VR // RET_V + h))
    wide = pl.BlockSpec((s_len, RET_V), lambda h: (0, h))
    narrow = pl.BlockSpec((s_len, RET_QK), lambda h: (0, h))
    smem = pl.BlockSpec(memory_space=pltpu.SMEM)
    return smem, q, k, v, wide, narrow


def ret_fwd(proj, lg, finish=None):
    s_len = proj.shape[0]
    c, n_chunks = RET_CHUNK, proj.shape[0] // RET_CHUNK

    def body(lg_ref, q_ref, k_ref, v_ref, *rest):
        if finish is not None:
            _, _, opre_ref, y_ref, fin_ref, fout_ref, st_f, st_b, send_sems, recv_sems = rest
            first, last = _grid_edge(1)
            copies = _gather_phase2(_weight_blocks(fin_ref, fout_ref), send_sems, recv_sems)
            pl.when(first)(lambda: _start_all(copies))
        else:
            opre_ref, y_ref, st_f, st_b = rest
        t = _ret_tables(lg_ref)
        dec = t["dec_f"] + t["dec_b"]
        st_f[...] = jnp.zeros_like(st_f)
        st_b[...] = jnp.zeros_like(st_b)

        def load(n):
            r0 = pl.multiple_of(n * c, c)
            q, k, v = q_ref[pl.ds(r0, c), :], k_ref[pl.ds(r0, c), :], v_ref[pl.ds(r0, c), :]
            return r0, q, (k.astype(F32) * RET_SCALE), v

        def fwd(n, carry):
            r0, q, kf, v = load(n)
            inner = (_dot_nt(q, kf.astype(BF16)) * dec).astype(BF16)
            opre_ref[pl.ds(r0, c), :] = _dot(inner, v) + _dot(q, st_f[...].astype(BF16)) * t["xi_f"]
            st_f[...] = st_f[...] * t["g_f"] + _dot_tn((kf * t["ze_f"]).astype(BF16), v)
            return carry

        def bwd(i, carry):
            r0, q, kf, v = load(n_chunks - 1 - i)
            o = opre_ref[pl.ds(r0, c), :] + _dot(q, st_b[...].astype(BF16)) * t["xi_b"]
            st_b[...] = st_b[...] * t["g_b"] + _dot_tn((kf * t["ze_b"]).astype(BF16), v)
            opre_ref[pl.ds(r0, c), :] = o
            y_ref[pl.ds(r0, c), :] = o * lax.rsqrt(jnp.mean(o * o, axis=-1, keepdims=True) + NORM_EPS)
            return carry

        lax.fori_loop(0, n_chunks, fwd, 0, unroll=min(RET_UNROLL, n_chunks))
        lax.fori_loop(0, n_chunks, bwd, 0, unroll=min(RET_UNROLL, n_chunks))
        if finish is not None:
            pl.when(last)(lambda: _wait_all(copies))

    smem, q, k, v, wide, _ = _ret_specs(s_len)
    out = jax.ShapeDtypeStruct((s_len, RET_HEADS * RET_V), F32)
    in_specs, out_specs, out_shape = [smem, q, k, v], [wide, wide], [out, out]
    scratch = [pltpu.VMEM((RET_QK, RET_V), F32), pltpu.VMEM((RET_QK, RET_V), F32)]
    aliases, extra = {}, ()
    if finish is not None:
        in_specs += [HBM, HBM]
        out_specs += [HBM, HBM]
        out_shape += _full_weight_shapes()
        scratch += GATHER2_SEMS
        aliases, extra = {4: 2, 5: 3}, tuple(finish)
    return pl.pallas_call(
        body, name="ret_fwd_gather2" if finish is not None else "ret_fwd", grid=(RET_HEADS,), in_specs=in_specs,
        out_specs=out_specs, out_shape=out_shape, scratch_shapes=scratch, input_output_aliases=aliases,
        compiler_params=_params("arbitrary"),
    )(lg, proj, proj, proj, *extra)


def ret_bwd(proj, lg, o_pre, dy):
    s_len = proj.shape[0]
    c, n_chunks = RET_CHUNK, proj.shape[0] // RET_CHUNK
    cf = float(c)

    def body(lg_ref, q_ref, k_ref, v_ref, o_ref, dy_ref, dq_ref, dk_ref, dv_ref, glf_ref, glb_ref,
             st_f, dst_b, st_b, dst_f, keep_sf, keep_dtb, acc_f, acc_b, acc_sf, acc_sb):
        t = _ret_tables(lg_ref)
        dec = t["dec_f"] + t["dec_b"]
        e_f, e_b, ci = t["rel"] * t["dec_f"], -t["rel"] * t["dec_b"], t["ci"]
        for ref in (st_f, dst_b, st_b, dst_f, acc_f, acc_b, acc_sf, acc_sb):
            ref[...] = jnp.zeros_like(ref)

        def load(n):
            r0 = pl.multiple_of(n * c, c)
            q, k, v = q_ref[pl.ds(r0, c), :], k_ref[pl.ds(r0, c), :], v_ref[pl.ds(r0, c), :]
            o, dyv = o_ref[pl.ds(r0, c), :], dy_ref[pl.ds(r0, c), :]
            rr = lax.rsqrt(jnp.mean(o * o, axis=-1, keepdims=True) + NORM_EPS)
            y = o * rr
            do = (rr * (dyv - y * jnp.mean(dyv * y, axis=-1, keepdims=True))).astype(BF16)
            return r0, q, k.astype(F32) * RET_SCALE, v, do

        def fwd(n, carry):
            r0, q, kf, v, do = load(n)
            qf, kb = q.astype(F32), kf.astype(BF16)
            a = _dot_nt(q, kb)
            b = _dot_nt(do, v)
            da = (b * dec).astype(BF16)
            ab = a * b
            sf_b, dtb_b = st_f[...].astype(BF16), dst_b[...].astype(BF16)
            dq_inter = _dot_nt(do, sf_b) * t["xi_f"]
            dk_inter = _dot_nt(v, dtb_b) * t["ze_b"]
            acc_f[...] += e_f * ab + (ci + 1.0) * (qf * dq_inter)
            acc_b[...] += e_b * ab + ci * (kf * dk_inter)
            dq_ref[pl.ds(r0, c), :] = _dot(da, kb) + dq_inter
            dk_ref[pl.ds(r0, c), :] = _dot_tn(da, q) + dk_inter
            dv_ref[pl.ds(r0, c), :] = _dot_tn((a * dec).astype(BF16), do) + _dot((kf * t["ze_b"]).astype(BF16), dtb_b)
            keep_sf[n] = sf_b
            keep_dtb[n] = dtb_b
            st_f[...] = st_f[...] * t["g_f"] + _dot_tn((kf * t["ze_f"]).astype(BF16), v)
            dst_b[...] = dst_b[...] * t["g_b"] + _dot_tn((qf * t["xi_b"]).astype(BF16), do)
            return carry

        def bwd(i, carry):
            n = n_chunks - 1 - i
            r0, q, kf, v, do = load(n)
            qf = q.astype(F32)
            tb_b, dsf_b = st_b[...].astype(BF16), dst_f[...].astype(BF16)
            dq_inter = _dot_nt(do, tb_b) * t["xi_b"]
            dk_inter = _dot_nt(v, dsf_b) * t["ze_f"]
            acc_b[...] += (cf - ci) * (qf * dq_inter)
            acc_f[...] += (cf - 1.0 - ci) * (kf * dk_inter)
            acc_sb[...] += keep_dtb[n].astype(F32) * st_b[...]
            acc_sf[...] += dst_f[...] * keep_sf[n].astype(F32)
            dq_ref[pl.ds(r0, c), :] += dq_inter
            dk_ref[pl.ds(r0, c), :] = (dk_ref[pl.ds(r0, c), :] + dk_inter) * RET_SCALE
            dv_ref[pl.ds(r0, c), :] += _dot((kf * t["ze_f"]).astype(BF16), dsf_b)
            st_b[...] = st_b[...] * t["g_b"] + _dot_tn((kf * t["ze_b"]).astype(BF16), v)
            dst_f[...] = dst_f[...] * t["g_f"] + _dot_tn((qf * t["xi_f"]).astype(BF16), do)
            return carry

        lax.fori_loop(0, n_chunks, fwd, 0, unroll=min(RET_UNROLL_BWD, n_chunks))
        lax.fori_loop(0, n_chunks, bwd, 0, unroll=min(RET_UNROLL_BWD, n_chunks))

        def total(x):
            return jnp.sum(jnp.sum(x, axis=1, keepdims=True), axis=0, keepdims=True)

        glf_ref[...] = jnp.broadcast_to(total(acc_f[...]) + cf * t["g_f"] * total(acc_sf[...]), (8, 128))
        glb_ref[...] = jnp.broadcast_to(total(acc_b[...]) + cf * t["g_b"] * total(acc_sb[...]), (8, 128))

    smem, q, k, v, wide, narrow = _ret_specs(s_len)
    scal = pl.BlockSpec((None, 8, 128), lambda h: (h, 0, 0))
    state = pltpu.VMEM((RET_QK, RET_V), F32)
    square = pltpu.VMEM((RET_CHUNK, RET_QK), F32)
    keep = pltpu.VMEM((n_chunks, RET_QK, RET_V), BF16)
    return pl.pallas_call(
        body, name="ret_bwd", grid=(RET_HEADS,), in_specs=[smem, q, k, v, wide, wide],
        out_specs=[narrow, narrow, wide, scal, scal],
        out_shape=[jax.ShapeDtypeStruct((s_len, RET_HEADS * RET_QK), F32), jax.ShapeDtypeStruct((s_len, RET_HEADS * RET_QK), F32),
                   jax.ShapeDtypeStruct((s_len, RET_HEADS * RET_V), F32),
                   jax.ShapeDtypeStruct((RET_HEADS, 8, 128), F32), jax.ShapeDtypeStruct((RET_HEADS, 8, 128), F32)],
        scratch_shapes=[state, state, state, state, keep, keep, square, square, state, state],
        compiler_params=_params("arbitrary"),
    )(lg, proj, proj, proj, o_pre, dy)


def _silu_parts(z):
    sig = _sigmoid(z)
    return z * sig, sig * (1.0 + z * (1.0 - sig))


def outproj_fwd(x, gate, w_out, attn_outs, y_r, proj):
    s_len = x.shape[0]
    tm = min(256, s_len)

    def per_head(w):
        return jnp.concatenate([jnp.broadcast_to(w[:, h:h + 1], (tm, HEAD_DIM)) for h in range(N_HEADS_ATTN)], axis=1)

    def body(x_ref, gate_ref, w_ref, o1, l1, o2, l2, o3, l3, yr_ref, za_ref, zr_ref,
             xn_ref, oa_ref, lse_ref, lse4_ref, lse16_ref, so2, sl2, so3, sl3):
        silu_r, _ = _silu_parts(zr_ref[...].astype(F32))
        out_r = _dot((yr_ref[...] * silu_r).astype(BF16), w_ref[ATTN_WIDTH:, :])
        for src, dst in ((o2, so2), (l2, sl2), (o3, so3), (l3, sl3)):
            _merge_classes(src, dst)
        la, lb, lc = l1[...], _read_stage(sl2), _read_stage(sl3)
        m = jnp.maximum(jnp.maximum(la, lb), lc)
        lse = m + jnp.log(jnp.exp(la - m) + jnp.exp(lb - m) + jnp.exp(lc - m))
        o_a = (per_head(jnp.exp(la - lse)) * o1[...] + per_head(jnp.exp(lb - lse)) * _read_stage(so2)
               + per_head(jnp.exp(lc - lse)) * _read_stage(so3))
        oa_ref[...] = o_a
        lse_ref[...] = lse
        _fill_stage(sl2, lse)
        _split_classes(sl2, [lse4_ref, lse16_ref])
        silu_a, _ = _silu_parts(za_ref[...].astype(F32))
        out_a = _dot((o_a * silu_a).astype(BF16), w_ref[:ATTN_WIDTH, :])
        xn_ref[...] = x_ref[...] + gate_ref[...] * (out_a + out_r)

    row = lambda w: pl.BlockSpec((tm, w), lambda i: (i, 0))
    half, stat = row(ATTN_WIDTH), row(LANES)
    cls = [_class_block(dil, tm, ATTN_WIDTH, lambda i: (0, i, 0)) for dil in DILATIONS[1:]]
    cls_stat = [_class_block(dil, tm, LANES, lambda i: (0, i, 0)) for dil in DILATIONS[1:]]
    flat = [a for pair in attn_outs for a in pair]
    sds = jax.ShapeDtypeStruct
    return pl.pallas_call(
        body, name="outproj_fwd", grid=(s_len // tm,),
        in_specs=[row(D_MODEL), pl.BlockSpec((1, D_MODEL), lambda i: (0, 0)),
                  pl.BlockSpec((D_MODEL, D_MODEL), lambda i: (0, 0)),
                  half, stat, cls[0], cls_stat[0], cls[1], cls_stat[1], half,
                  pl.BlockSpec((tm, ATTN_WIDTH), lambda i: (i, COL_ZA // ATTN_WIDTH)),
                  pl.BlockSpec((tm, ATTN_WIDTH), lambda i: (i, COL_ZR // ATTN_WIDTH))],
        out_specs=[row(D_MODEL), half, stat] + cls_stat,
        out_shape=[sds((s_len, D_MODEL), F32), sds((s_len, ATTN_WIDTH), F32), sds((s_len, LANES), F32)]
                  + [sds((dil, s_len // dil, LANES), F32) for dil in DILATIONS[1:]],
        scratch_shapes=[_stage_shape(tm, ATTN_WIDTH), _stage_shape(tm, LANES)] * 2,
        compiler_params=_params("arbitrary"),
    )(x, gate, w_out, *flat, y_r, proj, proj)


def loss_head(x, gain, target):
    s_len = x.shape[0]
    tm = min(256, s_len)

    def body(x_ref, g_ref, t_ref, dx_ref, loss_ref, dg_ref):
        @pl.when(pl.program_id(0) == 0)
        def _():
            loss_ref[...] = jnp.zeros_like(loss_ref)
            dg_ref[...] = jnp.zeros_like(dg_ref)
        xv, g = x_ref[...], g_ref[...]
        r = lax.rsqrt(jnp.mean(xv * xv, axis=-1, keepdims=True) + NORM_EPS)
        xn = xv * r
        err = xn * g - t_ref[...]
        part = 0.5 * jnp.sum(jnp.mean(err * err, axis=-1, keepdims=True), axis=0, keepdims=True)
        loss_ref[...] += jnp.broadcast_to(part, loss_ref.shape)
        dy = err * (1.0 / D_MODEL)
        dg_ref[...] += jnp.sum(dy * xn, axis=0, keepdims=True)
        dxn = dy * g
        dx_ref[...] = r * (dxn - xn * jnp.mean(dxn * xn, axis=-1, keepdims=True))

    row = pl.BlockSpec((tm, D_MODEL), lambda i: (i, 0))
    vec = pl.BlockSpec((1, D_MODEL), lambda i: (0, 0))
    return pl.pallas_call(
        body, name="loss_head", grid=(s_len // tm,), in_specs=[row, vec, row],
        out_specs=[row, pl.BlockSpec((8, 128), lambda i: (0, 0)), vec],
        out_shape=[jax.ShapeDtypeStruct((s_len, D_MODEL), F32), jax.ShapeDtypeStruct((8, 128), F32),
                   jax.ShapeDtypeStruct((1, D_MODEL), F32)],
        compiler_params=_params("arbitrary"),
    )(x, gain, target)


def outproj_bwd(dxn, gate, w_out, o_a, y_r, proj):
    s_len = dxn.shape[0]
    tm = min(256, s_len)

    def body(dx_ref, gate_ref, w_ref, oa_ref, yr_ref, za_ref, zr_ref,
             doa_ref, dl_ref, dyr_ref, dza_ref, dzr_ref, y_ref, dxb_ref, do4, do16, dl4, dl16, stage, stat_stage):
        dxv = dx_ref[...]
        dxb_ref[...] = dxv.astype(BF16)
        dy = _dot_nt((dxv * gate_ref[...]).astype(BF16), w_ref[...])
        dy_a, dy_r = dy[:, :ATTN_WIDTH], dy[:, ATTN_WIDTH:]
        o_a, y_rv = oa_ref[...], yr_ref[...]
        silu_a, dsilu_a = _silu_parts(za_ref[...].astype(F32))
        silu_r, dsilu_r = _silu_parts(zr_ref[...].astype(F32))
        do_a = dy_a * silu_a
        doa_ref[...] = do_a.astype(BF16)
        _fill_stage(stage, do_a)
        _split_classes(stage, [do4, do16])
        prod = do_a * o_a
        lane = _iota((tm, LANES), 1)
        delta = jnp.zeros((tm, LANES), F32)
        for h in range(N_HEADS_ATTN):
            delta = jnp.where(lane == h, jnp.sum(prod[:, h * HEAD_DIM:(h + 1) * HEAD_DIM], axis=1, keepdims=True), delta)
        dl_ref[...] = delta
        _fill_stage(stat_stage, delta)
        _split_classes(stat_stage, [dl4, dl16])
        dyr_ref[...] = dy_r * silu_r
        dza_ref[...] = (dy_a * o_a * dsilu_a).astype(BF16)
        dzr_ref[...] = (dy_r * y_rv * dsilu_r).astype(BF16)
        y_ref[...] = jnp.concatenate([(o_a * silu_a).astype(BF16), (y_rv * silu_r).astype(BF16)], axis=1)

    row = lambda w: pl.BlockSpec((tm, w), lambda i: (i, 0))
    half = row(ATTN_WIDTH)
    sds = lambda w, dt: jax.ShapeDtypeStruct((s_len, w), dt)
    in_specs = [row(D_MODEL), pl.BlockSpec((1, D_MODEL), lambda i: (0, 0)),
                pl.BlockSpec((D_MODEL, D_MODEL), lambda i: (0, 0)), half, half,
                pl.BlockSpec((tm, ATTN_WIDTH), lambda i: (i, COL_ZA // ATTN_WIDTH)),
                pl.BlockSpec((tm, ATTN_WIDTH), lambda i: (i, COL_ZR // ATTN_WIDTH))]
    cls = [_class_block(dil, tm, ATTN_WIDTH, lambda i: (0, i, 0)) for dil in DILATIONS[1:]]
    cls_stat = [_class_block(dil, tm, LANES, lambda i: (0, i, 0)) for dil in DILATIONS[1:]]
    out_specs = [half, row(LANES), half, half, half, row(D_MODEL), row(D_MODEL)] + cls + cls_stat
    out_shape = [sds(ATTN_WIDTH, BF16), sds(LANES, F32), sds(ATTN_WIDTH, F32), sds(ATTN_WIDTH, BF16),
                 sds(ATTN_WIDTH, BF16), sds(D_MODEL, BF16), sds(D_MODEL, BF16)]
    out_shape += [jax.ShapeDtypeStruct((dil, s_len // dil, ATTN_WIDTH), BF16) for dil in DILATIONS[1:]]
    out_shape += [jax.ShapeDtypeStruct((dil, s_len // dil, LANES), F32) for dil in DILATIONS[1:]]
    return pl.pallas_call(
        body, name="outproj_bwd", grid=(s_len // tm,),
        in_specs=in_specs, out_specs=out_specs, out_shape=out_shape,
        scratch_shapes=[_stage_shape(tm, ATTN_WIDTH), _stage_shape(tm, LANES)],
        compiler_params=_params("arbitrary"),
    )(dxn, gate, w_out, o_a, y_r, proj, proj)


def wout_grad(y, dxb, gate, w_out):
    s_len = y.shape[0]
    tf, ts = 512, min(512, s_len)

    def body(y_ref, dx_ref, gate_ref, w_ref, dw_ref, dgate_ref, acc):
        f, s = pl.program_id(0), pl.program_id(1)

        @pl.when((f == 0) & (s == 0))
        def _():
            dgate_ref[...] = jnp.zeros_like(dgate_ref)

        @pl.when(s == 0)
        def _():
            acc[...] = jnp.zeros_like(acc)
        acc[...] += _dot_tn(y_ref[...], dx_ref[...])

        @pl.when(s == pl.num_programs(1) - 1)
        def _():
            m = acc[...]
            dw_ref[...] = (m * gate_ref[...]).astype(BF16).reshape(dw_ref.shape)
            dgate_ref[...] += jnp.sum(m * w_ref[...].astype(F32), axis=0, keepdims=True)

    per = tf // W_OUT_SHARD
    return pl.pallas_call(
        body, name="wout_grad", grid=(D_MODEL // tf, s_len // ts),
        in_specs=[pl.BlockSpec((ts, tf), lambda f, s: (s, f)), pl.BlockSpec((ts, D_MODEL), lambda f, s: (s, 0)),
                  pl.BlockSpec((1, D_MODEL), lambda f, s: (0, 0)), pl.BlockSpec((tf, D_MODEL), lambda f, s: (f, 0))],
        out_specs=[pl.BlockSpec((per, W_OUT_SHARD, D_MODEL), lambda f, s: (f, 0, 0)),
                   pl.BlockSpec((1, D_MODEL), lambda f, s: (0, 0))],
        out_shape=[jax.ShapeDtypeStruct((N_DEV, W_OUT_SHARD, D_MODEL), BF16), jax.ShapeDtypeStruct((1, D_MODEL), F32)],
        scratch_shapes=[pltpu.VMEM((tf, D_MODEL), F32)],
        compiler_params=_params("arbitrary", "arbitrary"),
    )(y, dxb, gate, w_out)


def assemble_dproj(dqkv_a, dz_a, dq_r, dk_r, dv_r, dz_r):
    s_len = dz_a.shape[0]
    tm = min(256, s_len)

    def body(*refs):
        pat, (dza, dqr, dkr, dvr, dzr, out, stage) = refs[:9], refs[9:]
        for t in range(3):
            tot = pat[t][...].astype(F32)
            for p in (1, 2):
                _merge_classes(pat[3 * p + t], stage)
                tot = tot + _read_stage(stage)
            out[:, t * ATTN_WIDTH:(t + 1) * ATTN_WIDTH] = tot.astype(BF16)
        out[:, COL_ZA:COL_QR] = dza[...]
        out[:, COL_QR:COL_KR] = dqr[...].astype(BF16)
        out[:, COL_KR:COL_VR] = dkr[...].astype(BF16)
        out[:, COL_VR:COL_ZR] = dvr[...].astype(BF16)
        out[:, COL_ZR:IN_W] = dzr[...]

    row = lambda w: pl.BlockSpec((tm, w), lambda i: (i, 0))
    cls = [_class_block(dil, tm, ATTN_WIDTH, lambda i: (0, i, 0)) for dil in DILATIONS[1:]]
    flat = [dqkv_a[p][t] for p in range(3) for t in range(3)]
    return pl.pallas_call(
        body, name="assemble_dproj", grid=(s_len // tm,),
        in_specs=[row(ATTN_WIDTH)] * 3 + [cls[0]] * 3 + [cls[1]] * 3
                 + [row(ATTN_WIDTH), row(512), row(512), row(ATTN_WIDTH), row(ATTN_WIDTH)],
        out_specs=row(IN_W), out_shape=jax.ShapeDtypeStruct((s_len, IN_W), BF16),
        scratch_shapes=[_stage_shape(tm, ATTN_WIDTH)],
        compiler_params=_params("arbitrary"),
    )(*flat, dz_a, dq_r, dk_r, dv_r, dz_r)


def inproj_bwd(dproj, w, x, g, scale1p, dxn, exchange):
    s_len = x.shape[0]
    tm, tk = min(512, s_len), 1024

    def body(dp_ref, w_ref, x_ref, g_ref, sc_ref, dxn_ref, pa_ref, pb_ref, dx_ref, st_ref, ra_ref, rb_ref,
             acc, send_sems, recv_sems):
        first, last = _grid_edge(2)
        copies = _reduce_phase2([pa_ref, pb_ref], [ra_ref, rb_ref], send_sems, recv_sems)
        pl.when(first)(lambda: _start_all(copies))
        i, k = pl.program_id(0), pl.program_id(1)

        @pl.when((i == 0) & (k == 0))
        def _():
            st_ref[...] = jnp.zeros_like(st_ref)

        @pl.when(k == 0)
        def _():
            acc[...] = jnp.zeros_like(acc)
        acc[...] += _dot_nt(dp_ref[...], w_ref[...])

        @pl.when(k == pl.num_programs(1) - 1)
        def _():
            dh, xv = acc[...], x_ref[...]
            r = lax.rsqrt(jnp.mean(xv * xv, axis=-1, keepdims=True) + NORM_EPS)
            xn = xv * r
            st_ref[0:1, :] += jnp.sum(dh, axis=0, keepdims=True)
            st_ref[3:4, :] += jnp.sum(dh * xn, axis=0, keepdims=True)
            dn = dh * (sc_ref[...] * g_ref[...])
            dx_ref[...] = r * (dn - xn * jnp.mean(dn * xn, axis=-1, keepdims=True)) + dxn_ref[...]

        @pl.when(last)
        def _():
            st_ref[1:2, :] = st_ref[3:4, :] * g_ref[...]
            st_ref[2:3, :] = st_ref[3:4, :] * sc_ref[...]
            _wait_all(copies)

    row = pl.BlockSpec((tm, D_MODEL), lambda i, k: (i, 0))
    vec = pl.BlockSpec((1, D_MODEL), lambda i, k: (0, 0))
    return pl.pallas_call(
        body, name="inproj_bwd_reduce2", grid=(s_len // tm, IN_W // tk),
        in_specs=[pl.BlockSpec((tm, tk), lambda i, k: (i, k)), pl.BlockSpec((D_MODEL, tk), lambda i, k: (0, k)),
                  row, vec, vec, row, HBM, HBM],
        out_specs=[row, pl.BlockSpec((8, D_MODEL), lambda i, k: (0, 0)), HBM, HBM],
        out_shape=[jax.ShapeDtypeStruct((s_len, D_MODEL), F32), jax.ShapeDtypeStruct((8, D_MODEL), F32)]
                  + _landing_shapes(3),
        scratch_shapes=[pltpu.VMEM((tm, D_MODEL), F32)] + REDUCE2_SEMS,
        compiler_params=_params("arbitrary", "arbitrary"),
    )(dproj, w, x, g, scale1p, dxn, *exchange)


def win_grad(h, dproj):
    s_len = h.shape[0]
    ts = min(1024, s_len)

    def body(h_ref, dp_ref, dw_ref, acc):
        s = pl.program_id(1)

        @pl.when(s == 0)
        def _():
            acc[...] = jnp.zeros_like(acc)
        acc[...] += _dot_tn(h_ref[...], dp_ref[...])

        @pl.when(s == pl.num_programs(1) - 1)
        def _():
            dw_ref[...] = acc[...].astype(BF16)

    return pl.pallas_call(
        body, name="win_grad", grid=(N_DEV, s_len // ts),
        in_specs=[pl.BlockSpec((ts, D_MODEL), lambda j, s: (s, 0)), pl.BlockSpec((ts, W_IN_SHARD), lambda j, s: (s, j))],
        out_specs=pl.BlockSpec((None, D_MODEL, W_IN_SHARD), lambda j, s: (j, 0, 0)),
        out_shape=jax.ShapeDtypeStruct((N_DEV, D_MODEL, W_IN_SHARD), BF16),
        scratch_shapes=[pltpu.VMEM((D_MODEL, W_IN_SHARD), F32)],
        compiler_params=_params("arbitrary", "arbitrary"),
    )(h, dproj)


def ada_fwd(c_all, w_ada):
    def body(c_ref, w_ref, act_ref, part_ref):
        cv = c_ref[...]
        act = cv * _sigmoid(cv)
        act_ref[...] = act
        part_ref[...] = _dot(act.astype(BF16), w_ref[...].astype(BF16))

    return pl.pallas_call(
        body, name="ada_fwd", grid=(DEPTH,),
        in_specs=[pl.BlockSpec((N_DEV, D_MODEL), lambda l: (0, 0)),
                  pl.BlockSpec((None, D_MODEL, W_ADA_SHARD), lambda l: (l, 0, 0))],
        out_specs=[pl.BlockSpec((N_DEV, D_MODEL), lambda l: (0, 0)),
                   pl.BlockSpec((None, N_DEV, W_ADA_SHARD), lambda l: (l, 0, 0))],
        out_shape=[jax.ShapeDtypeStruct((N_DEV, D_MODEL), F32), jax.ShapeDtypeStruct((DEPTH, N_DEV, W_ADA_SHARD), F32)],
        compiler_params=_params("arbitrary"),
    )(c_all, w_ada)


def _adamw(w, g, m, v):
    m = ADAM_B1 * m + (1.0 - ADAM_B1) * g
    v = ADAM_B2 * v + (1.0 - ADAM_B2) * (g * g)
    delta = -ADAM_LR * ((m * ADAM_C1) / (jnp.sqrt(v * ADAM_C2) + ADAM_EPS) + ADAM_WD * w)
    return delta, m, v


def ada_update(act_t, dmod, w, m, v):
    tr = 512

    def body(a_ref, d_ref, w_ref, m_ref, v_ref, g_out, dl_out, m_out, v_out):
        a = a_ref[...].astype(BF16).astype(F32)
        d = d_ref[...].astype(BF16).astype(F32)
        g = a[:, 0:1] * d[0:1, :]
        for b in range(1, N_DEV):
            g = g + a[:, b:b + 1] * d[b:b + 1, :]
        g_out[...] = g
        dl_out[...], m_out[...], v_out[...] = _adamw(w_ref[...], g, m_ref[...], v_ref[...])

    blk = pl.BlockSpec((None, tr, W_ADA_SHARD), lambda l, r: (l, r, 0))
    out = jax.ShapeDtypeStruct(w.shape, F32)
    return pl.pallas_call(
        body, name="ada_update", grid=(DEPTH, D_MODEL // tr),
        in_specs=[pl.BlockSpec((tr, N_DEV), lambda l, r: (r, 0)),
                  pl.BlockSpec((None, N_DEV, W_ADA_SHARD), lambda l, r: (l, 0, 0)), blk, blk, blk],
        out_specs=[blk] * 4, out_shape=[out] * 4, compiler_params=_params("arbitrary", "arbitrary"),
    )(act_t, dmod, w, m, v)


def chip_sum(pos, grads, landed, name):
    _, _, n_rows, n_cols = grads.shape
    tr = min(512, n_rows)

    def chip(k, pos_ref):
        return (pos_ref[0] ^ (k // 2)) * 2 + (pos_ref[1] ^ (k % 2))

    def body(pos_ref, g_ref, r_ref, out_ref):
        out_ref[...] = (g_ref[...].astype(F32) + r_ref[...].astype(F32)).astype(BF16)

    return pl.pallas_call(
        body, name=name,
        grid_spec=pltpu.PrefetchScalarGridSpec(
            num_scalar_prefetch=1, grid=(N_DEV // 2, n_rows // tr),
            in_specs=[pl.BlockSpec((None, None, tr, n_cols), lambda k, r, p: (chip(k, p), p[2], r, 0)),
                      pl.BlockSpec((None, tr, n_cols), lambda k, r, p: (chip(k, p), r, 0))],
            out_specs=pl.BlockSpec((None, tr, n_cols), lambda k, r, p: (k, r, 0))),
        out_shape=jax.ShapeDtypeStruct((N_DEV // 2, n_rows, n_cols), BF16),
        compiler_params=_params("arbitrary", "arbitrary"),
    )(pos, grads, landed)


def shard_update(layer, own, others, w, m, v, prev, name):
    _, n_rows, n_cols = w.shape
    tr = min(256, n_rows)

    def body(own_ref, oth_ref, w_ref, m_ref, v_ref, *rest):
        g_out, dl_out, m_out, v_out = rest[-4:]
        g = own_ref[...].astype(F32)
        for k in range(3):
            g = g + oth_ref[k].astype(F32)
        g_out[...] = g
        dl_out[...], m_out[...], v_out[...] = _adamw(w_ref[...], g, m_ref[...], v_ref[...])

    blk = pl.BlockSpec((None, tr, n_cols), lambda r: (layer, r, 0))
    out = jax.ShapeDtypeStruct(w.shape, F32)
    in_specs = [pl.BlockSpec((None, tr, n_cols), lambda r: (0, r, 0)), pl.BlockSpec((3, tr, n_cols), lambda r: (0, r, 0)),
                blk, blk, blk]
    aliases, extra = {}, ()
    if prev is not None:
        in_specs += [HBM] * 4
        aliases = {5 + t: t for t in range(4)}
        extra = tuple(prev)
    return pl.pallas_call(
        body, name=name, grid=(n_rows // tr,), in_specs=in_specs, out_specs=[blk] * 4, out_shape=[out] * 4,
        input_output_aliases=aliases, compiler_params=_params("arbitrary"),
    )(own, others, w, m, v, *extra)


def small_update(parts, w, m, v):
    def body(p_ref, w_ref, m_ref, v_ref, g_out, dl_out, m_out, v_out):
        g = p_ref[0]
        for k in range(1, N_DEV):
            g = g + p_ref[k]
        g_out[...] = g
        dl_out[...], m_out[...], v_out[...] = _adamw(w_ref[...], g, m_ref[...], v_ref[...])

    out = jax.ShapeDtypeStruct(w.shape, F32)
    return pl.pallas_call(body, name="small_update", out_shape=[out] * 4, compiler_params=_params())(parts, w, m, v)


def _two_level_allgather(srcs, dst_block, send_sems, recv_sems, local_sems):
    x, y, c = _position()
    me, sibling = (x, y, c), (x, y, 1 - c)
    chips = [(1 - x, y), (x, 1 - y), (1 - x, 1 - y)]
    n = len(srcs)

    def copy(a, k, block, to, src=None):
        dst = dst_block(a, _flat(*block))
        return pltpu.make_async_remote_copy(
            src_ref=dst if src is None else src, dst_ref=dst, send_sem=send_sems.at[a * 7 + k],
            recv_sem=recv_sems.at[a * 7 + k], device_id=to, device_id_type=MESH)

    mine = [pltpu.make_async_copy(srcs[a], dst_block(a, _flat(*me)), local_sems.at[a]) for a in range(n)]
    for cp in mine:
        cp.start()
    first = []
    for a in range(n):
        first.append(copy(a, 0, me, sibling, src=srcs[a]))
        first += [copy(a, 1 + j, me, (*chip, c), src=srcs[a]) for j, chip in enumerate(chips)]
    for cp in first:
        cp.start()
    passed = []
    for j, chip in enumerate(chips):
        for a in range(n):
            copy(a, 1 + j, (*chip, c), me).wait_recv()
            fwd = copy(a, 4 + j, (*chip, c), sibling)
            fwd.start()
            passed.append(fwd)
    for a in range(n):
        copy(a, 0, sibling, me).wait_recv()
        for j, chip in enumerate(chips):
            copy(a, 4 + j, (*chip, 1 - c), me).wait_recv()
    for cp in first + passed:
        cp.wait_send()
    for cp in mine:
        cp.wait()


def allgather_rows(x, name):
    def body(x_ref, out_ref, send_sems, recv_sems, local_sems):
        _two_level_allgather([x_ref], lambda a, idx: out_ref.at[idx], send_sems, recv_sems, local_sems)

    vmem = pl.BlockSpec(memory_space=pltpu.VMEM)
    return pl.pallas_call(
        body, name=name, in_specs=[vmem], out_specs=vmem,
        out_shape=jax.ShapeDtypeStruct((N_DEV,) + x.shape, x.dtype),
        scratch_shapes=[pltpu.SemaphoreType.DMA((7,)), pltpu.SemaphoreType.DMA((7,)), pltpu.SemaphoreType.DMA((1,))],
        compiler_params=_params(),
    )(x)


def _full_weight_shapes():
    return [jax.ShapeDtypeStruct((D_MODEL, IN_W), BF16), jax.ShapeDtypeStruct((D_MODEL, D_MODEL), BF16)]


def allgather_weights(w_in_b, w_out_b):
    def body(win_ref, wout_ref, fin_ref, fout_ref, s1, r1, l1, s2, r2):
        block = _weight_blocks(fin_ref, fout_ref)
        first = _gather_phase1([win_ref.at[0], wout_ref.at[0]], block, s1, r1, l1)
        _start_all(first)
        _wait_all(first)
        second = _gather_phase2(block, s2, r2)
        _start_all(second)
        _wait_all(second)

    return pl.pallas_call(
        body, name="allgather_weights", in_specs=[HBM, HBM], out_specs=[HBM, HBM], out_shape=_full_weight_shapes(),
        scratch_shapes=GATHER1_SEMS + GATHER2_SEMS, compiler_params=_params(),
    )(w_in_b, w_out_b)


def reduce_first(dw_in, dw_out, name):
    def body(ga_ref, gb_ref, ra_ref, rb_ref, send_sems, recv_sems):
        copies = _reduce_phase1([ga_ref, gb_ref], [ra_ref, rb_ref], send_sems, recv_sems)
        _start_all(copies)
        _wait_all(copies)

    return pl.pallas_call(
        body, name=name, in_specs=[HBM, HBM], out_specs=[HBM, HBM], out_shape=_landing_shapes(4),
        scratch_shapes=REDUCE1_SEMS, compiler_params=_params(),
    )(*_split_cores((dw_in, dw_out)))


def _one_class(a):
    return a.reshape((1,) + a.shape)


def layer_fwd(x, g, scale, shift, gate, w_in, w_out, lg, gather=None):
    proj, h, qkv4, qkv16, *began = inproj_fwd(x, g, 1.0 + scale, shift, w_in, gather)
    qkv = (_one_class(proj), qkv4, qkv16)
    attn_outs = [attn_fwd(arr, dil) for dil, arr in zip(DILATIONS, qkv)]
    attn_outs[0] = tuple(a[0] for a in attn_outs[0])
    o_pre, y_r, *next_weights = ret_fwd(proj, lg, began if gather is not None else None)
    next_weights = tuple(next_weights) if gather is not None else None
    x_new, o_a, lse, lse4, lse16 = outproj_fwd(x, gate, w_out, attn_outs, y_r, proj)
    saved = dict(x=x, proj=proj, h=h, qkv=qkv, o_a=o_a, lse=(_one_class(lse), lse4, lse16), o_pre=o_pre, y_r=y_r)
    return x_new, saved, next_weights


def layer_bwd(dxn, saved, g, scale, gate, w_in, w_out, lg, pos):
    proj = saved["proj"]
    do_a, delta, dyr, dz_a, dz_r, y, dxb, do4, do16, dl4, dl16 = outproj_bwd(
        dxn, gate, w_out, saved["o_a"], saved["y_r"], proj)
    dw_out, dgate = wout_grad(y, dxb, gate, w_out)
    dq_r, dk_r, dv_r, glf, glb = ret_bwd(proj, lg, saved["o_pre"], dyr)
    dos, deltas = (_one_class(do_a), do4, do16), (_one_class(delta), dl4, dl16)
    dqkv_a = [attn_bwd(arr, d_o, lse, dl, dil)
              for dil, arr, d_o, lse, dl in zip(DILATIONS, saved["qkv"], dos, saved["lse"], deltas)]
    dqkv_a[0] = [t[0] for t in dqkv_a[0]]
    dproj = assemble_dproj(dqkv_a, dz_a, dq_r, dk_r, dv_r, dz_r)
    dw_in = win_grad(saved["h"], dproj)
    landed = reduce_first(dw_in, dw_out, "reduce_first")
    sums = [chip_sum(pos, g4, r, "chip_sum") for g4, r in zip(_split_cores((dw_in, dw_out)), landed)]
    dx, stats, *others = inproj_bwd(dproj, w_in, saved["x"], g, 1.0 + scale, dxn, sums)
    dlg = jnp.concatenate([glf[:, 0, 0], glb[:, 0, 0]])
    return dx, stats[0:1], stats[1:2], dgate, stats[2:3], dlg, (sums, others)


ROWS_B_ADA = DEPTH * 3 * D_MODEL // 128
ROWS_GAIN = DEPTH * D_MODEL // 128
ROWS_FINAL = D_MODEL // 128
ROWS_MISC = 8
ROWS_SMALL = ROWS_B_ADA + ROWS_GAIN + ROWS_FINAL + ROWS_MISC


def _pack_small(b_ada_like, gain_like, final_like, dec_f, dec_b, loss=None):
    misc = jnp.zeros((ROWS_MISC, 128), F32)
    misc = misc.at[0, :2 * DEPTH * RET_HEADS].set(jnp.concatenate([dec_f.reshape(-1), dec_b.reshape(-1)]))
    if loss is not None:
        misc = misc.at[1, 0].set(loss)
    return jnp.concatenate([b_ada_like.reshape(ROWS_B_ADA, 128), gain_like.reshape(ROWS_GAIN, 128),
                            final_like.reshape(ROWS_FINAL, 128), misc], axis=0)


def _unpack_small(p):
    r0, r1, r2 = ROWS_B_ADA, ROWS_B_ADA + ROWS_GAIN, ROWS_B_ADA + ROWS_GAIN + ROWS_FINAL
    n = DEPTH * RET_HEADS
    return (p[:r0].reshape(DEPTH, 3 * D_MODEL), p[r0:r1].reshape(DEPTH, D_MODEL), p[r1:r2].reshape(D_MODEL),
            p[r2, :n].reshape(DEPTH, RET_HEADS), p[r2, n:2 * n].reshape(DEPTH, RET_HEADS))


def kernel(x, c, norm_gain, w_ada, b_ada, w_in, w_out, ret_decay_logit_f, ret_decay_logit_b, final_gain, loss_target, m_norm_gain, m_w_ada, m_b_ada, m_w_in, m_w_out, m_ret_decay_logit_f, m_ret_decay_logit_b, m_final_gain, v_norm_gain, v_w_ada, v_b_ada, v_w_in, v_w_out, v_ret_decay_logit_f, v_ret_decay_logit_b, v_final_gain):
    px, py, pc = _position()
    me = _flat(px, py, pc)
    pos = jnp.stack([px, py, pc]).astype(jnp.int32)
    x2, target = x[0], loss_target[0]

    w_in_b, w_out_b = w_in.astype(BF16), w_out.astype(BF16)
    weights = allgather_weights(w_in_b, w_out_b)

    c_all = allgather_rows(c.reshape(D_MODEL // 128, 128), "allgather_c").reshape(N_DEV, D_MODEL)
    act, mod_part = ada_fwd(c_all, w_ada)
    mod_all = allgather_rows(mod_part.reshape(-1, 128), "allgather_mod").reshape(N_DEV, DEPTH, N_DEV, W_ADA_SHARD)
    mod = lax.dynamic_index_in_dim(mod_all, me, axis=2, keepdims=False)
    mod = mod.transpose(1, 0, 2).reshape(DEPTH, 3 * D_MODEL) + b_ada
    shift, scale, gate = mod[:, :D_MODEL], mod[:, D_MODEL:2 * D_MODEL], mod[:, 2 * D_MODEL:]

    lg = jnp.concatenate([jax.nn.log_sigmoid(ret_decay_logit_f), jax.nn.log_sigmoid(ret_decay_logit_b)], axis=1)

    h = x2
    saved, layer_weights = [], []
    for l in range(DEPTH):
        layer_weights.append(weights)
        gather = (w_in_b, w_out_b, l + 1) if l + 1 < DEPTH else None
        h, sv, weights = layer_fwd(h, norm_gain[l:l + 1], scale[l:l + 1], shift[l:l + 1], gate[l:l + 1],
                                   *layer_weights[l], lg[l], gather)
        saved.append(sv)
    dh, loss_part, dfinal = loss_head(h, final_gain.reshape(1, D_MODEL), target)

    dmod, dgain, dlg, reduced = [None] * DEPTH, [None] * DEPTH, [None] * DEPTH, [None] * DEPTH
    for l in reversed(range(DEPTH)):
        dh, dshift, dscale, dgate, dg, dlg[l], reduced[l] = layer_bwd(
            dh, saved[l], norm_gain[l:l + 1], scale[l:l + 1], gate[l:l + 1], *layer_weights[l], lg[l], pos)
        dmod[l] = jnp.concatenate([dshift, dscale, dgate], axis=1)
        dgain[l] = dg

    dlg = jnp.stack(dlg)
    dlogit_f = dlg[:, :RET_HEADS] * jax.nn.sigmoid(-ret_decay_logit_f)
    dlogit_b = dlg[:, RET_HEADS:] * jax.nn.sigmoid(-ret_decay_logit_b)
    packed = _pack_small(jnp.concatenate(dmod, axis=0), jnp.concatenate(dgain, axis=0), dfinal, dlogit_f, dlogit_b,
                         loss=loss_part[0, 0])
    gathered = allgather_rows(packed, "allgather_small")
    small = small_update(gathered,
                         _pack_small(b_ada, norm_gain, final_gain, ret_decay_logit_f, ret_decay_logit_b),
                         _pack_small(m_b_ada, m_norm_gain, m_final_gain, m_ret_decay_logit_f, m_ret_decay_logit_b),
                         _pack_small(v_b_ada, v_norm_gain, v_final_gain, v_ret_decay_logit_f, v_ret_decay_logit_b))
    loss = small[0][ROWS_B_ADA + ROWS_GAIN + ROWS_FINAL + 1, 0]
    (g_b_ada, g_gain, g_final, g_dec_f, g_dec_b), (d_b_ada, d_gain, d_final, d_dec_f, d_dec_b), \
        (m_b_ada2, m_gain2, m_final2, m_dec_f2, m_dec_b2), (v_b_ada2, v_gain2, v_final2, v_dec_f2, v_dec_b2) = \
        [_unpack_small(p) for p in small]

    dmod_all = gathered[:, :ROWS_B_ADA].reshape(N_DEV, DEPTH, 3 * D_MODEL)
    dmod_mine = lax.dynamic_slice_in_dim(dmod_all, me * W_ADA_SHARD, W_ADA_SHARD, axis=2).transpose(1, 0, 2)
    g_w_ada, d_w_ada, m_w_ada2, v_w_ada2 = ada_update(act.T, dmod_mine, w_ada, m_w_ada, v_w_ada)

    upd_in = upd_out = None
    for l in reversed(range(DEPTH)):
        upd_in = shard_update(l, reduced[l][0][0], reduced[l][1][0], w_in, m_w_in, v_w_in, upd_in, f"w_in_update_{l}")
        upd_out = shard_update(l, reduced[l][0][1], reduced[l][1][1], w_out, m_w_out, v_w_out, upd_out, f"w_out_update_{l}")
    g_w_in, d_w_in, m_w_in2, v_w_in2 = upd_in
    g_w_out, d_w_out, m_w_out2, v_w_out2 = upd_out

    return (loss, dh[None],
            g_gain, g_w_ada, g_b_ada, g_w_in, g_w_out, g_dec_f, g_dec_b, g_final,
            d_gain, d_w_ada, d_b_ada, d_w_in, d_w_out, d_dec_f, d_dec_b, d_final,
            m_gain2, m_w_ada2, m_b_ada2, m_w_in2, m_w_out2, m_dec_f2, m_dec_b2, m_final2,
            v_gain2, v_w_ada2, v_b_ada2, v_w_in2, v_w_out2, v_dec_f2, v_dec_b2, v_final2)
```

```python
import functools
import math

import jax
import jax.numpy as jnp
from jax import lax
from jax.experimental import pallas as pl
from jax.experimental.pallas import tpu as pltpu

F32, BF16 = jnp.float32, jnp.bfloat16

D_MODEL = 2048
DEPTH = 4
N_DEV = 8
ATTN_WIDTH = 1024
HEAD_DIM = 128
N_HEADS_ATTN = 8
DILATIONS = (1, 4, 16)
RADIUS = 64
RET_HEADS = 4
RET_QK = 128
RET_V = 256
RET_CHUNK = 128
IN_W = 7168
QKV_A = 3 * ATTN_WIDTH
COL_ZA, COL_QR, COL_KR, COL_VR, COL_ZR = 3072, 4096, 4608, 5120, 6144
W_IN_SHARD = IN_W // N_DEV
W_OUT_SHARD = D_MODEL // N_DEV
W_ADA_SHARD = 3 * D_MODEL // N_DEV
NORM_EPS = 1e-6
MASK_VALUE = -1e30
ATTN_SCALE = HEAD_DIM ** -0.5
RET_SCALE = RET_QK ** -0.5
LN2 = math.log(2.0)

ADAM_LR, ADAM_B1, ADAM_B2, ADAM_EPS, ADAM_WD, ADAM_STEP = 0.001, 0.9, 0.999, 1e-08, 0.01, 10
ADAM_C1 = 1.0 / (1.0 - ADAM_B1 ** ADAM_STEP)
ADAM_C2 = 1.0 / (1.0 - ADAM_B2 ** ADAM_STEP)

VMEM_LIMIT_BYTES = 56 * 1024 * 1024
MESH = pl.DeviceIdType.MESH


def _params(*sem):
    return pltpu.CompilerParams(dimension_semantics=sem if sem else None, vmem_limit_bytes=VMEM_LIMIT_BYTES)


def _dot(a, b):
    return jnp.dot(a, b, preferred_element_type=F32)


def _dot_nt(a, b):
    return lax.dot_general(a, b, (((1,), (1,)), ((), ())), preferred_element_type=F32)


def _dot_tn(a, b):
    return lax.dot_general(a, b, (((0,), (0,)), ((), ())), preferred_element_type=F32)


def _iota(shape, dim):
    return lax.broadcasted_iota(jnp.int32, shape, dim)


def _sigmoid(z):
    return 1.0 / (1.0 + jnp.exp(-z))


HBM = pl.BlockSpec(memory_space=pl.ANY)


def _position():
    return lax.axis_index("x"), lax.axis_index("y"), lax.axis_index("c")


def _flat(px, py, pc):
    return 4 * px + 2 * py + pc


def _remote(src, dst, send_sem, recv_sem, to):
    return pltpu.make_async_remote_copy(src_ref=src, dst_ref=dst, send_sem=send_sem, recv_sem=recv_sem,
                                        device_id=to, device_id_type=MESH)


def _weight_blocks(fin_ref, fout_ref):
    def block(a, idx):
        if a == 0:
            return fin_ref.at[:, pl.ds(pl.multiple_of(idx * W_IN_SHARD, 128), W_IN_SHARD)]
        return fout_ref.at[pl.ds(pl.multiple_of(idx * W_OUT_SHARD, W_OUT_SHARD), W_OUT_SHARD), :]
    return block


GATHER1_SEMS = [pltpu.SemaphoreType.DMA((8,)), pltpu.SemaphoreType.DMA((8,)), pltpu.SemaphoreType.DMA((2,))]
GATHER2_SEMS = [pltpu.SemaphoreType.DMA((6,)), pltpu.SemaphoreType.DMA((6,))]
REDUCE1_SEMS = [pltpu.SemaphoreType.DMA((2,)), pltpu.SemaphoreType.DMA((2,))]
REDUCE2_SEMS = [pltpu.SemaphoreType.DMA((6,)), pltpu.SemaphoreType.DMA((6,))]


def _gather_phase1(srcs, block, send_sems, recv_sems, local_sems):
    x, y, c = _position()
    mine = [block(a, _flat(x, y, c)) for a in range(2)]
    copies = [pltpu.make_async_copy(srcs[a], mine[a], local_sems.at[a]) for a in range(2)]
    for a in range(2):
        copies.append(_remote(srcs[a], mine[a], send_sems.at[4 * a], recv_sems.at[4 * a], (x, y, 1 - c)))
        for j, (px, py) in enumerate([(1 - x, y), (x, 1 - y), (1 - x, 1 - y)]):
            copies.append(_remote(srcs[a], mine[a], send_sems.at[4 * a + 1 + j], recv_sems.at[4 * a + 1 + j], (px, py, c)))
    return copies


def _gather_phase2(block, send_sems, recv_sems):
    x, y, c = _position()
    copies = []
    for a in range(2):
        for j, (px, py) in enumerate([(1 - x, y), (x, 1 - y), (1 - x, 1 - y)]):
            blk = block(a, _flat(px, py, c))
            copies.append(_remote(blk, blk, send_sems.at[3 * a + j], recv_sems.at[3 * a + j], (x, y, 1 - c)))
    return copies


def _reduce_phase1(grads, landings, send_sems, recv_sems):
    x, y, c = _position()
    return [_remote(g.at[:, 1 - c], r, send_sems.at[a], recv_sems.at[a], (x, y, 1 - c))
            for a, (g, r) in enumerate(zip(grads, landings))]


def _reduce_phase2(sums, landings, send_sems, recv_sems):
    x, y, c = _position()
    copies = []
    for a, (p, r) in enumerate(zip(sums, landings)):
        for k in (1, 2, 3):
            to = (1 - x if k & 2 else x, 1 - y if k & 1 else y, c)
            copies.append(_remote(p.at[k], r.at[k - 1], send_sems.at[3 * a + k - 1], recv_sems.at[3 * a + k - 1], to))
    return copies


def _landing_shapes(n):
    return [jax.ShapeDtypeStruct((n, D_MODEL, W_IN_SHARD), BF16), jax.ShapeDtypeStruct((n, W_OUT_SHARD, D_MODEL), BF16)]


def _split_cores(slabs):
    return tuple(s.reshape((N_DEV // 2, 2) + s.shape[1:]) for s in slabs)


def _start_all(copies):
    for cp in copies:
        cp.start()


def _wait_all(copies):
    for cp in copies:
        cp.wait()


def _grid_edge(n_axes):
    first = last = None
    for ax in range(n_axes):
        f = pl.program_id(ax) == 0
        e = pl.program_id(ax) == pl.num_programs(ax) - 1
        first = f if first is None else first & f
        last = e if last is None else last & e
    return first, last


LANES = 128


def _stage_shape(rows, width):
    return pltpu.VMEM((width // LANES, rows, LANES), F32)


def _fill_stage(stage_ref, value):
    for t in range(stage_ref.shape[0]):
        stage_ref[t] = value[:, t * LANES:(t + 1) * LANES]


def _read_stage(stage_ref):
    return jnp.concatenate([stage_ref[t] for t in range(stage_ref.shape[0])], axis=1)


def _split_classes(stage_ref, out_refs):
    n_t, rows, _ = stage_ref.shape
    for out_ref in out_refs:
        dil = out_ref.shape[0]
        for r in range(dil):
            for t in range(n_t):
                piece = stage_ref[t, pl.ds(r, rows // dil, stride=dil), :]
                out_ref[r, :, t * LANES:(t + 1) * LANES] = piece.astype(out_ref.dtype)


def _merge_classes(in_ref, stage_ref):
    dil, per = in_ref.shape[0], in_ref.shape[1]
    for r in range(dil):
        for t in range(stage_ref.shape[0]):
            stage_ref[t, pl.ds(r, per, stride=dil), :] = in_ref[r, :, t * LANES:(t + 1) * LANES].astype(F32)


def _class_block(dil, tm, width, index_map):
    return pl.BlockSpec((dil, tm // dil, width), index_map)


def inproj_fwd(x, g, scale1p, shift, w, gather=None):
    s_len = x.shape[0]
    tm, tn = min(1024, s_len), 512

    n_qkv = QKV_A // tn

    def body(x_ref, g_ref, sc_ref, sh_ref, w_ref, *rest):
        if gather is not None:
            (win_ref, wout_ref, proj_ref, h_ref, q4_ref, q16_ref, fin_ref, fout_ref, stage,
             send_sems, recv_sems, local_sems) = rest
            first, last = _grid_edge(2)
            copies = _gather_phase1([win_ref.at[gather[2]], wout_ref.at[gather[2]]], _weight_blocks(fin_ref, fout_ref),
                                    send_sems, recv_sems, local_sems)
            pl.when(first)(lambda: _start_all(copies))
        else:
            proj_ref, h_ref, q4_ref, q16_ref, stage = rest

        @pl.when(pl.program_id(1) == 0)
        def _():
            xv = x_ref[...]
            r = lax.rsqrt(jnp.mean(xv * xv, axis=-1, keepdims=True) + NORM_EPS)
            h_ref[...] = ((xv * r * g_ref[...]) * sc_ref[...] + sh_ref[...]).astype(BF16)
        res = _dot(h_ref[...], w_ref[...])
        proj_ref[...] = res.astype(BF16)

        @pl.when(pl.program_id(1) < n_qkv)
        def _():
            _fill_stage(stage, res)
            _split_classes(stage, [q4_ref, q16_ref])
        if gather is not None:
            pl.when(last)(lambda: _wait_all(copies))

    vec = pl.BlockSpec((1, D_MODEL), lambda i, j: (0, 0))
    in_specs = [pl.BlockSpec((tm, D_MODEL), lambda i, j: (i, 0)), vec, vec, vec,
                pl.BlockSpec((D_MODEL, tn), lambda i, j: (0, j))]
    out_specs = [pl.BlockSpec((tm, tn), lambda i, j: (i, j)), pl.BlockSpec((tm, D_MODEL), lambda i, j: (i, 0))]
    out_shape = [jax.ShapeDtypeStruct((s_len, IN_W), BF16), jax.ShapeDtypeStruct((s_len, D_MODEL), BF16)]
    for dil in DILATIONS[1:]:
        out_specs.append(pl.BlockSpec((dil, tm // dil, tn), lambda i, j: (0, i, jnp.minimum(j, n_qkv - 1))))
        out_shape.append(jax.ShapeDtypeStruct((dil, s_len // dil, QKV_A), BF16))
    scratch = [_stage_shape(tm, tn)]
    extra = ()
    if gather is not None:
        in_specs += [HBM, HBM]
        out_specs += [HBM, HBM]
        out_shape += [jax.ShapeDtypeStruct((D_MODEL, IN_W), BF16), jax.ShapeDtypeStruct((D_MODEL, D_MODEL), BF16)]
        scratch += GATHER1_SEMS
        extra = tuple(gather[:2])
    return pl.pallas_call(
        body, name="inproj_fwd_gather" if gather is not None else "inproj_fwd", grid=(s_len // tm, IN_W // tn),
        in_specs=in_specs, out_specs=out_specs, out_shape=out_shape, scratch_shapes=scratch,
        compiler_params=_params("arbitrary", "arbitrary"),
    )(x, g, scale1p, shift, w, *extra)


MASK_DISTANCE = 1e33
ATTN_TILE = 128


ATTN_UNROLL_FWD, ATTN_UNROLL_BWD = 15, 10


def _attn_plan(sub_len, unroll):
    tq = min(sub_len, ATTN_TILE)
    win = min(sub_len, tq + 2 * RADIUS)
    heads = 1 if sub_len > 1024 else (2 if sub_len > 256 else N_HEADS_ATTN)
    return tq, win, sub_len // tq, heads, unroll


def _attn_tiles(sub_len, tq, win, n_tiles, unroll, tile):
    tile(0, 0, 0)
    if n_tiles > 2:
        def mid(i, carry):
            q0 = pl.multiple_of(i * tq, tq)
            tile(q0, pl.multiple_of(q0 - RADIUS, RADIUS), 1)
            return carry
        lax.fori_loop(1, n_tiles - 1, mid, 0, unroll=min(unroll, n_tiles - 2))
    if n_tiles > 1:
        tile(sub_len - tq, sub_len - win, 2)


def _attn_bias(bias_ref, head, heads, tq, win, dil):
    h = pl.program_id(1) * heads + head
    slope = jnp.exp(-(h + 1).astype(F32) * LN2 * jnp.ones((1, 1), F32))
    rel = _iota((tq, win), 1) - _iota((tq, win), 0)
    for v, off in enumerate((0, RADIUS, win - tq)):
        dist = jnp.abs(rel - off)
        bias_ref[v] = slope * jnp.where(dist <= RADIUS, (dist * dil).astype(F32), MASK_DISTANCE)


def _attn_specs(n_cls, sub_len, heads):
    width = heads * HEAD_DIM
    per = ATTN_WIDTH // width

    def col(part):
        return pl.BlockSpec((None, sub_len, width), lambda r, g: (r, 0, part * per + g))
    return col, (n_cls, N_HEADS_ATTN // heads)


def _stat_spec(sub_len):
    return pl.BlockSpec((None, sub_len, LANES), lambda r, g: (r, 0, 0))


def _stat_column(ref, rows, lane, h_abs):
    return jnp.sum(jnp.where(lane == h_abs, ref[rows, :], 0.0), axis=1, keepdims=True)


def attn_fwd(qkv, dil):
    n_cls, sub_len, _ = qkv.shape
    tq, win, n_tiles, heads, unroll = _attn_plan(sub_len, ATTN_UNROLL_FWD)

    def body(q_ref, k_ref, v_ref, o_ref, lse_ref, bias_ref):
        @pl.when(pl.program_id(1) == 0)
        def _():
            lse_ref[...] = jnp.zeros_like(lse_ref)
        lane = _iota((tq, LANES), 1)
        for head in range(heads):
            lanes = slice(head * HEAD_DIM, (head + 1) * HEAD_DIM)
            h_abs = pl.program_id(1) * heads + head
            _attn_bias(bias_ref, head, heads, tq, win, dil)

            def tile(q0, start, variant):
                s = _dot_nt(q_ref[pl.ds(q0, tq), lanes], k_ref[pl.ds(start, win), lanes]) * ATTN_SCALE - bias_ref[variant]
                m = jnp.max(s, axis=1, keepdims=True)
                p = jnp.exp(s - m)
                den = jnp.sum(p, axis=1, keepdims=True)
                o_ref[pl.ds(q0, tq), lanes] = _dot(p.astype(BF16), v_ref[pl.ds(start, win), lanes]) / den
                rows = pl.ds(q0, tq)
                lse_ref[rows, :] = jnp.where(lane == h_abs, m + jnp.log(den), lse_ref[rows, :])

            _attn_tiles(sub_len, tq, win, n_tiles, unroll, tile)

    col, grid = _attn_specs(n_cls, sub_len, heads)
    return pl.pallas_call(
        body, name=f"attn_fwd_d{dil}", grid=grid,
        in_specs=[col(0), col(1), col(2)], out_specs=[col(0), _stat_spec(sub_len)],
        out_shape=[jax.ShapeDtypeStruct((n_cls, sub_len, ATTN_WIDTH), F32), jax.ShapeDtypeStruct((n_cls, sub_len, LANES), F32)],
        scratch_shapes=[pltpu.VMEM((3, tq, win), F32)],
        compiler_params=_params("arbitrary", "arbitrary"),
    )(qkv, qkv, qkv)


def attn_bwd(qkv, do, lse, delta, dil):
    n_cls, sub_len, _ = qkv.shape
    tq, win, n_tiles, heads, unroll = _attn_plan(sub_len, ATTN_UNROLL_BWD)

    def body(q_ref, k_ref, v_ref, do_ref, lse_ref, dl_ref, dq_ref, dk_ref, dv_ref, bias_ref, dk_acc, dv_acc):
        lane = _iota((tq, LANES), 1)
        for head in range(heads):
            lanes = slice(head * HEAD_DIM, (head + 1) * HEAD_DIM)
            h_abs = pl.program_id(1) * heads + head
            _attn_bias(bias_ref, head, heads, tq, win, dil)
            dk_acc[...] = jnp.zeros_like(dk_acc)
            dv_acc[...] = jnp.zeros_like(dv_acc)

            def tile(q0, start, variant):
                q = q_ref[pl.ds(q0, tq), lanes]
                k = k_ref[pl.ds(start, win), lanes]
                v = v_ref[pl.ds(start, win), lanes]
                dov = do_ref[pl.ds(q0, tq), lanes]
                s = _dot_nt(q, k) * ATTN_SCALE - bias_ref[variant]
                rows = pl.ds(q0, tq)
                p = jnp.exp(s - _stat_column(lse_ref, rows, lane, h_abs))
                ds = (p * (_dot_nt(dov, v) - _stat_column(dl_ref, rows, lane, h_abs))).astype(BF16)
                dq_ref[pl.ds(q0, tq), lanes] = (_dot(ds, k) * ATTN_SCALE).astype(BF16)
                dk_acc[pl.ds(start, win), :] += _dot_tn(ds, q) * ATTN_SCALE
                dv_acc[pl.ds(start, win), :] += _dot_tn(p.astype(BF16), dov)

            _attn_tiles(sub_len, tq, win, n_tiles, unroll, tile)
            dk_ref[:, lanes] = dk_acc[...].astype(BF16)
            dv_ref[:, lanes] = dv_acc[...].astype(BF16)

    col, grid = _attn_specs(n_cls, sub_len, heads)
    out = jax.ShapeDtypeStruct((n_cls, sub_len, ATTN_WIDTH), BF16)
    return pl.pallas_call(
        body, name=f"attn_bwd_d{dil}", grid=grid,
        in_specs=[col(0), col(1), col(2), col(0), _stat_spec(sub_len), _stat_spec(sub_len)],
        out_specs=[col(0), col(0), col(0)], out_shape=[out, out, out],
        scratch_shapes=[pltpu.VMEM((3, tq, win), F32), pltpu.VMEM((sub_len, HEAD_DIM), F32),
                        pltpu.VMEM((sub_len, HEAD_DIM), F32)],
        compiler_params=_params("arbitrary", "arbitrary"),
    )(qkv, qkv, qkv, do, lse, delta)


RET_UNROLL = 8
RET_UNROLL_BWD = 8


def _ret_tables(lg_ref):
    h = pl.program_id(0)
    one = jnp.ones((1, 1), F32)
    lgf, lgb = lg_ref[h] * one, lg_ref[RET_HEADS + h] * one
    c = RET_CHUNK
    rel = (_iota((c, c), 0) - _iota((c, c), 1)).astype(F32)
    dec_f = jnp.where(rel >= 0, jnp.exp(jnp.maximum(rel, 0.0) * lgf), 0.0)
    dec_b = jnp.where(rel <= 0, jnp.exp(jnp.maximum(-rel, 0.0) * lgb), 0.0)
    ci = _iota((c, 1), 0).astype(F32)
    tab = dict(rel=rel, dec_f=dec_f, dec_b=dec_b, ci=ci,
               xi_f=jnp.exp((ci + 1.0) * lgf), ze_f=jnp.exp((c - 1.0 - ci) * lgf), g_f=jnp.exp(c * lgf),
               xi_b=jnp.exp((c - ci) * lgb), ze_b=jnp.exp(ci * lgb), g_b=jnp.exp(c * lgb))
    return tab


def _ret_specs(s_len):
    q = pl.BlockSpec((s_len, RET_QK), lambda h: (0, COL_QR // RET_QK + h))
    k = pl.BlockSpec((s_len, RET_QK), lambda h: (0, COL_KR // RET_QK + h))
    v = pl.BlockSpec((s_len, RET_V), lambda h: (0, COL_VR // RET_V + h))
    wide = pl.BlockSpec((s_len, RET_V), lambda h: (0, h))
    narrow = pl.BlockSpec((s_len, RET_QK), lambda h: (0, h))
    smem = pl.BlockSpec(memory_space=pltpu.SMEM)
    return smem, q, k, v, wide, narrow


def ret_fwd(proj, lg, finish=None):
    s_len = proj.shape[0]
    c, n_chunks = RET_CHUNK, proj.shape[0] // RET_CHUNK

    def body(lg_ref, q_ref, k_ref, v_ref, *rest):
        if finish is not None:
            _, _, opre_ref, y_ref, fin_ref, fout_ref, st_f, st_b, send_sems, recv_sems = rest
            first, last = _grid_edge(1)
            copies = _gather_phase2(_weight_blocks(fin_ref, fout_ref), send_sems, recv_sems)
            pl.when(first)(lambda: _start_all(copies))
        else:
            opre_ref, y_ref, st_f, st_b = rest
        t = _ret_tables(lg_ref)
        dec = t["dec_f"] + t["dec_b"]
        st_f[...] = jnp.zeros_like(st_f)
        st_b[...] = jnp.zeros_like(st_b)

        def load(n):
            r0 = pl.multiple_of(n * c, c)
            q, k, v = q_ref[pl.ds(r0, c), :], k_ref[pl.ds(r0, c), :], v_ref[pl.ds(r0, c), :]
            return r0, q, (k.astype(F32) * RET_SCALE), v

        def fwd(n, carry):
            r0, q, kf, v = load(n)
            inner = (_dot_nt(q, kf.astype(BF16)) * dec).astype(BF16)
            opre_ref[pl.ds(r0, c), :] = _dot(inner, v) + _dot(q, st_f[...].astype(BF16)) * t["xi_f"]
            st_f[...] = st_f[...] * t["g_f"] + _dot_tn((kf * t["ze_f"]).astype(BF16), v)
            return carry

        def bwd(i, carry):
            r0, q, kf, v = load(n_chunks - 1 - i)
            o = opre_ref[pl.ds(r0, c), :] + _dot(q, st_b[...].astype(BF16)) * t["xi_b"]
            st_b[...] = st_b[...] * t["g_b"] + _dot_tn((kf * t["ze_b"]).astype(BF16), v)
            opre_ref[pl.ds(r0, c), :] = o
            y_ref[pl.ds(r0, c), :] = o * lax.rsqrt(jnp.mean(o * o, axis=-1, keepdims=True) + NORM_EPS)
            return carry

        lax.fori_loop(0, n_chunks, fwd, 0, unroll=min(RET_UNROLL, n_chunks))
        lax.fori_loop(0, n_chunks, bwd, 0, unroll=min(RET_UNROLL, n_chunks))
        if finish is not None:
            pl.when(last)(lambda: _wait_all(copies))

    smem, q, k, v, wide, _ = _ret_specs(s_len)
    out = jax.ShapeDtypeStruct((s_len, RET_HEADS * RET_V), F32)
    in_specs, out_specs, out_shape = [smem, q, k, v], [wide, wide], [out, out]
    scratch = [pltpu.VMEM((RET_QK, RET_V), F32), pltpu.VMEM((RET_QK, RET_V), F32)]
    aliases, extra = {}, ()
    if finish is not None:
        in_specs += [HBM, HBM]
        out_specs += [HBM, HBM]
        out_shape += _full_weight_shapes()
        scratch += GATHER2_SEMS
        aliases, extra = {4: 2, 5: 3}, tuple(finish)
    return pl.pallas_call(
        body, name="ret_fwd_gather2" if finish is not None else "ret_fwd", grid=(RET_HEADS,), in_specs=in_specs,
        out_specs=out_specs, out_shape=out_shape, scratch_shapes=scratch, input_output_aliases=aliases,
        compiler_params=_params("arbitrary"),
    )(lg, proj, proj, proj, *extra)


def ret_bwd(proj, lg, o_pre, dy):
    s_len = proj.shape[0]
    c, n_chunks = RET_CHUNK, proj.shape[0] // RET_CHUNK
    cf = float(c)

    def body(lg_ref, q_ref, k_ref, v_ref, o_ref, dy_ref, dq_ref, dk_ref, dv_ref, glf_ref, glb_ref,
             st_f, dst_b, st_b, dst_f, keep_sf, keep_dtb, acc_f, acc_b, acc_sf, acc_sb):
        t = _ret_tables(lg_ref)
        dec = t["dec_f"] + t["dec_b"]
        e_f, e_b, ci = t["rel"] * t["dec_f"], -t["rel"] * t["dec_b"], t["ci"]
        for ref in (st_f, dst_b, st_b, dst_f, acc_f, acc_b, acc_sf, acc_sb):
            ref[...] = jnp.zeros_like(ref)

        def load(n):
            r0 = pl.multiple_of(n * c, c)
            q, k, v = q_ref[pl.ds(r0, c), :], k_ref[pl.ds(r0, c), :], v_ref[pl.ds(r0, c), :]
            o, dyv = o_ref[pl.ds(r0, c), :], dy_ref[pl.ds(r0, c), :]
            rr = lax.rsqrt(jnp.mean(o * o, axis=-1, keepdims=True) + NORM_EPS)
            y = o * rr
            do = (rr * (dyv - y * jnp.mean(dyv * y, axis=-1, keepdims=True))).astype(BF16)
            return r0, q, k.astype(F32) * RET_SCALE, v, do

        def fwd(n, carry):
            r0, q, kf, v, do = load(n)
            qf, kb = q.astype(F32), kf.astype(BF16)
            a = _dot_nt(q, kb)
            b = _dot_nt(do, v)
            da = (b * dec).astype(BF16)
            ab = a * b
            sf_b, dtb_b = st_f[...].astype(BF16), dst_b[...].astype(BF16)
            dq_inter = _dot_nt(do, sf_b) * t["xi_f"]
            dk_inter = _dot_nt(v, dtb_b) * t["ze_b"]
            acc_f[...] += e_f * ab + (ci + 1.0) * (qf * dq_inter)
            acc_b[...] += e_b * ab + ci * (kf * dk_inter)
            dq_ref[pl.ds(r0, c), :] = _dot(da, kb) + dq_inter
            dk_ref[pl.ds(r0, c), :] = _dot_tn(da, q) + dk_inter
            dv_ref[pl.ds(r0, c), :] = _dot_tn((a * dec).astype(BF16), do) + _dot((kf * t["ze_b"]).astype(BF16), dtb_b)
            keep_sf[n] = sf_b
            keep_dtb[n] = dtb_b
            st_f[...] = st_f[...] * t["g_f"] + _dot_tn((kf * t["ze_f"]).astype(BF16), v)
            dst_b[...] = dst_b[...] * t["g_b"] + _dot_tn((qf * t["xi_b"]).astype(BF16), do)
            return carry

        def bwd(i, carry):
            n = n_chunks - 1 - i
            r0, q, kf, v, do = load(n)
            qf = q.astype(F32)
            tb_b, dsf_b = st_b[...].astype(BF16), dst_f[...].astype(BF16)
            dq_inter = _dot_nt(do, tb_b) * t["xi_b"]
            dk_inter = _dot_nt(v, dsf_b) * t["ze_f"]
            acc_b[...] += (cf - ci) * (qf * dq_inter)
            acc_f[...] += (cf - 1.0 - ci) * (kf * dk_inter)
            acc_sb[...] += keep_dtb[n].astype(F32) * st_b[...]
            acc_sf[...] += dst_f[...] * keep_sf[n].astype(F32)
            dq_ref[pl.ds(r0, c), :] += dq_inter
            dk_ref[pl.ds(r0, c), :] = (dk_ref[pl.ds(r0, c), :] + dk_inter) * RET_SCALE
            dv_ref[pl.ds(r0, c), :] += _dot((kf * t["ze_f"]).astype(BF16), dsf_b)
            st_b[...] = st_b[...] * t["g_b"] + _dot_tn((kf * t["ze_b"]).astype(BF16), v)
            dst_f[...] = dst_f[...] * t["g_f"] + _dot_tn((qf * t["xi_f"]).astype(BF16), do)
            return carry

        lax.fori_loop(0, n_chunks, fwd, 0, unroll=min(RET_UNROLL_BWD, n_chunks))
        lax.fori_loop(0, n_chunks, bwd, 0, unroll=min(RET_UNROLL_BWD, n_chunks))

        def total(x):
            return jnp.sum(jnp.sum(x, axis=1, keepdims=True), axis=0, keepdims=True)

        glf_ref[...] = jnp.broadcast_to(total(acc_f[...]) + cf * t["g_f"] * total(acc_sf[...]), (8, 128))
        glb_ref[...] = jnp.broadcast_to(total(acc_b[...]) + cf * t["g_b"] * total(acc_sb[...]), (8, 128))

    smem, q, k, v, wide, narrow = _ret_specs(s_len)
    scal = pl.BlockSpec((None, 8, 128), lambda h: (h, 0, 0))
    state = pltpu.VMEM((RET_QK, RET_V), F32)
    square = pltpu.VMEM((RET_CHUNK, RET_QK), F32)
    keep = pltpu.VMEM((n_chunks, RET_QK, RET_V), BF16)
    return pl.pallas_call(
        body, name="ret_bwd", grid=(RET_HEADS,), in_specs=[smem, q, k, v, wide, wide],
        out_specs=[narrow, narrow, wide, scal, scal],
        out_shape=[jax.ShapeDtypeStruct((s_len, RET_HEADS * RET_QK), F32), jax.ShapeDtypeStruct((s_len, RET_HEADS * RET_QK), F32),
                   jax.ShapeDtypeStruct((s_len, RET_HEADS * RET_V), F32),
                   jax.ShapeDtypeStruct((RET_HEADS, 8, 128), F32), jax.ShapeDtypeStruct((RET_HEADS, 8, 128), F32)],
        scratch_shapes=[state, state, state, state, keep, keep, square, square, state, state],
        compiler_params=_params("arbitrary"),
    )(lg, proj, proj, proj, o_pre, dy)


def _silu_parts(z):
    sig = _sigmoid(z)
    return z * sig, sig * (1.0 + z * (1.0 - sig))


def outproj_fwd(x, gate, w_out, attn_outs, y_r, proj):
    s_len = x.shape[0]
    tm = min(256, s_len)

    def per_head(w):
        return jnp.concatenate([jnp.broadcast_to(w[:, h:h + 1], (tm, HEAD_DIM)) for h in range(N_HEADS_ATTN)], axis=1)

    def body(x_ref, gate_ref, w_ref, o1, l1, o2, l2, o3, l3, yr_ref, za_ref, zr_ref,
             xn_ref, oa_ref, lse_ref, lse4_ref, lse16_ref, so2, sl2, so3, sl3):
        silu_r, _ = _silu_parts(zr_ref[...].astype(F32))
        out_r = _dot((yr_ref[...] * silu_r).astype(BF16), w_ref[ATTN_WIDTH:, :])
        for src, dst in ((o2, so2), (l2, sl2), (o3, so3), (l3, sl3)):
            _merge_classes(src, dst)
        la, lb, lc = l1[...], _read_stage(sl2), _read_stage(sl3)
        m = jnp.maximum(jnp.maximum(la, lb), lc)
        lse = m + jnp.log(jnp.exp(la - m) + jnp.exp(lb - m) + jnp.exp(lc - m))
        o_a = (per_head(jnp.exp(la - lse)) * o1[...] + per_head(jnp.exp(lb - lse)) * _read_stage(so2)
               + per_head(jnp.exp(lc - lse)) * _read_stage(so3))
        oa_ref[...] = o_a
        lse_ref[...] = lse
        _fill_stage(sl2, lse)
        _split_classes(sl2, [lse4_ref, lse16_ref])
        silu_a, _ = _silu_parts(za_ref[...].astype(F32))
        out_a = _dot((o_a * silu_a).astype(BF16), w_ref[:ATTN_WIDTH, :])
        xn_ref[...] = x_ref[...] + gate_ref[...] * (out_a + out_r)

    row = lambda w: pl.BlockSpec((tm, w), lambda i: (i, 0))
    half, stat = row(ATTN_WIDTH), row(LANES)
    cls = [_class_block(dil, tm, ATTN_WIDTH, lambda i: (0, i, 0)) for dil in DILATIONS[1:]]
    cls_stat = [_class_block(dil, tm, LANES, lambda i: (0, i, 0)) for dil in DILATIONS[1:]]
    flat = [a for pair in attn_outs for a in pair]
    sds = jax.ShapeDtypeStruct
    return pl.pallas_call(
        body, name="outproj_fwd", grid=(s_len // tm,),
        in_specs=[row(D_MODEL), pl.BlockSpec((1, D_MODEL), lambda i: (0, 0)),
                  pl.BlockSpec((D_MODEL, D_MODEL), lambda i: (0, 0)),
                  half, stat, cls[0], cls_stat[0], cls[1], cls_stat[1], half,
                  pl.BlockSpec((tm, ATTN_WIDTH), lambda i: (i, COL_ZA // ATTN_WIDTH)),
                  pl.BlockSpec((tm, ATTN_WIDTH), lambda i: (i, COL_ZR // ATTN_WIDTH))],
        out_specs=[row(D_MODEL), half, stat] + cls_stat,
        out_shape=[sds((s_len, D_MODEL), F32), sds((s_len, ATTN_WIDTH), F32), sds((s_len, LANES), F32)]
                  + [sds((dil, s_len // dil, LANES), F32) for dil in DILATIONS[1:]],
        scratch_shapes=[_stage_shape(tm, ATTN_WIDTH), _stage_shape(tm, LANES)] * 2,
        compiler_params=_params("arbitrary"),
    )(x, gate, w_out, *flat, y_r, proj, proj)


def loss_head(x, gain, target):
    s_len = x.shape[0]
    tm = min(256, s_len)

    def body(x_ref, g_ref, t_ref, dx_ref, loss_ref, dg_ref):
        @pl.when(pl.program_id(0) == 0)
        def _():
            loss_ref[...] = jnp.zeros_like(loss_ref)
            dg_ref[...] = jnp.zeros_like(dg_ref)
        xv, g = x_ref[...], g_ref[...]
        r = lax.rsqrt(jnp.mean(xv * xv, axis=-1, keepdims=True) + NORM_EPS)
        xn = xv * r
        err = xn * g - t_ref[...]
        part = 0.5 * jnp.sum(jnp.mean(err * err, axis=-1, keepdims=True), axis=0, keepdims=True)
        loss_ref[...] += jnp.broadcast_to(part, loss_ref.shape)
        dy = err * (1.0 / D_MODEL)
        dg_ref[...] += jnp.sum(dy * xn, axis=0, keepdims=True)
        dxn = dy * g
        dx_ref[...] = r * (dxn - xn * jnp.mean(dxn * xn, axis=-1, keepdims=True))

    row = pl.BlockSpec((tm, D_MODEL), lambda i: (i, 0))
    vec = pl.BlockSpec((1, D_MODEL), lambda i: (0, 0))
    return pl.pallas_call(
        body, name="loss_head", grid=(s_len // tm,), in_specs=[row, vec, row],
        out_specs=[row, pl.BlockSpec((8, 128), lambda i: (0, 0)), vec],
        out_shape=[jax.ShapeDtypeStruct((s_len, D_MODEL), F32), jax.ShapeDtypeStruct((8, 128), F32),
                   jax.ShapeDtypeStruct((1, D_MODEL), F32)],
        compiler_params=_params("arbitrary"),
    )(x, gain, target)


def outproj_bwd(dxn, gate, w_out, o_a, y_r, proj):
    s_len = dxn.shape[0]
    tm = min(256, s_len)

    def body(dx_ref, gate_ref, w_ref, oa_ref, yr_ref, za_ref, zr_ref,
             doa_ref, dl_ref, dyr_ref, dza_ref, dzr_ref, y_ref, dxb_ref, do4, do16, dl4, dl16, stage, stat_stage):
        dxv = dx_ref[...]
        dxb_ref[...] = dxv.astype(BF16)
        dy = _dot_nt((dxv * gate_ref[...]).astype(BF16), w_ref[...])
        dy_a, dy_r = dy[:, :ATTN_WIDTH], dy[:, ATTN_WIDTH:]
        o_a, y_rv = oa_ref[...], yr_ref[...]
        silu_a, dsilu_a = _silu_parts(za_ref[...].astype(F32))
        silu_r, dsilu_r = _silu_parts(zr_ref[...].astype(F32))
        do_a = dy_a * silu_a
        doa_ref[...] = do_a.astype(BF16)
        _fill_stage(stage, do_a)
        _split_classes(stage, [do4, do16])
        prod = do_a * o_a
        lane = _iota((tm, LANES), 1)
        delta = jnp.zeros((tm, LANES), F32)
        for h in range(N_HEADS_ATTN):
            delta = jnp.where(lane == h, jnp.sum(prod[:, h * HEAD_DIM:(h + 1) * HEAD_DIM], axis=1, keepdims=True), delta)
        dl_ref[...] = delta
        _fill_stage(stat_stage, delta)
        _split_classes(stat_stage, [dl4, dl16])
        dyr_ref[...] = dy_r * silu_r
        dza_ref[...] = (dy_a * o_a * dsilu_a).astype(BF16)
        dzr_ref[...] = (dy_r * y_rv * dsilu_r).astype(BF16)
        y_ref[...] = jnp.concatenate([(o_a * silu_a).astype(BF16), (y_rv * silu_r).astype(BF16)], axis=1)

    row = lambda w: pl.BlockSpec((tm, w), lambda i: (i, 0))
    half = row(ATTN_WIDTH)
    sds = lambda w, dt: jax.ShapeDtypeStruct((s_len, w), dt)
    in_specs = [row(D_MODEL), pl.BlockSpec((1, D_MODEL), lambda i: (0, 0)),
                pl.BlockSpec((D_MODEL, D_MODEL), lambda i: (0, 0)), half, half,
                pl.BlockSpec((tm, ATTN_WIDTH), lambda i: (i, COL_ZA // ATTN_WIDTH)),
                pl.BlockSpec((tm, ATTN_WIDTH), lambda i: (i, COL_ZR // ATTN_WIDTH))]
    cls = [_class_block(dil, tm, ATTN_WIDTH, lambda i: (0, i, 0)) for dil in DILATIONS[1:]]
    cls_stat = [_class_block(dil, tm, LANES, lambda i: (0, i, 0)) for dil in DILATIONS[1:]]
    out_specs = [half, row(LANES), half, half, half, row(D_MODEL), row(D_MODEL)] + cls + cls_stat
    out_shape = [sds(ATTN_WIDTH, BF16), sds(LANES, F32), sds(ATTN_WIDTH, F32), sds(ATTN_WIDTH, BF16),
                 sds(ATTN_WIDTH, BF16), sds(D_MODEL, BF16), sds(D_MODEL, BF16)]
    out_shape += [jax.ShapeDtypeStruct((dil, s_len // dil, ATTN_WIDTH), BF16) for dil in DILATIONS[1:]]
    out_shape += [jax.ShapeDtypeStruct((dil, s_len // dil, LANES), F32) for dil in DILATIONS[1:]]
    return pl.pallas_call(
        body, name="outproj_bwd", grid=(s_len // tm,),
        in_specs=in_specs, out_specs=out_specs, out_shape=out_shape,
        scratch_shapes=[_stage_shape(tm, ATTN_WIDTH), _stage_shape(tm, LANES)],
        compiler_params=_params("arbitrary"),
    )(dxn, gate, w_out, o_a, y_r, proj, proj)


def wout_grad(y, dxb, gate, w_out):
    s_len = y.shape[0]
    tf, ts = 512, min(1024, s_len)

    def body(y_ref, dx_ref, gate_ref, w_ref, dw_ref, dgate_ref, acc):
        f, s = pl.program_id(0), pl.program_id(1)

        @pl.when((f == 0) & (s == 0))
        def _():
            dgate_ref[...] = jnp.zeros_like(dgate_ref)

        @pl.when(s == 0)
        def _():
            acc[...] = jnp.zeros_like(acc)
        acc[...] += _dot_tn(y_ref[...], dx_ref[...])

        @pl.when(s == pl.num_programs(1) - 1)
        def _():
            m = acc[...]
            dw_ref[...] = (m * gate_ref[...]).astype(BF16).reshape(dw_ref.shape)
            dgate_ref[...] += jnp.sum(m * w_ref[...].astype(F32), axis=0, keepdims=True)

    per = tf // W_OUT_SHARD
    return pl.pallas_call(
        body, name="wout_grad", grid=(D_MODEL // tf, s_len // ts),
        in_specs=[pl.BlockSpec((ts, tf), lambda f, s: (s, f)), pl.BlockSpec((ts, D_MODEL), lambda f, s: (s, 0)),
                  pl.BlockSpec((1, D_MODEL), lambda f, s: (0, 0)), pl.BlockSpec((tf, D_MODEL), lambda f, s: (f, 0))],
        out_specs=[pl.BlockSpec((per, W_OUT_SHARD, D_MODEL), lambda f, s: (f, 0, 0)),
                   pl.BlockSpec((1, D_MODEL), lambda f, s: (0, 0))],
        out_shape=[jax.ShapeDtypeStruct((N_DEV, W_OUT_SHARD, D_MODEL), BF16), jax.ShapeDtypeStruct((1, D_MODEL), F32)],
        scratch_shapes=[pltpu.VMEM((tf, D_MODEL), F32)],
        compiler_params=_params("arbitrary", "arbitrary"),
    )(y, dxb, gate, w_out)


def assemble_dproj(dqkv_a, dz_a, dq_r, dk_r, dv_r, dz_r):
    s_len = dz_a.shape[0]
    tm = min(512, s_len)

    def body(*refs):
        pat, (dza, dqr, dkr, dvr, dzr, out, stage) = refs[:9], refs[9:]
        for t in range(3):
            tot = pat[t][...].astype(F32)
            for p in (1, 2):
                _merge_classes(pat[3 * p + t], stage)
                tot = tot + _read_stage(stage)
            out[:, t * ATTN_WIDTH:(t + 1) * ATTN_WIDTH] = tot.astype(BF16)
        out[:, COL_ZA:COL_QR] = dza[...]
        out[:, COL_QR:COL_KR] = dqr[...].astype(BF16)
        out[:, COL_KR:COL_VR] = dkr[...].astype(BF16)
        out[:, COL_VR:COL_ZR] = dvr[...].astype(BF16)
        out[:, COL_ZR:IN_W] = dzr[...]

    row = lambda w: pl.BlockSpec((tm, w), lambda i: (i, 0))
    cls = [_class_block(dil, tm, ATTN_WIDTH, lambda i: (0, i, 0)) for dil in DILATIONS[1:]]
    flat = [dqkv_a[p][t] for p in range(3) for t in range(3)]
    return pl.pallas_call(
        body, name="assemble_dproj", grid=(s_len // tm,),
        in_specs=[row(ATTN_WIDTH)] * 3 + [cls[0]] * 3 + [cls[1]] * 3
                 + [row(ATTN_WIDTH), row(512), row(512), row(ATTN_WIDTH), row(ATTN_WIDTH)],
        out_specs=row(IN_W), out_shape=jax.ShapeDtypeStruct((s_len, IN_W), BF16),
        scratch_shapes=[_stage_shape(tm, ATTN_WIDTH)],
        compiler_params=_params("arbitrary"),
    )(*flat, dz_a, dq_r, dk_r, dv_r, dz_r)


def inproj_bwd(dproj, w, x, g, scale1p, dxn, exchange):
    s_len = x.shape[0]
    tm, tk = min(512, s_len), 1024

    def body(dp_ref, w_ref, x_ref, g_ref, sc_ref, dxn_ref, pa_ref, pb_ref, dx_ref, st_ref, ra_ref, rb_ref,
             acc, send_sems, recv_sems):
        first, last = _grid_edge(2)
        copies = _reduce_phase2([pa_ref, pb_ref], [ra_ref, rb_ref], send_sems, recv_sems)
        pl.when(first)(lambda: _start_all(copies))
        i, k = pl.program_id(0), pl.program_id(1)

        @pl.when((i == 0) & (k == 0))
        def _():
            st_ref[...] = jnp.zeros_like(st_ref)

        @pl.when(k == 0)
        def _():
            acc[...] = jnp.zeros_like(acc)
        acc[...] += _dot_nt(dp_ref[...], w_ref[...])

        @pl.when(k == pl.num_programs(1) - 1)
        def _():
            dh, xv = acc[...], x_ref[...]
            r = lax.rsqrt(jnp.mean(xv * xv, axis=-1, keepdims=True) + NORM_EPS)
            xn = xv * r
            st_ref[0:1, :] += jnp.sum(dh, axis=0, keepdims=True)
            st_ref[3:4, :] += jnp.sum(dh * xn, axis=0, keepdims=True)
            dn = dh * (sc_ref[...] * g_ref[...])
            dx_ref[...] = r * (dn - xn * jnp.mean(dn * xn, axis=-1, keepdims=True)) + dxn_ref[...]

        @pl.when(last)
        def _():
            st_ref[1:2, :] = st_ref[3:4, :] * g_ref[...]
            st_ref[2:3, :] = st_ref[3:4, :] * sc_ref[...]
            _wait_all(copies)

    row = pl.BlockSpec((tm, D_MODEL), lambda i, k: (i, 0))
    vec = pl.BlockSpec((1, D_MODEL), lambda i, k: (0, 0))
    return pl.pallas_call(
        body, name="inproj_bwd_reduce2", grid=(s_len // tm, IN_W // tk),
        in_specs=[pl.BlockSpec((tm, tk), lambda i, k: (i, k)), pl.BlockSpec((D_MODEL, tk), lambda i, k: (0, k)),
                  row, vec, vec, row, HBM, HBM],
        out_specs=[row, pl.BlockSpec((8, D_MODEL), lambda i, k: (0, 0)), HBM, HBM],
        out_shape=[jax.ShapeDtypeStruct((s_len, D_MODEL), F32), jax.ShapeDtypeStruct((8, D_MODEL), F32)]
                  + _landing_shapes(3),
        scratch_shapes=[pltpu.VMEM((tm, D_MODEL), F32)] + REDUCE2_SEMS,
        compiler_params=_params("arbitrary", "arbitrary"),
    )(dproj, w, x, g, scale1p, dxn, *exchange)


def win_grad(h, dproj):
    s_len = h.shape[0]
    ts = min(2048, s_len)

    def body(h_ref, dp_ref, dw_ref, acc):
        s = pl.program_id(1)

        @pl.when(s == 0)
        def _():
            acc[...] = jnp.zeros_like(acc)
        acc[...] += _dot_tn(h_ref[...], dp_ref[...])

        @pl.when(s == pl.num_programs(1) - 1)
        def _():
            dw_ref[...] = acc[...].astype(BF16)

    return pl.pallas_call(
        body, name="win_grad", grid=(N_DEV, s_len // ts),
        in_specs=[pl.BlockSpec((ts, D_MODEL), lambda j, s: (s, 0)), pl.BlockSpec((ts, W_IN_SHARD), lambda j, s: (s, j))],
        out_specs=pl.BlockSpec((None, D_MODEL, W_IN_SHARD), lambda j, s: (j, 0, 0)),
        out_shape=jax.ShapeDtypeStruct((N_DEV, D_MODEL, W_IN_SHARD), BF16),
        scratch_shapes=[pltpu.VMEM((D_MODEL, W_IN_SHARD), F32)],
        compiler_params=_params("arbitrary", "arbitrary"),
    )(h, dproj)


def ada_fwd(c_all, w_ada):
    def body(c_ref, w_ref, act_ref, part_ref):
        cv = c_ref[...]
        act = cv * _sigmoid(cv)
        act_ref[...] = act
        part_ref[...] = _dot(act.astype(BF16), w_ref[...].astype(BF16))

    return pl.pallas_call(
        body, name="ada_fwd", grid=(DEPTH,),
        in_specs=[pl.BlockSpec((N_DEV, D_MODEL), lambda l: (0, 0)),
                  pl.BlockSpec((None, D_MODEL, W_ADA_SHARD), lambda l: (l, 0, 0))],
        out_specs=[pl.BlockSpec((N_DEV, D_MODEL), lambda l: (0, 0)),
                   pl.BlockSpec((None, N_DEV, W_ADA_SHARD), lambda l: (l, 0, 0))],
        out_shape=[jax.ShapeDtypeStruct((N_DEV, D_MODEL), F32), jax.ShapeDtypeStruct((DEPTH, N_DEV, W_ADA_SHARD), F32)],
        compiler_params=_params("arbitrary"),
    )(c_all, w_ada)


def _adamw(w, g, m, v):
    m = ADAM_B1 * m + (1.0 - ADAM_B1) * g
    v = ADAM_B2 * v + (1.0 - ADAM_B2) * (g * g)
    delta = -ADAM_LR * ((m * ADAM_C1) / (jnp.sqrt(v * ADAM_C2) + ADAM_EPS) + ADAM_WD * w)
    return delta, m, v


def ada_update(act_t, dmod, w, m, v):
    tr = 512

    def body(a_ref, d_ref, w_ref, m_ref, v_ref, g_out, dl_out, m_out, v_out):
        a = a_ref[...].astype(BF16).astype(F32)
        d = d_ref[...].astype(BF16).astype(F32)
        g = a[:, 0:1] * d[0:1, :]
        for b in range(1, N_DEV):
            g = g + a[:, b:b + 1] * d[b:b + 1, :]
        g_out[...] = g
        dl_out[...], m_out[...], v_out[...] = _adamw(w_ref[...], g, m_ref[...], v_ref[...])

    blk = pl.BlockSpec((None, tr, W_ADA_SHARD), lambda l, r: (l, r, 0))
    out = jax.ShapeDtypeStruct(w.shape, F32)
    return pl.pallas_call(
        body, name="ada_update", grid=(DEPTH, D_MODEL // tr),
        in_specs=[pl.BlockSpec((tr, N_DEV), lambda l, r: (r, 0)),
                  pl.BlockSpec((None, N_DEV, W_ADA_SHARD), lambda l, r: (l, 0, 0)), blk, blk, blk],
        out_specs=[blk] * 4, out_shape=[out] * 4, compiler_params=_params("arbitrary", "arbitrary"),
    )(act_t, dmod, w, m, v)


def chip_sum(pos, grads, landed, name):
    _, _, n_rows, n_cols = grads.shape
    tr = min(512, n_rows)

    def chip(k, pos_ref):
        return (pos_ref[0] ^ (k // 2)) * 2 + (pos_ref[1] ^ (k % 2))

    def body(pos_ref, g_ref, r_ref, out_ref):
        out_ref[...] = (g_ref[...].astype(F32) + r_ref[...].astype(F32)).astype(BF16)

    return pl.pallas_call(
        body, name=name,
        grid_spec=pltpu.PrefetchScalarGridSpec(
            num_scalar_prefetch=1, grid=(N_DEV // 2, n_rows // tr),
            in_specs=[pl.BlockSpec((None, None, tr, n_cols), lambda k, r, p: (chip(k, p), p[2], r, 0)),
                      pl.BlockSpec((None, tr, n_cols), lambda k, r, p: (chip(k, p), r, 0))],
            out_specs=pl.BlockSpec((None, tr, n_cols), lambda k, r, p: (k, r, 0))),
        out_shape=jax.ShapeDtypeStruct((N_DEV // 2, n_rows, n_cols), BF16),
        compiler_params=_params("arbitrary", "arbitrary"),
    )(pos, grads, landed)


def shard_update(layer, own, others, w, m, v, prev, name):
    _, n_rows, n_cols = w.shape
    tr = min(256, n_rows)

    def body(own_ref, oth_ref, w_ref, m_ref, v_ref, *rest):
        g_out, dl_out, m_out, v_out = rest[-4:]
        g = own_ref[...].astype(F32)
        for k in range(3):
            g = g + oth_ref[k].astype(F32)
        g_out[...] = g
        dl_out[...], m_out[...], v_out[...] = _adamw(w_ref[...], g, m_ref[...], v_ref[...])

    blk = pl.BlockSpec((None, tr, n_cols), lambda r: (layer, r, 0))
    out = jax.ShapeDtypeStruct(w.shape, F32)
    in_specs = [pl.BlockSpec((None, tr, n_cols), lambda r: (0, r, 0)), pl.BlockSpec((3, tr, n_cols), lambda r: (0, r, 0)),
                blk, blk, blk]
    aliases, extra = {}, ()
    if prev is not None:
        in_specs += [HBM] * 4
        aliases = {5 + t: t for t in range(4)}
        extra = tuple(prev)
    return pl.pallas_call(
        body, name=name, grid=(n_rows // tr,), in_specs=in_specs, out_specs=[blk] * 4, out_shape=[out] * 4,
        input_output_aliases=aliases, compiler_params=_params("arbitrary"),
    )(own, others, w, m, v, *extra)


def small_update(parts, w, m, v):
    def body(p_ref, w_ref, m_ref, v_ref, g_out, dl_out, m_out, v_out):
        g = p_ref[0]
        for k in range(1, N_DEV):
            g = g + p_ref[k]
        g_out[...] = g
        dl_out[...], m_out[...], v_out[...] = _adamw(w_ref[...], g, m_ref[...], v_ref[...])

    out = jax.ShapeDtypeStruct(w.shape, F32)
    return pl.pallas_call(body, name="small_update", out_shape=[out] * 4, compiler_params=_params())(parts, w, m, v)


def _two_level_allgather(srcs, dst_block, send_sems, recv_sems, local_sems):
    x, y, c = _position()
    me, sibling = (x, y, c), (x, y, 1 - c)
    chips = [(1 - x, y), (x, 1 - y), (1 - x, 1 - y)]
    n = len(srcs)

    def copy(a, k, block, to, src=None):
        dst = dst_block(a, _flat(*block))
        return pltpu.make_async_remote_copy(
            src_ref=dst if src is None else src, dst_ref=dst, send_sem=send_sems.at[a * 7 + k],
            recv_sem=recv_sems.at[a * 7 + k], device_id=to, device_id_type=MESH)

    mine = [pltpu.make_async_copy(srcs[a], dst_block(a, _flat(*me)), local_sems.at[a]) for a in range(n)]
    for cp in mine:
        cp.start()
    first = []
    for a in range(n):
        first.append(copy(a, 0, me, sibling, src=srcs[a]))
        first += [copy(a, 1 + j, me, (*chip, c), src=srcs[a]) for j, chip in enumerate(chips)]
    for cp in first:
        cp.start()
    passed = []
    for j, chip in enumerate(chips):
        for a in range(n):
            copy(a, 1 + j, (*chip, c), me).wait_recv()
            fwd = copy(a, 4 + j, (*chip, c), sibling)
            fwd.start()
            passed.append(fwd)
    for a in range(n):
        copy(a, 0, sibling, me).wait_recv()
        for j, chip in enumerate(chips):
            copy(a, 4 + j, (*chip, 1 - c), me).wait_recv()
    for cp in first + passed:
        cp.wait_send()
    for cp in mine:
        cp.wait()


def allgather_rows(x, name):
    def body(x_ref, out_ref, send_sems, recv_sems, local_sems):
        _two_level_allgather([x_ref], lambda a, idx: out_ref.at[idx], send_sems, recv_sems, local_sems)

    vmem = pl.BlockSpec(memory_space=pltpu.VMEM)
    return pl.pallas_call(
        body, name=name, in_specs=[vmem], out_specs=vmem,
        out_shape=jax.ShapeDtypeStruct((N_DEV,) + x.shape, x.dtype),
        scratch_shapes=[pltpu.SemaphoreType.DMA((7,)), pltpu.SemaphoreType.DMA((7,)), pltpu.SemaphoreType.DMA((1,))],
        compiler_params=_params(),
    )(x)


def _full_weight_shapes():
    return [jax.ShapeDtypeStruct((D_MODEL, IN_W), BF16), jax.ShapeDtypeStruct((D_MODEL, D_MODEL), BF16)]


def allgather_weights(w_in_b, w_out_b):
    def body(win_ref, wout_ref, fin_ref, fout_ref, s1, r1, l1, s2, r2):
        block = _weight_blocks(fin_ref, fout_ref)
        first = _gather_phase1([win_ref.at[0], wout_ref.at[0]], block, s1, r1, l1)
        _start_all(first)
        _wait_all(first)
        second = _gather_phase2(block, s2, r2)
        _start_all(second)
        _wait_all(second)

    return pl.pallas_call(
        body, name="allgather_weights", in_specs=[HBM, HBM], out_specs=[HBM, HBM], out_shape=_full_weight_shapes(),
        scratch_shapes=GATHER1_SEMS + GATHER2_SEMS, compiler_params=_params(),
    )(w_in_b, w_out_b)


def reduce_first(dw_in, dw_out, name):
    def body(ga_ref, gb_ref, ra_ref, rb_ref, send_sems, recv_sems):
        copies = _reduce_phase1([ga_ref, gb_ref], [ra_ref, rb_ref], send_sems, recv_sems)
        _start_all(copies)
        _wait_all(copies)

    return pl.pallas_call(
        body, name=name, in_specs=[HBM, HBM], out_specs=[HBM, HBM], out_shape=_landing_shapes(4),
        scratch_shapes=REDUCE1_SEMS, compiler_params=_params(),
    )(*_split_cores((dw_in, dw_out)))


def _one_class(a):
    return a.reshape((1,) + a.shape)


def layer_fwd(x, g, scale, shift, gate, w_in, w_out, lg, gather=None):
    proj, h, qkv4, qkv16, *began = inproj_fwd(x, g, 1.0 + scale, shift, w_in, gather)
    qkv = (_one_class(proj), qkv4, qkv16)
    attn_outs = [attn_fwd(arr, dil) for dil, arr in zip(DILATIONS, qkv)]
    attn_outs[0] = tuple(a[0] for a in attn_outs[0])
    o_pre, y_r, *next_weights = ret_fwd(proj, lg, began if gather is not None else None)
    next_weights = tuple(next_weights) if gather is not None else None
    x_new, o_a, lse, lse4, lse16 = outproj_fwd(x, gate, w_out, attn_outs, y_r, proj)
    saved = dict(x=x, proj=proj, h=h, qkv=qkv, o_a=o_a, lse=(_one_class(lse), lse4, lse16), o_pre=o_pre, y_r=y_r)
    return x_new, saved, next_weights


def layer_bwd(dxn, saved, g, scale, gate, w_in, w_out, lg, pos):
    proj = saved["proj"]
    do_a, delta, dyr, dz_a, dz_r, y, dxb, do4, do16, dl4, dl16 = outproj_bwd(
        dxn, gate, w_out, saved["o_a"], saved["y_r"], proj)
    dw_out, dgate = wout_grad(y, dxb, gate, w_out)
    dq_r, dk_r, dv_r, glf, glb = ret_bwd(proj, lg, saved["o_pre"], dyr)
    dos, deltas = (_one_class(do_a), do4, do16), (_one_class(delta), dl4, dl16)
    dqkv_a = [attn_bwd(arr, d_o, lse, dl, dil)
              for dil, arr, d_o, lse, dl in zip(DILATIONS, saved["qkv"], dos, saved["lse"], deltas)]
    dqkv_a[0] = [t[0] for t in dqkv_a[0]]
    dproj = assemble_dproj(dqkv_a, dz_a, dq_r, dk_r, dv_r, dz_r)
    dw_in = win_grad(saved["h"], dproj)
    landed = reduce_first(dw_in, dw_out, "reduce_first")
    sums = [chip_sum(pos, g4, r, "chip_sum") for g4, r in zip(_split_cores((dw_in, dw_out)), landed)]
    dx, stats, *others = inproj_bwd(dproj, w_in, saved["x"], g, 1.0 + scale, dxn, sums)
    dlg = jnp.concatenate([glf[:, 0, 0], glb[:, 0, 0]])
    return dx, stats[0:1], stats[1:2], dgate, stats[2:3], dlg, (sums, others)


ROWS_B_ADA = DEPTH * 3 * D_MODEL // 128
ROWS_GAIN = DEPTH * D_MODEL // 128
ROWS_FINAL = D_MODEL // 128
ROWS_MISC = 8
ROWS_SMALL = ROWS_B_ADA + ROWS_GAIN + ROWS_FINAL + ROWS_MISC


def _pack_small(b_ada_like, gain_like, final_like, dec_f, dec_b, loss=None):
    misc = jnp.zeros((ROWS_MISC, 128), F32)
    misc = misc.at[0, :2 * DEPTH * RET_HEADS].set(jnp.concatenate([dec_f.reshape(-1), dec_b.reshape(-1)]))
    if loss is not None:
        misc = misc.at[1, 0].set(loss)
    return jnp.concatenate([b_ada_like.reshape(ROWS_B_ADA, 128), gain_like.reshape(ROWS_GAIN, 128),
                            final_like.reshape(ROWS_FINAL, 128), misc], axis=0)


def _unpack_small(p):
    r0, r1, r2 = ROWS_B_ADA, ROWS_B_ADA + ROWS_GAIN, ROWS_B_ADA + ROWS_GAIN + ROWS_FINAL
    n = DEPTH * RET_HEADS
    return (p[:r0].reshape(DEPTH, 3 * D_MODEL), p[r0:r1].reshape(DEPTH, D_MODEL), p[r1:r2].reshape(D_MODEL),
            p[r2, :n].reshape(DEPTH, RET_HEADS), p[r2, n:2 * n].reshape(DEPTH, RET_HEADS))


def kernel(x, c, norm_gain, w_ada, b_ada, w_in, w_out, ret_decay_logit_f, ret_decay_logit_b, final_gain, loss_target, m_norm_gain, m_w_ada, m_b_ada, m_w_in, m_w_out, m_ret_decay_logit_f, m_ret_decay_logit_b, m_final_gain, v_norm_gain, v_w_ada, v_b_ada, v_w_in, v_w_out, v_ret_decay_logit_f, v_ret_decay_logit_b, v_final_gain):
    px, py, pc = _position()
    me = _flat(px, py, pc)
    pos = jnp.stack([px, py, pc]).astype(jnp.int32)
    x2, target = x[0], loss_target[0]

    w_in_b, w_out_b = w_in.astype(BF16), w_out.astype(BF16)
    weights = allgather_weights(w_in_b, w_out_b)

    c_all = allgather_rows(c.reshape(D_MODEL // 128, 128), "allgather_c").reshape(N_DEV, D_MODEL)
    act, mod_part = ada_fwd(c_all, w_ada)
    mod_all = allgather_rows(mod_part.reshape(-1, 128), "allgather_mod").reshape(N_DEV, DEPTH, N_DEV, W_ADA_SHARD)
    mod = lax.dynamic_index_in_dim(mod_all, me, axis=2, keepdims=False)
    mod = mod.transpose(1, 0, 2).reshape(DEPTH, 3 * D_MODEL) + b_ada
    shift, scale, gate = mod[:, :D_MODEL], mod[:, D_MODEL:2 * D_MODEL], mod[:, 2 * D_MODEL:]

    lg = jnp.concatenate([jax.nn.log_sigmoid(ret_decay_logit_f), jax.nn.log_sigmoid(ret_decay_logit_b)], axis=1)

    h = x2
    saved, layer_weights = [], []
    for l in range(DEPTH):
        layer_weights.append(weights)
        gather = (w_in_b, w_out_b, l + 1) if l + 1 < DEPTH else None
        h, sv, weights = layer_fwd(h, norm_gain[l:l + 1], scale[l:l + 1], shift[l:l + 1], gate[l:l + 1],
                                   *layer_weights[l], lg[l], gather)
        saved.append(sv)
    dh, loss_part, dfinal = loss_head(h, final_gain.reshape(1, D_MODEL), target)

    dmod, dgain, dlg, reduced = [None] * DEPTH, [None] * DEPTH, [None] * DEPTH, [None] * DEPTH
    for l in reversed(range(DEPTH)):
        dh, dshift, dscale, dgate, dg, dlg[l], reduced[l] = layer_bwd(
            dh, saved[l], norm_gain[l:l + 1], scale[l:l + 1], gate[l:l + 1], *layer_weights[l], lg[l], pos)
        dmod[l] = jnp.concatenate([dshift, dscale, dgate], axis=1)
        dgain[l] = dg

    dlg = jnp.stack(dlg)
    dlogit_f = dlg[:, :RET_HEADS] * jax.nn.sigmoid(-ret_decay_logit_f)
    dlogit_b = dlg[:, RET_HEADS:] * jax.nn.sigmoid(-ret_decay_logit_b)
    packed = _pack_small(jnp.concatenate(dmod, axis=0), jnp.concatenate(dgain, axis=0), dfinal, dlogit_f, dlogit_b,
                         loss=loss_part[0, 0])
    gathered = allgather_rows(packed, "allgather_small")
    small = small_update(gathered,
                         _pack_small(b_ada, norm_gain, final_gain, ret_decay_logit_f, ret_decay_logit_b),
                         _pack_small(m_b_ada, m_norm_gain, m_final_gain, m_ret_decay_logit_f, m_ret_decay_logit_b),
                         _pack_small(v_b_ada, v_norm_gain, v_final_gain, v_ret_decay_logit_f, v_ret_decay_logit_b))
    loss = small[0][ROWS_B_ADA + ROWS_GAIN + ROWS_FINAL + 1, 0]
    (g_b_ada, g_gain, g_final, g_dec_f, g_dec_b), (d_b_ada, d_gain, d_final, d_dec_f, d_dec_b), \
        (m_b_ada2, m_gain2, m_final2, m_dec_f2, m_dec_b2), (v_b_ada2, v_gain2, v_final2, v_dec_f2, v_dec_b2) = \
        [_unpack_small(p) for p in small]

    dmod_all = gathered[:, :ROWS_B_ADA].reshape(N_DEV, DEPTH, 3 * D_MODEL)
    dmod_mine = lax.dynamic_slice_in_dim(dmod_all, me * W_ADA_SHARD, W_ADA_SHARD, axis=2).transpose(1, 0, 2)
    g_w_ada, d_w_ada, m_w_ada2, v_w_ada2 = ada_update(act.T, dmod_mine, w_ada, m_w_ada, v_w_ada)

    upd_in = upd_out = None
    for l in reversed(range(DEPTH)):
        upd_in = shard_update(l, reduced[l][0][0], reduced[l][1][0], w_in, m_w_in, v_w_in, upd_in, f"w_in_update_{l}")
        upd_out = shard_update(l, reduced[l][0][1], reduced[l][1][1], w_out, m_w_out, v_w_out, upd_out, f"w_out_update_{l}")
    g_w_in, d_w_in, m_w_in2, v_w_in2 = upd_in
    g_w_out, d_w_out, m_w_out2, v_w_out2 = upd_out

    return (loss, dh[None],
            g_gain, g_w_ada, g_b_ada, g_w_in, g_w_out, g_dec_f, g_dec_b, g_final,
            d_gain, d_w_ada, d_b_ada, d_w_in, d_w_out, d_dec_f, d_dec_b, d_final,
            m_gain2, m_w_ada2, m_b_ada2, m_w_in2, m_w_out2, m_dec_f2, m_dec_b2, m_final2,
            v_gain2, v_w_ada2, v_b_ada2, v_w_in2, v_w_out2, v_dec_f2, v_dec_b2, v_final2)
```

```python
import functools
import math

import jax
import jax.numpy as jnp
from jax import lax
from jax.experimental import pallas as pl
from jax.experimental.pallas import tpu as pltpu

F32, BF16 = jnp.float32, jnp.bfloat16

D_MODEL = 2048
DEPTH = 4
N_DEV = 8
ATTN_WIDTH = 1024
HEAD_DIM = 128
N_HEADS_ATTN = 8
DILATIONS = (1, 4, 16)
RADIUS = 64
RET_HEADS = 4
RET_QK = 128
RET_V = 256
RET_CHUNK = 256
IN_W = 7168
QKV_A = 3 * ATTN_WIDTH
COL_ZA, COL_QR, COL_KR, COL_VR, COL_ZR = 3072, 4096, 4608, 5120, 6144
W_IN_SHARD = IN_W // N_DEV
W_OUT_SHARD = D_MODEL // N_DEV
W_ADA_SHARD = 3 * D_MODEL // N_DEV
NORM_EPS = 1e-6
MASK_VALUE = -1e30
ATTN_SCALE = HEAD_DIM ** -0.5
RET_SCALE = RET_QK ** -0.5
LN2 = math.log(2.0)

ADAM_LR, ADAM_B1, ADAM_B2, ADAM_EPS, ADAM_WD, ADAM_STEP = 0.001, 0.9, 0.999, 1e-08, 0.01, 10
ADAM_C1 = 1.0 / (1.0 - ADAM_B1 ** ADAM_STEP)
ADAM_C2 = 1.0 / (1.0 - ADAM_B2 ** ADAM_STEP)

VMEM_LIMIT_BYTES = 56 * 1024 * 1024
MESH = pl.DeviceIdType.MESH


def _params(*sem):
    return pltpu.CompilerParams(dimension_semantics=sem if sem else None, vmem_limit_bytes=VMEM_LIMIT_BYTES)


def _dot(a, b):
    return jnp.dot(a, b, preferred_element_type=F32)


def _dot_nt(a, b):
    return lax.dot_general(a, b, (((1,), (1,)), ((), ())), preferred_element_type=F32)


def _dot_tn(a, b):
    return lax.dot_general(a, b, (((0,), (0,)), ((), ())), preferred_element_type=F32)


def _iota(shape, dim):
    return lax.broadcasted_iota(jnp.int32, shape, dim)


def _sigmoid(z):
    return 1.0 / (1.0 + jnp.exp(-z))


HBM = pl.BlockSpec(memory_space=pl.ANY)


def _position():
    return lax.axis_index("x"), lax.axis_index("y"), lax.axis_index("c")


def _flat(px, py, pc):
    return 4 * px + 2 * py + pc


def _remote(src, dst, send_sem, recv_sem, to):
    return pltpu.make_async_remote_copy(src_ref=src, dst_ref=dst, send_sem=send_sem, recv_sem=recv_sem,
                                        device_id=to, device_id_type=MESH)


def _weight_blocks(fin_ref, fout_ref):
    def block(a, idx):
        if a == 0:
            return fin_ref.at[:, pl.ds(pl.multiple_of(idx * W_IN_SHARD, 128), W_IN_SHARD)]
        return fout_ref.at[pl.ds(pl.multiple_of(idx * W_OUT_SHARD, W_OUT_SHARD), W_OUT_SHARD), :]
    return block


GATHER1_SEMS = [pltpu.SemaphoreType.DMA((8,)), pltpu.SemaphoreType.DMA((8,)), pltpu.SemaphoreType.DMA((2,))]
GATHER2_SEMS = [pltpu.SemaphoreType.DMA((6,)), pltpu.SemaphoreType.DMA((6,))]
REDUCE1_SEMS = [pltpu.SemaphoreType.DMA((2,)), pltpu.SemaphoreType.DMA((2,))]
REDUCE2_SEMS = [pltpu.SemaphoreType.DMA((6,)), pltpu.SemaphoreType.DMA((6,))]


def _gather_phase1(srcs, block, send_sems, recv_sems, local_sems):
    x, y, c = _position()
    mine = [block(a, _flat(x, y, c)) for a in range(2)]
    copies = [pltpu.make_async_copy(srcs[a], mine[a], local_sems.at[a]) for a in range(2)]
    for a in range(2):
        copies.append(_remote(srcs[a], mine[a], send_sems.at[4 * a], recv_sems.at[4 * a], (x, y, 1 - c)))
        for j, (px, py) in enumerate([(1 - x, y), (x, 1 - y), (1 - x, 1 - y)]):
            copies.append(_remote(srcs[a], mine[a], send_sems.at[4 * a + 1 + j], recv_sems.at[4 * a + 1 + j], (px, py, c)))
    return copies


def _gather_phase2(block, send_sems, recv_sems):
    x, y, c = _position()
    copies = []
    for a in range(2):
        for j, (px, py) in enumerate([(1 - x, y), (x, 1 - y), (1 - x, 1 - y)]):
            blk = block(a, _flat(px, py, c))
            copies.append(_remote(blk, blk, send_sems.at[3 * a + j], recv_sems.at[3 * a + j], (x, y, 1 - c)))
    return copies


def _reduce_phase1(grads, landings, send_sems, recv_sems):
    x, y, c = _position()
    return [_remote(g.at[:, 1 - c], r, send_sems.at[a], recv_sems.at[a], (x, y, 1 - c))
            for a, (g, r) in enumerate(zip(grads, landings))]


def _reduce_phase2(sums, landings, send_sems, recv_sems):
    x, y, c = _position()
    copies = []
    for a, (p, r) in enumerate(zip(sums, landings)):
        for k in (1, 2, 3):
            to = (1 - x if k & 2 else x, 1 - y if k & 1 else y, c)
            copies.append(_remote(p.at[k], r.at[k - 1], send_sems.at[3 * a + k - 1], recv_sems.at[3 * a + k - 1], to))
    return copies


def _landing_shapes(n):
    return [jax.ShapeDtypeStruct((n, D_MODEL, W_IN_SHARD), BF16), jax.ShapeDtypeStruct((n, W_OUT_SHARD, D_MODEL), BF16)]


def _split_cores(slabs):
    return tuple(s.reshape((N_DEV // 2, 2) + s.shape[1:]) for s in slabs)


def _start_all(copies):
    for cp in copies:
        cp.start()


def _wait_all(copies):
    for cp in copies:
        cp.wait()


def _grid_edge(n_axes):
    first = last = None
    for ax in range(n_axes):
        f = pl.program_id(ax) == 0
        e = pl.program_id(ax) == pl.num_programs(ax) - 1
        first = f if first is None else first & f
        last = e if last is None else last & e
    return first, last


LANES = 128


def _stage_shape(rows, width):
    return pltpu.VMEM((width // LANES, rows, LANES), F32)


def _fill_stage(stage_ref, value):
    for t in range(stage_ref.shape[0]):
        stage_ref[t] = value[:, t * LANES:(t + 1) * LANES]


def _read_stage(stage_ref):
    return jnp.concatenate([stage_ref[t] for t in range(stage_ref.shape[0])], axis=1)


def _split_classes(stage_ref, out_refs):
    n_t, rows, _ = stage_ref.shape
    for out_ref in out_refs:
        dil = out_ref.shape[0]
        for r in range(dil):
            for t in range(n_t):
                piece = stage_ref[t, pl.ds(r, rows // dil, stride=dil), :]
                out_ref[r, :, t * LANES:(t + 1) * LANES] = piece.astype(out_ref.dtype)


def _merge_classes(in_ref, stage_ref):
    dil, per = in_ref.shape[0], in_ref.shape[1]
    for r in range(dil):
        for t in range(stage_ref.shape[0]):
            stage_ref[t, pl.ds(r, per, stride=dil), :] = in_ref[r, :, t * LANES:(t + 1) * LANES].astype(F32)


def _class_block(dil, tm, width, index_map):
    return pl.BlockSpec((dil, tm // dil, width), index_map)


def inproj_fwd(x, g, scale1p, shift, w, gather=None):
    s_len = x.shape[0]
    tm, tn = min(1024, s_len), 512

    n_qkv = QKV_A // tn

    def body(x_ref, g_ref, sc_ref, sh_ref, w_ref, *rest):
        if gather is not None:
            (win_ref, wout_ref, proj_ref, h_ref, q4_ref, q16_ref, fin_ref, fout_ref, stage,
             send_sems, recv_sems, local_sems) = rest
            first, last = _grid_edge(2)
            copies = _gather_phase1([win_ref.at[gather[2]], wout_ref.at[gather[2]]], _weight_blocks(fin_ref, fout_ref),
                                    send_sems, recv_sems, local_sems)
            pl.when(first)(lambda: _start_all(copies))
        else:
            proj_ref, h_ref, q4_ref, q16_ref, stage = rest

        @pl.when(pl.program_id(1) == 0)
        def _():
            xv = x_ref[...]
            r = lax.rsqrt(jnp.mean(xv * xv, axis=-1, keepdims=True) + NORM_EPS)
            h_ref[...] = ((xv * r * g_ref[...]) * sc_ref[...] + sh_ref[...]).astype(BF16)
        res = _dot(h_ref[...], w_ref[...])
        proj_ref[...] = res.astype(BF16)

        @pl.when(pl.program_id(1) < n_qkv)
        def _():
            _fill_stage(stage, res)
            _split_classes(stage, [q4_ref, q16_ref])
        if gather is not None:
            pl.when(last)(lambda: _wait_all(copies))

    vec = pl.BlockSpec((1, D_MODEL), lambda i, j: (0, 0))
    in_specs = [pl.BlockSpec((tm, D_MODEL), lambda i, j: (i, 0)), vec, vec, vec,
                pl.BlockSpec((D_MODEL, tn), lambda i, j: (0, j))]
    out_specs = [pl.BlockSpec((tm, tn), lambda i, j: (i, j)), pl.BlockSpec((tm, D_MODEL), lambda i, j: (i, 0))]
    out_shape = [jax.ShapeDtypeStruct((s_len, IN_W), BF16), jax.ShapeDtypeStruct((s_len, D_MODEL), BF16)]
    for dil in DILATIONS[1:]:
        out_specs.append(pl.BlockSpec((dil, tm // dil, tn), lambda i, j: (0, i, jnp.minimum(j, n_qkv - 1))))
        out_shape.append(jax.ShapeDtypeStruct((dil, s_len // dil, QKV_A), BF16))
    scratch = [_stage_shape(tm, tn)]
    extra = ()
    if gather is not None:
        in_specs += [HBM, HBM]
        out_specs += [HBM, HBM]
        out_shape += [jax.ShapeDtypeStruct((D_MODEL, IN_W), BF16), jax.ShapeDtypeStruct((D_MODEL, D_MODEL), BF16)]
        scratch += GATHER1_SEMS
        extra = tuple(gather[:2])
    return pl.pallas_call(
        body, name="inproj_fwd_gather" if gather is not None else "inproj_fwd", grid=(s_len // tm, IN_W // tn),
        in_specs=in_specs, out_specs=out_specs, out_shape=out_shape, scratch_shapes=scratch,
        compiler_params=_params("arbitrary", "arbitrary"),
    )(x, g, scale1p, shift, w, *extra)


MASK_DISTANCE = 1e33
ATTN_TILE = 128


ATTN_UNROLL_FWD, ATTN_UNROLL_BWD = 15, 10


def _attn_plan(sub_len, unroll):
    tq = min(sub_len, ATTN_TILE)
    win = min(sub_len, tq + 2 * RADIUS)
    heads = 1 if sub_len > 1024 else (2 if sub_len > 256 else N_HEADS_ATTN)
    return tq, win, sub_len // tq, heads, unroll


def _attn_tiles(sub_len, tq, win, n_tiles, unroll, tile):
    tile(0, 0, 0)
    if n_tiles > 2:
        def mid(i, carry):
            q0 = pl.multiple_of(i * tq, tq)
            tile(q0, pl.multiple_of(q0 - RADIUS, RADIUS), 1)
            return carry
        lax.fori_loop(1, n_tiles - 1, mid, 0, unroll=min(unroll, n_tiles - 2))
    if n_tiles > 1:
        tile(sub_len - tq, sub_len - win, 2)


def _attn_bias(bias_ref, head, heads, tq, win, dil):
    h = pl.program_id(1) * heads + head
    slope = jnp.exp(-(h + 1).astype(F32) * LN2 * jnp.ones((1, 1), F32))
    rel = _iota((tq, win), 1) - _iota((tq, win), 0)
    for v, off in enumerate((0, RADIUS, win - tq)):
        dist = jnp.abs(rel - off)
        bias_ref[v] = slope * jnp.where(dist <= RADIUS, (dist * dil).astype(F32), MASK_DISTANCE)


def _attn_specs(n_cls, sub_len, heads):
    width = heads * HEAD_DIM
    per = ATTN_WIDTH // width

    def col(part):
        return pl.BlockSpec((None, sub_len, width), lambda r, g: (r, 0, part * per + g))
    return col, (n_cls, N_HEADS_ATTN // heads)


def _stat_spec(sub_len):
    return pl.BlockSpec((None, sub_len, LANES), lambda r, g: (r, 0, 0))


def _stat_column(ref, rows, lane, h_abs):
    return jnp.sum(jnp.where(lane == h_abs, ref[rows, :], 0.0), axis=1, keepdims=True)


def attn_fwd(qkv, dil):
    n_cls, sub_len, _ = qkv.shape
    tq, win, n_tiles, heads, unroll = _attn_plan(sub_len, ATTN_UNROLL_FWD)

    def body(q_ref, k_ref, v_ref, o_ref, lse_ref, bias_ref):
        @pl.when(pl.program_id(1) == 0)
        def _():
            lse_ref[...] = jnp.zeros_like(lse_ref)
        lane = _iota((tq, LANES), 1)
        for head in range(heads):
            lanes = slice(head * HEAD_DIM, (head + 1) * HEAD_DIM)
            h_abs = pl.program_id(1) * heads + head
            _attn_bias(bias_ref, head, heads, tq, win, dil)

            def tile(q0, start, variant):
                s = _dot_nt(q_ref[pl.ds(q0, tq), lanes], k_ref[pl.ds(start, win), lanes]) * ATTN_SCALE - bias_ref[variant]
                m = jnp.max(s, axis=1, keepdims=True)
                p = jnp.exp(s - m)
                den = jnp.sum(p, axis=1, keepdims=True)
                o_ref[pl.ds(q0, tq), lanes] = _dot(p.astype(BF16), v_ref[pl.ds(start, win), lanes]) / den
                rows = pl.ds(q0, tq)
                lse_ref[rows, :] = jnp.where(lane == h_abs, m + jnp.log(den), lse_ref[rows, :])

            _attn_tiles(sub_len, tq, win, n_tiles, unroll, tile)

    col, grid = _attn_specs(n_cls, sub_len, heads)
    return pl.pallas_call(
        body, name=f"attn_fwd_d{dil}", grid=grid,
        in_specs=[col(0), col(1), col(2)], out_specs=[col(0), _stat_spec(sub_len)],
        out_shape=[jax.ShapeDtypeStruct((n_cls, sub_len, ATTN_WIDTH), F32), jax.ShapeDtypeStruct((n_cls, sub_len, LANES), F32)],
        scratch_shapes=[pltpu.VMEM((3, tq, win), F32)],
        compiler_params=_params("arbitrary", "arbitrary"),
    )(qkv, qkv, qkv)


def attn_bwd(qkv, do, lse, delta, dil):
    n_cls, sub_len, _ = qkv.shape
    tq, win, n_tiles, heads, unroll = _attn_plan(sub_len, ATTN_UNROLL_BWD)

    def body(q_ref, k_ref, v_ref, do_ref, lse_ref, dl_ref, dq_ref, dk_ref, dv_ref, bias_ref, dk_acc, dv_acc):
        lane = _iota((tq, LANES), 1)
        for head in range(heads):
            lanes = slice(head * HEAD_DIM, (head + 1) * HEAD_DIM)
            h_abs = pl.program_id(1) * heads + head
            _attn_bias(bias_ref, head, heads, tq, win, dil)
            dk_acc[...] = jnp.zeros_like(dk_acc)
            dv_acc[...] = jnp.zeros_like(dv_acc)

            def tile(q0, start, variant):
                q = q_ref[pl.ds(q0, tq), lanes]
                k = k_ref[pl.ds(start, win), lanes]
                v = v_ref[pl.ds(start, win), lanes]
                dov = do_ref[pl.ds(q0, tq), lanes]
                s = _dot_nt(q, k) * ATTN_SCALE - bias_ref[variant]
                rows = pl.ds(q0, tq)
                p = jnp.exp(s - _stat_column(lse_ref, rows, lane, h_abs))
                ds = (p * (_dot_nt(dov, v) - _stat_column(dl_ref, rows, lane, h_abs))).astype(BF16)
                dq_ref[pl.ds(q0, tq), lanes] = (_dot(ds, k) * ATTN_SCALE).astype(BF16)
                dk_acc[pl.ds(start, win), :] += _dot_tn(ds, q) * ATTN_SCALE
                dv_acc[pl.ds(start, win), :] += _dot_tn(p.astype(BF16), dov)

            _attn_tiles(sub_len, tq, win, n_tiles, unroll, tile)
            dk_ref[:, lanes] = dk_acc[...].astype(BF16)
            dv_ref[:, lanes] = dv_acc[...].astype(BF16)

    col, grid = _attn_specs(n_cls, sub_len, heads)
    out = jax.ShapeDtypeStruct((n_cls, sub_len, ATTN_WIDTH), BF16)
    return pl.pallas_call(
        body, name=f"attn_bwd_d{dil}", grid=grid,
        in_specs=[col(0), col(1), col(2), col(0), _stat_spec(sub_len), _stat_spec(sub_len)],
        out_specs=[col(0), col(0), col(0)], out_shape=[out, out, out],
        scratch_shapes=[pltpu.VMEM((3, tq, win), F32), pltpu.VMEM((sub_len, HEAD_DIM), F32),
                        pltpu.VMEM((sub_len, HEAD_DIM), F32)],
        compiler_params=_params("arbitrary", "arbitrary"),
    )(qkv, qkv, qkv, do, lse, delta)


RET_UNROLL = 8
RET_UNROLL_BWD = 8


def _ret_tables(lg_ref):
    h = pl.program_id(0)
    one = jnp.ones((1, 1), F32)
    lgf, lgb = lg_ref[h] * one, lg_ref[RET_HEADS + h] * one
    c = RET_CHUNK
    rel = (_iota((c, c), 0) - _iota((c, c), 1)).astype(F32)
    dec_f = jnp.where(rel >= 0, jnp.exp(jnp.maximum(rel, 0.0) * lgf), 0.0)
    dec_b = jnp.where(rel <= 0, jnp.exp(jnp.maximum(-rel, 0.0) * lgb), 0.0)
    ci = _iota((c, 1), 0).astype(F32)
    tab = dict(rel=rel, dec_f=dec_f, dec_b=dec_b, ci=ci,
               xi_f=jnp.exp((ci + 1.0) * lgf), ze_f=jnp.exp((c - 1.0 - ci) * lgf), g_f=jnp.exp(c * lgf),
               xi_b=jnp.exp((c - ci) * lgb), ze_b=jnp.exp(ci * lgb), g_b=jnp.exp(c * lgb))
    return tab


def _ret_specs(s_len):
    q = pl.BlockSpec((s_len, RET_QK), lambda h: (0, COL_QR // RET_QK + h))
    k = pl.BlockSpec((s_len, RET_QK), lambda h: (0, COL_KR // RET_QK + h))
    v = pl.BlockSpec((s_len, RET_V), lambda h: (0, COL_VR // RET_V + h))
    wide = pl.BlockSpec((s_len, RET_V), lambda h: (0, h))
    narrow = pl.BlockSpec((s_len, RET_QK), lambda h: (0, h))
    smem = pl.BlockSpec(memory_space=pltpu.SMEM)
    return smem, q, k, v, wide, narrow


def ret_fwd(proj, lg, finish=None):
    s_len = proj.shape[0]
    c, n_chunks = RET_CHUNK, proj.shape[0] // RET_CHUNK

    def body(lg_ref, q_ref, k_ref, v_ref, *rest):
        if finish is not None:
            _, _, opre_ref, y_ref, fin_ref, fout_ref, st_f, st_b, send_sems, recv_sems = rest
            first, last = _grid_edge(1)
            copies = _gather_phase2(_weight_blocks(fin_ref, fout_ref), send_sems, recv_sems)
            pl.when(first)(lambda: _start_all(copies))
        else:
            opre_ref, y_ref, st_f, st_b = rest
        t = _ret_tables(lg_ref)
        dec = t["dec_f"] + t["dec_b"]
        st_f[...] = jnp.zeros_like(st_f)
        st_b[...] = jnp.zeros_like(st_b)

        def load(n):
            r0 = pl.multiple_of(n * c, c)
            q, k, v = q_ref[pl.ds(r0, c), :], k_ref[pl.ds(r0, c), :], v_ref[pl.ds(r0, c), :]
            return r0, q, (k.astype(F32) * RET_SCALE), v

        def fwd(n, carry):
            r0, q, kf, v = load(n)
            inner = (_dot_nt(q, kf.astype(BF16)) * dec).astype(BF16)
            opre_ref[pl.ds(r0, c), :] = _dot(inner, v) + _dot(q, st_f[...].astype(BF16)) * t["xi_f"]
            st_f[...] = st_f[...] * t["g_f"] + _dot_tn((kf * t["ze_f"]).astype(BF16), v)
            return carry

        def bwd(i, carry):
            r0, q, kf, v = load(n_chunks - 1 - i)
            o = opre_ref[pl.ds(r0, c), :] + _dot(q, st_b[...].astype(BF16)) * t["xi_b"]
            st_b[...] = st_b[...] * t["g_b"] + _dot_tn((kf * t["ze_b"]).astype(BF16), v)
            opre_ref[pl.ds(r0, c), :] = o
            y_ref[pl.ds(r0, c), :] = o * lax.rsqrt(jnp.mean(o * o, axis=-1, keepdims=True) + NORM_EPS)
            return carry

        lax.fori_loop(0, n_chunks, fwd, 0, unroll=min(RET_UNROLL, n_chunks))
        lax.fori_loop(0, n_chunks, bwd, 0, unroll=min(RET_UNROLL, n_chunks))
        if finish is not None:
            pl.when(last)(lambda: _wait_all(copies))

    smem, q, k, v, wide, _ = _ret_specs(s_len)
    out = jax.ShapeDtypeStruct((s_len, RET_HEADS * RET_V), F32)
    in_specs, out_specs, out_shape = [smem, q, k, v], [wide, wide], [out, out]
    scratch = [pltpu.VMEM((RET_QK, RET_V), F32), pltpu.VMEM((RET_QK, RET_V), F32)]
    aliases, extra = {}, ()
    if finish is not None:
        in_specs += [HBM, HBM]
        out_specs += [HBM, HBM]
        out_shape += _full_weight_shapes()
        scratch += GATHER2_SEMS
        aliases, extra = {4: 2, 5: 3}, tuple(finish)
    return pl.pallas_call(
        body, name="ret_fwd_gather2" if finish is not None else "ret_fwd", grid=(RET_HEADS,), in_specs=in_specs,
        out_specs=out_specs, out_shape=out_shape, scratch_shapes=scratch, input_output_aliases=aliases,
        compiler_params=_params("arbitrary"),
    )(lg, proj, proj, proj, *extra)


def ret_bwd(proj, lg, o_pre, dy):
    s_len = proj.shape[0]
    c, n_chunks = RET_CHUNK, proj.shape[0] // RET_CHUNK
    cf = float(c)

    def body(lg_ref, q_ref, k_ref, v_ref, o_ref, dy_ref, dq_ref, dk_ref, dv_ref, glf_ref, glb_ref,
             st_f, dst_b, st_b, dst_f, keep_sf, keep_dtb, acc_f, acc_b, acc_sf, acc_sb):
        t = _ret_tables(lg_ref)
        dec = t["dec_f"] + t["dec_b"]
        e_f, e_b, ci = t["rel"] * t["dec_f"], -t["rel"] * t["dec_b"], t["ci"]
        for ref in (st_f, dst_b, st_b, dst_f, acc_f, acc_b, acc_sf, acc_sb):
            ref[...] = jnp.zeros_like(ref)

        def load(n):
            r0 = pl.multiple_of(n * c, c)
            q, k, v = q_ref[pl.ds(r0, c), :], k_ref[pl.ds(r0, c), :], v_ref[pl.ds(r0, c), :]
            o, dyv = o_ref[pl.ds(r0, c), :], dy_ref[pl.ds(r0, c), :]
            rr = lax.rsqrt(jnp.mean(o * o, axis=-1, keepdims=True) + NORM_EPS)
            y = o * rr
            do = (rr * (dyv - y * jnp.mean(dyv * y, axis=-1, keepdims=True))).astype(BF16)
            return r0, q, k.astype(F32) * RET_SCALE, v, do

        def fold(x):
            return functools.reduce(lambda a, b: a + b, [x[:, i * RET_QK:(i + 1) * RET_QK] for i in range(c // RET_QK)])

        def fwd(n, carry):
            r0, q, kf, v, do = load(n)
            qf, kb = q.astype(F32), kf.astype(BF16)
            a = _dot_nt(q, kb)
            b = _dot_nt(do, v)
            da = (b * dec).astype(BF16)
            ab = a * b
            sf_b, dtb_b = st_f[...].astype(BF16), dst_b[...].astype(BF16)
            dq_inter = _dot_nt(do, sf_b) * t["xi_f"]
            dk_inter = _dot_nt(v, dtb_b) * t["ze_b"]
            acc_f[...] += fold(e_f * ab) + (ci + 1.0) * (qf * dq_inter)
            acc_b[...] += fold(e_b * ab) + ci * (kf * dk_inter)
            dq_ref[pl.ds(r0, c), :] = _dot(da, kb) + dq_inter
            dk_ref[pl.ds(r0, c), :] = _dot_tn(da, q) + dk_inter
            dv_ref[pl.ds(r0, c), :] = _dot_tn((a * dec).astype(BF16), do) + _dot((kf * t["ze_b"]).astype(BF16), dtb_b)
            keep_sf[n] = sf_b
            keep_dtb[n] = dtb_b
            st_f[...] = st_f[...] * t["g_f"] + _dot_tn((kf * t["ze_f"]).astype(BF16), v)
            dst_b[...] = dst_b[...] * t["g_b"] + _dot_tn((qf * t["xi_b"]).astype(BF16), do)
            return carry

        def bwd(i, carry):
            n = n_chunks - 1 - i
            r0, q, kf, v, do = load(n)
            qf = q.astype(F32)
            tb_b, dsf_b = st_b[...].astype(BF16), dst_f[...].astype(BF16)
            dq_inter = _dot_nt(do, tb_b) * t["xi_b"]
            dk_inter = _dot_nt(v, dsf_b) * t["ze_f"]
            acc_b[...] += (cf - ci) * (qf * dq_inter)
            acc_f[...] += (cf - 1.0 - ci) * (kf * dk_inter)
            acc_sb[...] += keep_dtb[n].astype(F32) * st_b[...]
            acc_sf[...] += dst_f[...] * keep_sf[n].astype(F32)
            dq_ref[pl.ds(r0, c), :] += dq_inter
            dk_ref[pl.ds(r0, c), :] = (dk_ref[pl.ds(r0, c), :] + dk_inter) * RET_SCALE
            dv_ref[pl.ds(r0, c), :] += _dot((kf * t["ze_f"]).astype(BF16), dsf_b)
            st_b[...] = st_b[...] * t["g_b"] + _dot_tn((kf * t["ze_b"]).astype(BF16), v)
            dst_f[...] = dst_f[...] * t["g_f"] + _dot_tn((qf * t["xi_f"]).astype(BF16), do)
            return carry

        lax.fori_loop(0, n_chunks, fwd, 0, unroll=min(RET_UNROLL_BWD, n_chunks))
        lax.fori_loop(0, n_chunks, bwd, 0, unroll=min(RET_UNROLL_BWD, n_chunks))

        def total(x):
            return jnp.sum(jnp.sum(x, axis=1, keepdims=True), axis=0, keepdims=True)

        glf_ref[...] = jnp.broadcast_to(total(acc_f[...]) + cf * t["g_f"] * total(acc_sf[...]), (8, 128))
        glb_ref[...] = jnp.broadcast_to(total(acc_b[...]) + cf * t["g_b"] * total(acc_sb[...]), (8, 128))

    smem, q, k, v, wide, narrow = _ret_specs(s_len)
    scal = pl.BlockSpec((None, 8, 128), lambda h: (h, 0, 0))
    state = pltpu.VMEM((RET_QK, RET_V), F32)
    square = pltpu.VMEM((RET_CHUNK, RET_QK), F32)
    keep = pltpu.VMEM((n_chunks, RET_QK, RET_V), BF16)
    return pl.pallas_call(
        body, name="ret_bwd", grid=(RET_HEADS,), in_specs=[smem, q, k, v, wide, wide],
        out_specs=[narrow, narrow, wide, scal, scal],
        out_shape=[jax.ShapeDtypeStruct((s_len, RET_HEADS * RET_QK), F32), jax.ShapeDtypeStruct((s_len, RET_HEADS * RET_QK), F32),
                   jax.ShapeDtypeStruct((s_len, RET_HEADS * RET_V), F32),
                   jax.ShapeDtypeStruct((RET_HEADS, 8, 128), F32), jax.ShapeDtypeStruct((RET_HEADS, 8, 128), F32)],
        scratch_shapes=[state, state, state, state, keep, keep, square, square, state, state],
        compiler_params=_params("arbitrary"),
    )(lg, proj, proj, proj, o_pre, dy)


def _silu_parts(z):
    sig = _sigmoid(z)
    return z * sig, sig * (1.0 + z * (1.0 - sig))


def outproj_fwd(x, gate, w_out, attn_outs, y_r, proj):
    s_len = x.shape[0]
    tm = min(256, s_len)

    def per_head(w):
        return jnp.concatenate([jnp.broadcast_to(w[:, h:h + 1], (tm, HEAD_DIM)) for h in range(N_HEADS_ATTN)], axis=1)

    def body(x_ref, gate_ref, w_ref, o1, l1, o2, l2, o3, l3, yr_ref, za_ref, zr_ref,
             xn_ref, oa_ref, lse_ref, lse4_ref, lse16_ref, so2, sl2, so3, sl3):
        silu_r, _ = _silu_parts(zr_ref[...].astype(F32))
        out_r = _dot((yr_ref[...] * silu_r).astype(BF16), w_ref[ATTN_WIDTH:, :])
        for src, dst in ((o2, so2), (l2, sl2), (o3, so3), (l3, sl3)):
            _merge_classes(src, dst)
        la, lb, lc = l1[...], _read_stage(sl2), _read_stage(sl3)
        m = jnp.maximum(jnp.maximum(la, lb), lc)
        lse = m + jnp.log(jnp.exp(la - m) + jnp.exp(lb - m) + jnp.exp(lc - m))
        o_a = (per_head(jnp.exp(la - lse)) * o1[...] + per_head(jnp.exp(lb - lse)) * _read_stage(so2)
               + per_head(jnp.exp(lc - lse)) * _read_stage(so3))
        oa_ref[...] = o_a
        lse_ref[...] = lse
        _fill_stage(sl2, lse)
        _split_classes(sl2, [lse4_ref, lse16_ref])
        silu_a, _ = _silu_parts(za_ref[...].astype(F32))
        out_a = _dot((o_a * silu_a).astype(BF16), w_ref[:ATTN_WIDTH, :])
        xn_ref[...] = x_ref[...] + gate_ref[...] * (out_a + out_r)

    row = lambda w: pl.BlockSpec((tm, w), lambda i: (i, 0))
    half, stat = row(ATTN_WIDTH), row(LANES)
    cls = [_class_block(dil, tm, ATTN_WIDTH, lambda i: (0, i, 0)) for dil in DILATIONS[1:]]
    cls_stat = [_class_block(dil, tm, LANES, lambda i: (0, i, 0)) for dil in DILATIONS[1:]]
    flat = [a for pair in attn_outs for a in pair]
    sds = jax.ShapeDtypeStruct
    return pl.pallas_call(
        body, name="outproj_fwd", grid=(s_len // tm,),
        in_specs=[row(D_MODEL), pl.BlockSpec((1, D_MODEL), lambda i: (0, 0)),
                  pl.BlockSpec((D_MODEL, D_MODEL), lambda i: (0, 0)),
                  half, stat, cls[0], cls_stat[0], cls[1], cls_stat[1], half,
                  pl.BlockSpec((tm, ATTN_WIDTH), lambda i: (i, COL_ZA // ATTN_WIDTH)),
                  pl.BlockSpec((tm, ATTN_WIDTH), lambda i: (i, COL_ZR // ATTN_WIDTH))],
        out_specs=[row(D_MODEL), half, stat] + cls_stat,
        out_shape=[sds((s_len, D_MODEL), F32), sds((s_len, ATTN_WIDTH), F32), sds((s_len, LANES), F32)]
                  + [sds((dil, s_len // dil, LANES), F32) for dil in DILATIONS[1:]],
        scratch_shapes=[_stage_shape(tm, ATTN_WIDTH), _stage_shape(tm, LANES)] * 2,
        compiler_params=_params("arbitrary"),
    )(x, gate, w_out, *flat, y_r, proj, proj)


def loss_head(x, gain, target):
    s_len = x.shape[0]
    tm = min(256, s_len)

    def body(x_ref, g_ref, t_ref, dx_ref, loss_ref, dg_ref):
        @pl.when(pl.program_id(0) == 0)
        def _():
            loss_ref[...] = jnp.zeros_like(loss_ref)
            dg_ref[...] = jnp.zeros_like(dg_ref)
        xv, g = x_ref[...], g_ref[...]
        r = lax.rsqrt(jnp.mean(xv * xv, axis=-1, keepdims=True) + NORM_EPS)
        xn = xv * r
        err = xn * g - t_ref[...]
        part = 0.5 * jnp.sum(jnp.mean(err * err, axis=-1, keepdims=True), axis=0, keepdims=True)
        loss_ref[...] += jnp.broadcast_to(part, loss_ref.shape)
        dy = err * (1.0 / D_MODEL)
        dg_ref[...] += jnp.sum(dy * xn, axis=0, keepdims=True)
        dxn = dy * g
        dx_ref[...] = r * (dxn - xn * jnp.mean(dxn * xn, axis=-1, keepdims=True))

    row = pl.BlockSpec((tm, D_MODEL), lambda i: (i, 0))
    vec = pl.BlockSpec((1, D_MODEL), lambda i: (0, 0))
    return pl.pallas_call(
        body, name="loss_head", grid=(s_len // tm,), in_specs=[row, vec, row],
        out_specs=[row, pl.BlockSpec((8, 128), lambda i: (0, 0)), vec],
        out_shape=[jax.ShapeDtypeStruct((s_len, D_MODEL), F32), jax.ShapeDtypeStruct((8, 128), F32),
                   jax.ShapeDtypeStruct((1, D_MODEL), F32)],
        compiler_params=_params("arbitrary"),
    )(x, gain, target)


def outproj_bwd(dxn, gate, w_out, o_a, y_r, proj):
    s_len = dxn.shape[0]
    tm = min(256, s_len)

    def body(dx_ref, gate_ref, w_ref, oa_ref, yr_ref, za_ref, zr_ref,
             doa_ref, dl_ref, dyr_ref, dza_ref, dzr_ref, y_ref, dxb_ref, do4, do16, dl4, dl16, stage, stat_stage):
        dxv = dx_ref[...]
        dxb_ref[...] = dxv.astype(BF16)
        dy = _dot_nt((dxv * gate_ref[...]).astype(BF16), w_ref[...])
        dy_a, dy_r = dy[:, :ATTN_WIDTH], dy[:, ATTN_WIDTH:]
        o_a, y_rv = oa_ref[...], yr_ref[...]
        silu_a, dsilu_a = _silu_parts(za_ref[...].astype(F32))
        silu_r, dsilu_r = _silu_parts(zr_ref[...].astype(F32))
        do_a = dy_a * silu_a
        doa_ref[...] = do_a.astype(BF16)
        _fill_stage(stage, do_a)
        _split_classes(stage, [do4, do16])
        prod = do_a * o_a
        lane = _iota((tm, LANES), 1)
        delta = jnp.zeros((tm, LANES), F32)
        for h in range(N_HEADS_ATTN):
            delta = jnp.where(lane == h, jnp.sum(prod[:, h * HEAD_DIM:(h + 1) * HEAD_DIM], axis=1, keepdims=True), delta)
        dl_ref[...] = delta
        _fill_stage(stat_stage, delta)
        _split_classes(stat_stage, [dl4, dl16])
        dyr_ref[...] = dy_r * silu_r
        dza_ref[...] = (dy_a * o_a * dsilu_a).astype(BF16)
        dzr_ref[...] = (dy_r * y_rv * dsilu_r).astype(BF16)
        y_ref[...] = jnp.concatenate([(o_a * silu_a).astype(BF16), (y_rv * silu_r).astype(BF16)], axis=1)

    row = lambda w: pl.BlockSpec((tm, w), lambda i: (i, 0))
    half = row(ATTN_WIDTH)
    sds = lambda w, dt: jax.ShapeDtypeStruct((s_len, w), dt)
    in_specs = [row(D_MODEL), pl.BlockSpec((1, D_MODEL), lambda i: (0, 0)),
                pl.BlockSpec((D_MODEL, D_MODEL), lambda i: (0, 0)), half, half,
                pl.BlockSpec((tm, ATTN_WIDTH), lambda i: (i, COL_ZA // ATTN_WIDTH)),
                pl.BlockSpec((tm, ATTN_WIDTH), lambda i: (i, COL_ZR // ATTN_WIDTH))]
    cls = [_class_block(dil, tm, ATTN_WIDTH, lambda i: (0, i, 0)) for dil in DILATIONS[1:]]
    cls_stat = [_class_block(dil, tm, LANES, lambda i: (0, i, 0)) for dil in DILATIONS[1:]]
    out_specs = [half, row(LANES), half, half, half, row(D_MODEL), row(D_MODEL)] + cls + cls_stat
    out_shape = [sds(ATTN_WIDTH, BF16), sds(LANES, F32), sds(ATTN_WIDTH, F32), sds(ATTN_WIDTH, BF16),
                 sds(ATTN_WIDTH, BF16), sds(D_MODEL, BF16), sds(D_MODEL, BF16)]
    out_shape += [jax.ShapeDtypeStruct((dil, s_len // dil, ATTN_WIDTH), BF16) for dil in DILATIONS[1:]]
    out_shape += [jax.ShapeDtypeStruct((dil, s_len // dil, LANES), F32) for dil in DILATIONS[1:]]
    return pl.pallas_call(
        body, name="outproj_bwd", grid=(s_len // tm,),
        in_specs=in_specs, out_specs=out_specs, out_shape=out_shape,
        scratch_shapes=[_stage_shape(tm, ATTN_WIDTH), _stage_shape(tm, LANES)],
        compiler_params=_params("arbitrary"),
    )(dxn, gate, w_out, o_a, y_r, proj, proj)


def wout_grad(y, dxb, gate, w_out):
    s_len = y.shape[0]
    tf, ts = 512, min(1024, s_len)

    def body(y_ref, dx_ref, gate_ref, w_ref, dw_ref, dgate_ref, acc):
        f, s = pl.program_id(0), pl.program_id(1)

        @pl.when((f == 0) & (s == 0))
        def _():
            dgate_ref[...] = jnp.zeros_like(dgate_ref)

        @pl.when(s == 0)
        def _():
            acc[...] = jnp.zeros_like(acc)
        acc[...] += _dot_tn(y_ref[...], dx_ref[...])

        @pl.when(s == pl.num_programs(1) - 1)
        def _():
            m = acc[...]
            dw_ref[...] = (m * gate_ref[...]).astype(BF16).reshape(dw_ref.shape)
            dgate_ref[...] += jnp.sum(m * w_ref[...].astype(F32), axis=0, keepdims=True)

    per = tf // W_OUT_SHARD
    return pl.pallas_call(
        body, name="wout_grad", grid=(D_MODEL // tf, s_len // ts),
        in_specs=[pl.BlockSpec((ts, tf), lambda f, s: (s, f)), pl.BlockSpec((ts, D_MODEL), lambda f, s: (s, 0)),
                  pl.BlockSpec((1, D_MODEL), lambda f, s: (0, 0)), pl.BlockSpec((tf, D_MODEL), lambda f, s: (f, 0))],
        out_specs=[pl.BlockSpec((per, W_OUT_SHARD, D_MODEL), lambda f, s: (f, 0, 0)),
                   pl.BlockSpec((1, D_MODEL), lambda f, s: (0, 0))],
        out_shape=[jax.ShapeDtypeStruct((N_DEV, W_OUT_SHARD, D_MODEL), BF16), jax.ShapeDtypeStruct((1, D_MODEL), F32)],
        scratch_shapes=[pltpu.VMEM((tf, D_MODEL), F32)],
        compiler_params=_params("arbitrary", "arbitrary"),
    )(y, dxb, gate, w_out)


def assemble_dproj(dqkv_a, dz_a, dq_r, dk_r, dv_r, dz_r):
    s_len = dz_a.shape[0]
    tm = min(512, s_len)

    def body(*refs):
        pat, (dza, dqr, dkr, dvr, dzr, out, stage) = refs[:9], refs[9:]
        for t in range(3):
            tot = pat[t][...].astype(F32)
            for p in (1, 2):
                _merge_classes(pat[3 * p + t], stage)
                tot = tot + _read_stage(stage)
            out[:, t * ATTN_WIDTH:(t + 1) * ATTN_WIDTH] = tot.astype(BF16)
        out[:, COL_ZA:COL_QR] = dza[...]
        out[:, COL_QR:COL_KR] = dqr[...].astype(BF16)
        out[:, COL_KR:COL_VR] = dkr[...].astype(BF16)
        out[:, COL_VR:COL_ZR] = dvr[...].astype(BF16)
        out[:, COL_ZR:IN_W] = dzr[...]

    row = lambda w: pl.BlockSpec((tm, w), lambda i: (i, 0))
    cls = [_class_block(dil, tm, ATTN_WIDTH, lambda i: (0, i, 0)) for dil in DILATIONS[1:]]
    flat = [dqkv_a[p][t] for p in range(3) for t in range(3)]
    return pl.pallas_call(
        body, name="assemble_dproj", grid=(s_len // tm,),
        in_specs=[row(ATTN_WIDTH)] * 3 + [cls[0]] * 3 + [cls[1]] * 3
                 + [row(ATTN_WIDTH), row(512), row(512), row(ATTN_WIDTH), row(ATTN_WIDTH)],
        out_specs=row(IN_W), out_shape=jax.ShapeDtypeStruct((s_len, IN_W), BF16),
        scratch_shapes=[_stage_shape(tm, ATTN_WIDTH)],
        compiler_params=_params("arbitrary"),
    )(*flat, dz_a, dq_r, dk_r, dv_r, dz_r)


def inproj_bwd(dproj, w, x, g, scale1p, dxn, exchange):
    s_len = x.shape[0]
    tm, tk = min(512, s_len), 1024

    def body(dp_ref, w_ref, x_ref, g_ref, sc_ref, dxn_ref, pa_ref, pb_ref, dx_ref, st_ref, ra_ref, rb_ref,
             acc, send_sems, recv_sems):
        first, last = _grid_edge(2)
        copies = _reduce_phase2([pa_ref, pb_ref], [ra_ref, rb_ref], send_sems, recv_sems)
        pl.when(first)(lambda: _start_all(copies))
        i, k = pl.program_id(0), pl.program_id(1)

        @pl.when((i == 0) & (k == 0))
        def _():
            st_ref[...] = jnp.zeros_like(st_ref)

        @pl.when(k == 0)
        def _():
            acc[...] = jnp.zeros_like(acc)
        acc[...] += _dot_nt(dp_ref[...], w_ref[...])

        @pl.when(k == pl.num_programs(1) - 1)
        def _():
            dh, xv = acc[...], x_ref[...]
            r = lax.rsqrt(jnp.mean(xv * xv, axis=-1, keepdims=True) + NORM_EPS)
            xn = xv * r
            st_ref[0:1, :] += jnp.sum(dh, axis=0, keepdims=True)
            st_ref[3:4, :] += jnp.sum(dh * xn, axis=0, keepdims=True)
            dn = dh * (sc_ref[...] * g_ref[...])
            dx_ref[...] = r * (dn - xn * jnp.mean(dn * xn, axis=-1, keepdims=True)) + dxn_ref[...]

        @pl.when(last)
        def _():
            st_ref[1:2, :] = st_ref[3:4, :] * g_ref[...]
            st_ref[2:3, :] = st_ref[3:4, :] * sc_ref[...]
            _wait_all(copies)

    row = pl.BlockSpec((tm, D_MODEL), lambda i, k: (i, 0))
    vec = pl.BlockSpec((1, D_MODEL), lambda i, k: (0, 0))
    return pl.pallas_call(
        body, name="inproj_bwd_reduce2", grid=(s_len // tm, IN_W // tk),
        in_specs=[pl.BlockSpec((tm, tk), lambda i, k: (i, k)), pl.BlockSpec((D_MODEL, tk), lambda i, k: (0, k)),
                  row, vec, vec, row, HBM, HBM],
        out_specs=[row, pl.BlockSpec((8, D_MODEL), lambda i, k: (0, 0)), HBM, HBM],
        out_shape=[jax.ShapeDtypeStruct((s_len, D_MODEL), F32), jax.ShapeDtypeStruct((8, D_MODEL), F32)]
                  + _landing_shapes(3),
        scratch_shapes=[pltpu.VMEM((tm, D_MODEL), F32)] + REDUCE2_SEMS,
        compiler_params=_params("arbitrary", "arbitrary"),
    )(dproj, w, x, g, scale1p, dxn, *exchange)


def win_grad(h, dproj):
    s_len = h.shape[0]
    ts = min(2048, s_len)

    def body(h_ref, dp_ref, dw_ref, acc):
        s = pl.program_id(1)

        @pl.when(s == 0)
        def _():
            acc[...] = jnp.zeros_like(acc)
        acc[...] += _dot_tn(h_ref[...], dp_ref[...])

        @pl.when(s == pl.num_programs(1) - 1)
        def _():
            dw_ref[...] = acc[...].astype(BF16)

    return pl.pallas_call(
        body, name="win_grad", grid=(N_DEV, s_len // ts),
        in_specs=[pl.BlockSpec((ts, D_MODEL), lambda j, s: (s, 0)), pl.BlockSpec((ts, W_IN_SHARD), lambda j, s: (s, j))],
        out_specs=pl.BlockSpec((None, D_MODEL, W_IN_SHARD), lambda j, s: (j, 0, 0)),
        out_shape=jax.ShapeDtypeStruct((N_DEV, D_MODEL, W_IN_SHARD), BF16),
        scratch_shapes=[pltpu.VMEM((D_MODEL, W_IN_SHARD), F32)],
        compiler_params=_params("arbitrary", "arbitrary"),
    )(h, dproj)


def ada_fwd(c_all, w_ada):
    def body(c_ref, w_ref, act_ref, part_ref):
        cv = c_ref[...]
        act = cv * _sigmoid(cv)
        act_ref[...] = act
        part_ref[...] = _dot(act.astype(BF16), w_ref[...].astype(BF16))

    return pl.pallas_call(
        body, name="ada_fwd", grid=(DEPTH,),
        in_specs=[pl.BlockSpec((N_DEV, D_MODEL), lambda l: (0, 0)),
                  pl.BlockSpec((None, D_MODEL, W_ADA_SHARD), lambda l: (l, 0, 0))],
        out_specs=[pl.BlockSpec((N_DEV, D_MODEL), lambda l: (0, 0)),
                   pl.BlockSpec((None, N_DEV, W_ADA_SHARD), lambda l: (l, 0, 0))],
        out_shape=[jax.ShapeDtypeStruct((N_DEV, D_MODEL), F32), jax.ShapeDtypeStruct((DEPTH, N_DEV, W_ADA_SHARD), F32)],
        compiler_params=_params("arbitrary"),
    )(c_all, w_ada)


def _adamw(w, g, m, v):
    m = ADAM_B1 * m + (1.0 - ADAM_B1) * g
    v = ADAM_B2 * v + (1.0 - ADAM_B2) * (g * g)
    delta = -ADAM_LR * ((m * ADAM_C1) / (jnp.sqrt(v * ADAM_C2) + ADAM_EPS) + ADAM_WD * w)
    return delta, m, v


def ada_update(act_t, dmod, w, m, v):
    tr = 512

    def body(a_ref, d_ref, w_ref, m_ref, v_ref, g_out, dl_out, m_out, v_out):
        a = a_ref[...].astype(BF16).astype(F32)
        d = d_ref[...].astype(BF16).astype(F32)
        g = a[:, 0:1] * d[0:1, :]
        for b in range(1, N_DEV):
            g = g + a[:, b:b + 1] * d[b:b + 1, :]
        g_out[...] = g
        dl_out[...], m_out[...], v_out[...] = _adamw(w_ref[...], g, m_ref[...], v_ref[...])

    blk = pl.BlockSpec((None, tr, W_ADA_SHARD), lambda l, r: (l, r, 0))
    out = jax.ShapeDtypeStruct(w.shape, F32)
    return pl.pallas_call(
        body, name="ada_update", grid=(DEPTH, D_MODEL // tr),
        in_specs=[pl.BlockSpec((tr, N_DEV), lambda l, r: (r, 0)),
                  pl.BlockSpec((None, N_DEV, W_ADA_SHARD), lambda l, r: (l, 0, 0)), blk, blk, blk],
        out_specs=[blk] * 4, out_shape=[out] * 4, compiler_params=_params("arbitrary", "arbitrary"),
    )(act_t, dmod, w, m, v)


def chip_sum(pos, grads, landed, name):
    _, _, n_rows, n_cols = grads.shape
    tr = min(512, n_rows)

    def chip(k, pos_ref):
        return (pos_ref[0] ^ (k // 2)) * 2 + (pos_ref[1] ^ (k % 2))

    def body(pos_ref, g_ref, r_ref, out_ref):
        out_ref[...] = (g_ref[...].astype(F32) + r_ref[...].astype(F32)).astype(BF16)

    return pl.pallas_call(
        body, name=name,
        grid_spec=pltpu.PrefetchScalarGridSpec(
            num_scalar_prefetch=1, grid=(N_DEV // 2, n_rows // tr),
            in_specs=[pl.BlockSpec((None, None, tr, n_cols), lambda k, r, p: (chip(k, p), p[2], r, 0)),
                      pl.BlockSpec((None, tr, n_cols), lambda k, r, p: (chip(k, p), r, 0))],
            out_specs=pl.BlockSpec((None, tr, n_cols), lambda k, r, p: (k, r, 0))),
        out_shape=jax.ShapeDtypeStruct((N_DEV // 2, n_rows, n_cols), BF16),
        compiler_params=_params("arbitrary", "arbitrary"),
    )(pos, grads, landed)


def shard_update(layer, own, others, w, m, v, prev, name):
    _, n_rows, n_cols = w.shape
    tr = min(256, n_rows)

    def body(own_ref, oth_ref, w_ref, m_ref, v_ref, *rest):
        g_out, dl_out, m_out, v_out = rest[-4:]
        g = own_ref[...].astype(F32)
        for k in range(3):
            g = g + oth_ref[k].astype(F32)
        g_out[...] = g
        dl_out[...], m_out[...], v_out[...] = _adamw(w_ref[...], g, m_ref[...], v_ref[...])

    blk = pl.BlockSpec((None, tr, n_cols), lambda r: (layer, r, 0))
    out = jax.ShapeDtypeStruct(w.shape, F32)
    in_specs = [pl.BlockSpec((None, tr, n_cols), lambda r: (0, r, 0)), pl.BlockSpec((3, tr, n_cols), lambda r: (0, r, 0)),
                blk, blk, blk]
    aliases, extra = {}, ()
    if prev is not None:
        in_specs += [HBM] * 4
        aliases = {5 + t: t for t in range(4)}
        extra = tuple(prev)
    return pl.pallas_call(
        body, name=name, grid=(n_rows // tr,), in_specs=in_specs, out_specs=[blk] * 4, out_shape=[out] * 4,
        input_output_aliases=aliases, compiler_params=_params("arbitrary"),
    )(own, others, w, m, v, *extra)


def small_update(parts, w, m, v):
    def body(p_ref, w_ref, m_ref, v_ref, g_out, dl_out, m_out, v_out):
        g = p_ref[0]
        for k in range(1, N_DEV):
            g = g + p_ref[k]
        g_out[...] = g
        dl_out[...], m_out[...], v_out[...] = _adamw(w_ref[...], g, m_ref[...], v_ref[...])

    out = jax.ShapeDtypeStruct(w.shape, F32)
    return pl.pallas_call(body, name="small_update", out_shape=[out] * 4, compiler_params=_params())(parts, w, m, v)


def _two_level_allgather(srcs, dst_block, send_sems, recv_sems, local_sems):
    x, y, c = _position()
    me, sibling = (x, y, c), (x, y, 1 - c)
    chips = [(1 - x, y), (x, 1 - y), (1 - x, 1 - y)]
    n = len(srcs)

    def copy(a, k, block, to, src=None):
        dst = dst_block(a, _flat(*block))
        return pltpu.make_async_remote_copy(
            src_ref=dst if src is None else src, dst_ref=dst, send_sem=send_sems.at[a * 7 + k],
            recv_sem=recv_sems.at[a * 7 + k], device_id=to, device_id_type=MESH)

    mine = [pltpu.make_async_copy(srcs[a], dst_block(a, _flat(*me)), local_sems.at[a]) for a in range(n)]
    for cp in mine:
        cp.start()
    first = []
    for a in range(n):
        first.append(copy(a, 0, me, sibling, src=srcs[a]))
        first += [copy(a, 1 + j, me, (*chip, c), src=srcs[a]) for j, chip in enumerate(chips)]
    for cp in first:
        cp.start()
    passed = []
    for j, chip in enumerate(chips):
        for a in range(n):
            copy(a, 1 + j, (*chip, c), me).wait_recv()
            fwd = copy(a, 4 + j, (*chip, c), sibling)
            fwd.start()
            passed.append(fwd)
    for a in range(n):
        copy(a, 0, sibling, me).wait_recv()
        for j, chip in enumerate(chips):
            copy(a, 4 + j, (*chip, 1 - c), me).wait_recv()
    for cp in first + passed:
        cp.wait_send()
    for cp in mine:
        cp.wait()


def allgather_rows(x, name):
    def body(x_ref, out_ref, send_sems, recv_sems, local_sems):
        _two_level_allgather([x_ref], lambda a, idx: out_ref.at[idx], send_sems, recv_sems, local_sems)

    vmem = pl.BlockSpec(memory_space=pltpu.VMEM)
    return pl.pallas_call(
        body, name=name, in_specs=[vmem], out_specs=vmem,
        out_shape=jax.ShapeDtypeStruct((N_DEV,) + x.shape, x.dtype),
        scratch_shapes=[pltpu.SemaphoreType.DMA((7,)), pltpu.SemaphoreType.DMA((7,)), pltpu.SemaphoreType.DMA((1,))],
        compiler_params=_params(),
    )(x)


def _full_weight_shapes():
    return [jax.ShapeDtypeStruct((D_MODEL, IN_W), BF16), jax.ShapeDtypeStruct((D_MODEL, D_MODEL), BF16)]


def allgather_weights(w_in_b, w_out_b):
    def body(win_ref, wout_ref, fin_ref, fout_ref, s1, r1, l1, s2, r2):
        block = _weight_blocks(fin_ref, fout_ref)
        first = _gather_phase1([win_ref.at[0], wout_ref.at[0]], block, s1, r1, l1)
        _start_all(first)
        _wait_all(first)
        second = _gather_phase2(block, s2, r2)
        _start_all(second)
        _wait_all(second)

    return pl.pallas_call(
        body, name="allgather_weights", in_specs=[HBM, HBM], out_specs=[HBM, HBM], out_shape=_full_weight_shapes(),
        scratch_shapes=GATHER1_SEMS + GATHER2_SEMS, compiler_params=_params(),
    )(w_in_b, w_out_b)


def reduce_first(dw_in, dw_out, name):
    def body(ga_ref, gb_ref, ra_ref, rb_ref, send_sems, recv_sems):
        copies = _reduce_phase1([ga_ref, gb_ref], [ra_ref, rb_ref], send_sems, recv_sems)
        _start_all(copies)
        _wait_all(copies)

    return pl.pallas_call(
        body, name=name, in_specs=[HBM, HBM], out_specs=[HBM, HBM], out_shape=_landing_shapes(4),
        scratch_shapes=REDUCE1_SEMS, compiler_params=_params(),
    )(*_split_cores((dw_in, dw_out)))


def _one_class(a):
    return a.reshape((1,) + a.shape)


def layer_fwd(x, g, scale, shift, gate, w_in, w_out, lg, gather=None):
    proj, h, qkv4, qkv16, *began = inproj_fwd(x, g, 1.0 + scale, shift, w_in, gather)
    qkv = (_one_class(proj), qkv4, qkv16)
    attn_outs = [attn_fwd(arr, dil) for dil, arr in zip(DILATIONS, qkv)]
    attn_outs[0] = tuple(a[0] for a in attn_outs[0])
    o_pre, y_r, *next_weights = ret_fwd(proj, lg, began if gather is not None else None)
    next_weights = tuple(next_weights) if gather is not None else None
    x_new, o_a, lse, lse4, lse16 = outproj_fwd(x, gate, w_out, attn_outs, y_r, proj)
    saved = dict(x=x, proj=proj, h=h, qkv=qkv, o_a=o_a, lse=(_one_class(lse), lse4, lse16), o_pre=o_pre, y_r=y_r)
    return x_new, saved, next_weights


def layer_bwd(dxn, saved, g, scale, gate, w_in, w_out, lg, pos):
    proj = saved["proj"]
    do_a, delta, dyr, dz_a, dz_r, y, dxb, do4, do16, dl4, dl16 = outproj_bwd(
        dxn, gate, w_out, saved["o_a"], saved["y_r"], proj)
    dw_out, dgate = wout_grad(y, dxb, gate, w_out)
    dq_r, dk_r, dv_r, glf, glb = ret_bwd(proj, lg, saved["o_pre"], dyr)
    dos, deltas = (_one_class(do_a), do4, do16), (_one_class(delta), dl4, dl16)
    dqkv_a = [attn_bwd(arr, d_o, lse, dl, dil)
              for dil, arr, d_o, lse, dl in zip(DILATIONS, saved["qkv"], dos, saved["lse"], deltas)]
    dqkv_a[0] = [t[0] for t in dqkv_a[0]]
    dproj = assemble_dproj(dqkv_a, dz_a, dq_r, dk_r, dv_r, dz_r)
    dw_in = win_grad(saved["h"], dproj)
    landed = reduce_first(dw_in, dw_out, "reduce_first")
    sums = [chip_sum(pos, g4, r, "chip_sum") for g4, r in zip(_split_cores((dw_in, dw_out)), landed)]
    dx, stats, *others = inproj_bwd(dproj, w_in, saved["x"], g, 1.0 + scale, dxn, sums)
    dlg = jnp.concatenate([glf[:, 0, 0], glb[:, 0, 0]])
    return dx, stats[0:1], stats[1:2], dgate, stats[2:3], dlg, (sums, others)


ROWS_B_ADA = DEPTH * 3 * D_MODEL // 128
ROWS_GAIN = DEPTH * D_MODEL // 128
ROWS_FINAL = D_MODEL // 128
ROWS_MISC = 8
ROWS_SMALL = ROWS_B_ADA + ROWS_GAIN + ROWS_FINAL + ROWS_MISC


def _pack_small(b_ada_like, gain_like, final_like, dec_f, dec_b, loss=None):
    misc = jnp.zeros((ROWS_MISC, 128), F32)
    misc = misc.at[0, :2 * DEPTH * RET_HEADS].set(jnp.concatenate([dec_f.reshape(-1), dec_b.reshape(-1)]))
    if loss is not None:
        misc = misc.at[1, 0].set(loss)
    return jnp.concatenate([b_ada_like.reshape(ROWS_B_ADA, 128), gain_like.reshape(ROWS_GAIN, 128),
                            final_like.reshape(ROWS_FINAL, 128), misc], axis=0)


def _unpack_small(p):
    r0, r1, r2 = ROWS_B_ADA, ROWS_B_ADA + ROWS_GAIN, ROWS_B_ADA + ROWS_GAIN + ROWS_FINAL
    n = DEPTH * RET_HEADS
    return (p[:r0].reshape(DEPTH, 3 * D_MODEL), p[r0:r1].reshape(DEPTH, D_MODEL), p[r1:r2].reshape(D_MODEL),
            p[r2, :n].reshape(DEPTH, RET_HEADS), p[r2, n:2 * n].reshape(DEPTH, RET_HEADS))


def kernel(x, c, norm_gain, w_ada, b_ada, w_in, w_out, ret_decay_logit_f, ret_decay_logit_b, final_gain, loss_target, m_norm_gain, m_w_ada, m_b_ada, m_w_in, m_w_out, m_ret_decay_logit_f, m_ret_decay_logit_b, m_final_gain, v_norm_gain, v_w_ada, v_b_ada, v_w_in, v_w_out, v_ret_decay_logit_f, v_ret_decay_logit_b, v_final_gain):
    px, py, pc = _position()
    me = _flat(px, py, pc)
    pos = jnp.stack([px, py, pc]).astype(jnp.int32)
    x2, target = x[0], loss_target[0]

    w_in_b, w_out_b = w_in.astype(BF16), w_out.astype(BF16)
    weights = allgather_weights(w_in_b, w_out_b)

    c_all = allgather_rows(c.reshape(D_MODEL // 128, 128), "allgather_c").reshape(N_DEV, D_MODEL)
    act, mod_part = ada_fwd(c_all, w_ada)
    mod_all = allgather_rows(mod_part.reshape(-1, 128), "allgather_mod").reshape(N_DEV, DEPTH, N_DEV, W_ADA_SHARD)
    mod = lax.dynamic_index_in_dim(mod_all, me, axis=2, keepdims=False)
    mod = mod.transpose(1, 0, 2).reshape(DEPTH, 3 * D_MODEL) + b_ada
    shift, scale, gate = mod[:, :D_MODEL], mod[:, D_MODEL:2 * D_MODEL], mod[:, 2 * D_MODEL:]

    lg = jnp.concatenate([jax.nn.log_sigmoid(ret_decay_logit_f), jax.nn.log_sigmoid(ret_decay_logit_b)], axis=1)

    h = x2
    saved, layer_weights = [], []
    for l in range(DEPTH):
        layer_weights.append(weights)
        gather = (w_in_b, w_out_b, l + 1) if l + 1 < DEPTH else None
        h, sv, weights = layer_fwd(h, norm_gain[l:l + 1], scale[l:l + 1], shift[l:l + 1], gate[l:l + 1],
                                   *layer_weights[l], lg[l], gather)
        saved.append(sv)
    dh, loss_part, dfinal = loss_head(h, final_gain.reshape(1, D_MODEL), target)

    dmod, dgain, dlg, reduced = [None] * DEPTH, [None] * DEPTH, [None] * DEPTH, [None] * DEPTH
    for l in reversed(range(DEPTH)):
        dh, dshift, dscale, dgate, dg, dlg[l], reduced[l] = layer_bwd(
            dh, saved[l], norm_gain[l:l + 1], scale[l:l + 1], gate[l:l + 1], *layer_weights[l], lg[l], pos)
        dmod[l] = jnp.concatenate([dshift, dscale, dgate], axis=1)
        dgain[l] = dg

    dlg = jnp.stack(dlg)
    dlogit_f = dlg[:, :RET_HEADS] * jax.nn.sigmoid(-ret_decay_logit_f)
    dlogit_b = dlg[:, RET_HEADS:] * jax.nn.sigmoid(-ret_decay_logit_b)
    packed = _pack_small(jnp.concatenate(dmod, axis=0), jnp.concatenate(dgain, axis=0), dfinal, dlogit_f, dlogit_b,
                         loss=loss_part[0, 0])
    gathered = allgather_rows(packed, "allgather_small")
    small = small_update(gathered,
                         _pack_small(b_ada, norm_gain, final_gain, ret_decay_logit_f, ret_decay_logit_b),
                         _pack_small(m_b_ada, m_norm_gain, m_final_gain, m_ret_decay_logit_f, m_ret_decay_logit_b),
                         _pack_small(v_b_ada, v_norm_gain, v_final_gain, v_ret_decay_logit_f, v_ret_decay_logit_b))
    loss = small[0][ROWS_B_ADA + ROWS_GAIN + ROWS_FINAL + 1, 0]
    (g_b_ada, g_gain, g_final, g_dec_f, g_dec_b), (d_b_ada, d_gain, d_final, d_dec_f, d_dec_b), \
        (m_b_ada2, m_gain2, m_final2, m_dec_f2, m_dec_b2), (v_b_ada2, v_gain2, v_final2, v_dec_f2, v_dec_b2) = \
        [_unpack_small(p) for p in small]

    dmod_all = gathered[:, :ROWS_B_ADA].reshape(N_DEV, DEPTH, 3 * D_MODEL)
    dmod_mine = lax.dynamic_slice_in_dim(dmod_all, me * W_ADA_SHARD, W_ADA_SHARD, axis=2).transpose(1, 0, 2)
    g_w_ada, d_w_ada, m_w_ada2, v_w_ada2 = ada_update(act.T, dmod_mine, w_ada, m_w_ada, v_w_ada)

    upd_in = upd_out = None
    for l in reversed(range(DEPTH)):
        upd_in = shard_update(l, reduced[l][0][0], reduced[l][1][0], w_in, m_w_in, v_w_in, upd_in, f"w_in_update_{l}")
        upd_out = shard_update(l, reduced[l][0][1], reduced[l][1][1], w_out, m_w_out, v_w_out, upd_out, f"w_out_update_{l}")
    g_w_in, d_w_in, m_w_in2, v_w_in2 = upd_in
    g_w_out, d_w_out, m_w_out2, v_w_out2 = upd_out

    return (loss, dh[None],
            g_gain, g_w_ada, g_b_ada, g_w_in, g_w_out, g_dec_f, g_dec_b, g_final,
            d_gain, d_w_ada, d_b_ada, d_w_in, d_w_out, d_dec_f, d_dec_b, d_final,
            m_gain2, m_w_ada2, m_b_ada2, m_w_in2, m_w_out2, m_dec_f2, m_dec_b2, m_final2,
            v_gain2, v_w_ada2, v_b_ada2, v_w_in2, v_w_out2, v_dec_f2, v_dec_b2, v_final2)
```

```python
import functools
import math

import jax
import jax.numpy as jnp
from jax import lax
from jax.experimental import pallas as pl
from jax.experimental.pallas import tpu as pltpu

F32, BF16 = jnp.float32, jnp.bfloat16

D_MODEL = 2048
DEPTH = 4
N_DEV = 8
ATTN_WIDTH = 1024
HEAD_DIM = 128
N_HEADS_ATTN = 8
DILATIONS = (1, 4, 16)
RADIUS = 64
RET_HEADS = 4
RET_QK = 128
RET_V = 256
RET_CHUNK = 256
IN_W = 7168
QKV_A = 3 * ATTN_WIDTH
COL_ZA, COL_QR, COL_KR, COL_VR, COL_ZR = 3072, 4096, 4608, 5120, 6144
W_IN_SHARD = IN_W // N_DEV
W_OUT_SHARD = D_MODEL // N_DEV
W_ADA_SHARD = 3 * D_MODEL // N_DEV
NORM_EPS = 1e-6
MASK_VALUE = -1e30
ATTN_SCALE = HEAD_DIM ** -0.5
RET_SCALE = RET_QK ** -0.5
LN2 = math.log(2.0)

ADAM_LR, ADAM_B1, ADAM_B2, ADAM_EPS, ADAM_WD, ADAM_STEP = 0.001, 0.9, 0.999, 1e-08, 0.01, 10
ADAM_C1 = 1.0 / (1.0 - ADAM_B1 ** ADAM_STEP)
ADAM_C2 = 1.0 / (1.0 - ADAM_B2 ** ADAM_STEP)

VMEM_LIMIT_BYTES = 56 * 1024 * 1024
MESH = pl.DeviceIdType.MESH


def _params(*sem):
    return pltpu.CompilerParams(dimension_semantics=sem if sem else None, vmem_limit_bytes=VMEM_LIMIT_BYTES)


def _dot(a, b):
    return jnp.dot(a, b, preferred_element_type=F32)


def _dot_nt(a, b):
    return lax.dot_general(a, b, (((1,), (1,)), ((), ())), preferred_element_type=F32)


def _dot_tn(a, b):
    return lax.dot_general(a, b, (((0,), (0,)), ((), ())), preferred_element_type=F32)


def _iota(shape, dim):
    return lax.broadcasted_iota(jnp.int32, shape, dim)


def _sigmoid(z):
    return 1.0 / (1.0 + jnp.exp(-z))


HBM = pl.BlockSpec(memory_space=pl.ANY)


def _position():
    return lax.axis_index("x"), lax.axis_index("y"), lax.axis_index("c")


def _flat(px, py, pc):
    return 4 * px + 2 * py + pc


def _remote(src, dst, send_sem, recv_sem, to):
    return pltpu.make_async_remote_copy(src_ref=src, dst_ref=dst, send_sem=send_sem, recv_sem=recv_sem,
                                        device_id=to, device_id_type=MESH)


def _weight_blocks(fin_ref, fout_ref):
    def block(a, idx):
        if a == 0:
            return fin_ref.at[:, pl.ds(pl.multiple_of(idx * W_IN_SHARD, 128), W_IN_SHARD)]
        return fout_ref.at[pl.ds(pl.multiple_of(idx * W_OUT_SHARD, W_OUT_SHARD), W_OUT_SHARD), :]
    return block


GATHER1_SEMS = [pltpu.SemaphoreType.DMA((8,)), pltpu.SemaphoreType.DMA((8,)), pltpu.SemaphoreType.DMA((2,))]
GATHER2_SEMS = [pltpu.SemaphoreType.DMA((6,)), pltpu.SemaphoreType.DMA((6,))]
REDUCE1_SEMS = [pltpu.SemaphoreType.DMA((2,)), pltpu.SemaphoreType.DMA((2,))]
REDUCE2_SEMS = [pltpu.SemaphoreType.DMA((6,)), pltpu.SemaphoreType.DMA((6,))]


def _gather_phase1(srcs, block, send_sems, recv_sems, local_sems):
    x, y, c = _position()
    mine = [block(a, _flat(x, y, c)) for a in range(2)]
    copies = [pltpu.make_async_copy(srcs[a], mine[a], local_sems.at[a]) for a in range(2)]
    for a in range(2):
        copies.append(_remote(srcs[a], mine[a], send_sems.at[4 * a], recv_sems.at[4 * a], (x, y, 1 - c)))
        for j, (px, py) in enumerate([(1 - x, y), (x, 1 - y), (1 - x, 1 - y)]):
            copies.append(_remote(srcs[a], mine[a], send_sems.at[4 * a + 1 + j], recv_sems.at[4 * a + 1 + j], (px, py, c)))
    return copies


def _gather_phase2(block, send_sems, recv_sems):
    x, y, c = _position()
    copies = []
    for a in range(2):
        for j, (px, py) in enumerate([(1 - x, y), (x, 1 - y), (1 - x, 1 - y)]):
            blk = block(a, _flat(px, py, c))
            copies.append(_remote(blk, blk, send_sems.at[3 * a + j], recv_sems.at[3 * a + j], (x, y, 1 - c)))
    return copies


def _reduce_phase1(grads, landings, send_sems, recv_sems):
    x, y, c = _position()
    return [_remote(g.at[:, 1 - c], r, send_sems.at[a], recv_sems.at[a], (x, y, 1 - c))
            for a, (g, r) in enumerate(zip(grads, landings))]


def _reduce_phase2(sums, landings, send_sems, recv_sems):
    x, y, c = _position()
    copies = []
    for a, (p, r) in enumerate(zip(sums, landings)):
        for k in (1, 2, 3):
            to = (1 - x if k & 2 else x, 1 - y if k & 1 else y, c)
            copies.append(_remote(p.at[k], r.at[k - 1], send_sems.at[3 * a + k - 1], recv_sems.at[3 * a + k - 1], to))
    return copies


def _landing_shapes(n):
    return [jax.ShapeDtypeStruct((n, D_MODEL, W_IN_SHARD), BF16), jax.ShapeDtypeStruct((n, W_OUT_SHARD, D_MODEL), BF16)]


def _split_cores(slabs):
    return tuple(s.reshape((N_DEV // 2, 2) + s.shape[1:]) for s in slabs)


def _start_all(copies):
    for cp in copies:
        cp.start()


def _wait_all(copies):
    for cp in copies:
        cp.wait()


def _grid_edge(n_axes):
    first = last = None
    for ax in range(n_axes):
        f = pl.program_id(ax) == 0
        e = pl.program_id(ax) == pl.num_programs(ax) - 1
        first = f if first is None else first & f
        last = e if last is None else last & e
    return first, last


LANES = 128


def _stage_shape(rows, width):
    return pltpu.VMEM((width // LANES, rows, LANES), F32)


def _fill_stage(stage_ref, value):
    for t in range(stage_ref.shape[0]):
        stage_ref[t] = value[:, t * LANES:(t + 1) * LANES]


def _read_stage(stage_ref):
    return jnp.concatenate([stage_ref[t] for t in range(stage_ref.shape[0])], axis=1)


def _split_classes(stage_ref, out_refs):
    n_t, rows, _ = stage_ref.shape
    for out_ref in out_refs:
        dil = out_ref.shape[0]
        for r in range(dil):
            for t in range(n_t):
                piece = stage_ref[t, pl.ds(r, rows // dil, stride=dil), :]
                out_ref[r, :, t * LANES:(t + 1) * LANES] = piece.astype(out_ref.dtype)


def _merge_classes(in_ref, stage_ref):
    dil, per = in_ref.shape[0], in_ref.shape[1]
    for r in range(dil):
        for t in range(stage_ref.shape[0]):
            stage_ref[t, pl.ds(r, per, stride=dil), :] = in_ref[r, :, t * LANES:(t + 1) * LANES].astype(F32)


def _class_block(dil, tm, width, index_map):
    return pl.BlockSpec((dil, tm // dil, width), index_map)


def inproj_fwd(x, g, scale1p, shift, w, gather=None):
    s_len = x.shape[0]
    tm, tn = min(1024, s_len), 512

    n_qkv = QKV_A // tn

    def body(x_ref, g_ref, sc_ref, sh_ref, w_ref, *rest):
        if gather is not None:
            (win_ref, wout_ref, proj_ref, h_ref, q4_ref, q16_ref, fin_ref, fout_ref, stage,
             send_sems, recv_sems, local_sems) = rest
            first, last = _grid_edge(2)
            copies = _gather_phase1([win_ref.at[gather[2]], wout_ref.at[gather[2]]], _weight_blocks(fin_ref, fout_ref),
                                    send_sems, recv_sems, local_sems)
            pl.when(first)(lambda: _start_all(copies))
        else:
            proj_ref, h_ref, q4_ref, q16_ref, stage = rest

        @pl.when(pl.program_id(1) == 0)
        def _():
            xv = x_ref[...]
            r = lax.rsqrt(jnp.mean(xv * xv, axis=-1, keepdims=True) + NORM_EPS)
            h_ref[...] = ((xv * r * g_ref[...]) * sc_ref[...] + sh_ref[...]).astype(BF16)
        res = _dot(h_ref[...], w_ref[...])
        proj_ref[...] = res.astype(BF16)

        @pl.when(pl.program_id(1) < n_qkv)
        def _():
            _fill_stage(stage, res)
            _split_classes(stage, [q4_ref, q16_ref])
        if gather is not None:
            pl.when(last)(lambda: _wait_all(copies))

    vec = pl.BlockSpec((1, D_MODEL), lambda i, j: (0, 0))
    in_specs = [pl.BlockSpec((tm, D_MODEL), lambda i, j: (i, 0)), vec, vec, vec,
                pl.BlockSpec((D_MODEL, tn), lambda i, j: (0, j))]
    out_specs = [pl.BlockSpec((tm, tn), lambda i, j: (i, j)), pl.BlockSpec((tm, D_MODEL), lambda i, j: (i, 0))]
    out_shape = [jax.ShapeDtypeStruct((s_len, IN_W), BF16), jax.ShapeDtypeStruct((s_len, D_MODEL), BF16)]
    for dil in DILATIONS[1:]:
        out_specs.append(pl.BlockSpec((dil, tm // dil, tn), lambda i, j: (0, i, jnp.minimum(j, n_qkv - 1))))
        out_shape.append(jax.ShapeDtypeStruct((dil, s_len // dil, QKV_A), BF16))
    scratch = [_stage_shape(tm, tn)]
    extra = ()
    if gather is not None:
        in_specs += [HBM, HBM]
        out_specs += [HBM, HBM]
        out_shape += [jax.ShapeDtypeStruct((D_MODEL, IN_W), BF16), jax.ShapeDtypeStruct((D_MODEL, D_MODEL), BF16)]
        scratch += GATHER1_SEMS
        extra = tuple(gather[:2])
    return pl.pallas_call(
        body, name="inproj_fwd_gather" if gather is not None else "inproj_fwd", grid=(s_len // tm, IN_W // tn),
        in_specs=in_specs, out_specs=out_specs, out_shape=out_shape, scratch_shapes=scratch,
        compiler_params=_params("arbitrary", "arbitrary"),
    )(x, g, scale1p, shift, w, *extra)


MASK_DISTANCE = 1e33
ATTN_TILE = 128


ATTN_UNROLL_FWD, ATTN_UNROLL_BWD = 15, 10


def _attn_plan(sub_len, unroll):
    tq = min(sub_len, ATTN_TILE)
    win = min(sub_len, tq + 2 * RADIUS)
    heads = 1 if sub_len > 1024 else (2 if sub_len > 256 else N_HEADS_ATTN)
    return tq, win, sub_len // tq, heads, unroll


def _attn_tiles(sub_len, tq, win, n_tiles, unroll, tile):
    tile(0, 0, 0)
    if n_tiles > 2:
        def mid(i, carry):
            q0 = pl.multiple_of(i * tq, tq)
            tile(q0, pl.multiple_of(q0 - RADIUS, RADIUS), 1)
            return carry
        lax.fori_loop(1, n_tiles - 1, mid, 0, unroll=min(unroll, n_tiles - 2))
    if n_tiles > 1:
        tile(sub_len - tq, sub_len - win, 2)


def _attn_bias(bias_ref, head, heads, tq, win, dil):
    h = pl.program_id(1) * heads + head
    slope = jnp.exp(-(h + 1).astype(F32) * LN2 * jnp.ones((1, 1), F32))
    rel = _iota((tq, win), 1) - _iota((tq, win), 0)
    for v, off in enumerate((0, RADIUS, win - tq)):
        dist = jnp.abs(rel - off)
        bias_ref[v] = slope * jnp.where(dist <= RADIUS, (dist * dil).astype(F32), MASK_DISTANCE)


def _attn_specs(n_cls, sub_len, heads):
    width = heads * HEAD_DIM
    per = ATTN_WIDTH // width

    def col(part):
        return pl.BlockSpec((None, sub_len, width), lambda r, g: (r, 0, part * per + g))
    return col, (n_cls, N_HEADS_ATTN // heads)


def _stat_spec(sub_len):
    return pl.BlockSpec((None, sub_len, LANES), lambda r, g: (r, 0, 0))


def _stat_column(ref, rows, lane, h_abs):
    return jnp.sum(jnp.where(lane == h_abs, ref[rows, :], 0.0), axis=1, keepdims=True)


def attn_fwd(qkv, dil):
    n_cls, sub_len, _ = qkv.shape
    tq, win, n_tiles, heads, unroll = _attn_plan(sub_len, ATTN_UNROLL_FWD)

    def body(q_ref, k_ref, v_ref, o_ref, lse_ref, bias_ref):
        @pl.when(pl.program_id(1) == 0)
        def _():
            lse_ref[...] = jnp.zeros_like(lse_ref)
        lane = _iota((tq, LANES), 1)
        for head in range(heads):
            lanes = slice(head * HEAD_DIM, (head + 1) * HEAD_DIM)
            h_abs = pl.program_id(1) * heads + head
            _attn_bias(bias_ref, head, heads, tq, win, dil)

            def tile(q0, start, variant):
                s = _dot_nt(q_ref[pl.ds(q0, tq), lanes], k_ref[pl.ds(start, win), lanes]) * ATTN_SCALE - bias_ref[variant]
                m = jnp.max(s, axis=1, keepdims=True)
                p = jnp.exp(s - m)
                den = jnp.sum(p, axis=1, keepdims=True)
                o_ref[pl.ds(q0, tq), lanes] = _dot(p.astype(BF16), v_ref[pl.ds(start, win), lanes]) / den
                rows = pl.ds(q0, tq)
                lse_ref[rows, :] = jnp.where(lane == h_abs, m + jnp.log(den), lse_ref[rows, :])

            _attn_tiles(sub_len, tq, win, n_tiles, unroll, tile)

    col, grid = _attn_specs(n_cls, sub_len, heads)
    return pl.pallas_call(
        body, name=f"attn_fwd_d{dil}", grid=grid,
        in_specs=[col(0), col(1), col(2)], out_specs=[col(0), _stat_spec(sub_len)],
        out_shape=[jax.ShapeDtypeStruct((n_cls, sub_len, ATTN_WIDTH), F32), jax.ShapeDtypeStruct((n_cls, sub_len, LANES), F32)],
        scratch_shapes=[pltpu.VMEM((3, tq, win), F32)],
        compiler_params=_params("arbitrary", "arbitrary"),
    )(qkv, qkv, qkv)


def attn_bwd(qkv, do, lse, delta, dil):
    n_cls, sub_len, _ = qkv.shape
    tq, win, n_tiles, heads, unroll = _attn_plan(sub_len, ATTN_UNROLL_BWD)

    def body(q_ref, k_ref, v_ref, do_ref, lse_ref, dl_ref, dq_ref, dk_ref, dv_ref, bias_ref, dk_acc, dv_acc):
        lane = _iota((tq, LANES), 1)
        for head in range(heads):
            lanes = slice(head * HEAD_DIM, (head + 1) * HEAD_DIM)
            h_abs = pl.program_id(1) * heads + head
            _attn_bias(bias_ref, head, heads, tq, win, dil)
            dk_acc[...] = jnp.zeros_like(dk_acc)
            dv_acc[...] = jnp.zeros_like(dv_acc)

            def tile(q0, start, variant):
                q = q_ref[pl.ds(q0, tq), lanes]
                k = k_ref[pl.ds(start, win), lanes]
                v = v_ref[pl.ds(start, win), lanes]
                dov = do_ref[pl.ds(q0, tq), lanes]
                s = _dot_nt(q, k) * ATTN_SCALE - bias_ref[variant]
                rows = pl.ds(q0, tq)
                p = jnp.exp(s - _stat_column(lse_ref, rows, lane, h_abs))
                ds = (p * (_dot_nt(dov, v) - _stat_column(dl_ref, rows, lane, h_abs))).astype(BF16)
                dq_ref[pl.ds(q0, tq), lanes] = (_dot(ds, k) * ATTN_SCALE).astype(BF16)
                dk_acc[pl.ds(start, win), :] += _dot_tn(ds, q) * ATTN_SCALE
                dv_acc[pl.ds(start, win), :] += _dot_tn(p.astype(BF16), dov)

            _attn_tiles(sub_len, tq, win, n_tiles, unroll, tile)
            dk_ref[:, lanes] = dk_acc[...].astype(BF16)
            dv_ref[:, lanes] = dv_acc[...].astype(BF16)

    col, grid = _attn_specs(n_cls, sub_len, heads)
    out = jax.ShapeDtypeStruct((n_cls, sub_len, ATTN_WIDTH), BF16)
    return pl.pallas_call(
        body, name=f"attn_bwd_d{dil}", grid=grid,
        in_specs=[col(0), col(1), col(2), col(0), _stat_spec(sub_len), _stat_spec(sub_len)],
        out_specs=[col(0), col(0), col(0)], out_shape=[out, out, out],
        scratch_shapes=[pltpu.VMEM((3, tq, win), F32), pltpu.VMEM((sub_len, HEAD_DIM), F32),
                        pltpu.VMEM((sub_len, HEAD_DIM), F32)],
        compiler_params=_params("arbitrary", "arbitrary"),
    )(qkv, qkv, qkv, do, lse, delta)


RET_UNROLL = 16
RET_UNROLL_BWD = 8


def _ret_tables(lg_ref):
    h = pl.program_id(0)
    one = jnp.ones((1, 1), F32)
    lgf, lgb = lg_ref[h] * one, lg_ref[RET_HEADS + h] * one
    c = RET_CHUNK
    rel = (_iota((c, c), 0) - _iota((c, c), 1)).astype(F32)
    dec_f = jnp.where(rel >= 0, jnp.exp(jnp.maximum(rel, 0.0) * lgf), 0.0)
    dec_b = jnp.where(rel <= 0, jnp.exp(jnp.maximum(-rel, 0.0) * lgb), 0.0)
    ci = _iota((c, 1), 0).astype(F32)
    tab = dict(rel=rel, dec_f=dec_f, dec_b=dec_b, ci=ci,
               xi_f=jnp.exp((ci + 1.0) * lgf), ze_f=jnp.exp((c - 1.0 - ci) * lgf), g_f=jnp.exp(c * lgf),
               xi_b=jnp.exp((c - ci) * lgb), ze_b=jnp.exp(ci * lgb), g_b=jnp.exp(c * lgb))
    return tab


def _ret_specs(s_len):
    q = pl.BlockSpec((s_len, RET_QK), lambda h: (0, COL_QR // RET_QK + h))
    k = pl.BlockSpec((s_len, RET_QK), lambda h: (0, COL_KR // RET_QK + h))
    v = pl.BlockSpec((s_len, RET_V), lambda h: (0, COL_VR // RET_V + h))
    wide = pl.BlockSpec((s_len, RET_V), lambda h: (0, h))
    narrow = pl.BlockSpec((s_len, RET_QK), lambda h: (0, h))
    smem = pl.BlockSpec(memory_space=pltpu.SMEM)
    return smem, q, k, v, wide, narrow


def ret_fwd(proj, lg, finish=None):
    s_len = proj.shape[0]
    c, n_chunks = RET_CHUNK, proj.shape[0] // RET_CHUNK

    def body(lg_ref, q_ref, k_ref, v_ref, *rest):
        if finish is not None:
            _, _, opre_ref, y_ref, fin_ref, fout_ref, st_f, st_b, send_sems, recv_sems = rest
            first, last = _grid_edge(1)
            copies = _gather_phase2(_weight_blocks(fin_ref, fout_ref), send_sems, recv_sems)
            pl.when(first)(lambda: _start_all(copies))
        else:
            opre_ref, y_ref, st_f, st_b = rest
        t = _ret_tables(lg_ref)
        dec = t["dec_f"] + t["dec_b"]
        st_f[...] = jnp.zeros_like(st_f)
        st_b[...] = jnp.zeros_like(st_b)

        def load(n):
            r0 = pl.multiple_of(n * c, c)
            q, k, v = q_ref[pl.ds(r0, c), :], k_ref[pl.ds(r0, c), :], v_ref[pl.ds(r0, c), :]
            return r0, q, (k.astype(F32) * RET_SCALE), v

        def fwd(n, carry):
            r0, q, kf, v = load(n)
            inner = (_dot_nt(q, kf.astype(BF16)) * dec).astype(BF16)
            opre_ref[pl.ds(r0, c), :] = _dot(inner, v) + _dot(q, st_f[...].astype(BF16)) * t["xi_f"]
            st_f[...] = st_f[...] * t["g_f"] + _dot_tn((kf * t["ze_f"]).astype(BF16), v)
            return carry

        def bwd(i, carry):
            r0, q, kf, v = load(n_chunks - 1 - i)
            o = opre_ref[pl.ds(r0, c), :] + _dot(q, st_b[...].astype(BF16)) * t["xi_b"]
            st_b[...] = st_b[...] * t["g_b"] + _dot_tn((kf * t["ze_b"]).astype(BF16), v)
            opre_ref[pl.ds(r0, c), :] = o
            y_ref[pl.ds(r0, c), :] = o * lax.rsqrt(jnp.mean(o * o, axis=-1, keepdims=True) + NORM_EPS)
            return carry

        lax.fori_loop(0, n_chunks, fwd, 0, unroll=min(RET_UNROLL, n_chunks))
        lax.fori_loop(0, n_chunks, bwd, 0, unroll=min(RET_UNROLL, n_chunks))
        if finish is not None:
            pl.when(last)(lambda: _wait_all(copies))

    smem, q, k, v, wide, _ = _ret_specs(s_len)
    out = jax.ShapeDtypeStruct((s_len, RET_HEADS * RET_V), F32)
    in_specs, out_specs, out_shape = [smem, q, k, v], [wide, wide], [out, out]
    scratch = [pltpu.VMEM((RET_QK, RET_V), F32), pltpu.VMEM((RET_QK, RET_V), F32)]
    aliases, extra = {}, ()
    if finish is not None:
        in_specs += [HBM, HBM]
        out_specs += [HBM, HBM]
        out_shape += _full_weight_shapes()
        scratch += GATHER2_SEMS
        aliases, extra = {4: 2, 5: 3}, tuple(finish)
    return pl.pallas_call(
        body, name="ret_fwd_gather2" if finish is not None else "ret_fwd", grid=(RET_HEADS,), in_specs=in_specs,
        out_specs=out_specs, out_shape=out_shape, scratch_shapes=scratch, input_output_aliases=aliases,
        compiler_params=_params("arbitrary"),
    )(lg, proj, proj, proj, *extra)


def ret_bwd(proj, lg, o_pre, dy):
    s_len = proj.shape[0]
    c, n_chunks = RET_CHUNK, proj.shape[0] // RET_CHUNK
    cf = float(c)

    def body(lg_ref, q_ref, k_ref, v_ref, o_ref, dy_ref, dq_ref, dk_ref, dv_ref, glf_ref, glb_ref,
             st_f, dst_b, st_b, dst_f, keep_sf, keep_dtb, acc_f, acc_b, acc_sf, acc_sb):
        t = _ret_tables(lg_ref)
        dec = t["dec_f"] + t["dec_b"]
        e_f, e_b, ci = t["rel"] * t["dec_f"], -t["rel"] * t["dec_b"], t["ci"]
        for ref in (st_f, dst_b, st_b, dst_f, acc_f, acc_b, acc_sf, acc_sb):
            ref[...] = jnp.zeros_like(ref)

        def load(n):
            r0 = pl.multiple_of(n * c, c)
            q, k, v = q_ref[pl.ds(r0, c), :], k_ref[pl.ds(r0, c), :], v_ref[pl.ds(r0, c), :]
            o, dyv = o_ref[pl.ds(r0, c), :], dy_ref[pl.ds(r0, c), :]
            rr = lax.rsqrt(jnp.mean(o * o, axis=-1, keepdims=True) + NORM_EPS)
            y = o * rr
            do = (rr * (dyv - y * jnp.mean(dyv * y, axis=-1, keepdims=True))).astype(BF16)
            return r0, q, k.astype(F32) * RET_SCALE, v, do

        def fold(x):
            return functools.reduce(lambda a, b: a + b, [x[:, i * RET_QK:(i + 1) * RET_QK] for i in range(c // RET_QK)])

        def fwd(n, carry):
            r0, q, kf, v, do = load(n)
            qf, kb = q.astype(F32), kf.astype(BF16)
            a = _dot_nt(q, kb)
            b = _dot_nt(do, v)
            da = (b * dec).astype(BF16)
            ab = a * b
            sf_b, dtb_b = st_f[...].astype(BF16), dst_b[...].astype(BF16)
            dq_inter = _dot_nt(do, sf_b) * t["xi_f"]
            dk_inter = _dot_nt(v, dtb_b) * t["ze_b"]
            acc_f[...] += fold(e_f * ab) + (ci + 1.0) * (qf * dq_inter)
            acc_b[...] += fold(e_b * ab) + ci * (kf * dk_inter)
            dq_ref[pl.ds(r0, c), :] = _dot(da, kb) + dq_inter
            dk_ref[pl.ds(r0, c), :] = _dot_tn(da, q) + dk_inter
            dv_ref[pl.ds(r0, c), :] = _dot_tn((a * dec).astype(BF16), do) + _dot((kf * t["ze_b"]).astype(BF16), dtb_b)
            keep_sf[n] = sf_b
            keep_dtb[n] = dtb_b
            st_f[...] = st_f[...] * t["g_f"] + _dot_tn((kf * t["ze_f"]).astype(BF16), v)
            dst_b[...] = dst_b[...] * t["g_b"] + _dot_tn((qf * t["xi_b"]).astype(BF16), do)
            return carry

        def bwd(i, carry):
            n = n_chunks - 1 - i
            r0, q, kf, v, do = load(n)
            qf = q.astype(F32)
            tb_b, dsf_b = st_b[...].astype(BF16), dst_f[...].astype(BF16)
            dq_inter = _dot_nt(do, tb_b) * t["xi_b"]
            dk_inter = _dot_nt(v, dsf_b) * t["ze_f"]
            acc_b[...] += (cf - ci) * (qf * dq_inter)
            acc_f[...] += (cf - 1.0 - ci) * (kf * dk_inter)
            acc_sb[...] += keep_dtb[n].astype(F32) * st_b[...]
            acc_sf[...] += dst_f[...] * keep_sf[n].astype(F32)
            dq_ref[pl.ds(r0, c), :] += dq_inter
            dk_ref[pl.ds(r0, c), :] = (dk_ref[pl.ds(r0, c), :] + dk_inter) * RET_SCALE
            dv_ref[pl.ds(r0, c), :] += _dot((kf * t["ze_f"]).astype(BF16), dsf_b)
            st_b[...] = st_b[...] * t["g_b"] + _dot_tn((kf * t["ze_b"]).astype(BF16), v)
            dst_f[...] = dst_f[...] * t["g_f"] + _dot_tn((qf * t["xi_f"]).astype(BF16), do)
            return carry

        lax.fori_loop(0, n_chunks, fwd, 0, unroll=min(RET_UNROLL_BWD, n_chunks))
        lax.fori_loop(0, n_chunks, bwd, 0, unroll=min(RET_UNROLL_BWD, n_chunks))

        def total(x):
            return jnp.sum(jnp.sum(x, axis=1, keepdims=True), axis=0, keepdims=True)

        glf_ref[...] = jnp.broadcast_to(total(acc_f[...]) + cf * t["g_f"] * total(acc_sf[...]), (8, 128))
        glb_ref[...] = jnp.broadcast_to(total(acc_b[...]) + cf * t["g_b"] * total(acc_sb[...]), (8, 128))

    smem, q, k, v, wide, narrow = _ret_specs(s_len)
    scal = pl.BlockSpec((None, 8, 128), lambda h: (h, 0, 0))
    state = pltpu.VMEM((RET_QK, RET_V), F32)
    square = pltpu.VMEM((RET_CHUNK, RET_QK), F32)
    keep = pltpu.VMEM((n_chunks, RET_QK, RET_V), BF16)
    return pl.pallas_call(
        body, name="ret_bwd", grid=(RET_HEADS,), in_specs=[smem, q, k, v, wide, wide],
        out_specs=[narrow, narrow, wide, scal, scal],
        out_shape=[jax.ShapeDtypeStruct((s_len, RET_HEADS * RET_QK), F32), jax.ShapeDtypeStruct((s_len, RET_HEADS * RET_QK), F32),
                   jax.ShapeDtypeStruct((s_len, RET_HEADS * RET_V), F32),
                   jax.ShapeDtypeStruct((RET_HEADS, 8, 128), F32), jax.ShapeDtypeStruct((RET_HEADS, 8, 128), F32)],
        scratch_shapes=[state, state, state, state, keep, keep, square, square, state, state],
        compiler_params=_params("arbitrary"),
    )(lg, proj, proj, proj, o_pre, dy)


def _silu_parts(z):
    sig = _sigmoid(z)
    return z * sig, sig * (1.0 + z * (1.0 - sig))


def outproj_fwd(x, gate, w_out, attn_outs, y_r, proj):
    s_len = x.shape[0]
    tm = min(256, s_len)

    def per_head(w):
        return jnp.concatenate([jnp.broadcast_to(w[:, h:h + 1], (tm, HEAD_DIM)) for h in range(N_HEADS_ATTN)], axis=1)

    def body(x_ref, gate_ref, w_ref, o1, l1, o2, l2, o3, l3, yr_ref, za_ref, zr_ref,
             xn_ref, oa_ref, lse_ref, lse4_ref, lse16_ref, so2, sl2, so3, sl3):
        silu_r, _ = _silu_parts(zr_ref[...].astype(F32))
        out_r = _dot((yr_ref[...] * silu_r).astype(BF16), w_ref[ATTN_WIDTH:, :])
        for src, dst in ((o2, so2), (l2, sl2), (o3, so3), (l3, sl3)):
            _merge_classes(src, dst)
        la, lb, lc = l1[...], _read_stage(sl2), _read_stage(sl3)
        m = jnp.maximum(jnp.maximum(la, lb), lc)
        lse = m + jnp.log(jnp.exp(la - m) + jnp.exp(lb - m) + jnp.exp(lc - m))
        o_a = (per_head(jnp.exp(la - lse)) * o1[...] + per_head(jnp.exp(lb - lse)) * _read_stage(so2)
               + per_head(jnp.exp(lc - lse)) * _read_stage(so3))
        oa_ref[...] = o_a
        lse_ref[...] = lse
        _fill_stage(sl2, lse)
        _split_classes(sl2, [lse4_ref, lse16_ref])
        silu_a, _ = _silu_parts(za_ref[...].astype(F32))
        out_a = _dot((o_a * silu_a).astype(BF16), w_ref[:ATTN_WIDTH, :])
        xn_ref[...] = x_ref[...] + gate_ref[...] * (out_a + out_r)

    row = lambda w: pl.BlockSpec((tm, w), lambda i: (i, 0))
    half, stat = row(ATTN_WIDTH), row(LANES)
    cls = [_class_block(dil, tm, ATTN_WIDTH, lambda i: (0, i, 0)) for dil in DILATIONS[1:]]
    cls_stat = [_class_block(dil, tm, LANES, lambda i: (0, i, 0)) for dil in DILATIONS[1:]]
    flat = [a for pair in attn_outs for a in pair]
    sds = jax.ShapeDtypeStruct
    return pl.pallas_call(
        body, name="outproj_fwd", grid=(s_len // tm,),
        in_specs=[row(D_MODEL), pl.BlockSpec((1, D_MODEL), lambda i: (0, 0)),
                  pl.BlockSpec((D_MODEL, D_MODEL), lambda i: (0, 0)),
                  half, stat, cls[0], cls_stat[0], cls[1], cls_stat[1], half,
                  pl.BlockSpec((tm, ATTN_WIDTH), lambda i: (i, COL_ZA // ATTN_WIDTH)),
                  pl.BlockSpec((tm, ATTN_WIDTH), lambda i: (i, COL_ZR // ATTN_WIDTH))],
        out_specs=[row(D_MODEL), half, stat] + cls_stat,
        out_shape=[sds((s_len, D_MODEL), F32), sds((s_len, ATTN_WIDTH), F32), sds((s_len, LANES), F32)]
                  + [sds((dil, s_len // dil, LANES), F32) for dil in DILATIONS[1:]],
        scratch_shapes=[_stage_shape(tm, ATTN_WIDTH), _stage_shape(tm, LANES)] * 2,
        compiler_params=_params("arbitrary"),
    )(x, gate, w_out, *flat, y_r, proj, proj)


def loss_head(x, gain, target):
    s_len = x.shape[0]
    tm = min(256, s_len)

    def body(x_ref, g_ref, t_ref, dx_ref, loss_ref, dg_ref):
        @pl.when(pl.program_id(0) == 0)
        def _():
            loss_ref[...] = jnp.zeros_like(loss_ref)
            dg_ref[...] = jnp.zeros_like(dg_ref)
        xv, g = x_ref[...], g_ref[...]
        r = lax.rsqrt(jnp.mean(xv * xv, axis=-1, keepdims=True) + NORM_EPS)
        xn = xv * r
        err = xn * g - t_ref[...]
        part = 0.5 * jnp.sum(jnp.mean(err * err, axis=-1, keepdims=True), axis=0, keepdims=True)
        loss_ref[...] += jnp.broadcast_to(part, loss_ref.shape)
        dy = err * (1.0 / D_MODEL)
        dg_ref[...] += jnp.sum(dy * xn, axis=0, keepdims=True)
        dxn = dy * g
        dx_ref[...] = r * (dxn - xn * jnp.mean(dxn * xn, axis=-1, keepdims=True))

    row = pl.BlockSpec((tm, D_MODEL), lambda i: (i, 0))
    vec = pl.BlockSpec((1, D_MODEL), lambda i: (0, 0))
    return pl.pallas_call(
        body, name="loss_head", grid=(s_len // tm,), in_specs=[row, vec, row],
        out_specs=[row, pl.BlockSpec((8, 128), lambda i: (0, 0)), vec],
        out_shape=[jax.ShapeDtypeStruct((s_len, D_MODEL), F32), jax.ShapeDtypeStruct((8, 128), F32),
                   jax.ShapeDtypeStruct((1, D_MODEL), F32)],
        compiler_params=_params("arbitrary"),
    )(x, gain, target)


def outproj_bwd(dxn, gate, w_out, o_a, y_r, proj):
    s_len = dxn.shape[0]
    tm = min(256, s_len)

    def body(dx_ref, gate_ref, w_ref, oa_ref, yr_ref, za_ref, zr_ref,
             doa_ref, dl_ref, dyr_ref, dza_ref, dzr_ref, y_ref, dxb_ref, do4, do16, dl4, dl16, stage, stat_stage):
        dxv = dx_ref[...]
        dxb_ref[...] = dxv.astype(BF16)
        dy = _dot_nt((dxv * gate_ref[...]).astype(BF16), w_ref[...])
        dy_a, dy_r = dy[:, :ATTN_WIDTH], dy[:, ATTN_WIDTH:]
        o_a, y_rv = oa_ref[...], yr_ref[...]
        silu_a, dsilu_a = _silu_parts(za_ref[...].astype(F32))
        silu_r, dsilu_r = _silu_parts(zr_ref[...].astype(F32))
        do_a = dy_a * silu_a
        doa_ref[...] = do_a.astype(BF16)
        _fill_stage(stage, do_a)
        _split_classes(stage, [do4, do16])
        prod = do_a * o_a
        lane = _iota((tm, LANES), 1)
        delta = jnp.zeros((tm, LANES), F32)
        for h in range(N_HEADS_ATTN):
            delta = jnp.where(lane == h, jnp.sum(prod[:, h * HEAD_DIM:(h + 1) * HEAD_DIM], axis=1, keepdims=True), delta)
        dl_ref[...] = delta
        _fill_stage(stat_stage, delta)
        _split_classes(stat_stage, [dl4, dl16])
        dyr_ref[...] = dy_r * silu_r
        dza_ref[...] = (dy_a * o_a * dsilu_a).astype(BF16)
        dzr_ref[...] = (dy_r * y_rv * dsilu_r).astype(BF16)
        y_ref[...] = jnp.concatenate([(o_a * silu_a).astype(BF16), (y_rv * silu_r).astype(BF16)], axis=1)

    row = lambda w: pl.BlockSpec((tm, w), lambda i: (i, 0))
    half = row(ATTN_WIDTH)
    sds = lambda w, dt: jax.ShapeDtypeStruct((s_len, w), dt)
    in_specs = [row(D_MODEL), pl.BlockSpec((1, D_MODEL), lambda i: (0, 0)),
                pl.BlockSpec((D_MODEL, D_MODEL), lambda i: (0, 0)), half, half,
                pl.BlockSpec((tm, ATTN_WIDTH), lambda i: (i, COL_ZA // ATTN_WIDTH)),
                pl.BlockSpec((tm, ATTN_WIDTH), lambda i: (i, COL_ZR // ATTN_WIDTH))]
    cls = [_class_block(dil, tm, ATTN_WIDTH, lambda i: (0, i, 0)) for dil in DILATIONS[1:]]
    cls_stat = [_class_block(dil, tm, LANES, lambda i: (0, i, 0)) for dil in DILATIONS[1:]]
    out_specs = [half, row(LANES), half, half, half, row(D_MODEL), row(D_MODEL)] + cls + cls_stat
    out_shape = [sds(ATTN_WIDTH, BF16), sds(LANES, F32), sds(ATTN_WIDTH, F32), sds(ATTN_WIDTH, BF16),
                 sds(ATTN_WIDTH, BF16), sds(D_MODEL, BF16), sds(D_MODEL, BF16)]
    out_shape += [jax.ShapeDtypeStruct((dil, s_len // dil, ATTN_WIDTH), BF16) for dil in DILATIONS[1:]]
    out_shape += [jax.ShapeDtypeStruct((dil, s_len // dil, LANES), F32) for dil in DILATIONS[1:]]
    return pl.pallas_call(
        body, name="outproj_bwd", grid=(s_len // tm,),
        in_specs=in_specs, out_specs=out_specs, out_shape=out_shape,
        scratch_shapes=[_stage_shape(tm, ATTN_WIDTH), _stage_shape(tm, LANES)],
        compiler_params=_params("arbitrary"),
    )(dxn, gate, w_out, o_a, y_r, proj, proj)


def wout_grad(y, dxb, gate, w_out):
    s_len = y.shape[0]
    tf, ts = 512, min(1024, s_len)

    def body(y_ref, dx_ref, gate_ref, w_ref, dw_ref, dgate_ref, acc):
        f, s = pl.program_id(0), pl.program_id(1)

        @pl.when((f == 0) & (s == 0))
        def _():
            dgate_ref[...] = jnp.zeros_like(dgate_ref)

        @pl.when(s == 0)
        def _():
            acc[...] = jnp.zeros_like(acc)
        acc[...] += _dot_tn(y_ref[...], dx_ref[...])

        @pl.when(s == pl.num_programs(1) - 1)
        def _():
            m = acc[...]
            dw_ref[...] = (m * gate_ref[...]).astype(BF16).reshape(dw_ref.shape)
            dgate_ref[...] += jnp.sum(m * w_ref[...].astype(F32), axis=0, keepdims=True)

    per = tf // W_OUT_SHARD
    return pl.pallas_call(
        body, name="wout_grad", grid=(D_MODEL // tf, s_len // ts),
        in_specs=[pl.BlockSpec((ts, tf), lambda f, s: (s, f)), pl.BlockSpec((ts, D_MODEL), lambda f, s: (s, 0)),
                  pl.BlockSpec((1, D_MODEL), lambda f, s: (0, 0)), pl.BlockSpec((tf, D_MODEL), lambda f, s: (f, 0))],
        out_specs=[pl.BlockSpec((per, W_OUT_SHARD, D_MODEL), lambda f, s: (f, 0, 0)),
                   pl.BlockSpec((1, D_MODEL), lambda f, s: (0, 0))],
        out_shape=[jax.ShapeDtypeStruct((N_DEV, W_OUT_SHARD, D_MODEL), BF16), jax.ShapeDtypeStruct((1, D_MODEL), F32)],
        scratch_shapes=[pltpu.VMEM((tf, D_MODEL), F32)],
        compiler_params=_params("arbitrary", "arbitrary"),
    )(y, dxb, gate, w_out)


def assemble_dproj(dqkv_a, dz_a, dq_r, dk_r, dv_r, dz_r):
    s_len = dz_a.shape[0]
    tm = min(512, s_len)

    def body(*refs):
        pat, (dza, dqr, dkr, dvr, dzr, out, stage) = refs[:9], refs[9:]
        for t in range(3):
            tot = pat[t][...].astype(F32)
            for p in (1, 2):
                _merge_classes(pat[3 * p + t], stage)
                tot = tot + _read_stage(stage)
            out[:, t * ATTN_WIDTH:(t + 1) * ATTN_WIDTH] = tot.astype(BF16)
        out[:, COL_ZA:COL_QR] = dza[...]
        out[:, COL_QR:COL_KR] = dqr[...].astype(BF16)
        out[:, COL_KR:COL_VR] = dkr[...].astype(BF16)
        out[:, COL_VR:COL_ZR] = dvr[...].astype(BF16)
        out[:, COL_ZR:IN_W] = dzr[...]

    row = lambda w: pl.BlockSpec((tm, w), lambda i: (i, 0))
    cls = [_class_block(dil, tm, ATTN_WIDTH, lambda i: (0, i, 0)) for dil in DILATIONS[1:]]
    flat = [dqkv_a[p][t] for p in range(3) for t in range(3)]
    return pl.pallas_call(
        body, name="assemble_dproj", grid=(s_len // tm,),
        in_specs=[row(ATTN_WIDTH)] * 3 + [cls[0]] * 3 + [cls[1]] * 3
                 + [row(ATTN_WIDTH), row(512), row(512), row(ATTN_WIDTH), row(ATTN_WIDTH)],
        out_specs=row(IN_W), out_shape=jax.ShapeDtypeStruct((s_len, IN_W), BF16),
        scratch_shapes=[_stage_shape(tm, ATTN_WIDTH)],
        compiler_params=_params("arbitrary"),
    )(*flat, dz_a, dq_r, dk_r, dv_r, dz_r)


def inproj_bwd(dproj, w, x, g, scale1p, dxn, exchange):
    s_len = x.shape[0]
    tm, tk = min(512, s_len), 1024

    def body(dp_ref, w_ref, x_ref, g_ref, sc_ref, dxn_ref, pa_ref, pb_ref, dx_ref, st_ref, ra_ref, rb_ref,
             acc, send_sems, recv_sems):
        first, last = _grid_edge(2)
        copies = _reduce_phase2([pa_ref, pb_ref], [ra_ref, rb_ref], send_sems, recv_sems)
        pl.when(first)(lambda: _start_all(copies))
        i, k = pl.program_id(0), pl.program_id(1)

        @pl.when((i == 0) & (k == 0))
        def _():
            st_ref[...] = jnp.zeros_like(st_ref)

        @pl.when(k == 0)
        def _():
            acc[...] = jnp.zeros_like(acc)
        acc[...] += _dot_nt(dp_ref[...], w_ref[...])

        @pl.when(k == pl.num_programs(1) - 1)
        def _():
            dh, xv = acc[...], x_ref[...]
            r = lax.rsqrt(jnp.mean(xv * xv, axis=-1, keepdims=True) + NORM_EPS)
            xn = xv * r
            st_ref[0:1, :] += jnp.sum(dh, axis=0, keepdims=True)
            st_ref[3:4, :] += jnp.sum(dh * xn, axis=0, keepdims=True)
            dn = dh * (sc_ref[...] * g_ref[...])
            dx_ref[...] = r * (dn - xn * jnp.mean(dn * xn, axis=-1, keepdims=True)) + dxn_ref[...]

        @pl.when(last)
        def _():
            st_ref[1:2, :] = st_ref[3:4, :] * g_ref[...]
            st_ref[2:3, :] = st_ref[3:4, :] * sc_ref[...]
            _wait_all(copies)

    row = pl.BlockSpec((tm, D_MODEL), lambda i, k: (i, 0))
    vec = pl.BlockSpec((1, D_MODEL), lambda i, k: (0, 0))
    return pl.pallas_call(
        body, name="inproj_bwd_reduce2", grid=(s_len // tm, IN_W // tk),
        in_specs=[pl.BlockSpec((tm, tk), lambda i, k: (i, k)), pl.BlockSpec((D_MODEL, tk), lambda i, k: (0, k)),
                  row, vec, vec, row, HBM, HBM],
        out_specs=[row, pl.BlockSpec((8, D_MODEL), lambda i, k: (0, 0)), HBM, HBM],
        out_shape=[jax.ShapeDtypeStruct((s_len, D_MODEL), F32), jax.ShapeDtypeStruct((8, D_MODEL), F32)]
                  + _landing_shapes(3),
        scratch_shapes=[pltpu.VMEM((tm, D_MODEL), F32)] + REDUCE2_SEMS,
        compiler_params=_params("arbitrary", "arbitrary"),
    )(dproj, w, x, g, scale1p, dxn, *exchange)


def win_grad(h, dproj):
    s_len = h.shape[0]
    ts = min(2048, s_len)

    def body(h_ref, dp_ref, dw_ref, acc):
        s = pl.program_id(1)

        @pl.when(s == 0)
        def _():
            acc[...] = jnp.zeros_like(acc)
        acc[...] += _dot_tn(h_ref[...], dp_ref[...])

        @pl.when(s == pl.num_programs(1) - 1)
        def _():
            dw_ref[...] = acc[...].astype(BF16)

    return pl.pallas_call(
        body, name="win_grad", grid=(N_DEV, s_len // ts),
        in_specs=[pl.BlockSpec((ts, D_MODEL), lambda j, s: (s, 0)), pl.BlockSpec((ts, W_IN_SHARD), lambda j, s: (s, j))],
        out_specs=pl.BlockSpec((None, D_MODEL, W_IN_SHARD), lambda j, s: (j, 0, 0)),
        out_shape=jax.ShapeDtypeStruct((N_DEV, D_MODEL, W_IN_SHARD), BF16),
        scratch_shapes=[pltpu.VMEM((D_MODEL, W_IN_SHARD), F32)],
        compiler_params=_params("arbitrary", "arbitrary"),
    )(h, dproj)


def ada_fwd(c_all, w_ada):
    def body(c_ref, w_ref, act_ref, part_ref):
        cv = c_ref[...]
        act = cv * _sigmoid(cv)
        act_ref[...] = act
        part_ref[...] = _dot(act.astype(BF16), w_ref[...].astype(BF16))

    return pl.pallas_call(
        body, name="ada_fwd", grid=(DEPTH,),
        in_specs=[pl.BlockSpec((N_DEV, D_MODEL), lambda l: (0, 0)),
                  pl.BlockSpec((None, D_MODEL, W_ADA_SHARD), lambda l: (l, 0, 0))],
        out_specs=[pl.BlockSpec((N_DEV, D_MODEL), lambda l: (0, 0)),
                   pl.BlockSpec((None, N_DEV, W_ADA_SHARD), lambda l: (l, 0, 0))],
        out_shape=[jax.ShapeDtypeStruct((N_DEV, D_MODEL), F32), jax.ShapeDtypeStruct((DEPTH, N_DEV, W_ADA_SHARD), F32)],
        compiler_params=_params("arbitrary"),
    )(c_all, w_ada)


def _adamw(w, g, m, v):
    m = ADAM_B1 * m + (1.0 - ADAM_B1) * g
    v = ADAM_B2 * v + (1.0 - ADAM_B2) * (g * g)
    delta = -ADAM_LR * ((m * ADAM_C1) / (jnp.sqrt(v * ADAM_C2) + ADAM_EPS) + ADAM_WD * w)
    return delta, m, v


def ada_update(act_t, dmod, w, m, v):
    tr = 512

    def body(a_ref, d_ref, w_ref, m_ref, v_ref, g_out, dl_out, m_out, v_out):
        a = a_ref[...].astype(BF16).astype(F32)
        d = d_ref[...].astype(BF16).astype(F32)
        g = a[:, 0:1] * d[0:1, :]
        for b in range(1, N_DEV):
            g = g + a[:, b:b + 1] * d[b:b + 1, :]
        g_out[...] = g
        dl_out[...], m_out[...], v_out[...] = _adamw(w_ref[...], g, m_ref[...], v_ref[...])

    blk = pl.BlockSpec((None, tr, W_ADA_SHARD), lambda l, r: (l, r, 0))
    out = jax.ShapeDtypeStruct(w.shape, F32)
    return pl.pallas_call(
        body, name="ada_update", grid=(DEPTH, D_MODEL // tr),
        in_specs=[pl.BlockSpec((tr, N_DEV), lambda l, r: (r, 0)),
                  pl.BlockSpec((None, N_DEV, W_ADA_SHARD), lambda l, r: (l, 0, 0)), blk, blk, blk],
        out_specs=[blk] * 4, out_shape=[out] * 4, compiler_params=_params("arbitrary", "arbitrary"),
    )(act_t, dmod, w, m, v)


def chip_sum(pos, grads, landed, name):
    _, _, n_rows, n_cols = grads.shape
    tr = min(512, n_rows)

    def chip(k, pos_ref):
        return (pos_ref[0] ^ (k // 2)) * 2 + (pos_ref[1] ^ (k % 2))

    def body(pos_ref, g_ref, r_ref, out_ref):
        out_ref[...] = (g_ref[...].astype(F32) + r_ref[...].astype(F32)).astype(BF16)

    return pl.pallas_call(
        body, name=name,
        grid_spec=pltpu.PrefetchScalarGridSpec(
            num_scalar_prefetch=1, grid=(N_DEV // 2, n_rows // tr),
            in_specs=[pl.BlockSpec((None, None, tr, n_cols), lambda k, r, p: (chip(k, p), p[2], r, 0)),
                      pl.BlockSpec((None, tr, n_cols), lambda k, r, p: (chip(k, p), r, 0))],
            out_specs=pl.BlockSpec((None, tr, n_cols), lambda k, r, p: (k, r, 0))),
        out_shape=jax.ShapeDtypeStruct((N_DEV // 2, n_rows, n_cols), BF16),
        compiler_params=_params("arbitrary", "arbitrary"),
    )(pos, grads, landed)


def shard_update(layer, own, others, w, m, v, prev, name):
    _, n_rows, n_cols = w.shape
    tr = min(256, n_rows)

    def body(own_ref, oth_ref, w_ref, m_ref, v_ref, *rest):
        g_out, dl_out, m_out, v_out = rest[-4:]
        g = own_ref[...].astype(F32)
        for k in range(3):
            g = g + oth_ref[k].astype(F32)
        g_out[...] = g
        dl_out[...], m_out[...], v_out[...] = _adamw(w_ref[...], g, m_ref[...], v_ref[...])

    blk = pl.BlockSpec((None, tr, n_cols), lambda r: (layer, r, 0))
    out = jax.ShapeDtypeStruct(w.shape, F32)
    in_specs = [pl.BlockSpec((None, tr, n_cols), lambda r: (0, r, 0)), pl.BlockSpec((3, tr, n_cols), lambda r: (0, r, 0)),
                blk, blk, blk]
    aliases, extra = {}, ()
    if prev is not None:
        in_specs += [HBM] * 4
        aliases = {5 + t: t for t in range(4)}
        extra = tuple(prev)
    return pl.pallas_call(
        body, name=name, grid=(n_rows // tr,), in_specs=in_specs, out_specs=[blk] * 4, out_shape=[out] * 4,
        input_output_aliases=aliases, compiler_params=_params("arbitrary"),
    )(own, others, w, m, v, *extra)


def small_update(parts, w, m, v):
    def body(p_ref, w_ref, m_ref, v_ref, g_out, dl_out, m_out, v_out):
        g = p_ref[0]
        for k in range(1, N_DEV):
            g = g + p_ref[k]
        g_out[...] = g
        dl_out[...], m_out[...], v_out[...] = _adamw(w_ref[...], g, m_ref[...], v_ref[...])

    out = jax.ShapeDtypeStruct(w.shape, F32)
    return pl.pallas_call(body, name="small_update", out_shape=[out] * 4, compiler_params=_params())(parts, w, m, v)


def _two_level_allgather(srcs, dst_block, send_sems, recv_sems, local_sems):
    x, y, c = _position()
    me, sibling = (x, y, c), (x, y, 1 - c)
    chips = [(1 - x, y), (x, 1 - y), (1 - x, 1 - y)]
    n = len(srcs)

    def copy(a, k, block, to, src=None):
        dst = dst_block(a, _flat(*block))
        return pltpu.make_async_remote_copy(
            src_ref=dst if src is None else src, dst_ref=dst, send_sem=send_sems.at[a * 7 + k],
            recv_sem=recv_sems.at[a * 7 + k], device_id=to, device_id_type=MESH)

    mine = [pltpu.make_async_copy(srcs[a], dst_block(a, _flat(*me)), local_sems.at[a]) for a in range(n)]
    for cp in mine:
        cp.start()
    first = []
    for a in range(n):
        first.append(copy(a, 0, me, sibling, src=srcs[a]))
        first += [copy(a, 1 + j, me, (*chip, c), src=srcs[a]) for j, chip in enumerate(chips)]
    for cp in first:
        cp.start()
    passed = []
    for j, chip in enumerate(chips):
        for a in range(n):
            copy(a, 1 + j, (*chip, c), me).wait_recv()
            fwd = copy(a, 4 + j, (*chip, c), sibling)
            fwd.start()
            passed.append(fwd)
    for a in range(n):
        copy(a, 0, sibling, me).wait_recv()
        for j, chip in enumerate(chips):
            copy(a, 4 + j, (*chip, 1 - c), me).wait_recv()
    for cp in first + passed:
        cp.wait_send()
    for cp in mine:
        cp.wait()


def allgather_rows(x, name):
    def body(x_ref, out_ref, send_sems, recv_sems, local_sems):
        _two_level_allgather([x_ref], lambda a, idx: out_ref.at[idx], send_sems, recv_sems, local_sems)

    vmem = pl.BlockSpec(memory_space=pltpu.VMEM)
    return pl.pallas_call(
        body, name=name, in_specs=[vmem], out_specs=vmem,
        out_shape=jax.ShapeDtypeStruct((N_DEV,) + x.shape, x.dtype),
        scratch_shapes=[pltpu.SemaphoreType.DMA((7,)), pltpu.SemaphoreType.DMA((7,)), pltpu.SemaphoreType.DMA((1,))],
        compiler_params=_params(),
    )(x)


def _full_weight_shapes():
    return [jax.ShapeDtypeStruct((D_MODEL, IN_W), BF16), jax.ShapeDtypeStruct((D_MODEL, D_MODEL), BF16)]


def allgather_weights(w_in_b, w_out_b):
    def body(win_ref, wout_ref, fin_ref, fout_ref, s1, r1, l1, s2, r2):
        block = _weight_blocks(fin_ref, fout_ref)
        first = _gather_phase1([win_ref.at[0], wout_ref.at[0]], block, s1, r1, l1)
        _start_all(first)
        _wait_all(first)
        second = _gather_phase2(block, s2, r2)
        _start_all(second)
        _wait_all(second)

    return pl.pallas_call(
        body, name="allgather_weights", in_specs=[HBM, HBM], out_specs=[HBM, HBM], out_shape=_full_weight_shapes(),
        scratch_shapes=GATHER1_SEMS + GATHER2_SEMS, compiler_params=_params(),
    )(w_in_b, w_out_b)


def reduce_first(dw_in, dw_out, name):
    def body(ga_ref, gb_ref, ra_ref, rb_ref, send_sems, recv_sems):
        copies = _reduce_phase1([ga_ref, gb_ref], [ra_ref, rb_ref], send_sems, recv_sems)
        _start_all(copies)
        _wait_all(copies)

    return pl.pallas_call(
        body, name=name, in_specs=[HBM, HBM], out_specs=[HBM, HBM], out_shape=_landing_shapes(4),
        scratch_shapes=REDUCE1_SEMS, compiler_params=_params(),
    )(*_split_cores((dw_in, dw_out)))


def _one_class(a):
    return a.reshape((1,) + a.shape)


def layer_fwd(x, g, scale, shift, gate, w_in, w_out, lg, gather=None):
    proj, h, qkv4, qkv16, *began = inproj_fwd(x, g, 1.0 + scale, shift, w_in, gather)
    qkv = (_one_class(proj), qkv4, qkv16)
    attn_outs = [attn_fwd(arr, dil) for dil, arr in zip(DILATIONS, qkv)]
    attn_outs[0] = tuple(a[0] for a in attn_outs[0])
    o_pre, y_r, *next_weights = ret_fwd(proj, lg, began if gather is not None else None)
    next_weights = tuple(next_weights) if gather is not None else None
    x_new, o_a, lse, lse4, lse16 = outproj_fwd(x, gate, w_out, attn_outs, y_r, proj)
    saved = dict(x=x, proj=proj, h=h, qkv=qkv, o_a=o_a, lse=(_one_class(lse), lse4, lse16), o_pre=o_pre, y_r=y_r)
    return x_new, saved, next_weights


def layer_bwd(dxn, saved, g, scale, gate, w_in, w_out, lg, pos):
    proj = saved["proj"]
    do_a, delta, dyr, dz_a, dz_r, y, dxb, do4, do16, dl4, dl16 = outproj_bwd(
        dxn, gate, w_out, saved["o_a"], saved["y_r"], proj)
    dw_out, dgate = wout_grad(y, dxb, gate, w_out)
    dq_r, dk_r, dv_r, glf, glb = ret_bwd(proj, lg, saved["o_pre"], dyr)
    dos, deltas = (_one_class(do_a), do4, do16), (_one_class(delta), dl4, dl16)
    dqkv_a = [attn_bwd(arr, d_o, lse, dl, dil)
              for dil, arr, d_o, lse, dl in zip(DILATIONS, saved["qkv"], dos, saved["lse"], deltas)]
    dqkv_a[0] = [t[0] for t in dqkv_a[0]]
    dproj = assemble_dproj(dqkv_a, dz_a, dq_r, dk_r, dv_r, dz_r)
    dw_in = win_grad(saved["h"], dproj)
    landed = reduce_first(dw_in, dw_out, "reduce_first")
    sums = [chip_sum(pos, g4, r, "chip_sum") for g4, r in zip(_split_cores((dw_in, dw_out)), landed)]
    dx, stats, *others = inproj_bwd(dproj, w_in, saved["x"], g, 1.0 + scale, dxn, sums)
    dlg = jnp.concatenate([glf[:, 0, 0], glb[:, 0, 0]])
    return dx, stats[0:1], stats[1:2], dgate, stats[2:3], dlg, (sums, others)


ROWS_B_ADA = DEPTH * 3 * D_MODEL // 128
ROWS_GAIN = DEPTH * D_MODEL // 128
ROWS_FINAL = D_MODEL // 128
ROWS_MISC = 8
ROWS_SMALL = ROWS_B_ADA + ROWS_GAIN + ROWS_FINAL + ROWS_MISC


def _pack_small(b_ada_like, gain_like, final_like, dec_f, dec_b, loss=None):
    misc = jnp.zeros((ROWS_MISC, 128), F32)
    misc = misc.at[0, :2 * DEPTH * RET_HEADS].set(jnp.concatenate([dec_f.reshape(-1), dec_b.reshape(-1)]))
    if loss is not None:
        misc = misc.at[1, 0].set(loss)
    return jnp.concatenate([b_ada_like.reshape(ROWS_B_ADA, 128), gain_like.reshape(ROWS_GAIN, 128),
                            final_like.reshape(ROWS_FINAL, 128), misc], axis=0)


def _unpack_small(p):
    r0, r1, r2 = ROWS_B_ADA, ROWS_B_ADA + ROWS_GAIN, ROWS_B_ADA + ROWS_GAIN + ROWS_FINAL
    n = DEPTH * RET_HEADS
    return (p[:r0].reshape(DEPTH, 3 * D_MODEL), p[r0:r1].reshape(DEPTH, D_MODEL), p[r1:r2].reshape(D_MODEL),
            p[r2, :n].reshape(DEPTH, RET_HEADS), p[r2, n:2 * n].reshape(DEPTH, RET_HEADS))


def kernel(x, c, norm_gain, w_ada, b_ada, w_in, w_out, ret_decay_logit_f, ret_decay_logit_b, final_gain, loss_target, m_norm_gain, m_w_ada, m_b_ada, m_w_in, m_w_out, m_ret_decay_logit_f, m_ret_decay_logit_b, m_final_gain, v_norm_gain, v_w_ada, v_b_ada, v_w_in, v_w_out, v_ret_decay_logit_f, v_ret_decay_logit_b, v_final_gain):
    px, py, pc = _position()
    me = _flat(px, py, pc)
    pos = jnp.stack([px, py, pc]).astype(jnp.int32)
    x2, target = x[0], loss_target[0]

    w_in_b, w_out_b = w_in.astype(BF16), w_out.astype(BF16)
    weights = allgather_weights(w_in_b, w_out_b)

    c_all = allgather_rows(c.reshape(D_MODEL // 128, 128), "allgather_c").reshape(N_DEV, D_MODEL)
    act, mod_part = ada_fwd(c_all, w_ada)
    mod_all = allgather_rows(mod_part.reshape(-1, 128), "allgather_mod").reshape(N_DEV, DEPTH, N_DEV, W_ADA_SHARD)
    mod = lax.dynamic_index_in_dim(mod_all, me, axis=2, keepdims=False)
    mod = mod.transpose(1, 0, 2).reshape(DEPTH, 3 * D_MODEL) + b_ada
    shift, scale, gate = mod[:, :D_MODEL], mod[:, D_MODEL:2 * D_MODEL], mod[:, 2 * D_MODEL:]

    lg = jnp.concatenate([jax.nn.log_sigmoid(ret_decay_logit_f), jax.nn.log_sigmoid(ret_decay_logit_b)], axis=1)

    h = x2
    saved, layer_weights = [], []
    for l in range(DEPTH):
        layer_weights.append(weights)
        gather = (w_in_b, w_out_b, l + 1) if l + 1 < DEPTH else None
        h, sv, weights = layer_fwd(h, norm_gain[l:l + 1], scale[l:l + 1], shift[l:l + 1], gate[l:l + 1],
                                   *layer_weights[l], lg[l], gather)
        saved.append(sv)
    dh, loss_part, dfinal = loss_head(h, final_gain.reshape(1, D_MODEL), target)

    dmod, dgain, dlg, reduced = [None] * DEPTH, [None] * DEPTH, [None] * DEPTH, [None] * DEPTH
    for l in reversed(range(DEPTH)):
        dh, dshift, dscale, dgate, dg, dlg[l], reduced[l] = layer_bwd(
            dh, saved[l], norm_gain[l:l + 1], scale[l:l + 1], gate[l:l + 1], *layer_weights[l], lg[l], pos)
        dmod[l] = jnp.concatenate([dshift, dscale, dgate], axis=1)
        dgain[l] = dg

    dlg = jnp.stack(dlg)
    dlogit_f = dlg[:, :RET_HEADS] * jax.nn.sigmoid(-ret_decay_logit_f)
    dlogit_b = dlg[:, RET_HEADS:] * jax.nn.sigmoid(-ret_decay_logit_b)
    packed = _pack_small(jnp.concatenate(dmod, axis=0), jnp.concatenate(dgain, axis=0), dfinal, dlogit_f, dlogit_b,
                         loss=loss_part[0, 0])
    gathered = allgather_rows(packed, "allgather_small")
    small = small_update(gathered,
                         _pack_small(b_ada, norm_gain, final_gain, ret_decay_logit_f, ret_decay_logit_b),
                         _pack_small(m_b_ada, m_norm_gain, m_final_gain, m_ret_decay_logit_f, m_ret_decay_logit_b),
                         _pack_small(v_b_ada, v_norm_gain, v_final_gain, v_ret_decay_logit_f, v_ret_decay_logit_b))
    loss = small[0][ROWS_B_ADA + ROWS_GAIN + ROWS_FINAL + 1, 0]
    (g_b_ada, g_gain, g_final, g_dec_f, g_dec_b), (d_b_ada, d_gain, d_final, d_dec_f, d_dec_b), \
        (m_b_ada2, m_gain2, m_final2, m_dec_f2, m_dec_b2), (v_b_ada2, v_gain2, v_final2, v_dec_f2, v_dec_b2) = \
        [_unpack_small(p) for p in small]

    dmod_all = gathered[:, :ROWS_B_ADA].reshape(N_DEV, DEPTH, 3 * D_MODEL)
    dmod_mine = lax.dynamic_slice_in_dim(dmod_all, me * W_ADA_SHARD, W_ADA_SHARD, axis=2).transpose(1, 0, 2)
    g_w_ada, d_w_ada, m_w_ada2, v_w_ada2 = ada_update(act.T, dmod_mine, w_ada, m_w_ada, v_w_ada)

    upd_in = upd_out = None
    for l in reversed(range(DEPTH)):
        upd_in = shard_update(l, reduced[l][0][0], reduced[l][1][0], w_in, m_w_in, v_w_in, upd_in, f"w_in_update_{l}")
        upd_out = shard_update(l, reduced[l][0][1], reduced[l][1][1], w_out, m_w_out, v_w_out, upd_out, f"w_out_update_{l}")
    g_w_in, d_w_in, m_w_in2, v_w_in2 = upd_in
    g_w_out, d_w_out, m_w_out2, v_w_out2 = upd_out

    return (loss, dh[None],
            g_gain, g_w_ada, g_b_ada, g_w_in, g_w_out, g_dec_f, g_dec_b, g_final,
            d_gain, d_w_ada, d_b_ada, d_w_in, d_w_out, d_dec_f, d_dec_b, d_final,
            m_gain2, m_w_ada2, m_b_ada2, m_w_in2, m_w_out2, m_dec_f2, m_dec_b2, m_final2,
            v_gain2, v_w_ada2, v_b_ada2, v_w_in2, v_w_out2, v_dec_f2, v_dec_b2, v_final2)
```

```python
import functools
import math

import jax
import jax.numpy as jnp
from jax import lax
from jax.experimental import pallas as pl
from jax.experimental.pallas import tpu as pltpu

F32, BF16 = jnp.float32, jnp.bfloat16

D_MODEL = 2048
DEPTH = 4
N_DEV = 8
ATTN_WIDTH = 1024
HEAD_DIM = 128
N_HEADS_ATTN = 8
DILATIONS = (1, 4, 16)
RADIUS = 64
RET_HEADS = 4
RET_QK = 128
RET_V = 256
RET_CHUNK = 256
IN_W = 7168
QKV_A = 3 * ATTN_WIDTH
COL_ZA, COL_QR, COL_KR, COL_VR, COL_ZR = 3072, 4096, 4608, 5120, 6144
W_IN_SHARD = IN_W // N_DEV
W_OUT_SHARD = D_MODEL // N_DEV
W_ADA_SHARD = 3 * D_MODEL // N_DEV
NORM_EPS = 1e-6
MASK_VALUE = -1e30
ATTN_SCALE = HEAD_DIM ** -0.5
RET_SCALE = RET_QK ** -0.5
LN2 = math.log(2.0)

ADAM_LR, ADAM_B1, ADAM_B2, ADAM_EPS, ADAM_WD, ADAM_STEP = 0.001, 0.9, 0.999, 1e-08, 0.01, 10
ADAM_C1 = 1.0 / (1.0 - ADAM_B1 ** ADAM_STEP)
ADAM_C2 = 1.0 / (1.0 - ADAM_B2 ** ADAM_STEP)

VMEM_LIMIT_BYTES = 56 * 1024 * 1024
MESH = pl.DeviceIdType.MESH


def _params(*sem):
    return pltpu.CompilerParams(dimension_semantics=sem if sem else None, vmem_limit_bytes=VMEM_LIMIT_BYTES)


def _dot(a, b):
    return jnp.dot(a, b, preferred_element_type=F32)


def _dot_nt(a, b):
    return lax.dot_general(a, b, (((1,), (1,)), ((), ())), preferred_element_type=F32)


def _dot_tn(a, b):
    return lax.dot_general(a, b, (((0,), (0,)), ((), ())), preferred_element_type=F32)


def _iota(shape, dim):
    return lax.broadcasted_iota(jnp.int32, shape, dim)


def _sigmoid(z):
    return 1.0 / (1.0 + jnp.exp(-z))


HBM = pl.BlockSpec(memory_space=pl.ANY)


def _position():
    return lax.axis_index("x"), lax.axis_index("y"), lax.axis_index("c")


def _flat(px, py, pc):
    return 4 * px + 2 * py + pc


def _remote(src, dst, send_sem, recv_sem, to):
    return pltpu.make_async_remote_copy(src_ref=src, dst_ref=dst, send_sem=send_sem, recv_sem=recv_sem,
                                        device_id=to, device_id_type=MESH)


def _weight_blocks(fin_ref, fout_ref):
    def block(a, idx):
        if a == 0:
            return fin_ref.at[:, pl.ds(pl.multiple_of(idx * W_IN_SHARD, 128), W_IN_SHARD)]
        return fout_ref.at[pl.ds(pl.multiple_of(idx * W_OUT_SHARD, W_OUT_SHARD), W_OUT_SHARD), :]
    return block


GATHER1_SEMS = [pltpu.SemaphoreType.DMA((8,)), pltpu.SemaphoreType.DMA((8,)), pltpu.SemaphoreType.DMA((2,))]
GATHER2_SEMS = [pltpu.SemaphoreType.DMA((6,)), pltpu.SemaphoreType.DMA((6,))]
REDUCE1_SEMS = [pltpu.SemaphoreType.DMA((2,)), pltpu.SemaphoreType.DMA((2,))]
REDUCE2_SEMS = [pltpu.SemaphoreType.DMA((6,)), pltpu.SemaphoreType.DMA((6,))]


def _gather_phase1(srcs, block, send_sems, recv_sems, local_sems):
    x, y, c = _position()
    mine = [block(a, _flat(x, y, c)) for a in range(2)]
    copies = [pltpu.make_async_copy(srcs[a], mine[a], local_sems.at[a]) for a in range(2)]
    for a in range(2):
        copies.append(_remote(srcs[a], mine[a], send_sems.at[4 * a], recv_sems.at[4 * a], (x, y, 1 - c)))
        for j, (px, py) in enumerate([(1 - x, y), (x, 1 - y), (1 - x, 1 - y)]):
            copies.append(_remote(srcs[a], mine[a], send_sems.at[4 * a + 1 + j], recv_sems.at[4 * a + 1 + j], (px, py, c)))
    return copies


def _gather_phase2(block, send_sems, recv_sems):
    x, y, c = _position()
    copies = []
    for a in range(2):
        for j, (px, py) in enumerate([(1 - x, y), (x, 1 - y), (1 - x, 1 - y)]):
            blk = block(a, _flat(px, py, c))
            copies.append(_remote(blk, blk, send_sems.at[3 * a + j], recv_sems.at[3 * a + j], (x, y, 1 - c)))
    return copies


def _reduce_phase1(grads, landings, send_sems, recv_sems):
    x, y, c = _position()
    return [_remote(g.at[:, 1 - c], r, send_sems.at[a], recv_sems.at[a], (x, y, 1 - c))
            for a, (g, r) in enumerate(zip(grads, landings))]


def _reduce_phase2(sums, landings, send_sems, recv_sems):
    x, y, c = _position()
    copies = []
    for a, (p, r) in enumerate(zip(sums, landings)):
        for k in (1, 2, 3):
            to = (1 - x if k & 2 else x, 1 - y if k & 1 else y, c)
            copies.append(_remote(p.at[k], r.at[k - 1], send_sems.at[3 * a + k - 1], recv_sems.at[3 * a + k - 1], to))
    return copies


def _landing_shapes(n):
    return [jax.ShapeDtypeStruct((n, D_MODEL, W_IN_SHARD), BF16), jax.ShapeDtypeStruct((n, W_OUT_SHARD, D_MODEL), BF16)]


def _split_cores(slabs):
    return tuple(s.reshape((N_DEV // 2, 2) + s.shape[1:]) for s in slabs)


def _start_all(copies):
    for cp in copies:
        cp.start()


def _wait_all(copies):
    for cp in copies:
        cp.wait()


def _grid_edge(n_axes):
    first = last = None
    for ax in range(n_axes):
        f = pl.program_id(ax) == 0
        e = pl.program_id(ax) == pl.num_programs(ax) - 1
        first = f if first is None else first & f
        last = e if last is None else last & e
    return first, last


LANES = 128


def _stage_shape(rows, width):
    return pltpu.VMEM((width // LANES, rows, LANES), F32)


def _fill_stage(stage_ref, value):
    for t in range(stage_ref.shape[0]):
        stage_ref[t] = value[:, t * LANES:(t + 1) * LANES]


def _read_stage(stage_ref):
    return jnp.concatenate([stage_ref[t] for t in range(stage_ref.shape[0])], axis=1)


def _split_classes(stage_ref, out_refs):
    n_t, rows, _ = stage_ref.shape
    for out_ref in out_refs:
        dil = out_ref.shape[0]
        for r in range(dil):
            for t in range(n_t):
                piece = stage_ref[t, pl.ds(r, rows // dil, stride=dil), :]
                out_ref[r, :, t * LANES:(t + 1) * LANES] = piece.astype(out_ref.dtype)


def _merge_classes(in_ref, stage_ref):
    dil, per = in_ref.shape[0], in_ref.shape[1]
    for r in range(dil):
        for t in range(stage_ref.shape[0]):
            stage_ref[t, pl.ds(r, per, stride=dil), :] = in_ref[r, :, t * LANES:(t + 1) * LANES].astype(F32)


def _class_block(dil, tm, width, index_map):
    return pl.BlockSpec((dil, tm // dil, width), index_map)


def inproj_fwd(x, g, scale1p, shift, w, gather=None):
    s_len = x.shape[0]
    tm, tn = min(1024, s_len), 512

    n_qkv = QKV_A // tn

    def body(x_ref, g_ref, sc_ref, sh_ref, w_ref, *rest):
        if gather is not None:
            (win_ref, wout_ref, proj_ref, h_ref, q4_ref, q16_ref, fin_ref, fout_ref, stage,
             send_sems, recv_sems, local_sems) = rest
            first, last = _grid_edge(2)
            copies = _gather_phase1([win_ref.at[gather[2]], wout_ref.at[gather[2]]], _weight_blocks(fin_ref, fout_ref),
                                    send_sems, recv_sems, local_sems)
            pl.when(first)(lambda: _start_all(copies))
        else:
            proj_ref, h_ref, q4_ref, q16_ref, stage = rest

        @pl.when(pl.program_id(1) == 0)
        def _():
            xv = x_ref[...]
            r = lax.rsqrt(jnp.mean(xv * xv, axis=-1, keepdims=True) + NORM_EPS)
            h_ref[...] = ((xv * r * g_ref[...]) * sc_ref[...] + sh_ref[...]).astype(BF16)
        res = _dot(h_ref[...], w_ref[...])
        proj_ref[...] = res.astype(BF16)

        @pl.when(pl.program_id(1) < n_qkv)
        def _():
            _fill_stage(stage, res)
            _split_classes(stage, [q4_ref, q16_ref])
        if gather is not None:
            pl.when(last)(lambda: _wait_all(copies))

    vec = pl.BlockSpec((1, D_MODEL), lambda i, j: (0, 0))
    in_specs = [pl.BlockSpec((tm, D_MODEL), lambda i, j: (i, 0)), vec, vec, vec,
                pl.BlockSpec((D_MODEL, tn), lambda i, j: (0, j))]
    out_specs = [pl.BlockSpec((tm, tn), lambda i, j: (i, j)), pl.BlockSpec((tm, D_MODEL), lambda i, j: (i, 0))]
    out_shape = [jax.ShapeDtypeStruct((s_len, IN_W), BF16), jax.ShapeDtypeStruct((s_len, D_MODEL), BF16)]
    for dil in DILATIONS[1:]:
        out_specs.append(pl.BlockSpec((dil, tm // dil, tn), lambda i, j: (0, i, jnp.minimum(j, n_qkv - 1))))
        out_shape.append(jax.ShapeDtypeStruct((dil, s_len // dil, QKV_A), BF16))
    scratch = [_stage_shape(tm, tn)]
    extra = ()
    if gather is not None:
        in_specs += [HBM, HBM]
        out_specs += [HBM, HBM]
        out_shape += [jax.ShapeDtypeStruct((D_MODEL, IN_W), BF16), jax.ShapeDtypeStruct((D_MODEL, D_MODEL), BF16)]
        scratch += GATHER1_SEMS
        extra = tuple(gather[:2])
    return pl.pallas_call(
        body, name="inproj_fwd_gather" if gather is not None else "inproj_fwd", grid=(s_len // tm, IN_W // tn),
        in_specs=in_specs, out_specs=out_specs, out_shape=out_shape, scratch_shapes=scratch,
        compiler_params=_params("arbitrary", "arbitrary"),
    )(x, g, scale1p, shift, w, *extra)


MASK_DISTANCE = 1e33
ATTN_TILE = 128


ATTN_UNROLL_FWD, ATTN_UNROLL_BWD = 30, 10


def _attn_plan(sub_len, unroll):
    tq = min(sub_len, ATTN_TILE)
    win = min(sub_len, tq + 2 * RADIUS)
    heads = 1 if sub_len > 1024 else (2 if sub_len > 256 else N_HEADS_ATTN)
    return tq, win, sub_len // tq, heads, unroll


def _attn_tiles(sub_len, tq, win, n_tiles, unroll, tile):
    tile(0, 0, 0)
    if n_tiles > 2:
        def mid(i, carry):
            q0 = pl.multiple_of(i * tq, tq)
            tile(q0, pl.multiple_of(q0 - RADIUS, RADIUS), 1)
            return carry
        lax.fori_loop(1, n_tiles - 1, mid, 0, unroll=min(unroll, n_tiles - 2))
    if n_tiles > 1:
        tile(sub_len - tq, sub_len - win, 2)


def _attn_bias(bias_ref, head, heads, tq, win, dil):
    h = pl.program_id(1) * heads + head
    slope = jnp.exp(-(h + 1).astype(F32) * LN2 * jnp.ones((1, 1), F32))
    rel = _iota((tq, win), 1) - _iota((tq, win), 0)
    for v, off in enumerate((0, RADIUS, win - tq)):
        dist = jnp.abs(rel - off)
        bias_ref[v] = slope * jnp.where(dist <= RADIUS, (dist * dil).astype(F32), MASK_DISTANCE)


def _attn_specs(n_cls, sub_len, heads):
    width = heads * HEAD_DIM
    per = ATTN_WIDTH // width

    def col(part):
        return pl.BlockSpec((None, sub_len, width), lambda r, g: (r, 0, part * per + g))
    return col, (n_cls, N_HEADS_ATTN // heads)


def _stat_spec(sub_len):
    return pl.BlockSpec((None, sub_len, LANES), lambda r, g: (r, 0, 0))


def _stat_column(ref, rows, lane, h_abs):
    return jnp.sum(jnp.where(lane == h_abs, ref[rows, :], 0.0), axis=1, keepdims=True)


def attn_fwd(qkv, dil):
    n_cls, sub_len, _ = qkv.shape
    tq, win, n_tiles, heads, unroll = _attn_plan(sub_len, ATTN_UNROLL_FWD)

    def body(q_ref, k_ref, v_ref, o_ref, lse_ref, bias_ref):
        @pl.when(pl.program_id(1) == 0)
        def _():
            lse_ref[...] = jnp.zeros_like(lse_ref)
        lane = _iota((tq, LANES), 1)
        for head in range(heads):
            lanes = slice(head * HEAD_DIM, (head + 1) * HEAD_DIM)
            h_abs = pl.program_id(1) * heads + head
            _attn_bias(bias_ref, head, heads, tq, win, dil)

            def tile(q0, start, variant):
                s = _dot_nt(q_ref[pl.ds(q0, tq), lanes], k_ref[pl.ds(start, win), lanes]) * ATTN_SCALE - bias_ref[variant]
                m = jnp.max(s, axis=1, keepdims=True)
                p = jnp.exp(s - m)
                den = jnp.sum(p, axis=1, keepdims=True)
                o_ref[pl.ds(q0, tq), lanes] = _dot(p.astype(BF16), v_ref[pl.ds(start, win), lanes]) / den
                rows = pl.ds(q0, tq)
                lse_ref[rows, :] = jnp.where(lane == h_abs, m + jnp.log(den), lse_ref[rows, :])

            _attn_tiles(sub_len, tq, win, n_tiles, unroll, tile)

    col, grid = _attn_specs(n_cls, sub_len, heads)
    return pl.pallas_call(
        body, name=f"attn_fwd_d{dil}", grid=grid,
        in_specs=[col(0), col(1), col(2)], out_specs=[col(0), _stat_spec(sub_len)],
        out_shape=[jax.ShapeDtypeStruct((n_cls, sub_len, ATTN_WIDTH), F32), jax.ShapeDtypeStruct((n_cls, sub_len, LANES), F32)],
        scratch_shapes=[pltpu.VMEM((3, tq, win), F32)],
        compiler_params=_params("arbitrary", "arbitrary"),
    )(qkv, qkv, qkv)


def attn_bwd(qkv, do, lse, delta, dil):
    n_cls, sub_len, _ = qkv.shape
    tq, win, n_tiles, heads, unroll = _attn_plan(sub_len, ATTN_UNROLL_BWD)

    def body(q_ref, k_ref, v_ref, do_ref, lse_ref, dl_ref, dq_ref, dk_ref, dv_ref, bias_ref, dk_acc, dv_acc):
        lane = _iota((tq, LANES), 1)
        for head in range(heads):
            lanes = slice(head * HEAD_DIM, (head + 1) * HEAD_DIM)
            h_abs = pl.program_id(1) * heads + head
            _attn_bias(bias_ref, head, heads, tq, win, dil)
            dk_acc[...] = jnp.zeros_like(dk_acc)
            dv_acc[...] = jnp.zeros_like(dv_acc)

            def tile(q0, start, variant):
                q = q_ref[pl.ds(q0, tq), lanes]
                k = k_ref[pl.ds(start, win), lanes]
                v = v_ref[pl.ds(start, win), lanes]
                dov = do_ref[pl.ds(q0, tq), lanes]
                s = _dot_nt(q, k) * ATTN_SCALE - bias_ref[variant]
                rows = pl.ds(q0, tq)
                p = jnp.exp(s - _stat_column(lse_ref, rows, lane, h_abs))
                ds = (p * (_dot_nt(dov, v) - _stat_column(dl_ref, rows, lane, h_abs))).astype(BF16)
                dq_ref[pl.ds(q0, tq), lanes] = (_dot(ds, k) * ATTN_SCALE).astype(BF16)
                dk_acc[pl.ds(start, win), :] += _dot_tn(ds, q) * ATTN_SCALE
                dv_acc[pl.ds(start, win), :] += _dot_tn(p.astype(BF16), dov)

            _attn_tiles(sub_len, tq, win, n_tiles, unroll, tile)
            dk_ref[:, lanes] = dk_acc[...].astype(BF16)
            dv_ref[:, lanes] = dv_acc[...].astype(BF16)

    col, grid = _attn_specs(n_cls, sub_len, heads)
    out = jax.ShapeDtypeStruct((n_cls, sub_len, ATTN_WIDTH), BF16)
    return pl.pallas_call(
        body, name=f"attn_bwd_d{dil}", grid=grid,
        in_specs=[col(0), col(1), col(2), col(0), _stat_spec(sub_len), _stat_spec(sub_len)],
        out_specs=[col(0), col(0), col(0)], out_shape=[out, out, out],
        scratch_shapes=[pltpu.VMEM((3, tq, win), F32), pltpu.VMEM((sub_len, HEAD_DIM), F32),
                        pltpu.VMEM((sub_len, HEAD_DIM), F32)],
        compiler_params=_params("arbitrary", "arbitrary"),
    )(qkv, qkv, qkv, do, lse, delta)


RET_UNROLL = 16
RET_UNROLL_BWD = 8


def _ret_tables(lg_ref):
    h = pl.program_id(0)
    one = jnp.ones((1, 1), F32)
    lgf, lgb = lg_ref[h] * one, lg_ref[RET_HEADS + h] * one
    c = RET_CHUNK
    rel = (_iota((c, c), 0) - _iota((c, c), 1)).astype(F32)
    dec_f = jnp.where(rel >= 0, jnp.exp(jnp.maximum(rel, 0.0) * lgf), 0.0)
    dec_b = jnp.where(rel <= 0, jnp.exp(jnp.maximum(-rel, 0.0) * lgb), 0.0)
    ci = _iota((c, 1), 0).astype(F32)
    tab = dict(rel=rel, dec_f=dec_f, dec_b=dec_b, ci=ci,
               xi_f=jnp.exp((ci + 1.0) * lgf), ze_f=jnp.exp((c - 1.0 - ci) * lgf), g_f=jnp.exp(c * lgf),
               xi_b=jnp.exp((c - ci) * lgb), ze_b=jnp.exp(ci * lgb), g_b=jnp.exp(c * lgb))
    return tab


def _ret_specs(s_len):
    q = pl.BlockSpec((s_len, RET_QK), lambda h: (0, COL_QR // RET_QK + h))
    k = pl.BlockSpec((s_len, RET_QK), lambda h: (0, COL_KR // RET_QK + h))
    v = pl.BlockSpec((s_len, RET_V), lambda h: (0, COL_VR // RET_V + h))
    wide = pl.BlockSpec((s_len, RET_V), lambda h: (0, h))
    narrow = pl.BlockSpec((s_len, RET_QK), lambda h: (0, h))
    smem = pl.BlockSpec(memory_space=pltpu.SMEM)
    return smem, q, k, v, wide, narrow


def ret_fwd(proj, lg, finish=None):
    s_len = proj.shape[0]
    c, n_chunks = RET_CHUNK, proj.shape[0] // RET_CHUNK

    def body(lg_ref, q_ref, k_ref, v_ref, *rest):
        if finish is not None:
            _, _, opre_ref, y_ref, fin_ref, fout_ref, st_f, st_b, send_sems, recv_sems = rest
            first, last = _grid_edge(1)
            copies = _gather_phase2(_weight_blocks(fin_ref, fout_ref), send_sems, recv_sems)
            pl.when(first)(lambda: _start_all(copies))
        else:
            opre_ref, y_ref, st_f, st_b = rest
        t = _ret_tables(lg_ref)
        dec = t["dec_f"] + t["dec_b"]
        st_f[...] = jnp.zeros_like(st_f)
        st_b[...] = jnp.zeros_like(st_b)

        def load(n):
            r0 = pl.multiple_of(n * c, c)
            q, k, v = q_ref[pl.ds(r0, c), :], k_ref[pl.ds(r0, c), :], v_ref[pl.ds(r0, c), :]
            return r0, q, (k.astype(F32) * RET_SCALE), v

        def fwd(n, carry):
            r0, q, kf, v = load(n)
            inner = (_dot_nt(q, kf.astype(BF16)) * dec).astype(BF16)
            opre_ref[pl.ds(r0, c), :] = _dot(inner, v) + _dot(q, st_f[...].astype(BF16)) * t["xi_f"]
            st_f[...] = st_f[...] * t["g_f"] + _dot_tn((kf * t["ze_f"]).astype(BF16), v)
            return carry

        def bwd(i, carry):
            r0, q, kf, v = load(n_chunks - 1 - i)
            o = opre_ref[pl.ds(r0, c), :] + _dot(q, st_b[...].astype(BF16)) * t["xi_b"]
            st_b[...] = st_b[...] * t["g_b"] + _dot_tn((kf * t["ze_b"]).astype(BF16), v)
            opre_ref[pl.ds(r0, c), :] = o
            y_ref[pl.ds(r0, c), :] = o * lax.rsqrt(jnp.mean(o * o, axis=-1, keepdims=True) + NORM_EPS)
            return carry

        lax.fori_loop(0, n_chunks, fwd, 0, unroll=min(RET_UNROLL, n_chunks))
        lax.fori_loop(0, n_chunks, bwd, 0, unroll=min(RET_UNROLL, n_chunks))
        if finish is not None:
            pl.when(last)(lambda: _wait_all(copies))

    smem, q, k, v, wide, _ = _ret_specs(s_len)
    out = jax.ShapeDtypeStruct((s_len, RET_HEADS * RET_V), F32)
    in_specs, out_specs, out_shape = [smem, q, k, v], [wide, wide], [out, out]
    scratch = [pltpu.VMEM((RET_QK, RET_V), F32), pltpu.VMEM((RET_QK, RET_V), F32)]
    aliases, extra = {}, ()
    if finish is not None:
        in_specs += [HBM, HBM]
        out_specs += [HBM, HBM]
        out_shape += _full_weight_shapes()
        scratch += GATHER2_SEMS
        aliases, extra = {4: 2, 5: 3}, tuple(finish)
    return pl.pallas_call(
        body, name="ret_fwd_gather2" if finish is not None else "ret_fwd", grid=(RET_HEADS,), in_specs=in_specs,
        out_specs=out_specs, out_shape=out_shape, scratch_shapes=scratch, input_output_aliases=aliases,
        compiler_params=_params("arbitrary"),
    )(lg, proj, proj, proj, *extra)


def ret_bwd(proj, lg, o_pre, dy):
    s_len = proj.shape[0]
    c, n_chunks = RET_CHUNK, proj.shape[0] // RET_CHUNK
    cf = float(c)

    def body(lg_ref, q_ref, k_ref, v_ref, o_ref, dy_ref, dq_ref, dk_ref, dv_ref, glf_ref, glb_ref,
             st_f, dst_b, st_b, dst_f, keep_sf, keep_dtb, acc_f, acc_b, acc_sf, acc_sb):
        t = _ret_tables(lg_ref)
        dec = t["dec_f"] + t["dec_b"]
        e_f, e_b, ci = t["rel"] * t["dec_f"], -t["rel"] * t["dec_b"], t["ci"]
        for ref in (st_f, dst_b, st_b, dst_f, acc_f, acc_b, acc_sf, acc_sb):
            ref[...] = jnp.zeros_like(ref)

        def load(n):
            r0 = pl.multiple_of(n * c, c)
            q, k, v = q_ref[pl.ds(r0, c), :], k_ref[pl.ds(r0, c), :], v_ref[pl.ds(r0, c), :]
            o, dyv = o_ref[pl.ds(r0, c), :], dy_ref[pl.ds(r0, c), :]
            rr = lax.rsqrt(jnp.mean(o * o, axis=-1, keepdims=True) + NORM_EPS)
            y = o * rr
            do = (rr * (dyv - y * jnp.mean(dyv * y, axis=-1, keepdims=True))).astype(BF16)
            return r0, q, k.astype(F32) * RET_SCALE, v, do

        def fold(x):
            return functools.reduce(lambda a, b: a + b, [x[:, i * RET_QK:(i + 1) * RET_QK] for i in range(c // RET_QK)])

        def fwd(n, carry):
            r0, q, kf, v, do = load(n)
            qf, kb = q.astype(F32), kf.astype(BF16)
            a = _dot_nt(q, kb)
            b = _dot_nt(do, v)
            da = (b * dec).astype(BF16)
            ab = a * b
            sf_b, dtb_b = st_f[...].astype(BF16), dst_b[...].astype(BF16)
            dq_inter = _dot_nt(do, sf_b) * t["xi_f"]
            dk_inter = _dot_nt(v, dtb_b) * t["ze_b"]
            acc_f[...] += fold(e_f * ab) + (ci + 1.0) * (qf * dq_inter)
            acc_b[...] += fold(e_b * ab) + ci * (kf * dk_inter)
            dq_ref[pl.ds(r0, c), :] = _dot(da, kb) + dq_inter
            dk_ref[pl.ds(r0, c), :] = _dot_tn(da, q) + dk_inter
            dv_ref[pl.ds(r0, c), :] = _dot_tn((a * dec).astype(BF16), do) + _dot((kf * t["ze_b"]).astype(BF16), dtb_b)
            keep_sf[n] = sf_b
            keep_dtb[n] = dtb_b
            st_f[...] = st_f[...] * t["g_f"] + _dot_tn((kf * t["ze_f"]).astype(BF16), v)
            dst_b[...] = dst_b[...] * t["g_b"] + _dot_tn((qf * t["xi_b"]).astype(BF16), do)
            return carry

        def bwd(i, carry):
            n = n_chunks - 1 - i
            r0, q, kf, v, do = load(n)
            qf = q.astype(F32)
            tb_b, dsf_b = st_b[...].astype(BF16), dst_f[...].astype(BF16)
            dq_inter = _dot_nt(do, tb_b) * t["xi_b"]
            dk_inter = _dot_nt(v, dsf_b) * t["ze_f"]
            acc_b[...] += (cf - ci) * (qf * dq_inter)
            acc_f[...] += (cf - 1.0 - ci) * (kf * dk_inter)
            acc_sb[...] += keep_dtb[n].astype(F32) * st_b[...]
            acc_sf[...] += dst_f[...] * keep_sf[n].astype(F32)
            dq_ref[pl.ds(r0, c), :] += dq_inter
            dk_ref[pl.ds(r0, c), :] = (dk_ref[pl.ds(r0, c), :] + dk_inter) * RET_SCALE
            dv_ref[pl.ds(r0, c), :] += _dot((kf * t["ze_f"]).astype(BF16), dsf_b)
            st_b[...] = st_b[...] * t["g_b"] + _dot_tn((kf * t["ze_b"]).astype(BF16), v)
            dst_f[...] = dst_f[...] * t["g_f"] + _dot_tn((qf * t["xi_f"]).astype(BF16), do)
            return carry

        lax.fori_loop(0, n_chunks, fwd, 0, unroll=min(RET_UNROLL_BWD, n_chunks))
        lax.fori_loop(0, n_chunks, bwd, 0, unroll=min(RET_UNROLL_BWD, n_chunks))

        def total(x):
            return jnp.sum(jnp.sum(x, axis=1, keepdims=True), axis=0, keepdims=True)

        glf_ref[...] = jnp.broadcast_to(total(acc_f[...]) + cf * t["g_f"] * total(acc_sf[...]), (8, 128))
        glb_ref[...] = jnp.broadcast_to(total(acc_b[...]) + cf * t["g_b"] * total(acc_sb[...]), (8, 128))

    smem, q, k, v, wide, narrow = _ret_specs(s_len)
    scal = pl.BlockSpec((None, 8, 128), lambda h: (h, 0, 0))
    state = pltpu.VMEM((RET_QK, RET_V), F32)
    square = pltpu.VMEM((RET_CHUNK, RET_QK), F32)
    keep = pltpu.VMEM((n_chunks, RET_QK, RET_V), BF16)
    return pl.pallas_call(
        body, name="ret_bwd", grid=(RET_HEADS,), in_specs=[smem, q, k, v, wide, wide],
        out_specs=[narrow, narrow, wide, scal, scal],
        out_shape=[jax.ShapeDtypeStruct((s_len, RET_HEADS * RET_QK), F32), jax.ShapeDtypeStruct((s_len, RET_HEADS * RET_QK), F32),
                   jax.ShapeDtypeStruct((s_len, RET_HEADS * RET_V), F32),
                   jax.ShapeDtypeStruct((RET_HEADS, 8, 128), F32), jax.ShapeDtypeStruct((RET_HEADS, 8, 128), F32)],
        scratch_shapes=[state, state, state, state, keep, keep, square, square, state, state],
        compiler_params=_params("arbitrary"),
    )(lg, proj, proj, proj, o_pre, dy)


def _silu_parts(z):
    sig = _sigmoid(z)
    return z * sig, sig * (1.0 + z * (1.0 - sig))


def outproj_fwd(x, gate, w_out, attn_outs, y_r, proj):
    s_len = x.shape[0]
    tm = min(256, s_len)

    def per_head(w):
        return jnp.concatenate([jnp.broadcast_to(w[:, h:h + 1], (tm, HEAD_DIM)) for h in range(N_HEADS_ATTN)], axis=1)

    def body(x_ref, gate_ref, w_ref, o1, l1, o2, l2, o3, l3, yr_ref, za_ref, zr_ref,
             xn_ref, oa_ref, lse_ref, lse4_ref, lse16_ref, so2, sl2, so3, sl3):
        silu_r, _ = _silu_parts(zr_ref[...].astype(F32))
        out_r = _dot((yr_ref[...] * silu_r).astype(BF16), w_ref[ATTN_WIDTH:, :])
        for src, dst in ((o2, so2), (l2, sl2), (o3, so3), (l3, sl3)):
            _merge_classes(src, dst)
        la, lb, lc = l1[...], _read_stage(sl2), _read_stage(sl3)
        m = jnp.maximum(jnp.maximum(la, lb), lc)
        lse = m + jnp.log(jnp.exp(la - m) + jnp.exp(lb - m) + jnp.exp(lc - m))
        o_a = (per_head(jnp.exp(la - lse)) * o1[...] + per_head(jnp.exp(lb - lse)) * _read_stage(so2)
               + per_head(jnp.exp(lc - lse)) * _read_stage(so3))
        oa_ref[...] = o_a
        lse_ref[...] = lse
        _fill_stage(sl2, lse)
        _split_classes(sl2, [lse4_ref, lse16_ref])
        silu_a, _ = _silu_parts(za_ref[...].astype(F32))
        out_a = _dot((o_a * silu_a).astype(BF16), w_ref[:ATTN_WIDTH, :])
        xn_ref[...] = x_ref[...] + gate_ref[...] * (out_a + out_r)

    row = lambda w: pl.BlockSpec((tm, w), lambda i: (i, 0))
    half, stat = row(ATTN_WIDTH), row(LANES)
    cls = [_class_block(dil, tm, ATTN_WIDTH, lambda i: (0, i, 0)) for dil in DILATIONS[1:]]
    cls_stat = [_class_block(dil, tm, LANES, lambda i: (0, i, 0)) for dil in DILATIONS[1:]]
    flat = [a for pair in attn_outs for a in pair]
    sds = jax.ShapeDtypeStruct
    return pl.pallas_call(
        body, name="outproj_fwd", grid=(s_len // tm,),
        in_specs=[row(D_MODEL), pl.BlockSpec((1, D_MODEL), lambda i: (0, 0)),
                  pl.BlockSpec((D_MODEL, D_MODEL), lambda i: (0, 0)),
                  half, stat, cls[0], cls_stat[0], cls[1], cls_stat[1], half,
                  pl.BlockSpec((tm, ATTN_WIDTH), lambda i: (i, COL_ZA // ATTN_WIDTH)),
                  pl.BlockSpec((tm, ATTN_WIDTH), lambda i: (i, COL_ZR // ATTN_WIDTH))],
        out_specs=[row(D_MODEL), half, stat] + cls_stat,
        out_shape=[sds((s_len, D_MODEL), F32), sds((s_len, ATTN_WIDTH), F32), sds((s_len, LANES), F32)]
                  + [sds((dil, s_len // dil, LANES), F32) for dil in DILATIONS[1:]],
        scratch_shapes=[_stage_shape(tm, ATTN_WIDTH), _stage_shape(tm, LANES)] * 2,
        compiler_params=_params("arbitrary"),
    )(x, gate, w_out, *flat, y_r, proj, proj)


def loss_head(x, gain, target):
    s_len = x.shape[0]
    tm = min(256, s_len)

    def body(x_ref, g_ref, t_ref, dx_ref, loss_ref, dg_ref):
        @pl.when(pl.program_id(0) == 0)
        def _():
            loss_ref[...] = jnp.zeros_like(loss_ref)
            dg_ref[...] = jnp.zeros_like(dg_ref)
        xv, g = x_ref[...], g_ref[...]
        r = lax.rsqrt(jnp.mean(xv * xv, axis=-1, keepdims=True) + NORM_EPS)
        xn = xv * r
        err = xn * g - t_ref[...]
        part = 0.5 * jnp.sum(jnp.mean(err * err, axis=-1, keepdims=True), axis=0, keepdims=True)
        loss_ref[...] += jnp.broadcast_to(part, loss_ref.shape)
        dy = err * (1.0 / D_MODEL)
        dg_ref[...] += jnp.sum(dy * xn, axis=0, keepdims=True)
        dxn = dy * g
        dx_ref[...] = r * (dxn - xn * jnp.mean(dxn * xn, axis=-1, keepdims=True))

    row = pl.BlockSpec((tm, D_MODEL), lambda i: (i, 0))
    vec = pl.BlockSpec((1, D_MODEL), lambda i: (0, 0))
    return pl.pallas_call(
        body, name="loss_head", grid=(s_len // tm,), in_specs=[row, vec, row],
        out_specs=[row, pl.BlockSpec((8, 128), lambda i: (0, 0)), vec],
        out_shape=[jax.ShapeDtypeStruct((s_len, D_MODEL), F32), jax.ShapeDtypeStruct((8, 128), F32),
                   jax.ShapeDtypeStruct((1, D_MODEL), F32)],
        compiler_params=_params("arbitrary"),
    )(x, gain, target)


def outproj_bwd(dxn, gate, w_out, o_a, y_r, proj):
    s_len = dxn.shape[0]
    tm = min(256, s_len)

    def body(dx_ref, gate_ref, w_ref, oa_ref, yr_ref, za_ref, zr_ref,
             doa_ref, dl_ref, dyr_ref, dza_ref, dzr_ref, y_ref, dxb_ref, do4, do16, dl4, dl16, stage, stat_stage):
        dxv = dx_ref[...]
        dxb_ref[...] = dxv.astype(BF16)
        dy = _dot_nt((dxv * gate_ref[...]).astype(BF16), w_ref[...])
        dy_a, dy_r = dy[:, :ATTN_WIDTH], dy[:, ATTN_WIDTH:]
        o_a, y_rv = oa_ref[...], yr_ref[...]
        silu_a, dsilu_a = _silu_parts(za_ref[...].astype(F32))
        silu_r, dsilu_r = _silu_parts(zr_ref[...].astype(F32))
        do_a = dy_a * silu_a
        doa_ref[...] = do_a.astype(BF16)
        _fill_stage(stage, do_a)
        _split_classes(stage, [do4, do16])
        prod = do_a * o_a
        lane = _iota((tm, LANES), 1)
        delta = jnp.zeros((tm, LANES), F32)
        for h in range(N_HEADS_ATTN):
            delta = jnp.where(lane == h, jnp.sum(prod[:, h * HEAD_DIM:(h + 1) * HEAD_DIM], axis=1, keepdims=True), delta)
        dl_ref[...] = delta
        _fill_stage(stat_stage, delta)
        _split_classes(stat_stage, [dl4, dl16])
        dyr_ref[...] = dy_r * silu_r
        dza_ref[...] = (dy_a * o_a * dsilu_a).astype(BF16)
        dzr_ref[...] = (dy_r * y_rv * dsilu_r).astype(BF16)
        y_ref[...] = jnp.concatenate([(o_a * silu_a).astype(BF16), (y_rv * silu_r).astype(BF16)], axis=1)

    row = lambda w: pl.BlockSpec((tm, w), lambda i: (i, 0))
    half = row(ATTN_WIDTH)
    sds = lambda w, dt: jax.ShapeDtypeStruct((s_len, w), dt)
    in_specs = [row(D_MODEL), pl.BlockSpec((1, D_MODEL), lambda i: (0, 0)),
                pl.BlockSpec((D_MODEL, D_MODEL), lambda i: (0, 0)), half, half,
                pl.BlockSpec((tm, ATTN_WIDTH), lambda i: (i, COL_ZA // ATTN_WIDTH)),
                pl.BlockSpec((tm, ATTN_WIDTH), lambda i: (i, COL_ZR // ATTN_WIDTH))]
    cls = [_class_block(dil, tm, ATTN_WIDTH, lambda i: (0, i, 0)) for dil in DILATIONS[1:]]
    cls_stat = [_class_block(dil, tm, LANES, lambda i: (0, i, 0)) for dil in DILATIONS[1:]]
    out_specs = [half, row(LANES), half, half, half, row(D_MODEL), row(D_MODEL)] + cls + cls_stat
    out_shape = [sds(ATTN_WIDTH, BF16), sds(LANES, F32), sds(ATTN_WIDTH, F32), sds(ATTN_WIDTH, BF16),
                 sds(ATTN_WIDTH, BF16), sds(D_MODEL, BF16), sds(D_MODEL, BF16)]
    out_shape += [jax.ShapeDtypeStruct((dil, s_len // dil, ATTN_WIDTH), BF16) for dil in DILATIONS[1:]]
    out_shape += [jax.ShapeDtypeStruct((dil, s_len // dil, LANES), F32) for dil in DILATIONS[1:]]
    return pl.pallas_call(
        body, name="outproj_bwd", grid=(s_len // tm,),
        in_specs=in_specs, out_specs=out_specs, out_shape=out_shape,
        scratch_shapes=[_stage_shape(tm, ATTN_WIDTH), _stage_shape(tm, LANES)],
        compiler_params=_params("arbitrary"),
    )(dxn, gate, w_out, o_a, y_r, proj, proj)


def wout_grad(y, dxb, gate, w_out):
    s_len = y.shape[0]
    tf, ts = 512, min(1024, s_len)

    def body(y_ref, dx_ref, gate_ref, w_ref, dw_ref, dgate_ref, acc):
        f, s = pl.program_id(0), pl.program_id(1)

        @pl.when((f == 0) & (s == 0))
        def _():
            dgate_ref[...] = jnp.zeros_like(dgate_ref)

        @pl.when(s == 0)
        def _():
            acc[...] = jnp.zeros_like(acc)
        acc[...] += _dot_tn(y_ref[...], dx_ref[...])

        @pl.when(s == pl.num_programs(1) - 1)
        def _():
            m = acc[...]
            dw_ref[...] = (m * gate_ref[...]).astype(BF16).reshape(dw_ref.shape)
            dgate_ref[...] += jnp.sum(m * w_ref[...].astype(F32), axis=0, keepdims=True)

    per = tf // W_OUT_SHARD
    return pl.pallas_call(
        body, name="wout_grad", grid=(D_MODEL // tf, s_len // ts),
        in_specs=[pl.BlockSpec((ts, tf), lambda f, s: (s, f)), pl.BlockSpec((ts, D_MODEL), lambda f, s: (s, 0)),
                  pl.BlockSpec((1, D_MODEL), lambda f, s: (0, 0)), pl.BlockSpec((tf, D_MODEL), lambda f, s: (f, 0))],
        out_specs=[pl.BlockSpec((per, W_OUT_SHARD, D_MODEL), lambda f, s: (f, 0, 0)),
                   pl.BlockSpec((1, D_MODEL), lambda f, s: (0, 0))],
        out_shape=[jax.ShapeDtypeStruct((N_DEV, W_OUT_SHARD, D_MODEL), BF16), jax.ShapeDtypeStruct((1, D_MODEL), F32)],
        scratch_shapes=[pltpu.VMEM((tf, D_MODEL), F32)],
        compiler_params=_params("arbitrary", "arbitrary"),
    )(y, dxb, gate, w_out)


def assemble_dproj(dqkv_a, dz_a, dq_r, dk_r, dv_r, dz_r):
    s_len = dz_a.shape[0]
    tm = min(512, s_len)

    def body(*refs):
        pat, (dza, dqr, dkr, dvr, dzr, out, stage) = refs[:9], refs[9:]
        for t in range(3):
            tot = pat[t][...].astype(F32)
            for p in (1, 2):
                _merge_classes(pat[3 * p + t], stage)
                tot = tot + _read_stage(stage)
            out[:, t * ATTN_WIDTH:(t + 1) * ATTN_WIDTH] = tot.astype(BF16)
        out[:, COL_ZA:COL_QR] = dza[...]
        out[:, COL_QR:COL_KR] = dqr[...].astype(BF16)
        out[:, COL_KR:COL_VR] = dkr[...].astype(BF16)
        out[:, COL_VR:COL_ZR] = dvr[...].astype(BF16)
        out[:, COL_ZR:IN_W] = dzr[...]

    row = lambda w: pl.BlockSpec((tm, w), lambda i: (i, 0))
    cls = [_class_block(dil, tm, ATTN_WIDTH, lambda i: (0, i, 0)) for dil in DILATIONS[1:]]
    flat = [dqkv_a[p][t] for p in range(3) for t in range(3)]
    return pl.pallas_call(
        body, name="assemble_dproj", grid=(s_len // tm,),
        in_specs=[row(ATTN_WIDTH)] * 3 + [cls[0]] * 3 + [cls[1]] * 3
                 + [row(ATTN_WIDTH), row(512), row(512), row(ATTN_WIDTH), row(ATTN_WIDTH)],
        out_specs=row(IN_W), out_shape=jax.ShapeDtypeStruct((s_len, IN_W), BF16),
        scratch_shapes=[_stage_shape(tm, ATTN_WIDTH)],
        compiler_params=_params("arbitrary"),
    )(*flat, dz_a, dq_r, dk_r, dv_r, dz_r)


def inproj_bwd(dproj, w, x, g, scale1p, dxn, exchange):
    s_len = x.shape[0]
    tm, tk = min(512, s_len), 1024

    def body(dp_ref, w_ref, x_ref, g_ref, sc_ref, dxn_ref, pa_ref, pb_ref, dx_ref, st_ref, ra_ref, rb_ref,
             acc, send_sems, recv_sems):
        first, last = _grid_edge(2)
        copies = _reduce_phase2([pa_ref, pb_ref], [ra_ref, rb_ref], send_sems, recv_sems)
        pl.when(first)(lambda: _start_all(copies))
        i, k = pl.program_id(0), pl.program_id(1)

        @pl.when((i == 0) & (k == 0))
        def _():
            st_ref[...] = jnp.zeros_like(st_ref)

        @pl.when(k == 0)
        def _():
            acc[...] = jnp.zeros_like(acc)
        acc[...] += _dot_nt(dp_ref[...], w_ref[...])

        @pl.when(k == pl.num_programs(1) - 1)
        def _():
            dh, xv = acc[...], x_ref[...]
            r = lax.rsqrt(jnp.mean(xv * xv, axis=-1, keepdims=True) + NORM_EPS)
            xn = xv * r
            st_ref[0:1, :] += jnp.sum(dh, axis=0, keepdims=True)
            st_ref[3:4, :] += jnp.sum(dh * xn, axis=0, keepdims=True)
            dn = dh * (sc_ref[...] * g_ref[...])
            dx_ref[...] = r * (dn - xn * jnp.mean(dn * xn, axis=-1, keepdims=True)) + dxn_ref[...]

        @pl.when(last)
        def _():
            st_ref[1:2, :] = st_ref[3:4, :] * g_ref[...]
            st_ref[2:3, :] = st_ref[3:4, :] * sc_ref[...]
            _wait_all(copies)

    row = pl.BlockSpec((tm, D_MODEL), lambda i, k: (i, 0))
    vec = pl.BlockSpec((1, D_MODEL), lambda i, k: (0, 0))
    return pl.pallas_call(
        body, name="inproj_bwd_reduce2", grid=(s_len // tm, IN_W // tk),
        in_specs=[pl.BlockSpec((tm, tk), lambda i, k: (i, k)), pl.BlockSpec((D_MODEL, tk), lambda i, k: (0, k)),
                  row, vec, vec, row, HBM, HBM],
        out_specs=[row, pl.BlockSpec((8, D_MODEL), lambda i, k: (0, 0)), HBM, HBM],
        out_shape=[jax.ShapeDtypeStruct((s_len, D_MODEL), F32), jax.ShapeDtypeStruct((8, D_MODEL), F32)]
                  + _landing_shapes(3),
        scratch_shapes=[pltpu.VMEM((tm, D_MODEL), F32)] + REDUCE2_SEMS,
        compiler_params=_params("arbitrary", "arbitrary"),
    )(dproj, w, x, g, scale1p, dxn, *exchange)


def win_grad(h, dproj):
    s_len = h.shape[0]
    ts = min(2048, s_len)

    def body(h_ref, dp_ref, dw_ref, acc):
        s = pl.program_id(1)

        @pl.when(s == 0)
        def _():
            acc[...] = jnp.zeros_like(acc)
        acc[...] += _dot_tn(h_ref[...], dp_ref[...])

        @pl.when(s == pl.num_programs(1) - 1)
        def _():
            dw_ref[...] = acc[...].astype(BF16)

    return pl.pallas_call(
        body, name="win_grad", grid=(N_DEV, s_len // ts),
        in_specs=[pl.BlockSpec((ts, D_MODEL), lambda j, s: (s, 0)), pl.BlockSpec((ts, W_IN_SHARD), lambda j, s: (s, j))],
        out_specs=pl.BlockSpec((None, D_MODEL, W_IN_SHARD), lambda j, s: (j, 0, 0)),
        out_shape=jax.ShapeDtypeStruct((N_DEV, D_MODEL, W_IN_SHARD), BF16),
        scratch_shapes=[pltpu.VMEM((D_MODEL, W_IN_SHARD), F32)],
        compiler_params=_params("arbitrary", "arbitrary"),
    )(h, dproj)


def ada_fwd(c_all, w_ada):
    def body(c_ref, w_ref, act_ref, part_ref):
        cv = c_ref[...]
        act = cv * _sigmoid(cv)
        act_ref[...] = act
        part_ref[...] = _dot(act.astype(BF16), w_ref[...].astype(BF16))

    return pl.pallas_call(
        body, name="ada_fwd", grid=(DEPTH,),
        in_specs=[pl.BlockSpec((N_DEV, D_MODEL), lambda l: (0, 0)),
                  pl.BlockSpec((None, D_MODEL, W_ADA_SHARD), lambda l: (l, 0, 0))],
        out_specs=[pl.BlockSpec((N_DEV, D_MODEL), lambda l: (0, 0)),
                   pl.BlockSpec((None, N_DEV, W_ADA_SHARD), lambda l: (l, 0, 0))],
        out_shape=[jax.ShapeDtypeStruct((N_DEV, D_MODEL), F32), jax.ShapeDtypeStruct((DEPTH, N_DEV, W_ADA_SHARD), F32)],
        compiler_params=_params("arbitrary"),
    )(c_all, w_ada)


def _adamw(w, g, m, v):
    m = ADAM_B1 * m + (1.0 - ADAM_B1) * g
    v = ADAM_B2 * v + (1.0 - ADAM_B2) * (g * g)
    delta = -ADAM_LR * ((m * ADAM_C1) / (jnp.sqrt(v * ADAM_C2) + ADAM_EPS) + ADAM_WD * w)
    return delta, m, v


def ada_update(act_t, dmod, w, m, v):
    tr = 512

    def body(a_ref, d_ref, w_ref, m_ref, v_ref, g_out, dl_out, m_out, v_out):
        a = a_ref[...].astype(BF16).astype(F32)
        d = d_ref[...].astype(BF16).astype(F32)
        g = a[:, 0:1] * d[0:1, :]
        for b in range(1, N_DEV):
            g = g + a[:, b:b + 1] * d[b:b + 1, :]
        g_out[...] = g
        dl_out[...], m_out[...], v_out[...] = _adamw(w_ref[...], g, m_ref[...], v_ref[...])

    blk = pl.BlockSpec((None, tr, W_ADA_SHARD), lambda l, r: (l, r, 0))
    out = jax.ShapeDtypeStruct(w.shape, F32)
    return pl.pallas_call(
        body, name="ada_update", grid=(DEPTH, D_MODEL // tr),
        in_specs=[pl.BlockSpec((tr, N_DEV), lambda l, r: (r, 0)),
                  pl.BlockSpec((None, N_DEV, W_ADA_SHARD), lambda l, r: (l, 0, 0)), blk, blk, blk],
        out_specs=[blk] * 4, out_shape=[out] * 4, compiler_params=_params("arbitrary", "arbitrary"),
    )(act_t, dmod, w, m, v)


def chip_sum(pos, grads, landed, name):
    _, _, n_rows, n_cols = grads.shape
    tr = min(512, n_rows)

    def chip(k, pos_ref):
        return (pos_ref[0] ^ (k // 2)) * 2 + (pos_ref[1] ^ (k % 2))

    def body(pos_ref, g_ref, r_ref, out_ref):
        out_ref[...] = (g_ref[...].astype(F32) + r_ref[...].astype(F32)).astype(BF16)

    return pl.pallas_call(
        body, name=name,
        grid_spec=pltpu.PrefetchScalarGridSpec(
            num_scalar_prefetch=1, grid=(N_DEV // 2, n_rows // tr),
            in_specs=[pl.BlockSpec((None, None, tr, n_cols), lambda k, r, p: (chip(k, p), p[2], r, 0)),
                      pl.BlockSpec((None, tr, n_cols), lambda k, r, p: (chip(k, p), r, 0))],
            out_specs=pl.BlockSpec((None, tr, n_cols), lambda k, r, p: (k, r, 0))),
        out_shape=jax.ShapeDtypeStruct((N_DEV // 2, n_rows, n_cols), BF16),
        compiler_params=_params("arbitrary", "arbitrary"),
    )(pos, grads, landed)


def shard_update(layer, own, others, w, m, v, prev, name):
    _, n_rows, n_cols = w.shape
    tr = min(256, n_rows)

    def body(own_ref, oth_ref, w_ref, m_ref, v_ref, *rest):
        g_out, dl_out, m_out, v_out = rest[-4:]
        g = own_ref[...].astype(F32)
        for k in range(3):
            g = g + oth_ref[k].astype(F32)
        g_out[...] = g
        dl_out[...], m_out[...], v_out[...] = _adamw(w_ref[...], g, m_ref[...], v_ref[...])

    blk = pl.BlockSpec((None, tr, n_cols), lambda r: (layer, r, 0))
    out = jax.ShapeDtypeStruct(w.shape, F32)
    in_specs = [pl.BlockSpec((None, tr, n_cols), lambda r: (0, r, 0)), pl.BlockSpec((3, tr, n_cols), lambda r: (0, r, 0)),
                blk, blk, blk]
    aliases, extra = {}, ()
    if prev is not None:
        in_specs += [HBM] * 4
        aliases = {5 + t: t for t in range(4)}
        extra = tuple(prev)
    return pl.pallas_call(
        body, name=name, grid=(n_rows // tr,), in_specs=in_specs, out_specs=[blk] * 4, out_shape=[out] * 4,
        input_output_aliases=aliases, compiler_params=_params("arbitrary"),
    )(own, others, w, m, v, *extra)


def small_update(parts, w, m, v):
    def body(p_ref, w_ref, m_ref, v_ref, g_out, dl_out, m_out, v_out):
        g = p_ref[0]
        for k in range(1, N_DEV):
            g = g + p_ref[k]
        g_out[...] = g
        dl_out[...], m_out[...], v_out[...] = _adamw(w_ref[...], g, m_ref[...], v_ref[...])

    out = jax.ShapeDtypeStruct(w.shape, F32)
    return pl.pallas_call(body, name="small_update", out_shape=[out] * 4, compiler_params=_params())(parts, w, m, v)


def _two_level_allgather(srcs, dst_block, send_sems, recv_sems, local_sems):
    x, y, c = _position()
    me, sibling = (x, y, c), (x, y, 1 - c)
    chips = [(1 - x, y), (x, 1 - y), (1 - x, 1 - y)]
    n = len(srcs)

    def copy(a, k, block, to, src=None):
        dst = dst_block(a, _flat(*block))
        return pltpu.make_async_remote_copy(
            src_ref=dst if src is None else src, dst_ref=dst, send_sem=send_sems.at[a * 7 + k],
            recv_sem=recv_sems.at[a * 7 + k], device_id=to, device_id_type=MESH)

    mine = [pltpu.make_async_copy(srcs[a], dst_block(a, _flat(*me)), local_sems.at[a]) for a in range(n)]
    for cp in mine:
        cp.start()
    first = []
    for a in range(n):
        first.append(copy(a, 0, me, sibling, src=srcs[a]))
        first += [copy(a, 1 + j, me, (*chip, c), src=srcs[a]) for j, chip in enumerate(chips)]
    for cp in first:
        cp.start()
    passed = []
    for j, chip in enumerate(chips):
        for a in range(n):
            copy(a, 1 + j, (*chip, c), me).wait_recv()
            fwd = copy(a, 4 + j, (*chip, c), sibling)
            fwd.start()
            passed.append(fwd)
    for a in range(n):
        copy(a, 0, sibling, me).wait_recv()
        for j, chip in enumerate(chips):
            copy(a, 4 + j, (*chip, 1 - c), me).wait_recv()
    for cp in first + passed:
        cp.wait_send()
    for cp in mine:
        cp.wait()


def allgather_rows(x, name):
    def body(x_ref, out_ref, send_sems, recv_sems, local_sems):
        _two_level_allgather([x_ref], lambda a, idx: out_ref.at[idx], send_sems, recv_sems, local_sems)

    vmem = pl.BlockSpec(memory_space=pltpu.VMEM)
    return pl.pallas_call(
        body, name=name, in_specs=[vmem], out_specs=vmem,
        out_shape=jax.ShapeDtypeStruct((N_DEV,) + x.shape, x.dtype),
        scratch_shapes=[pltpu.SemaphoreType.DMA((7,)), pltpu.SemaphoreType.DMA((7,)), pltpu.SemaphoreType.DMA((1,))],
        compiler_params=_params(),
    )(x)


def _full_weight_shapes():
    return [jax.ShapeDtypeStruct((D_MODEL, IN_W), BF16), jax.ShapeDtypeStruct((D_MODEL, D_MODEL), BF16)]


def allgather_weights(w_in_b, w_out_b):
    def body(win_ref, wout_ref, fin_ref, fout_ref, s1, r1, l1, s2, r2):
        block = _weight_blocks(fin_ref, fout_ref)
        first = _gather_phase1([win_ref.at[0], wout_ref.at[0]], block, s1, r1, l1)
        _start_all(first)
        _wait_all(first)
        second = _gather_phase2(block, s2, r2)
        _start_all(second)
        _wait_all(second)

    return pl.pallas_call(
        body, name="allgather_weights", in_specs=[HBM, HBM], out_specs=[HBM, HBM], out_shape=_full_weight_shapes(),
        scratch_shapes=GATHER1_SEMS + GATHER2_SEMS, compiler_params=_params(),
    )(w_in_b, w_out_b)


def reduce_first(dw_in, dw_out, name):
    def body(ga_ref, gb_ref, ra_ref, rb_ref, send_sems, recv_sems):
        copies = _reduce_phase1([ga_ref, gb_ref], [ra_ref, rb_ref], send_sems, recv_sems)
        _start_all(copies)
        _wait_all(copies)

    return pl.pallas_call(
        body, name=name, in_specs=[HBM, HBM], out_specs=[HBM, HBM], out_shape=_landing_shapes(4),
        scratch_shapes=REDUCE1_SEMS, compiler_params=_params(),
    )(*_split_cores((dw_in, dw_out)))


def _one_class(a):
    return a.reshape((1,) + a.shape)


def layer_fwd(x, g, scale, shift, gate, w_in, w_out, lg, gather=None):
    proj, h, qkv4, qkv16, *began = inproj_fwd(x, g, 1.0 + scale, shift, w_in, gather)
    qkv = (_one_class(proj), qkv4, qkv16)
    attn_outs = [attn_fwd(arr, dil) for dil, arr in zip(DILATIONS, qkv)]
    attn_outs[0] = tuple(a[0] for a in attn_outs[0])
    o_pre, y_r, *next_weights = ret_fwd(proj, lg, began if gather is not None else None)
    next_weights = tuple(next_weights) if gather is not None else None
    x_new, o_a, lse, lse4, lse16 = outproj_fwd(x, gate, w_out, attn_outs, y_r, proj)
    saved = dict(x=x, proj=proj, h=h, qkv=qkv, o_a=o_a, lse=(_one_class(lse), lse4, lse16), o_pre=o_pre, y_r=y_r)
    return x_new, saved, next_weights


def layer_bwd(dxn, saved, g, scale, gate, w_in, w_out, lg, pos):
    proj = saved["proj"]
    do_a, delta, dyr, dz_a, dz_r, y, dxb, do4, do16, dl4, dl16 = outproj_bwd(
        dxn, gate, w_out, saved["o_a"], saved["y_r"], proj)
    dw_out, dgate = wout_grad(y, dxb, gate, w_out)
    dq_r, dk_r, dv_r, glf, glb = ret_bwd(proj, lg, saved["o_pre"], dyr)
    dos, deltas = (_one_class(do_a), do4, do16), (_one_class(delta), dl4, dl16)
    dqkv_a = [attn_bwd(arr, d_o, lse, dl, dil)
              for dil, arr, d_o, lse, dl in zip(DILATIONS, saved["qkv"], dos, saved["lse"], deltas)]
    dqkv_a[0] = [t[0] for t in dqkv_a[0]]
    dproj = assemble_dproj(dqkv_a, dz_a, dq_r, dk_r, dv_r, dz_r)
    dw_in = win_grad(saved["h"], dproj)
    landed = reduce_first(dw_in, dw_out, "reduce_first")
    sums = [chip_sum(pos, g4, r, "chip_sum") for g4, r in zip(_split_cores((dw_in, dw_out)), landed)]
    dx, stats, *others = inproj_bwd(dproj, w_in, saved["x"], g, 1.0 + scale, dxn, sums)
    dlg = jnp.concatenate([glf[:, 0, 0], glb[:, 0, 0]])
    return dx, stats[0:1], stats[1:2], dgate, stats[2:3], dlg, (sums, others)


ROWS_B_ADA = DEPTH * 3 * D_MODEL // 128
ROWS_GAIN = DEPTH * D_MODEL // 128
ROWS_FINAL = D_MODEL // 128
ROWS_MISC = 8
ROWS_SMALL = ROWS_B_ADA + ROWS_GAIN + ROWS_FINAL + ROWS_MISC


def _pack_small(b_ada_like, gain_like, final_like, dec_f, dec_b, loss=None):
    misc = jnp.zeros((ROWS_MISC, 128), F32)
    misc = misc.at[0, :2 * DEPTH * RET_HEADS].set(jnp.concatenate([dec_f.reshape(-1), dec_b.reshape(-1)]))
    if loss is not None:
        misc = misc.at[1, 0].set(loss)
    return jnp.concatenate([b_ada_like.reshape(ROWS_B_ADA, 128), gain_like.reshape(ROWS_GAIN, 128),
                            final_like.reshape(ROWS_FINAL, 128), misc], axis=0)


def _unpack_small(p):
    r0, r1, r2 = ROWS_B_ADA, ROWS_B_ADA + ROWS_GAIN, ROWS_B_ADA + ROWS_GAIN + ROWS_FINAL
    n = DEPTH * RET_HEADS
    return (p[:r0].reshape(DEPTH, 3 * D_MODEL), p[r0:r1].reshape(DEPTH, D_MODEL), p[r1:r2].reshape(D_MODEL),
            p[r2, :n].reshape(DEPTH, RET_HEADS), p[r2, n:2 * n].reshape(DEPTH, RET_HEADS))


def kernel(x, c, norm_gain, w_ada, b_ada, w_in, w_out, ret_decay_logit_f, ret_decay_logit_b, final_gain, loss_target, m_norm_gain, m_w_ada, m_b_ada, m_w_in, m_w_out, m_ret_decay_logit_f, m_ret_decay_logit_b, m_final_gain, v_norm_gain, v_w_ada, v_b_ada, v_w_in, v_w_out, v_ret_decay_logit_f, v_ret_decay_logit_b, v_final_gain):
    px, py, pc = _position()
    me = _flat(px, py, pc)
    pos = jnp.stack([px, py, pc]).astype(jnp.int32)
    x2, target = x[0], loss_target[0]

    w_in_b, w_out_b = w_in.astype(BF16), w_out.astype(BF16)
    weights = allgather_weights(w_in_b, w_out_b)

    c_all = allgather_rows(c.reshape(D_MODEL // 128, 128), "allgather_c").reshape(N_DEV, D_MODEL)
    act, mod_part = ada_fwd(c_all, w_ada)
    mod_all = allgather_rows(mod_part.reshape(-1, 128), "allgather_mod").reshape(N_DEV, DEPTH, N_DEV, W_ADA_SHARD)
    mod = lax.dynamic_index_in_dim(mod_all, me, axis=2, keepdims=False)
    mod = mod.transpose(1, 0, 2).reshape(DEPTH, 3 * D_MODEL) + b_ada
    shift, scale, gate = mod[:, :D_MODEL], mod[:, D_MODEL:2 * D_MODEL], mod[:, 2 * D_MODEL:]

    lg = jnp.concatenate([jax.nn.log_sigmoid(ret_decay_logit_f), jax.nn.log_sigmoid(ret_decay_logit_b)], axis=1)

    h = x2
    saved, layer_weights = [], []
    for l in range(DEPTH):
        layer_weights.append(weights)
        gather = (w_in_b, w_out_b, l + 1) if l + 1 < DEPTH else None
        h, sv, weights = layer_fwd(h, norm_gain[l:l + 1], scale[l:l + 1], shift[l:l + 1], gate[l:l + 1],
                                   *layer_weights[l], lg[l], gather)
        saved.append(sv)
    dh, loss_part, dfinal = loss_head(h, final_gain.reshape(1, D_MODEL), target)

    dmod, dgain, dlg, reduced = [None] * DEPTH, [None] * DEPTH, [None] * DEPTH, [None] * DEPTH
    for l in reversed(range(DEPTH)):
        dh, dshift, dscale, dgate, dg, dlg[l], reduced[l] = layer_bwd(
            dh, saved[l], norm_gain[l:l + 1], scale[l:l + 1], gate[l:l + 1], *layer_weights[l], lg[l], pos)
        dmod[l] = jnp.concatenate([dshift, dscale, dgate], axis=1)
        dgain[l] = dg

    dlg = jnp.stack(dlg)
    dlogit_f = dlg[:, :RET_HEADS] * jax.nn.sigmoid(-ret_decay_logit_f)
    dlogit_b = dlg[:, RET_HEADS:] * jax.nn.sigmoid(-ret_decay_logit_b)
    packed = _pack_small(jnp.concatenate(dmod, axis=0), jnp.concatenate(dgain, axis=0), dfinal, dlogit_f, dlogit_b,
                         loss=loss_part[0, 0])
    gathered = allgather_rows(packed, "allgather_small")
    small = small_update(gathered,
                         _pack_small(b_ada, norm_gain, final_gain, ret_decay_logit_f, ret_decay_logit_b),
                         _pack_small(m_b_ada, m_norm_gain, m_final_gain, m_ret_decay_logit_f, m_ret_decay_logit_b),
                         _pack_small(v_b_ada, v_norm_gain, v_final_gain, v_ret_decay_logit_f, v_ret_decay_logit_b))
    loss = small[0][ROWS_B_ADA + ROWS_GAIN + ROWS_FINAL + 1, 0]
    (g_b_ada, g_gain, g_final, g_dec_f, g_dec_b), (d_b_ada, d_gain, d_final, d_dec_f, d_dec_b), \
        (m_b_ada2, m_gain2, m_final2, m_dec_f2, m_dec_b2), (v_b_ada2, v_gain2, v_final2, v_dec_f2, v_dec_b2) = \
        [_unpack_small(p) for p in small]

    dmod_all = gathered[:, :ROWS_B_ADA].reshape(N_DEV, DEPTH, 3 * D_MODEL)
    dmod_mine = lax.dynamic_slice_in_dim(dmod_all, me * W_ADA_SHARD, W_ADA_SHARD, axis=2).transpose(1, 0, 2)
    g_w_ada, d_w_ada, m_w_ada2, v_w_ada2 = ada_update(act.T, dmod_mine, w_ada, m_w_ada, v_w_ada)

    upd_in = upd_out = None
    for l in reversed(range(DEPTH)):
        upd_in = shard_update(l, reduced[l][0][0], reduced[l][1][0], w_in, m_w_in, v_w_in, upd_in, f"w_in_update_{l}")
        upd_out = shard_update(l, reduced[l][0][1], reduced[l][1][1], w_out, m_w_out, v_w_out, upd_out, f"w_out_update_{l}")
    g_w_in, d_w_in, m_w_in2, v_w_in2 = upd_in
    g_w_out, d_w_out, m_w_out2, v_w_out2 = upd_out

    return (loss, dh[None],
            g_gain, g_w_ada, g_b_ada, g_w_in, g_w_out, g_dec_f, g_dec_b, g_final,
            d_gain, d_w_ada, d_b_ada, d_w_in, d_w_out, d_dec_f, d_dec_b, d_final,
            m_gain2, m_w_ada2, m_b_ada2, m_w_in2, m_w_out2, m_dec_f2, m_dec_b2, m_final2,
            v_gain2, v_w_ada2, v_b_ada2, v_w_in2, v_w_out2, v_dec_f2, v_dec_b2, v_final2)
```
